```python
import math
import jax, jax.numpy as jnp
from jax import lax
import numpy as np

D_MODEL = 1024
BATCH = 2
SEQ = 8192
DEPTH = 1

ATTN_WIDTH = D_MODEL // 2
ATTN_HEAD_DIM = 64
N_ATTN_HEADS = ATTN_WIDTH // ATTN_HEAD_DIM
MLSTM_WIDTH = D_MODEL - ATTN_WIDTH
N_MLSTM_HEADS = 4
MLSTM_HEAD_DIM = MLSTM_WIDTH // N_MLSTM_HEADS
MLSTM_CHUNK = 64
CONV_WIDTH = 4
DILATED_PATTERNS = ((128, 1), (512, 4), (2048, 16))
ATTN_BLOCK = 128
ATTN_SCALE = ATTN_HEAD_DIM ** -0.5
D_FF = ((8 * D_MODEL // 3 + 255) // 256) * 256
N_MOD = 6
RMS_EPS = 1e-6
IN_SPLITS = (ATTN_WIDTH, ATTN_WIDTH, ATTN_WIDTH,
             MLSTM_WIDTH, MLSTM_WIDTH, MLSTM_WIDTH,
             MLSTM_WIDTH,
             N_MLSTM_HEADS, N_MLSTM_HEADS)
IN_COLS = sum(IN_SPLITS)

kernel_name = "hymba_dilated_attn_mlstm_adaln_block"


def rms_norm(x, g):
    xf = x.astype(jnp.float32)
    y = xf * lax.rsqrt(jnp.mean(xf * xf, axis=-1, keepdims=True) + RMS_EPS)
    return (y * g.astype(jnp.float32)).astype(x.dtype)


def modulate(h, shift, scale):
    return h * (1 + scale[:, None, :]) + shift[:, None, :]


def causal_short_conv(x, w, b):
    S = x.shape[1]
    xp = jnp.pad(x, ((0, 0), (CONV_WIDTH - 1, 0), (0, 0)))
    y = b
    for j in range(CONV_WIDTH):
        y = y + xp[:, j:j + S] * w[j]
    return y


def dilated_branch(q, k, v, window, dilation):
    B, S, H, Dh = q.shape
    n_back = window // dilation
    assert n_back <= ATTN_BLOCK and S % dilation == 0
    L = S // dilation
    nb = -(-L // ATTN_BLOCK)
    Lp = nb * ATTN_BLOCK

    def to_sub(t):
        t = t.reshape(B, L, dilation, H, Dh).transpose(0, 2, 1, 3, 4)
        t = jnp.pad(t, ((0, 0), (0, 0), (0, Lp - L), (0, 0), (0, 0)))
        return t.reshape(B, dilation, nb, ATTN_BLOCK, H, Dh)

    def with_prev(t):
        prev = jnp.pad(t, ((0, 0), (0, 0), (1, 0), (0, 0), (0, 0), (0, 0)))[:, :, :-1]
        return jnp.concatenate([prev, t], axis=3)

    qb = to_sub(q)
    kb = with_prev(to_sub(k))
    vb = with_prev(to_sub(v))
    s = jnp.einsum('brnqhd,brnkhd->brnhqk', qb, kb).astype(jnp.float32)
    qi = jnp.arange(ATTN_BLOCK)[:, None] + ATTN_BLOCK
    kj = jnp.arange(2 * ATTN_BLOCK)[None, :]
    rel = qi - kj
    band = (rel >= 0) & (rel <= n_back)
    valid = band[None] & ((jnp.arange(nb)[:, None, None] > 0) | (kj[None] >= ATTN_BLOCK))
    s = jnp.where(valid[None, None, :, None], s, -jnp.inf)
    lse = jax.nn.logsumexp(s, axis=-1)
    p = jnp.exp(s - lse[..., None])
    o = jnp.einsum('brnhqk,brnkhd->brnqhd', p.astype(v.dtype), vb)
    o = o.reshape(B, dilation, Lp, H, Dh)[:, :, :L]
    o = o.transpose(0, 2, 1, 3, 4).reshape(B, S, H, Dh)
    lse = lse.transpose(0, 1, 2, 4, 3).reshape(B, dilation, Lp, H)[:, :, :L]
    lse = lse.transpose(0, 2, 1, 3).reshape(B, S, H)
    return o, lse


def dilated_attention(q, k, v):
    outs, lses = [], []
    for window, dilation in DILATED_PATTERNS:
        o, lse = dilated_branch(q, k, v, window, dilation)
        outs.append(o)
        lses.append(lse)
    w = jax.nn.softmax(jnp.stack(lses, axis=0), axis=0)
    return jnp.einsum('pbsh,pbshd->bshd', w.astype(v.dtype), jnp.stack(outs, axis=0))


def mlstm_chunkwise(q, k, v, log_i, log_f):
    B, NH, S, D = q.shape
    L = MLSTM_CHUNK
    NC = S // L
    k = k * (D ** -0.5)

    def to_chunks(t):
        return jnp.moveaxis(t.reshape(B, NH, NC, L, *t.shape[3:]), 2, 0)

    xs = (to_chunks(q), to_chunks(k), to_chunks(v), to_chunks(log_i), to_chunks(log_f))
    causal = jnp.tril(jnp.ones((L, L), dtype=bool))

    def step(carry, inp):
        C, n, m = carry
        qc, kc, vc, ic, fc = inp
        b = jnp.cumsum(fc, axis=-1)
        log_d = b[..., :, None] - b[..., None, :] + ic[..., None, :]
        log_d = jnp.where(causal, log_d, -jnp.inf)
        m_inter = b + m[..., None]
        m_t = jnp.maximum(m_inter, jnp.max(log_d, axis=-1))
        d_mat = jnp.exp(log_d - m_t[..., None])
        inter = jnp.exp(m_inter - m_t)
        s_qk = jnp.einsum('bhtd,bhsd->bhts', qc, kc) * d_mat
        num = inter[..., None] * jnp.einsum('bhtd,bhde->bhte', qc, C) + jnp.einsum('bhts,bhse->bhte', s_qk, vc)
        nq = inter * jnp.einsum('bhtd,bhd->bht', qc, n) + jnp.sum(s_qk, axis=-1)
        h = num / jnp.maximum(jnp.abs(nq), jnp.exp(-m_t))[..., None]
        b_last = b[..., -1]
        w_log = b_last[..., None] - b + ic
        m_new = jnp.maximum(b_last + m, jnp.max(w_log, axis=-1))
        w = jnp.exp(w_log - m_new[..., None])
        decay = jnp.exp(b_last + m - m_new)
        C_new = decay[..., None, None] * C + jnp.einsum('bhs,bhsd,bhse->bhde', w, kc, vc)
        n_new = decay[..., None] * n + jnp.einsum('bhs,bhsd->bhd', w, kc)
        return (C_new, n_new, m_new), h

    init = (jnp.zeros((B, NH, D, D), jnp.float32),
            jnp.zeros((B, NH, D), jnp.float32),
            jnp.zeros((B, NH), jnp.float32))
    _, h = lax.scan(step, init, xs)
    return jnp.moveaxis(h, 0, 2).reshape(B, NH, S, D)


def token_mixer(h, w_in, w_conv, b_conv, b_igate, b_fgate, q_norm_g, k_norm_g, mlstm_norm_g, w_out):
    B, S, _ = h.shape
    xin = h @ w_in
    cuts = list(np.cumsum(IN_SPLITS)[:-1])
    qa, ka, va, qm, km, vm, og, ig, fg = jnp.split(xin, cuts, axis=-1)
    qa = rms_norm(qa.reshape(B, S, N_ATTN_HEADS, ATTN_HEAD_DIM), q_norm_g) * ATTN_SCALE
    ka = rms_norm(ka.reshape(B, S, N_ATTN_HEADS, ATTN_HEAD_DIM), k_norm_g)
    va = va.reshape(B, S, N_ATTN_HEADS, ATTN_HEAD_DIM)
    attn = dilated_attention(qa, ka, va).reshape(B, S, ATTN_WIDTH)
    qkm = jax.nn.silu(causal_short_conv(jnp.concatenate([qm, km], axis=-1), w_conv, b_conv))
    qm, km = jnp.split(qkm, 2, axis=-1)

    def heads(t):
        return t.reshape(B, S, N_MLSTM_HEADS, MLSTM_HEAD_DIM).transpose(0, 2, 1, 3).astype(jnp.float32)

    log_i = (ig + b_igate).astype(jnp.float32).transpose(0, 2, 1)
    log_f = jax.nn.log_sigmoid((fg + b_fgate).astype(jnp.float32)).transpose(0, 2, 1)
    hm = mlstm_chunkwise(heads(qm), heads(km), heads(vm), log_i, log_f)
    hm = hm.transpose(0, 2, 1, 3).astype(h.dtype)
    hm = rms_norm(hm, mlstm_norm_g.reshape(N_MLSTM_HEADS, MLSTM_HEAD_DIM)).reshape(B, S, MLSTM_WIDTH)
    hm = jax.nn.sigmoid(og) * hm
    return jnp.concatenate([attn, hm], axis=-1) @ w_out


def swiglu(h, w_gate, w_up, w_down):
    return (jax.nn.silu(h @ w_gate) * (h @ w_up)) @ w_down


def setup_inputs(seed: int = 0) -> dict:
    key = jax.random.key(seed)
    ks = jax.random.split(key, 20)
    f32 = jnp.float32
    D = D_MODEL

    def nrm(k, shape, scale):
        return jax.random.normal(k, shape, f32) * scale

    b_ada = nrm(ks[15], (DEPTH, N_MOD * D), 0.02)
    gate_cols = jnp.zeros((N_MOD, D), f32).at[2].set(1.0).at[5].set(1.0).reshape(-1)
    b_ada = b_ada + gate_cols
    f_bias = jnp.linspace(3.0, 6.0, N_MLSTM_HEADS, dtype=f32)[None, :]
    return {
        "x": nrm(ks[0], (BATCH, SEQ, D), 1.0),
        "c": nrm(ks[1], (BATCH, D), 1.0),
        "g_mix": 1.0 + nrm(ks[2], (DEPTH, D), 0.02),
        "w_in": nrm(ks[3], (DEPTH, D, IN_COLS), D ** -0.5),
        "w_conv": nrm(ks[4], (DEPTH, CONV_WIDTH, 2 * MLSTM_WIDTH), CONV_WIDTH ** -0.5),
        "b_conv": nrm(ks[5], (DEPTH, 2 * MLSTM_WIDTH), 0.02),
        "b_igate": nrm(ks[6], (DEPTH, N_MLSTM_HEADS), 0.1),
        "b_fgate": f_bias + nrm(ks[7], (DEPTH, N_MLSTM_HEADS), 0.1),
        "q_norm_g": 1.0 + nrm(ks[8], (DEPTH, ATTN_HEAD_DIM), 0.02),
        "k_norm_g": 1.0 + nrm(ks[9], (DEPTH, ATTN_HEAD_DIM), 0.02),
        "mlstm_norm_g": 1.0 + nrm(ks[10], (DEPTH, MLSTM_WIDTH), 0.02),
        "w_out": nrm(ks[11], (DEPTH, D, D), D ** -0.5),
        "g_ffn": 1.0 + nrm(ks[12], (DEPTH, D), 0.02),
        "w_gate": nrm(ks[13], (DEPTH, D, D_FF), D ** -0.5),
        "w_up": nrm(ks[14], (DEPTH, D, D_FF), D ** -0.5),
        "w_down": nrm(ks[16], (DEPTH, D_FF, D), D_FF ** -0.5),
        "w_ada": nrm(ks[17], (DEPTH, D, N_MOD * D), 0.1 * D ** -0.5),
        "b_ada": b_ada,
    }


def reference(x, c, g_mix, w_in, w_conv, b_conv, b_igate, b_fgate, q_norm_g, k_norm_g,
              mlstm_norm_g, w_out, g_ffn, w_gate, w_up, w_down, w_ada, b_ada):
    for l in range(DEPTH):
        mod = jax.nn.silu(c) @ w_ada[l] + b_ada[l]
        sh_m, sc_m, gt_m, sh_f, sc_f, gt_f = jnp.split(mod, N_MOD, axis=-1)
        h = modulate(rms_norm(x, g_mix[l]), sh_m, sc_m)
        y = token_mixer(h, w_in[l], w_conv[l], b_conv[l], b_igate[l], b_fgate[l],
                        q_norm_g[l], k_norm_g[l], mlstm_norm_g[l], w_out[l])
        x = x + gt_m[:, None, :] * y
        h = modulate(rms_norm(x, g_ffn[l]), sh_f, sc_f)
        x = x + gt_f[:, None, :] * swiglu(h, w_gate[l], w_up[l], w_down[l])
    return x
```

```python
import functools

import jax
import jax.numpy as jnp
from jax import lax
from jax.experimental import pallas as pl
from jax.experimental.pallas import tpu as pltpu

F32 = jnp.float32
BF16 = jnp.bfloat16

ATTN_HEAD_DIM = 64
N_MLSTM_HEADS = 4
CONV_WIDTH = 4
DILATED_PATTERNS = ((128, 1), (512, 4), (2048, 16))
ATTN_BLOCK = 128
N_MOD = 6
RMS_EPS = 1e-6

LANES = 128
VMEM_LIMIT_BYTES = 56 * 1024 * 1024
MOD_ROWS = 8
GATE_LANES = 128
INPROJ_ROWS = 512
ATTN_SUB_BLOCKS = 4
MLSTM_CHUNK = 128
FFN_ROWS = 512
FFN_COLS = 256

NEG_INF = float("-inf")


def _dot(a, b, **kw):
    return jnp.dot(a, b, preferred_element_type=F32, **kw)


def _dot_nt(a, b):
    return lax.dot_general(a, b, (((1,), (1,)), ((), ())), preferred_element_type=F32)


def _dot_tn(a, b):
    return lax.dot_general(a, b, (((0,), (0,)), ((), ())), preferred_element_type=F32)


def _sigmoid(z):
    return 1.0 / (1.0 + jnp.exp(-z))


def _log_sigmoid(z):
    return jnp.minimum(z, 0.0) - jnp.log1p(jnp.exp(-jnp.abs(z)))


def _mod_kernel(c_ref, w_ref, b_ref, o_ref):
    c = c_ref[...]
    sc = c * _sigmoid(c)
    o_ref[...] = _dot(sc, w_ref[...], precision=lax.Precision.HIGHEST) + b_ref[...]


def _mod_call(c_pad, w_ada, b_ada):
    d, n = w_ada.shape
    tn = n // 4
    return pl.pallas_call(
        _mod_kernel,
        out_shape=jax.ShapeDtypeStruct((MOD_ROWS, n), F32),
        grid=(n // tn,),
        in_specs=[pl.BlockSpec((MOD_ROWS, d), lambda j: (0, 0)),
                  pl.BlockSpec((d, tn), lambda j: (0, j)),
                  pl.BlockSpec((1, tn), lambda j: (0, j))],
        out_specs=pl.BlockSpec((MOD_ROWS, tn), lambda j: (0, j)),
        compiler_params=pltpu.CompilerParams(dimension_semantics=("arbitrary",),
                                             vmem_limit_bytes=VMEM_LIMIT_BYTES),
        name="mod",
    )(c_pad, w_ada, b_ada)


def _inproj_kernel(x_ref, mod_ref, g_ref, wa_ref, wm_ref, wg_ref, wgt_ref, gq_ref, gk_ref, hp_ref,
                   bcol_ref, brow_ref,
                   qa_ref, ka_ref, va_ref, qkm_ref, vm_ref, og_ref, gcol_ref, grow_ref,
                   *, attn_w, ml_w, scale):
    x = x_ref[0]
    ms = jnp.mean(x * x, axis=-1, keepdims=True)
    y = x * lax.rsqrt(ms + RMS_EPS) * g_ref[...]
    h = (y * (1.0 + mod_ref[0, 1:2, :]) + mod_ref[0, 0:1, :]).astype(BF16)

    xa = _dot(h, wa_ref[...])
    hp = hp_ref[...]

    def head_norm(t, g):
        msq = _dot((t * t).astype(BF16), hp)
        return t * lax.rsqrt(msq + RMS_EPS) * g

    qa_ref[0] = (head_norm(xa[:, :attn_w], gq_ref[...]) * scale).astype(BF16)
    ka_ref[0] = head_norm(xa[:, attn_w:2 * attn_w], gk_ref[...]).astype(BF16)
    va_ref[0] = xa[:, 2 * attn_w:].astype(BF16)

    xm = _dot(h, wm_ref[...])
    qkm_ref[0] = xm[:, :2 * ml_w].astype(BF16)
    vm_ref[0] = xm[:, 2 * ml_w:3 * ml_w].astype(BF16)
    og_ref[0] = _sigmoid(xm[:, 3 * ml_w:]).astype(BF16)

    nh = N_MLSTM_HEADS
    zc = _dot(h, wg_ref[...]) + bcol_ref[...]
    lane = lax.broadcasted_iota(jnp.int32, zc.shape, 1)
    gcol_ref[0] = jnp.where(lane < nh, zc, jnp.where(lane < 2 * nh, _log_sigmoid(zc), 0.0))
    zr = _dot_nt(wgt_ref[...], h)[:2 * nh] + brow_ref[...]
    row = lax.broadcasted_iota(jnp.int32, zr.shape, 0)
    grow_ref[0] = jnp.where(row < nh, zr, _log_sigmoid(zr))


def _inproj_call(x, mod, g_mix, wa, wm, wg, wgt, gq, gk, hp, bcol, brow, *, attn_w, ml_w, scale):
    b, s, d = x.shape
    tm = INPROJ_ROWS
    nh2 = 2 * N_MLSTM_HEADS
    const = lambda shape: pl.BlockSpec(shape, lambda bi, i: (0,) * len(shape))
    tok = lambda w: pl.BlockSpec((1, tm, w), lambda bi, i: (bi, i, 0))
    out_shape = (jax.ShapeDtypeStruct((b, s, attn_w), BF16),) * 3 + (
        jax.ShapeDtypeStruct((b, s, 2 * ml_w), BF16),
        jax.ShapeDtypeStruct((b, s, ml_w), BF16),
        jax.ShapeDtypeStruct((b, s, ml_w), BF16),
        jax.ShapeDtypeStruct((b, s, GATE_LANES), F32),
        jax.ShapeDtypeStruct((b, nh2, s), F32))
    return pl.pallas_call(
        functools.partial(_inproj_kernel, attn_w=attn_w, ml_w=ml_w, scale=scale),
        out_shape=out_shape,
        grid=(b, s // tm),
        in_specs=[tok(d),
                  pl.BlockSpec((1, N_MOD, d), lambda bi, i: (bi, 0, 0)),
                  const((1, d)), const(wa.shape), const(wm.shape), const(wg.shape), const(wgt.shape),
                  const((1, attn_w)), const((1, attn_w)), const(hp.shape),
                  const((1, GATE_LANES)), const((nh2, 1))],
        out_specs=(tok(attn_w), tok(attn_w), tok(attn_w), tok(2 * ml_w), tok(ml_w), tok(ml_w),
                   tok(GATE_LANES),
                   pl.BlockSpec((1, nh2, tm), lambda bi, i: (bi, 0, i))),
        compiler_params=pltpu.CompilerParams(dimension_semantics=("arbitrary", "arbitrary"),
                                             vmem_limit_bytes=VMEM_LIMIT_BYTES),
        name="inproj",
    )(x, mod, g_mix, wa, wm, wg, wgt, gq, gk, hp, bcol, brow)


def _attn_kernel(q_ref, kp_ref, kc_ref, vp_ref, vc_ref, o_ref, lse_ref, k_s, v_s, *, n_back, n_heads):
    blk = ATTN_BLOCK
    hd = ATTN_HEAD_DIM
    n = pl.program_id(2)
    k_s[0:blk, :] = kp_ref[0]
    k_s[blk:, :] = kc_ref[0]
    v_s[0:blk, :] = vp_ref[0]
    v_s[blk:, :] = vc_ref[0]

    row = lax.broadcasted_iota(jnp.int32, (blk, 2 * blk), 0)
    col = lax.broadcasted_iota(jnp.int32, (blk, 2 * blk), 1)
    lane = lax.broadcasted_iota(jnp.int32, (blk, LANES), 1)

    def sub_block(i, carry):
        r0 = pl.multiple_of(i * blk, blk)
        q = q_ref[0, pl.ds(r0, blk), :]
        ks = k_s[pl.ds(r0, 2 * blk), :]
        vs = v_s[pl.ds(r0, 2 * blk), :]
        first = jnp.logical_and(n == 0, i == 0)
        lo = jnp.maximum(row + (blk - n_back), jnp.where(first, blk, 0))
        valid = jnp.logical_and(col >= lo, col <= row + blk)
        outs = []
        stats = jnp.zeros((blk, LANES), F32)
        for h in range(n_heads):
            sl = slice(h * hd, (h + 1) * hd)
            s = _dot_nt(q[:, sl], ks[:, sl])
            s = jnp.where(valid, s, NEG_INF)
            m = jnp.max(s, axis=-1, keepdims=True)
            p = jnp.exp(s - m)
            l = jnp.sum(p, axis=-1, keepdims=True)
            o = _dot(p.astype(BF16), vs[:, sl])
            outs.append(o / l)
            stats = jnp.where(lane == h, m + jnp.log(l), stats)
        o_ref[0, pl.ds(r0, blk), :] = jnp.concatenate(outs, axis=-1).astype(o_ref.dtype)
        lse_ref[0, pl.ds(r0, blk), :] = stats
        return carry

    lax.fori_loop(0, ATTN_SUB_BLOCKS, sub_block, 0)


def _attn_call(q, k, v, *, window, dilation):
    b, s, w = q.shape
    n_back = window // dilation
    blk = ATTN_BLOCK
    assert n_back <= blk and s % dilation == 0
    ls = s // dilation
    step_rows = ATTN_SUB_BLOCKS * blk
    assert ls % step_rows == 0
    n_heads = w // ATTN_HEAD_DIM
    view = lambda t: t.reshape(b, ls, dilation * w)
    cur = pl.BlockSpec((1, step_rows, w), lambda bi, r, n: (bi, n, r))
    prev = pl.BlockSpec((1, blk, w),
                        lambda bi, r, n: (bi, jnp.maximum(n * ATTN_SUB_BLOCKS - 1, 0), r))
    o, lse = pl.pallas_call(
        functools.partial(_attn_kernel, n_back=n_back, n_heads=n_heads),
        out_shape=(jax.ShapeDtypeStruct((b, ls, dilation * w), BF16),
                   jax.ShapeDtypeStruct((b, ls, dilation * LANES), F32)),
        grid=(b, dilation, ls // step_rows),
        in_specs=[cur, prev, cur, prev, cur],
        out_specs=(cur, pl.BlockSpec((1, step_rows, LANES), lambda bi, r, n: (bi, n, r))),
        scratch_shapes=[pltpu.VMEM((blk + step_rows, w), BF16),
                        pltpu.VMEM((blk + step_rows, w), BF16)],
        compiler_params=pltpu.CompilerParams(
            dimension_semantics=("arbitrary", "arbitrary", "arbitrary"),
            vmem_limit_bytes=VMEM_LIMIT_BYTES),
        name=f"attn_d{dilation}",
    )(view(q), view(k), view(k), view(v), view(v))
    return o.reshape(b, s, w), lse.reshape(b, s, LANES)


def _mlstm_kernel(qk_ref, v_ref, og_ref, gcol_ref, grow_ref, wc_ref, bc_ref, gn_ref, tri_ref, hm_ref,
                  o_ref, tail_s, c_s, m_s, *, ml_w):
    L = MLSTM_CHUNK
    nh = N_MLSTM_HEADS
    dh = ml_w // nh
    ci = pl.program_id(1)

    @pl.when(ci == 0)
    def _():
        tail_s[...] = jnp.zeros_like(tail_s)
        c_s[...] = jnp.zeros_like(c_s)
        m_s[...] = jnp.zeros_like(m_s)

    x = qk_ref[0].astype(F32)
    tail = tail_s[...]
    rowi = lax.broadcasted_iota(jnp.int32, x.shape, 0)
    acc = bc_ref[...] + x * wc_ref[CONV_WIDTH - 1:CONV_WIDTH, :]
    for back in range(1, CONV_WIDTH):
        shifted = jnp.where(rowi >= back, pltpu.roll(x, back, 0), pltpu.roll(tail, back, 0))
        acc = acc + shifted * wc_ref[CONV_WIDTH - 1 - back:CONV_WIDTH - back, :]
    tail_s[...] = x
    qk = acc * _sigmoid(acc)

    gcol = gcol_ref[0]
    grow = grow_ref[0]
    tri = tri_ref[...]
    bcol_all = _dot(tri, gcol, precision=lax.Precision.HIGHEST)
    brow_all = lax.dot_general(grow, tri, (((1,), (1,)), ((), ())), precision=lax.Precision.HIGHEST,
                               preferred_element_type=F32)
    ti = lax.broadcasted_iota(jnp.int32, (L, L), 0)
    si = lax.broadcasted_iota(jnp.int32, (L, L), 1)
    causal = si <= ti
    v_all = v_ref[0]
    ones = jnp.ones((L, dh), BF16)
    hp = hm_ref[...]

    outs = []
    for h in range(nh):
        q = qk[:, h * dh:(h + 1) * dh].astype(BF16)
        k = (qk[:, ml_w + h * dh:ml_w + (h + 1) * dh] * (dh ** -0.5))
        v_ext = jnp.concatenate([v_all[:, h * dh:(h + 1) * dh], ones], axis=-1)
        b_c = jnp.broadcast_to(bcol_all[:, nh + h:nh + h + 1], (L, dh))
        i_c = jnp.broadcast_to(gcol[:, h:h + 1], (L, dh))
        b_r = brow_all[nh + h:nh + h + 1, :]
        i_r = grow[h:h + 1, :]
        m_prev = m_s[h, 0:1, :]
        c_prev = c_s[h]

        log_d = jnp.where(causal, b_c[:, 0:1] - b_r + i_r, NEG_INF)
        m_inter = b_c + m_prev
        m_t = jnp.maximum(m_inter, jnp.max(log_d, axis=-1, keepdims=True))
        d_mat = jnp.exp(log_d - m_t[:, 0:1])
        inter = jnp.exp(m_inter - m_t)
        s_qk = _dot_nt(q, k.astype(BF16)) * d_mat
        ext = jnp.concatenate([inter, inter], axis=-1) * _dot(q, c_prev.astype(BF16)) \
            + _dot(s_qk.astype(BF16), v_ext)
        num = ext[:, :dh]
        nq = ext[:, dh:]
        hh = num / jnp.maximum(jnp.abs(nq), jnp.exp(-m_t))
        msq = _dot((hh * hh).astype(BF16), hp)
        outs.append(hh * lax.rsqrt(msq + RMS_EPS))

        b_last = b_c[L - 1:L, :]
        w_log = b_last - b_c + i_c
        m_new = jnp.maximum(b_last + m_prev, jnp.max(w_log, axis=0, keepdims=True))
        wgt = jnp.exp(w_log - m_new)
        decay = jnp.exp(b_last + m_prev - m_new)
        kw = (k * wgt).astype(BF16)
        c_s[h] = jnp.concatenate([decay, decay], axis=-1) * c_prev + _dot_tn(kw, v_ext)
        m_s[h] = jnp.broadcast_to(m_new, m_s.shape[1:])

    hn = jnp.concatenate(outs, axis=-1) * gn_ref[...]
    o_ref[0] = (og_ref[0].astype(F32) * hn).astype(o_ref.dtype)


def _mlstm_call(qkm, vm, og, gcol, grow, w_conv, b_conv, g_norm, tri, hmean):
    b, s, ml_w = vm.shape
    L = MLSTM_CHUNK
    nh = N_MLSTM_HEADS
    dh = ml_w // nh
    const = lambda shape: pl.BlockSpec(shape, lambda bi, i: (0,) * len(shape))
    tok = lambda w: pl.BlockSpec((1, L, w), lambda bi, i: (bi, i, 0))
    return pl.pallas_call(
        functools.partial(_mlstm_kernel, ml_w=ml_w),
        out_shape=jax.ShapeDtypeStruct((b, s, ml_w), BF16),
        grid=(b, s // L),
        in_specs=[tok(2 * ml_w), tok(ml_w), tok(ml_w), tok(GATE_LANES),
                  pl.BlockSpec((1, 2 * nh, L), lambda bi, i: (bi, 0, i)),
                  const(w_conv.shape), const((1, 2 * ml_w)), const((1, ml_w)),
                  const((L, L)), const(hmean.shape)],
        out_specs=tok(ml_w),
        scratch_shapes=[pltpu.VMEM((L, 2 * ml_w), F32),
                        pltpu.VMEM((nh, dh, 2 * dh), F32),
                        pltpu.VMEM((nh, 8, dh), F32)],
        compiler_params=pltpu.CompilerParams(dimension_semantics=("arbitrary", "arbitrary"),
                                             vmem_limit_bytes=VMEM_LIMIT_BYTES),
        name="mlstm",
    )(qkm, vm, og, gcol, grow, w_conv, b_conv, g_norm, tri, hmean)


def _outffn_kernel(x_ref, mod_ref, g_ref, o1_ref, o2_ref, o3_ref, l1_ref, l2_ref, l3_ref, hm_ref, ex_ref,
                   woa_ref, wom_ref, wg_ref, wu_ref, wd_ref, out_ref, h_s, acc_s, *, d_ff):
    l1, l2, l3 = l1_ref[0], l2_ref[0], l3_ref[0]
    mx = jnp.maximum(jnp.maximum(l1, l2), l3)
    e1, e2, e3 = jnp.exp(l1 - mx), jnp.exp(l2 - mx), jnp.exp(l3 - mx)
    inv = 1.0 / (e1 + e2 + e3)
    ex = ex_ref[...]
    attn = (_dot((e1 * inv).astype(BF16), ex) * o1_ref[0].astype(F32)
            + _dot((e2 * inv).astype(BF16), ex) * o2_ref[0].astype(F32)
            + _dot((e3 * inv).astype(BF16), ex) * o3_ref[0].astype(F32))
    y = _dot(attn.astype(BF16), woa_ref[...]) + _dot(hm_ref[0], wom_ref[...])
    x1 = x_ref[0] + mod_ref[0, 2:3, :] * y
    ms = jnp.mean(x1 * x1, axis=-1, keepdims=True)
    hn = x1 * lax.rsqrt(ms + RMS_EPS) * g_ref[...]
    h_s[...] = (hn * (1.0 + mod_ref[0, 4:5, :]) + mod_ref[0, 3:4, :]).astype(BF16)
    acc_s[...] = x1

    def ff_step(j, carry):
        c0 = pl.multiple_of(j * FFN_COLS, FFN_COLS)
        hb = h_s[...]
        g = _dot(hb, wg_ref[:, pl.ds(c0, FFN_COLS)])
        u = _dot(hb, wu_ref[:, pl.ds(c0, FFN_COLS)])
        a = (g * _sigmoid(g) * u).astype(BF16)
        acc_s[...] += mod_ref[0, 5:6, :] * _dot(a, wd_ref[pl.ds(c0, FFN_COLS), :])
        return carry

    lax.fori_loop(0, d_ff // FFN_COLS, ff_step, 0)
    out_ref[0] = acc_s[...]


def _outffn_call(x, mod, g_ffn, os_, lses, hm, expand, woa, wom, wg, wu, wd):
    b, s, d = x.shape
    tm = FFN_ROWS
    d_ff = wg.shape[1]
    assert d_ff % FFN_COLS == 0
    aw = woa.shape[0]
    const = lambda shape: pl.BlockSpec(shape, lambda bi, i: (0,) * len(shape))
    tok = lambda w: pl.BlockSpec((1, tm, w), lambda bi, i: (bi, i, 0))
    return pl.pallas_call(
        functools.partial(_outffn_kernel, d_ff=d_ff),
        out_shape=jax.ShapeDtypeStruct((b, s, d), F32),
        grid=(b, s // tm),
        in_specs=[tok(d), pl.BlockSpec((1, N_MOD, d), lambda bi, i: (bi, 0, 0)), const((1, d)),
                  tok(aw), tok(aw), tok(aw), tok(LANES), tok(LANES), tok(LANES), tok(hm.shape[-1]),
                  const(expand.shape), const(woa.shape), const(wom.shape),
                  const(wg.shape), const(wu.shape), const(wd.shape)],
        out_specs=tok(d),
        scratch_shapes=[pltpu.VMEM((tm, d), BF16), pltpu.VMEM((tm, d), F32)],
        compiler_params=pltpu.CompilerParams(dimension_semantics=("arbitrary", "arbitrary"),
                                             vmem_limit_bytes=VMEM_LIMIT_BYTES),
        name="outffn",
    )(x, mod, g_ffn, *os_, *lses, hm, expand, woa, wom, wg, wu, wd)


def _block_diag_mean(width, group):
    idx = jnp.arange(width) // group
    return jnp.where(idx[:, None] == idx[None, :], 1.0 / group, 0.0).astype(BF16)


def kernel(x, c, g_mix, w_in, w_conv, b_conv, b_igate, b_fgate, q_norm_g, k_norm_g, mlstm_norm_g, w_out,
           g_ffn, w_gate, w_up, w_down, w_ada, b_ada):
    b, s, d = x.shape
    depth = g_mix.shape[0]
    attn_w = d // 2
    ml_w = d - attn_w
    nh = N_MLSTM_HEADS
    n_attn_heads = attn_w // ATTN_HEAD_DIM
    scale = ATTN_HEAD_DIM ** -0.5
    assert b <= MOD_ROWS and 2 * nh <= GATE_LANES

    c_pad = jnp.zeros((MOD_ROWS, d), F32).at[:b].set(c)
    hmean_attn = _block_diag_mean(attn_w, ATTN_HEAD_DIM)
    hmean_ml = _block_diag_mean(ml_w // nh, ml_w // nh)
    head_of_lane = jnp.arange(attn_w) // ATTN_HEAD_DIM
    expand = (jnp.arange(LANES)[:, None] == head_of_lane[None, :]).astype(BF16)
    tri = jnp.tril(jnp.ones((MLSTM_CHUNK, MLSTM_CHUNK), F32))

    for l in range(depth):
        mod = _mod_call(c_pad, w_ada[l], b_ada[l][None, :])[:b].reshape(b, N_MOD, d)

        n_main = 3 * attn_w + 4 * ml_w
        wa = w_in[l][:, :3 * attn_w].astype(BF16)
        wm = w_in[l][:, 3 * attn_w:n_main].astype(BF16)
        wg_cols = w_in[l][:, n_main:]
        wg = jnp.zeros((d, GATE_LANES), F32).at[:, :2 * nh].set(wg_cols).astype(BF16)
        wgt = jnp.zeros((16, d), F32).at[:2 * nh].set(wg_cols.T).astype(BF16)
        gate_bias = jnp.concatenate([b_igate[l], b_fgate[l]])
        bcol = jnp.zeros((1, GATE_LANES), F32).at[0, :2 * nh].set(gate_bias)
        brow = gate_bias[:, None]
        gq = jnp.tile(q_norm_g[l], n_attn_heads)[None, :]
        gk = jnp.tile(k_norm_g[l], n_attn_heads)[None, :]

        qa, ka, va, qkm, vm, og, gcol, grow = _inproj_call(
            x, mod, g_mix[l][None, :], wa, wm, wg, wgt, gq, gk, hmean_attn, bcol, brow,
            attn_w=attn_w, ml_w=ml_w, scale=scale)

        os_, lses = [], []
        for window, dilation in DILATED_PATTERNS:
            o, lse = _attn_call(qa, ka, va, window=window, dilation=dilation)
            os_.append(o)
            lses.append(lse)

        hm = _mlstm_call(qkm, vm, og, gcol, grow, w_conv[l], b_conv[l][None, :],
                         mlstm_norm_g[l][None, :], tri, hmean_ml)

        x = _outffn_call(x, mod, g_ffn[l][None, :], os_, lses, hm, expand,
                         w_out[l][:attn_w].astype(BF16), w_out[l][attn_w:].astype(BF16),
                         w_gate[l].astype(BF16), w_up[l].astype(BF16), w_down[l].astype(BF16))
    return x
```

```python
import functools

import jax
import jax.numpy as jnp
from jax import lax
from jax.experimental import pallas as pl
from jax.experimental.pallas import tpu as pltpu

F32 = jnp.float32
BF16 = jnp.bfloat16

ATTN_HEAD_DIM = 64
N_MLSTM_HEADS = 4
CONV_WIDTH = 4
DILATED_PATTERNS = ((128, 1), (512, 4), (2048, 16))
ATTN_BLOCK = 128
N_MOD = 6
RMS_EPS = 1e-6

LANES = 128
BF16_SUBLANES = 16
VMEM_LIMIT_BYTES = 56 * 1024 * 1024
MOD_ROWS = 8
GATE_LANES = 128
INPROJ_ROWS = 512
ATTN_SUB_BLOCKS = 4
MLSTM_CHUNK = 128
FFN_ROWS = 512
FFN_COLS = 256

NEG_INF = float("-inf")


def _dot(a, b, **kw):
    return jnp.dot(a, b, preferred_element_type=F32, **kw)


def _dot_nt(a, b, **kw):
    return lax.dot_general(a, b, (((1,), (1,)), ((), ())), preferred_element_type=F32, **kw)


def _dot_tn(a, b):
    return lax.dot_general(a, b, (((0,), (0,)), ((), ())), preferred_element_type=F32)


def _sigmoid(z):
    return 1.0 / (1.0 + jnp.exp(-z))


def _log_sigmoid(z):
    return jnp.minimum(z, 0.0) - jnp.log1p(jnp.exp(-jnp.abs(z)))


def _strided_shape(b, s, w, dilation):
    return (b, s // dilation, dilation * w)


def _mod_kernel(c_ref, w_ref, b_ref, o_ref):
    c = c_ref[...]
    sc = c * _sigmoid(c)
    o_ref[...] = _dot(sc, w_ref[...], precision=lax.Precision.HIGHEST) + b_ref[...]


def _mod_call(c_pad, w_ada, b_ada):
    d, n = w_ada.shape
    tn = n // 4
    return pl.pallas_call(
        _mod_kernel,
        out_shape=jax.ShapeDtypeStruct((MOD_ROWS, n), F32),
        grid=(n // tn,),
        in_specs=[pl.BlockSpec((MOD_ROWS, d), lambda j: (0, 0)),
                  pl.BlockSpec((d, tn), lambda j: (0, j)),
                  pl.BlockSpec((1, tn), lambda j: (0, j))],
        out_specs=pl.BlockSpec((MOD_ROWS, tn), lambda j: (0, j)),
        compiler_params=pltpu.CompilerParams(dimension_semantics=("arbitrary",),
                                             vmem_limit_bytes=VMEM_LIMIT_BYTES),
        name="mod",
    )(c_pad, w_ada, b_ada)


def _inproj_kernel(x_ref, mod_ref, g_ref, wa_ref, wm_ref, wg_ref, wgt_ref, gq_ref, gk_ref, hp_ref,
                   bcol_ref, brow_ref, *rest, attn_w, ml_w, scale, dilations):
    n_lay = len(dilations)
    q_refs, k_refs, v_refs = rest[:n_lay], rest[n_lay:2 * n_lay], rest[2 * n_lay:3 * n_lay]
    qkm_ref, vm_ref, og_ref, gcol_ref, grow_ref, perm_s = rest[3 * n_lay:]
    tm = x_ref.shape[1]

    def emit(val, refs):
        w = val.shape[-1]
        staged = False
        for d, ref in zip(dilations, refs):
            if d == 1:
                ref[0] = val.astype(BF16)
                continue
            if not staged:
                for cg in range(w // LANES):
                    perm_s[cg] = val[:, cg * LANES:(cg + 1) * LANES]
                staged = True
            for r in range(d):
                for cg in range(w // LANES):
                    c0 = r * w + cg * LANES
                    ref[0, :, c0:c0 + LANES] = perm_s[cg, pl.ds(r, tm // d, stride=d), :].astype(BF16)

    x = x_ref[0]
    ms = jnp.mean(x * x, axis=-1, keepdims=True)
    y = x * lax.rsqrt(ms + RMS_EPS) * g_ref[...]
    h = (y * (1.0 + mod_ref[0, 1:2, :]) + mod_ref[0, 0:1, :]).astype(BF16)

    xa = _dot(h, wa_ref[...])
    hp = hp_ref[...]

    def head_norm(t, g):
        msq = _dot((t * t).astype(BF16), hp)
        return t * lax.rsqrt(msq + RMS_EPS) * g

    emit(head_norm(xa[:, :attn_w], gq_ref[...]) * scale, q_refs)
    emit(head_norm(xa[:, attn_w:2 * attn_w], gk_ref[...]), k_refs)
    emit(xa[:, 2 * attn_w:], v_refs)

    xm = _dot(h, wm_ref[...])
    qkm_ref[0] = xm[:, :2 * ml_w].astype(BF16)
    vm_ref[0] = xm[:, 2 * ml_w:3 * ml_w].astype(BF16)
    og_ref[0] = _sigmoid(xm[:, 3 * ml_w:]).astype(BF16)

    nh = N_MLSTM_HEADS
    zc = _dot(h, wg_ref[...]) + bcol_ref[...]
    lane = lax.broadcasted_iota(jnp.int32, zc.shape, 1)
    gcol_ref[0] = jnp.where(lane < nh, zc, jnp.where(lane < 2 * nh, _log_sigmoid(zc), 0.0))
    zr = _dot_nt(wgt_ref[...], h)[:2 * nh] + brow_ref[...]
    row = lax.broadcasted_iota(jnp.int32, zr.shape, 0)
    grow_ref[0] = jnp.where(row < nh, zr, _log_sigmoid(zr))


def _inproj_call(x, mod, g_mix, wa, wm, wg, wgt, gq, gk, hp, bcol, brow, *, attn_w, ml_w, scale, dilations):
    b, s, d = x.shape
    tm = INPROJ_ROWS
    nh2 = 2 * N_MLSTM_HEADS
    assert all(tm % (dl * BF16_SUBLANES) == 0 for dl in dilations)
    const = lambda shape: pl.BlockSpec(shape, lambda bi, i: (0,) * len(shape))
    tok = lambda w: pl.BlockSpec((1, tm, w), lambda bi, i: (bi, i, 0))
    strided_shapes = tuple(jax.ShapeDtypeStruct(_strided_shape(b, s, attn_w, dl), BF16) for dl in dilations)
    strided_specs = tuple(pl.BlockSpec((1, tm // dl, dl * attn_w), lambda bi, i: (bi, i, 0)) for dl in dilations)
    out_shape = strided_shapes * 3 + (
        jax.ShapeDtypeStruct((b, s, 2 * ml_w), BF16),
        jax.ShapeDtypeStruct((b, s, ml_w), BF16),
        jax.ShapeDtypeStruct((b, s, ml_w), BF16),
        jax.ShapeDtypeStruct((b, s, GATE_LANES), F32),
        jax.ShapeDtypeStruct((b, nh2, s), F32))
    outs = pl.pallas_call(
        functools.partial(_inproj_kernel, attn_w=attn_w, ml_w=ml_w, scale=scale, dilations=dilations),
        out_shape=out_shape,
        grid=(b, s // tm),
        in_specs=[tok(d),
                  pl.BlockSpec((1, N_MOD, d), lambda bi, i: (bi, 0, 0)),
                  const((1, d)), const(wa.shape), const(wm.shape), const(wg.shape), const(wgt.shape),
                  const((1, attn_w)), const((1, attn_w)), const(hp.shape),
                  const((1, GATE_LANES)), const((nh2, 1))],
        out_specs=strided_specs * 3 + (tok(2 * ml_w), tok(ml_w), tok(ml_w), tok(GATE_LANES),
                                       pl.BlockSpec((1, nh2, tm), lambda bi, i: (bi, 0, i))),
        scratch_shapes=[pltpu.VMEM((attn_w // LANES, tm, LANES), F32)],
        compiler_params=pltpu.CompilerParams(dimension_semantics=("arbitrary", "arbitrary"),
                                             vmem_limit_bytes=VMEM_LIMIT_BYTES),
        name="inproj",
    )(x, mod, g_mix, wa, wm, wg, wgt, gq, gk, hp, bcol, brow)
    n_lay = len(dilations)
    return (outs[:n_lay], outs[n_lay:2 * n_lay], outs[2 * n_lay:3 * n_lay]) + tuple(outs[3 * n_lay:])


def _attn_kernel(q_ref, kp_ref, kc_ref, vp_ref, vc_ref, o_ref, lse_ref, k_s, v_s, *, n_back, n_heads):
    blk = ATTN_BLOCK
    hd = ATTN_HEAD_DIM
    n = pl.program_id(2)
    k_s[0:blk, :] = kp_ref[0]
    k_s[blk:, :] = kc_ref[0]
    v_s[0:blk, :] = vp_ref[0]
    v_s[blk:, :] = vc_ref[0]

    row = lax.broadcasted_iota(jnp.int32, (blk, 2 * blk), 0)
    col = lax.broadcasted_iota(jnp.int32, (blk, 2 * blk), 1)
    lane = lax.broadcasted_iota(jnp.int32, (blk, LANES), 1)

    def sub_block(i, carry):
        r0 = pl.multiple_of(i * blk, blk)
        q = q_ref[0, pl.ds(r0, blk), :]
        ks = k_s[pl.ds(r0, 2 * blk), :]
        vs = v_s[pl.ds(r0, 2 * blk), :]
        first = jnp.logical_and(n == 0, i == 0)
        lo = jnp.maximum(row + (blk - n_back), jnp.where(first, blk, 0))
        valid = jnp.logical_and(col >= lo, col <= row + blk)
        outs = []
        stats = jnp.zeros((blk, LANES), F32)
        for h in range(n_heads):
            sl = slice(h * hd, (h + 1) * hd)
            s = _dot_nt(q[:, sl], ks[:, sl])
            s = jnp.where(valid, s, NEG_INF)
            m = jnp.max(s, axis=-1, keepdims=True)
            p = jnp.exp(s - m)
            l = jnp.sum(p, axis=-1, keepdims=True)
            o = _dot(p.astype(BF16), vs[:, sl])
            outs.append(o / l)
            stats = jnp.where(lane == h, m + jnp.log(l), stats)
        o_ref[0, pl.ds(r0, blk), :] = jnp.concatenate(outs, axis=-1).astype(o_ref.dtype)
        lse_ref[0, pl.ds(r0, blk), :] = stats
        return carry

    lax.fori_loop(0, ATTN_SUB_BLOCKS, sub_block, 0)


def _attn_call(q, k, v, *, width, window, dilation):
    b, ls, _ = q.shape
    w = width
    n_back = window // dilation
    blk = ATTN_BLOCK
    assert n_back <= blk
    step_rows = ATTN_SUB_BLOCKS * blk
    assert ls % step_rows == 0
    n_heads = w // ATTN_HEAD_DIM
    assert n_heads <= LANES
    cur = pl.BlockSpec((1, step_rows, w), lambda bi, r, n: (bi, n, r))
    prev = pl.BlockSpec((1, blk, w),
                        lambda bi, r, n: (bi, jnp.maximum(n * ATTN_SUB_BLOCKS - 1, 0), r))
    return pl.pallas_call(
        functools.partial(_attn_kernel, n_back=n_back, n_heads=n_heads),
        out_shape=(jax.ShapeDtypeStruct((b, ls, dilation * w), BF16),
                   jax.ShapeDtypeStruct((b, ls, dilation * LANES), F32)),
        grid=(b, dilation, ls // step_rows),
        in_specs=[cur, prev, cur, prev, cur],
        out_specs=(cur, pl.BlockSpec((1, step_rows, LANES), lambda bi, r, n: (bi, n, r))),
        scratch_shapes=[pltpu.VMEM((blk + step_rows, w), BF16),
                        pltpu.VMEM((blk + step_rows, w), BF16)],
        compiler_params=pltpu.CompilerParams(
            dimension_semantics=("arbitrary", "arbitrary", "arbitrary"),
            vmem_limit_bytes=VMEM_LIMIT_BYTES),
        name=f"attn_d{dilation}",
    )(q, k, k, v, v)


def _mlstm_kernel(qk_ref, v_ref, og_ref, gcol_ref, grow_ref, wc_ref, bc_ref, gn_ref, tri_ref, hm_ref,
                  o_ref, tail_s, c_s, m_s, *, ml_w):
    L = MLSTM_CHUNK
    nh = N_MLSTM_HEADS
    dh = ml_w // nh
    ci = pl.program_id(1)

    @pl.when(ci == 0)
    def _():
        tail_s[...] = jnp.zeros_like(tail_s)
        c_s[...] = jnp.zeros_like(c_s)
        m_s[...] = jnp.zeros_like(m_s)

    x = qk_ref[0].astype(F32)
    tail = tail_s[...]
    rowi = lax.broadcasted_iota(jnp.int32, x.shape, 0)
    acc = bc_ref[...] + x * wc_ref[CONV_WIDTH - 1:CONV_WIDTH, :]
    for back in range(1, CONV_WIDTH):
        shifted = jnp.where(rowi >= back, pltpu.roll(x, back, 0), pltpu.roll(tail, back, 0))
        acc = acc + shifted * wc_ref[CONV_WIDTH - 1 - back:CONV_WIDTH - back, :]
    tail_s[...] = x
    qk = acc * _sigmoid(acc)

    gcol = gcol_ref[0]
    grow = grow_ref[0]
    tri = tri_ref[...]
    bcol_all = _dot(tri, gcol, precision=lax.Precision.HIGHEST)
    brow_all = _dot_nt(grow, tri, precision=lax.Precision.HIGHEST)
    ti = lax.broadcasted_iota(jnp.int32, (L, L), 0)
    si = lax.broadcasted_iota(jnp.int32, (L, L), 1)
    causal = si <= ti
    v_all = v_ref[0]
    ones = jnp.ones((L, dh), BF16)
    hp = hm_ref[...]

    outs = []
    for h in range(nh):
        q = qk[:, h * dh:(h + 1) * dh].astype(BF16)
        k = (qk[:, ml_w + h * dh:ml_w + (h + 1) * dh] * (dh ** -0.5))
        v_ext = jnp.concatenate([v_all[:, h * dh:(h + 1) * dh], ones], axis=-1)
        b_c = jnp.broadcast_to(bcol_all[:, nh + h:nh + h + 1], (L, dh))
        i_c = jnp.broadcast_to(gcol[:, h:h + 1], (L, dh))
        b_r = brow_all[nh + h:nh + h + 1, :]
        i_r = grow[h:h + 1, :]
        m_prev = m_s[h, 0:1, :]
        c_prev = c_s[h]

        log_d = jnp.where(causal, b_c[:, 0:1] - b_r + i_r, NEG_INF)
        m_inter = b_c + m_prev
        m_t = jnp.maximum(m_inter, jnp.max(log_d, axis=-1, keepdims=True))
        d_mat = jnp.exp(log_d - m_t[:, 0:1])
        inter = jnp.exp(m_inter - m_t)
        s_qk = _dot_nt(q, k.astype(BF16)) * d_mat
        ext = jnp.concatenate([inter, inter], axis=-1) * _dot(q, c_prev.astype(BF16)) \
            + _dot(s_qk.astype(BF16), v_ext)
        num = ext[:, :dh]
        nq = ext[:, dh:]
        hh = num / jnp.maximum(jnp.abs(nq), jnp.exp(-m_t))
        msq = _dot((hh * hh).astype(BF16), hp)
        outs.append(hh * lax.rsqrt(msq + RMS_EPS))

        b_last = b_c[L - 1:L, :]
        w_log = b_last - b_c + i_c
        m_new = jnp.maximum(b_last + m_prev, jnp.max(w_log, axis=0, keepdims=True))
        wgt = jnp.exp(w_log - m_new)
        decay = jnp.exp(b_last + m_prev - m_new)
        kw = (k * wgt).astype(BF16)
        c_s[h] = jnp.concatenate([decay, decay], axis=-1) * c_prev + _dot_tn(kw, v_ext)
        m_s[h] = jnp.broadcast_to(m_new, m_s.shape[1:])

    hn = jnp.concatenate(outs, axis=-1) * gn_ref[...]
    o_ref[0] = (og_ref[0].astype(F32) * hn).astype(o_ref.dtype)


def _mlstm_call(qkm, vm, og, gcol, grow, w_conv, b_conv, g_norm, tri, hmean):
    b, s, ml_w = vm.shape
    L = MLSTM_CHUNK
    nh = N_MLSTM_HEADS
    dh = ml_w // nh
    const = lambda shape: pl.BlockSpec(shape, lambda bi, i: (0,) * len(shape))
    tok = lambda w: pl.BlockSpec((1, L, w), lambda bi, i: (bi, i, 0))
    return pl.pallas_call(
        functools.partial(_mlstm_kernel, ml_w=ml_w),
        out_shape=jax.ShapeDtypeStruct((b, s, ml_w), BF16),
        grid=(b, s // L),
        in_specs=[tok(2 * ml_w), tok(ml_w), tok(ml_w), tok(GATE_LANES),
                  pl.BlockSpec((1, 2 * nh, L), lambda bi, i: (bi, 0, i)),
                  const(w_conv.shape), const((1, 2 * ml_w)), const((1, ml_w)),
                  const((L, L)), const(hmean.shape)],
        out_specs=tok(ml_w),
        scratch_shapes=[pltpu.VMEM((L, 2 * ml_w), F32),
                        pltpu.VMEM((nh, dh, 2 * dh), F32),
                        pltpu.VMEM((nh, MOD_ROWS, dh), F32)],
        compiler_params=pltpu.CompilerParams(dimension_semantics=("arbitrary", "arbitrary"),
                                             vmem_limit_bytes=VMEM_LIMIT_BYTES),
        name="mlstm",
    )(qkm, vm, og, gcol, grow, w_conv, b_conv, g_norm, tri, hmean)


def _outffn_kernel(x_ref, mod_ref, g_ref, *rest, d_ff, dilations):
    n_lay = len(dilations)
    o_refs, l_refs = rest[:n_lay], rest[n_lay:2 * n_lay]
    (hm_ref, ex_ref, woa_ref, wom_ref, wg_ref, wu_ref, wd_ref, out_ref,
     h_s, acc_s, operm_s, lperm_s) = rest[2 * n_lay:]
    tm = x_ref.shape[1]

    def natural(ref, d, scratch):
        if d == 1:
            return ref[0].astype(F32)
        n_cg = scratch.shape[0]
        for r in range(d):
            for cg in range(n_cg):
                c0 = (r * n_cg + cg) * LANES
                scratch[cg, pl.ds(r, tm // d, stride=d), :] = ref[0, :, c0:c0 + LANES].astype(F32)
        return jnp.concatenate([scratch[cg] for cg in range(n_cg)], axis=-1)

    lses = [natural(ref, d, lperm_s) for ref, d in zip(l_refs, dilations)]
    mx = functools.reduce(jnp.maximum, lses)
    es = [jnp.exp(l - mx) for l in lses]
    inv = 1.0 / functools.reduce(jnp.add, es)
    ex = ex_ref[...]
    attn = None
    for e, ref, d in zip(es, o_refs, dilations):
        term = _dot((e * inv).astype(BF16), ex) * natural(ref, d, operm_s)
        attn = term if attn is None else attn + term
    y = _dot(attn.astype(BF16), woa_ref[...]) + _dot(hm_ref[0], wom_ref[...])
    x1 = x_ref[0] + mod_ref[0, 2:3, :] * y
    ms = jnp.mean(x1 * x1, axis=-1, keepdims=True)
    hn = x1 * lax.rsqrt(ms + RMS_EPS) * g_ref[...]
    h_s[...] = (hn * (1.0 + mod_ref[0, 4:5, :]) + mod_ref[0, 3:4, :]).astype(BF16)
    acc_s[...] = x1

    def ff_step(j, carry):
        c0 = pl.multiple_of(j * FFN_COLS, FFN_COLS)
        hb = h_s[...]
        g = _dot(hb, wg_ref[:, pl.ds(c0, FFN_COLS)])
        u = _dot(hb, wu_ref[:, pl.ds(c0, FFN_COLS)])
        a = (g * _sigmoid(g) * u).astype(BF16)
        acc_s[...] += mod_ref[0, 5:6, :] * _dot(a, wd_ref[pl.ds(c0, FFN_COLS), :])
        return carry

    lax.fori_loop(0, d_ff // FFN_COLS, ff_step, 0)
    out_ref[0] = acc_s[...]


def _outffn_call(x, mod, g_ffn, os_, lses, hm, expand, woa, wom, wg, wu, wd, *, dilations):
    b, s, d = x.shape
    tm = FFN_ROWS
    d_ff = wg.shape[1]
    assert d_ff % FFN_COLS == 0
    assert all(tm % (dl * BF16_SUBLANES) == 0 for dl in dilations)
    aw = woa.shape[0]
    const = lambda shape: pl.BlockSpec(shape, lambda bi, i: (0,) * len(shape))
    tok = lambda w: pl.BlockSpec((1, tm, w), lambda bi, i: (bi, i, 0))
    strided = lambda w: tuple(pl.BlockSpec((1, tm // dl, dl * w), lambda bi, i: (bi, i, 0)) for dl in dilations)
    return pl.pallas_call(
        functools.partial(_outffn_kernel, d_ff=d_ff, dilations=dilations),
        out_shape=jax.ShapeDtypeStruct((b, s, d), F32),
        grid=(b, s // tm),
        in_specs=[tok(d), pl.BlockSpec((1, N_MOD, d), lambda bi, i: (bi, 0, 0)), const((1, d)),
                  *strided(aw), *strided(LANES), tok(hm.shape[-1]),
                  const(expand.shape), const(woa.shape), const(wom.shape),
                  const(wg.shape), const(wu.shape), const(wd.shape)],
        out_specs=tok(d),
        scratch_shapes=[pltpu.VMEM((tm, d), BF16), pltpu.VMEM((tm, d), F32),
                        pltpu.VMEM((aw // LANES, tm, LANES), F32), pltpu.VMEM((1, tm, LANES), F32)],
        compiler_params=pltpu.CompilerParams(dimension_semantics=("arbitrary", "arbitrary"),
                                             vmem_limit_bytes=VMEM_LIMIT_BYTES),
        name="outffn",
    )(x, mod, g_ffn, *os_, *lses, hm, expand, woa, wom, wg, wu, wd)


def _block_diag_mean(width, group):
    idx = jnp.arange(width) // group
    return jnp.where(idx[:, None] == idx[None, :], 1.0 / group, 0.0).astype(BF16)


def kernel(x, c, g_mix, w_in, w_conv, b_conv, b_igate, b_fgate, q_norm_g, k_norm_g, mlstm_norm_g, w_out,
           g_ffn, w_gate, w_up, w_down, w_ada, b_ada):
    b, s, d = x.shape
    depth = g_mix.shape[0]
    attn_w = d // 2
    ml_w = d - attn_w
    nh = N_MLSTM_HEADS
    n_attn_heads = attn_w // ATTN_HEAD_DIM
    scale = ATTN_HEAD_DIM ** -0.5
    dilations = tuple(dl for _, dl in DILATED_PATTERNS)
    assert b <= MOD_ROWS and 2 * nh <= GATE_LANES

    c_pad = jnp.zeros((MOD_ROWS, d), F32).at[:b].set(c)
    hmean_attn = _block_diag_mean(attn_w, ATTN_HEAD_DIM)
    hmean_ml = _block_diag_mean(ml_w // nh, ml_w // nh)
    head_of_lane = jnp.arange(attn_w) // ATTN_HEAD_DIM
    expand = (jnp.arange(LANES)[:, None] == head_of_lane[None, :]).astype(BF16)
    tri = jnp.tril(jnp.ones((MLSTM_CHUNK, MLSTM_CHUNK), F32))

    for l in range(depth):
        mod = _mod_call(c_pad, w_ada[l], b_ada[l][None, :])[:b].reshape(b, N_MOD, d)

        n_main = 3 * attn_w + 4 * ml_w
        wa = w_in[l][:, :3 * attn_w].astype(BF16)
        wm = w_in[l][:, 3 * attn_w:n_main].astype(BF16)
        wg_cols = w_in[l][:, n_main:]
        wg = jnp.zeros((d, GATE_LANES), F32).at[:, :2 * nh].set(wg_cols).astype(BF16)
        wgt = jnp.zeros((BF16_SUBLANES, d), F32).at[:2 * nh].set(wg_cols.T).astype(BF16)
        gate_bias = jnp.concatenate([b_igate[l], b_fgate[l]])
        bcol = jnp.zeros((1, GATE_LANES), F32).at[0, :2 * nh].set(gate_bias)
        brow = gate_bias[:, None]
        gq = jnp.tile(q_norm_g[l], n_attn_heads)[None, :]
        gk = jnp.tile(k_norm_g[l], n_attn_heads)[None, :]

        qs, ks, vs, qkm, vm, og, gcol, grow = _inproj_call(
            x, mod, g_mix[l][None, :], wa, wm, wg, wgt, gq, gk, hmean_attn, bcol, brow,
            attn_w=attn_w, ml_w=ml_w, scale=scale, dilations=dilations)

        os_, lses = [], []
        for (window, dilation), q, k, v in zip(DILATED_PATTERNS, qs, ks, vs):
            o, lse = _attn_call(q, k, v, width=attn_w, window=window, dilation=dilation)
            os_.append(o)
            lses.append(lse)

        hm = _mlstm_call(qkm, vm, og, gcol, grow, w_conv[l], b_conv[l][None, :],
                         mlstm_norm_g[l][None, :], tri, hmean_ml)

        x = _outffn_call(x, mod, g_ffn[l][None, :], os_, lses, hm, expand,
                         w_out[l][:attn_w].astype(BF16), w_out[l][attn_w:].astype(BF16),
                         w_gate[l].astype(BF16), w_up[l].astype(BF16), w_down[l].astype(BF16),
                         dilations=dilations)
    return x
```

```python
import functools

import jax
import jax.numpy as jnp
from jax import lax
from jax.experimental import pallas as pl
from jax.experimental.pallas import tpu as pltpu

F32 = jnp.float32
BF16 = jnp.bfloat16

ATTN_HEAD_DIM = 64
N_MLSTM_HEADS = 4
CONV_WIDTH = 4
DILATED_PATTERNS = ((128, 1), (512, 4), (2048, 16))
ATTN_BLOCK = 128
N_MOD = 6
RMS_EPS = 1e-6

LANES = 128
BF16_SUBLANES = 16
VMEM_LIMIT_BYTES = 56 * 1024 * 1024
MOD_ROWS = 8
GATE_LANES = 128
INPROJ_ROWS = 512
ATTN_SUB_BLOCKS = 4
MLSTM_CHUNK = 128
FFN_ROWS = 512
FFN_COLS = 256

NEG_INF = float("-inf")
LOG2E = 1.4426950408889634
LN2 = 0.6931471805599453


def _dot(a, b, **kw):
    return jnp.dot(a, b, preferred_element_type=F32, **kw)


def _dot_nt(a, b, **kw):
    return lax.dot_general(a, b, (((1,), (1,)), ((), ())), preferred_element_type=F32, **kw)


def _dot_tn(a, b):
    return lax.dot_general(a, b, (((0,), (0,)), ((), ())), preferred_element_type=F32)


def _sigmoid(z):
    return 1.0 / (1.0 + jnp.exp(-z))


def _log_sigmoid(z):
    return jnp.minimum(z, 0.0) - jnp.log1p(jnp.exp(-jnp.abs(z)))


def _strided_shape(b, s, w, dilation):
    return (b, s // dilation, dilation * w)


def _mod_kernel(c_ref, w_ref, b_ref, o_ref):
    c = c_ref[...]
    sc = c * _sigmoid(c)
    o_ref[...] = _dot(sc, w_ref[...], precision=lax.Precision.HIGHEST) + b_ref[...]


def _mod_call(c_pad, w_ada, b_ada):
    d, n = w_ada.shape
    tn = n // 4
    return pl.pallas_call(
        _mod_kernel,
        out_shape=jax.ShapeDtypeStruct((MOD_ROWS, n), F32),
        grid=(n // tn,),
        in_specs=[pl.BlockSpec((MOD_ROWS, d), lambda j: (0, 0)),
                  pl.BlockSpec((d, tn), lambda j: (0, j)),
                  pl.BlockSpec((1, tn), lambda j: (0, j))],
        out_specs=pl.BlockSpec((MOD_ROWS, tn), lambda j: (0, j)),
        compiler_params=pltpu.CompilerParams(dimension_semantics=("arbitrary",),
                                             vmem_limit_bytes=VMEM_LIMIT_BYTES),
        name="mod",
    )(c_pad, w_ada, b_ada)


def _inproj_kernel(x_ref, mod_ref, g_ref, wa_ref, wm_ref, wg_ref, wgt_ref, gq_ref, gk_ref, hp_ref,
                   bcol_ref, brow_ref, *rest, attn_w, ml_w, scale, dilations):
    n_lay = len(dilations)
    q_refs, k_refs, v_refs = rest[:n_lay], rest[n_lay:2 * n_lay], rest[2 * n_lay:3 * n_lay]
    qkm_ref, vm_ref, og_ref, gcol_ref, grow_ref, perm_s = rest[3 * n_lay:]
    tm = x_ref.shape[1]

    def emit(val, refs):
        w = val.shape[-1]
        staged = False
        for d, ref in zip(dilations, refs):
            if d == 1:
                ref[0] = val.astype(BF16)
                continue
            if not staged:
                for cg in range(w // LANES):
                    perm_s[cg] = val[:, cg * LANES:(cg + 1) * LANES]
                staged = True
            for r in range(d):
                for cg in range(w // LANES):
                    c0 = r * w + cg * LANES
                    ref[0, :, c0:c0 + LANES] = perm_s[cg, pl.ds(r, tm // d, stride=d), :].astype(BF16)

    x = x_ref[0]
    ms = jnp.mean(x * x, axis=-1, keepdims=True)
    y = x * lax.rsqrt(ms + RMS_EPS) * g_ref[...]
    h = (y * (1.0 + mod_ref[0, 1:2, :]) + mod_ref[0, 0:1, :]).astype(BF16)

    xa = _dot(h, wa_ref[...])
    hp = hp_ref[...]

    def head_norm(t, g):
        msq = _dot((t * t).astype(BF16), hp)
        return t * lax.rsqrt(msq + RMS_EPS) * g

    emit(head_norm(xa[:, :attn_w], gq_ref[...]) * scale, q_refs)
    emit(head_norm(xa[:, attn_w:2 * attn_w], gk_ref[...]), k_refs)
    emit(xa[:, 2 * attn_w:], v_refs)

    xm = _dot(h, wm_ref[...])
    qkm_ref[0] = xm[:, :2 * ml_w].astype(BF16)
    vm_ref[0] = xm[:, 2 * ml_w:3 * ml_w].astype(BF16)
    og_ref[0] = _sigmoid(xm[:, 3 * ml_w:]).astype(BF16)

    nh = N_MLSTM_HEADS
    zc = _dot(h, wg_ref[...]) + bcol_ref[...]
    lane = lax.broadcasted_iota(jnp.int32, zc.shape, 1)
    gcol_ref[0] = jnp.where(lane < nh, zc, jnp.where(lane < 2 * nh, _log_sigmoid(zc), 0.0))
    zr = _dot_nt(wgt_ref[...], h)[:2 * nh] + brow_ref[...]
    row = lax.broadcasted_iota(jnp.int32, zr.shape, 0)
    grow_ref[0] = jnp.where(row < nh, zr, _log_sigmoid(zr))


def _inproj_call(x, mod, g_mix, wa, wm, wg, wgt, gq, gk, hp, bcol, brow, *, attn_w, ml_w, scale, dilations):
    b, s, d = x.shape
    tm = INPROJ_ROWS
    nh2 = 2 * N_MLSTM_HEADS
    assert all(tm % (dl * BF16_SUBLANES) == 0 for dl in dilations)
    const = lambda shape: pl.BlockSpec(shape, lambda bi, i: (0,) * len(shape))
    tok = lambda w: pl.BlockSpec((1, tm, w), lambda bi, i: (bi, i, 0))
    strided_shapes = tuple(jax.ShapeDtypeStruct(_strided_shape(b, s, attn_w, dl), BF16) for dl in dilations)
    strided_specs = tuple(pl.BlockSpec((1, tm // dl, dl * attn_w), lambda bi, i: (bi, i, 0)) for dl in dilations)
    out_shape = strided_shapes * 3 + (
        jax.ShapeDtypeStruct((b, s, 2 * ml_w), BF16),
        jax.ShapeDtypeStruct((b, s, ml_w), BF16),
        jax.ShapeDtypeStruct((b, s, ml_w), BF16),
        jax.ShapeDtypeStruct((b, s, GATE_LANES), F32),
        jax.ShapeDtypeStruct((b, nh2, s), F32))
    outs = pl.pallas_call(
        functools.partial(_inproj_kernel, attn_w=attn_w, ml_w=ml_w, scale=scale, dilations=dilations),
        out_shape=out_shape,
        grid=(b, s // tm),
        in_specs=[tok(d),
                  pl.BlockSpec((1, N_MOD, d), lambda bi, i: (bi, 0, 0)),
                  const((1, d)), const(wa.shape), const(wm.shape), const(wg.shape), const(wgt.shape),
                  const((1, attn_w)), const((1, attn_w)), const(hp.shape),
                  const((1, GATE_LANES)), const((nh2, 1))],
        out_specs=strided_specs * 3 + (tok(2 * ml_w), tok(ml_w), tok(ml_w), tok(GATE_LANES),
                                       pl.BlockSpec((1, nh2, tm), lambda bi, i: (bi, 0, i))),
        scratch_shapes=[pltpu.VMEM((attn_w // LANES, tm, LANES), F32)],
        compiler_params=pltpu.CompilerParams(dimension_semantics=("arbitrary", "arbitrary"),
                                             vmem_limit_bytes=VMEM_LIMIT_BYTES),
        name="inproj",
    )(x, mod, g_mix, wa, wm, wg, wgt, gq, gk, hp, bcol, brow)
    n_lay = len(dilations)
    return (outs[:n_lay], outs[n_lay:2 * n_lay], outs[2 * n_lay:3 * n_lay]) + tuple(outs[3 * n_lay:])


def _attn_kernel(q_ref, kp_ref, kc_ref, vp_ref, vc_ref, o_ref, lse_ref, k_s, v_s, bias_s, s_s, p_s, m_s,
                 *, n_back, n_heads):
    blk = ATTN_BLOCK
    hd = ATTN_HEAD_DIM
    n_pairs = n_heads // 2
    n = pl.program_id(2)
    lane = lax.broadcasted_iota(jnp.int32, (blk, LANES), 1)
    first_head = lane < hd
    k_s[0:blk, :] = kp_ref[0]
    k_s[blk:, :] = kc_ref[0]
    ind_a = jnp.where(first_head, 1.0, 0.0).astype(BF16)
    ind_b = jnp.where(first_head, 0.0, 1.0).astype(BF16)
    for kb in range(ATTN_SUB_BLOCKS + 1):
        for j in range(n_pairs):
            cols = slice(j * LANES, (j + 1) * LANES)
            src = vp_ref[0, :, cols] if kb == 0 else vc_ref[0, (kb - 1) * blk:kb * blk, cols]
            ra, rb, c0 = 2 * kb * blk, (2 * kb + 1) * blk, 2 * j * LANES
            v_s[ra:ra + blk, c0:c0 + LANES] = jnp.where(first_head, src, 0)
            v_s[ra:ra + blk, c0 + LANES:c0 + 2 * LANES] = ind_a
            v_s[rb:rb + blk, c0:c0 + LANES] = jnp.where(first_head, 0, src)
            v_s[rb:rb + blk, c0 + LANES:c0 + 2 * LANES] = ind_b

    row = lax.broadcasted_iota(jnp.int32, (2 * blk, 2 * blk), 0) & (blk - 1)
    col = lax.broadcasted_iota(jnp.int32, (2 * blk, 2 * blk), 1)
    band = jnp.logical_and(col >= row + (blk - n_back), col <= row + blk)
    bias_s[0] = jnp.where(band, 0.0, NEG_INF)
    bias_s[1] = jnp.where(jnp.logical_and(band, col >= blk), 0.0, NEG_INF)

    units = [(i, j) for i in range(ATTN_SUB_BLOCKS) for j in range(n_pairs)]
    for u, (i, j) in enumerate(units):
        cols = slice(j * LANES, (j + 1) * LANES)
        q = q_ref[0, i * blk:(i + 1) * blk, cols]
        ks = k_s[i * blk:(i + 2) * blk, cols]
        bias = bias_s[jnp.where(n == 0, 1, 0)] if i == 0 else bias_s[0]
        q2 = jnp.concatenate([jnp.where(first_head, q, 0), jnp.where(first_head, 0, q)], axis=0)
        s_s[u] = _dot_nt(q2, ks) + bias
    for u, (i, j) in enumerate(units):
        s = s_s[u]
        m = jnp.max(s, axis=-1, keepdims=True)
        p = jnp.exp2(s - m).astype(BF16)
        p_s[u] = jnp.concatenate([p[:blk, :blk], p[blk:, :blk], p[:blk, blk:], p[blk:, blk:]], axis=1)
        m_s[u] = jnp.broadcast_to(m, (2 * blk, LANES))
    for i in range(ATTN_SUB_BLOCKS):
        outs = []
        stats = jnp.zeros((blk, LANES), F32)
        for j in range(n_pairs):
            u = i * n_pairs + j
            vw = v_s[2 * i * blk:2 * (i + 2) * blk, 2 * j * LANES:(2 * j + 2) * LANES]
            ov = _dot(p_s[u], vw)
            outs.append(ov[:, :LANES] / ov[:, LANES:])
            lse = (jnp.where(first_head, m_s[u, :blk], m_s[u, blk:]) + jnp.log2(ov[:, LANES:])) * LN2
            stats = jnp.where((lane & (hd - 1)) == j, lse, stats)
        o_ref[0, i * blk:(i + 1) * blk, :] = jnp.concatenate(outs, axis=-1).astype(o_ref.dtype)
        lse_ref[0, i * blk:(i + 1) * blk, :] = stats


def _attn_call(q, k, v, *, width, window, dilation):
    b, ls, _ = q.shape
    w = width
    n_back = window // dilation
    blk = ATTN_BLOCK
    assert n_back <= blk
    step_rows = ATTN_SUB_BLOCKS * blk
    assert ls % step_rows == 0
    n_heads = w // ATTN_HEAD_DIM
    assert n_heads <= LANES and n_heads % 2 == 0 and 2 * ATTN_HEAD_DIM == LANES
    n_units = ATTN_SUB_BLOCKS * (n_heads // 2)
    cur =pl.BlockSpec((1, step_rows, w), lambda bi, r, n: (bi, n, r))
    prev = pl.BlockSpec((1, blk, w),
                        lambda bi, r, n: (bi, jnp.maximum(n * ATTN_SUB_BLOCKS - 1, 0), r))
    return pl.pallas_call(
        functools.partial(_attn_kernel, n_back=n_back, n_heads=n_heads),
        out_shape=(jax.ShapeDtypeStruct((b, ls, dilation * w), BF16),
                   jax.ShapeDtypeStruct((b, ls, dilation * LANES), F32)),
        grid=(b, dilation, ls // step_rows),
        in_specs=[cur, prev, cur, prev, cur],
        out_specs=(cur, pl.BlockSpec((1, step_rows, LANES), lambda bi, r, n: (bi, n, r))),
        scratch_shapes=[pltpu.VMEM((blk + step_rows, w), BF16),
                        pltpu.VMEM((2 * (blk + step_rows), 2 * w), BF16),
                        pltpu.VMEM((2, 2 * blk, 2 * blk), F32),
                        pltpu.VMEM((n_units, 2 * blk, 2 * blk), F32),
                        pltpu.VMEM((n_units, blk, 4 * blk), BF16),
                        pltpu.VMEM((n_units, 2 * blk, LANES), F32)],
        compiler_params=pltpu.CompilerParams(
            dimension_semantics=("arbitrary", "arbitrary", "arbitrary"),
            vmem_limit_bytes=VMEM_LIMIT_BYTES),
        name=f"attn_d{dilation}",
    )(q, k, k, v, v)


def _mlstm_kernel(qk_ref, v_ref, og_ref, gcol_ref, grow_ref, wc_ref, bc_ref, gn_ref, tri_ref, hm_ref,
                  o_ref, tail_s, c_s, m_s, *, ml_w):
    L = MLSTM_CHUNK
    nh = N_MLSTM_HEADS
    dh = ml_w // nh
    ci = pl.program_id(1)

    @pl.when(ci == 0)
    def _():
        tail_s[...] = jnp.zeros_like(tail_s)
        c_s[...] = jnp.zeros_like(c_s)
        m_s[...] = jnp.zeros_like(m_s)

    x = qk_ref[0].astype(F32)
    tail = tail_s[...]
    rowi = lax.broadcasted_iota(jnp.int32, x.shape, 0)
    acc = bc_ref[...] + x * wc_ref[CONV_WIDTH - 1:CONV_WIDTH, :]
    for back in range(1, CONV_WIDTH):
        shifted = jnp.where(rowi >= back, pltpu.roll(x, back, 0), pltpu.roll(tail, back, 0))
        acc = acc + shifted * wc_ref[CONV_WIDTH - 1 - back:CONV_WIDTH - back, :]
    tail_s[...] = x
    qk = acc * _sigmoid(acc)

    gcol = gcol_ref[0]
    grow = grow_ref[0]
    tri = tri_ref[...]
    bcol_all = _dot(tri, gcol, precision=lax.Precision.HIGHEST)
    brow_all = _dot_nt(grow, tri, precision=lax.Precision.HIGHEST)
    ti = lax.broadcasted_iota(jnp.int32, (L, L), 0)
    si = lax.broadcasted_iota(jnp.int32, (L, L), 1)
    causal = si <= ti
    v_all = v_ref[0]
    ones = jnp.ones((L, dh), BF16)
    hp = hm_ref[...]

    outs = []
    for h in range(nh):
        q = qk[:, h * dh:(h + 1) * dh].astype(BF16)
        k = (qk[:, ml_w + h * dh:ml_w + (h + 1) * dh] * (dh ** -0.5))
        v_ext = jnp.concatenate([v_all[:, h * dh:(h + 1) * dh], ones], axis=-1)
        b_c = jnp.broadcast_to(bcol_all[:, nh + h:nh + h + 1], (L, dh))
        i_c = jnp.broadcast_to(gcol[:, h:h + 1], (L, dh))
        b_r = brow_all[nh + h:nh + h + 1, :]
        i_r = grow[h:h + 1, :]
        m_prev = m_s[h, 0:1, :]
        c_prev = c_s[h]

        log_d = jnp.where(causal, b_c[:, 0:1] - b_r + i_r, NEG_INF)
        m_inter = b_c + m_prev
        m_t = jnp.maximum(m_inter, jnp.max(log_d, axis=-1, keepdims=True))
        d_mat = jnp.exp(log_d - m_t[:, 0:1])
        inter = jnp.exp(m_inter - m_t)
        s_qk = _dot_nt(q, k.astype(BF16)) * d_mat
        ext = jnp.concatenate([inter, inter], axis=-1) * _dot(q, c_prev.astype(BF16)) \
            + _dot(s_qk.astype(BF16), v_ext)
        num = ext[:, :dh]
        nq = ext[:, dh:]
        hh = num / jnp.maximum(jnp.abs(nq), jnp.exp(-m_t))
        msq = _dot((hh * hh).astype(BF16), hp)
        outs.append(hh * lax.rsqrt(msq + RMS_EPS))

        b_last = b_c[L - 1:L, :]
        w_log = b_last - b_c + i_c
        m_new = jnp.maximum(b_last + m_prev, jnp.max(w_log, axis=0, keepdims=True))
        wgt = jnp.exp(w_log - m_new)
        decay = jnp.exp(b_last + m_prev - m_new)
        kw = (k * wgt).astype(BF16)
        c_s[h] = jnp.concatenate([decay, decay], axis=-1) * c_prev + _dot_tn(kw, v_ext)
        m_s[h] = jnp.broadcast_to(m_new, m_s.shape[1:])

    hn = jnp.concatenate(outs, axis=-1) * gn_ref[...]
    o_ref[0] = (og_ref[0].astype(F32) * hn).astype(o_ref.dtype)


def _mlstm_call(qkm, vm, og, gcol, grow, w_conv, b_conv, g_norm, tri, hmean):
    b, s, ml_w = vm.shape
    L = MLSTM_CHUNK
    nh = N_MLSTM_HEADS
    dh = ml_w // nh
    const = lambda shape: pl.BlockSpec(shape, lambda bi, i: (0,) * len(shape))
    tok = lambda w: pl.BlockSpec((1, L, w), lambda bi, i: (bi, i, 0))
    return pl.pallas_call(
        functools.partial(_mlstm_kernel, ml_w=ml_w),
        out_shape=jax.ShapeDtypeStruct((b, s, ml_w), BF16),
        grid=(b, s // L),
        in_specs=[tok(2 * ml_w), tok(ml_w), tok(ml_w), tok(GATE_LANES),
                  pl.BlockSpec((1, 2 * nh, L), lambda bi, i: (bi, 0, i)),
                  const(w_conv.shape), const((1, 2 * ml_w)), const((1, ml_w)),
                  const((L, L)), const(hmean.shape)],
        out_specs=tok(ml_w),
        scratch_shapes=[pltpu.VMEM((L, 2 * ml_w), F32),
                        pltpu.VMEM((nh, dh, 2 * dh), F32),
                        pltpu.VMEM((nh, MOD_ROWS, dh), F32)],
        compiler_params=pltpu.CompilerParams(dimension_semantics=("arbitrary", "arbitrary"),
                                             vmem_limit_bytes=VMEM_LIMIT_BYTES),
        name="mlstm",
    )(qkm, vm, og, gcol, grow, w_conv, b_conv, g_norm, tri, hmean)


def _outffn_kernel(x_ref, mod_ref, g_ref, *rest, d_ff, dilations):
    n_lay = len(dilations)
    o_refs, l_refs = rest[:n_lay], rest[n_lay:2 * n_lay]
    (hm_ref, ex_ref, woa_ref, wom_ref, wg_ref, wu_ref, wd_ref, out_ref,
     h_s, acc_s, operm_s, lperm_s) = rest[2 * n_lay:]
    tm = x_ref.shape[1]

    def natural(ref, d, scratch):
        if d == 1:
            return ref[0].astype(F32)
        n_cg = scratch.shape[0]
        for r in range(d):
            for cg in range(n_cg):
                c0 = (r * n_cg + cg) * LANES
                scratch[cg, pl.ds(r, tm // d, stride=d), :] = ref[0, :, c0:c0 + LANES].astype(F32)
        return jnp.concatenate([scratch[cg] for cg in range(n_cg)], axis=-1)

    lses = [natural(ref, d, lperm_s) for ref, d in zip(l_refs, dilations)]
    mx = functools.reduce(jnp.maximum, lses)
    es = [jnp.exp(l - mx) for l in lses]
    inv = 1.0 / functools.reduce(jnp.add, es)
    ex = ex_ref[...]
    attn = None
    for e, ref, d in zip(es, o_refs, dilations):
        term = _dot((e * inv).astype(BF16), ex) * natural(ref, d, operm_s)
        attn = term if attn is None else attn + term
    y = _dot(attn.astype(BF16), woa_ref[...]) + _dot(hm_ref[0], wom_ref[...])
    x1 = x_ref[0] + mod_ref[0, 2:3, :] * y
    ms = jnp.mean(x1 * x1, axis=-1, keepdims=True)
    hn = x1 * lax.rsqrt(ms + RMS_EPS) * g_ref[...]
    h_s[...] = (hn * (1.0 + mod_ref[0, 4:5, :]) + mod_ref[0, 3:4, :]).astype(BF16)
    acc_s[...] = x1

    def ff_step(j, carry):
        c0 = pl.multiple_of(j * FFN_COLS, FFN_COLS)
        hb = h_s[...]
        g = _dot(hb, wg_ref[:, pl.ds(c0, FFN_COLS)])
        u = _dot(hb, wu_ref[:, pl.ds(c0, FFN_COLS)])
        a = (g * _sigmoid(g) * u).astype(BF16)
        acc_s[...] += mod_ref[0, 5:6, :] * _dot(a, wd_ref[pl.ds(c0, FFN_COLS), :])
        return carry

    lax.fori_loop(0, d_ff // FFN_COLS, ff_step, 0)
    out_ref[0] = acc_s[...]


def _outffn_call(x, mod, g_ffn, os_, lses, hm, expand, woa, wom, wg, wu, wd, *, dilations):
    b, s, d = x.shape
    tm = FFN_ROWS
    d_ff = wg.shape[1]
    assert d_ff % FFN_COLS == 0
    assert all(tm % (dl * BF16_SUBLANES) == 0 for dl in dilations)
    aw = woa.shape[0]
    const = lambda shape: pl.BlockSpec(shape, lambda bi, i: (0,) * len(shape))
    tok = lambda w: pl.BlockSpec((1, tm, w), lambda bi, i: (bi, i, 0))
    strided = lambda w: tuple(pl.BlockSpec((1, tm // dl, dl * w), lambda bi, i: (bi, i, 0)) for dl in dilations)
    return pl.pallas_call(
        functools.partial(_outffn_kernel, d_ff=d_ff, dilations=dilations),
        out_shape=jax.ShapeDtypeStruct((b, s, d), F32),
        grid=(b, s // tm),
        in_specs=[tok(d), pl.BlockSpec((1, N_MOD, d), lambda bi, i: (bi, 0, 0)), const((1, d)),
                  *strided(aw), *strided(LANES), tok(hm.shape[-1]),
                  const(expand.shape), const(woa.shape), const(wom.shape),
                  const(wg.shape), const(wu.shape), const(wd.shape)],
        out_specs=tok(d),
        scratch_shapes=[pltpu.VMEM((tm, d), BF16), pltpu.VMEM((tm, d), F32),
                        pltpu.VMEM((aw // LANES, tm, LANES), F32), pltpu.VMEM((1, tm, LANES), F32)],
        compiler_params=pltpu.CompilerParams(dimension_semantics=("arbitrary", "arbitrary"),
                                             vmem_limit_bytes=VMEM_LIMIT_BYTES),
        name="outffn",
    )(x, mod, g_ffn, *os_, *lses, hm, expand, woa, wom, wg, wu, wd)


def _block_diag_mean(width, group):
    idx = jnp.arange(width) // group
    return jnp.where(idx[:, None] == idx[None, :], 1.0 / group, 0.0).astype(BF16)


def kernel(x, c, g_mix, w_in, w_conv, b_conv, b_igate, b_fgate, q_norm_g, k_norm_g, mlstm_norm_g, w_out,
           g_ffn, w_gate, w_up, w_down, w_ada, b_ada):
    b, s, d = x.shape
    depth = g_mix.shape[0]
    attn_w = d // 2
    ml_w = d - attn_w
    nh = N_MLSTM_HEADS
    n_attn_heads = attn_w // ATTN_HEAD_DIM
    scale = ATTN_HEAD_DIM ** -0.5 * LOG2E
    dilations = tuple(dl for _, dl in DILATED_PATTERNS)
    assert b <= MOD_ROWS and 2 * nh <= GATE_LANES

    c_pad = jnp.zeros((MOD_ROWS, d), F32).at[:b].set(c)
    hmean_attn = _block_diag_mean(attn_w, ATTN_HEAD_DIM)
    hmean_ml = _block_diag_mean(ml_w // nh, ml_w // nh)
    stat_lane = jnp.arange(LANES)
    head_of_stat = jnp.where(stat_lane % ATTN_HEAD_DIM < n_attn_heads // 2,
                             2 * (stat_lane % ATTN_HEAD_DIM) + stat_lane // ATTN_HEAD_DIM, -1)
    head_of_col = jnp.arange(attn_w) // ATTN_HEAD_DIM
    expand = (head_of_stat[:, None] == head_of_col[None, :]).astype(BF16)
    tri = jnp.tril(jnp.ones((MLSTM_CHUNK, MLSTM_CHUNK), F32))

    for l in range(depth):
        mod = _mod_call(c_pad, w_ada[l], b_ada[l][None, :])[:b].reshape(b, N_MOD, d)

        n_main = 3 * attn_w + 4 * ml_w
        wa = w_in[l][:, :3 * attn_w].astype(BF16)
        wm = w_in[l][:, 3 * attn_w:n_main].astype(BF16)
        wg_cols = w_in[l][:, n_main:]
        wg = jnp.zeros((d, GATE_LANES), F32).at[:, :2 * nh].set(wg_cols).astype(BF16)
        wgt = jnp.zeros((BF16_SUBLANES, d), F32).at[:2 * nh].set(wg_cols.T).astype(BF16)
        gate_bias = jnp.concatenate([b_igate[l], b_fgate[l]])
        bcol = jnp.zeros((1, GATE_LANES), F32).at[0, :2 * nh].set(gate_bias)
        brow = gate_bias[:, None]
        gq = jnp.tile(q_norm_g[l], n_attn_heads)[None, :]
        gk = jnp.tile(k_norm_g[l], n_attn_heads)[None, :]

        qs, ks, vs, qkm, vm, og, gcol, grow = _inproj_call(
            x, mod, g_mix[l][None, :], wa, wm, wg, wgt, gq, gk, hmean_attn, bcol, brow,
            attn_w=attn_w, ml_w=ml_w, scale=scale, dilations=dilations)

        os_, lses = [], []
        for (window, dilation), q, k, v in zip(DILATED_PATTERNS, qs, ks, vs):
            o, lse = _attn_call(q, k, v, width=attn_w, window=window, dilation=dilation)
            os_.append(o)
            lses.append(lse)

        hm = _mlstm_call(qkm, vm, og, gcol, grow, w_conv[l], b_conv[l][None, :],
                         mlstm_norm_g[l][None, :], tri, hmean_ml)

        x = _outffn_call(x, mod, g_ffn[l][None, :], os_, lses, hm, expand,
                         w_out[l][:attn_w].astype(BF16), w_out[l][attn_w:].astype(BF16),
                         w_gate[l].astype(BF16), w_up[l].astype(BF16), w_down[l].astype(BF16),
                         dilations=dilations)
    return x
```

```python
import functools

import jax
import jax.numpy as jnp
from jax import lax
from jax.experimental import pallas as pl
from jax.experimental.pallas import tpu as pltpu

F32 = jnp.float32
BF16 = jnp.bfloat16

ATTN_HEAD_DIM = 64
N_MLSTM_HEADS = 4
CONV_WIDTH = 4
DILATED_PATTERNS = ((128, 1), (512, 4), (2048, 16))
ATTN_BLOCK = 128
N_MOD = 6
RMS_EPS = 1e-6

LANES = 128
BF16_SUBLANES = 16
VMEM_LIMIT_BYTES = 56 * 1024 * 1024
MOD_ROWS = 8
GATE_LANES = 128
INPROJ_ROWS = 512
ATTN_SUB_BLOCKS = 4
MLSTM_CHUNK = 128
MLSTM_BATCH_ROWS = 1
FFN_ROWS = 512

NEG_INF = float("-inf")
LOG2E = 1.4426950408889634
LN2 = 0.6931471805599453


def _dot(a, b, **kw):
    return jnp.dot(a, b, preferred_element_type=F32, **kw)


def _dot_nt(a, b, **kw):
    return lax.dot_general(a, b, (((1,), (1,)), ((), ())), preferred_element_type=F32, **kw)


def _dot_tn(a, b):
    return lax.dot_general(a, b, (((0,), (0,)), ((), ())), preferred_element_type=F32)


def _sigmoid(z):
    return 1.0 / (1.0 + jnp.exp(-z))


def _log_sigmoid(z):
    return jnp.minimum(z, 0.0) - jnp.log1p(jnp.exp(-jnp.abs(z)))


def _strided_shape(b, s, w, dilation):
    return (b, s // dilation, dilation * w)


def _mod_kernel(c_ref, w_ref, b_ref, o_ref):
    c = c_ref[...]
    sc = c * _sigmoid(c)
    o_ref[...] = _dot(sc, w_ref[...], precision=lax.Precision.HIGHEST) + b_ref[...]


def _mod_call(c_pad, w_ada, b_ada):
    d, n = w_ada.shape
    tn = n // 4
    return pl.pallas_call(
        _mod_kernel,
        out_shape=jax.ShapeDtypeStruct((MOD_ROWS, n), F32),
        grid=(n // tn,),
        in_specs=[pl.BlockSpec((MOD_ROWS, d), lambda j: (0, 0)),
                  pl.BlockSpec((d, tn), lambda j: (0, j)),
                  pl.BlockSpec((1, tn), lambda j: (0, j))],
        out_specs=pl.BlockSpec((MOD_ROWS, tn), lambda j: (0, j)),
        compiler_params=pltpu.CompilerParams(dimension_semantics=("arbitrary",),
                                             vmem_limit_bytes=VMEM_LIMIT_BYTES),
        name="mod",
    )(c_pad, w_ada, b_ada)


def _inproj_kernel(x_ref, mod_ref, g_ref, wa_ref, wm_ref, wg_ref, wgt_ref, gq_ref, gk_ref, hp_ref,
                   bcol_ref, brow_ref, *rest, attn_w, ml_w, scale, dilations):
    n_lay = len(dilations)
    q_refs, k_refs, v_refs = rest[:n_lay], rest[n_lay:2 * n_lay], rest[2 * n_lay:3 * n_lay]
    qkm_ref, vm_ref, og_ref, gcol_ref, grow_ref, perm_s = rest[3 * n_lay:]
    tm = x_ref.shape[1]

    def emit(val, refs):
        w = val.shape[-1]
        staged = False
        for d, ref in zip(dilations, refs):
            if d == 1:
                ref[0] = val.astype(BF16)
                continue
            if not staged:
                for cg in range(w // LANES):
                    perm_s[cg] = val[:, cg * LANES:(cg + 1) * LANES]
                staged = True
            for r in range(d):
                for cg in range(w // LANES):
                    c0 = r * w + cg * LANES
                    ref[0, :, c0:c0 + LANES] = perm_s[cg, pl.ds(r, tm // d, stride=d), :].astype(BF16)

    x = x_ref[0]
    ms = jnp.mean(x * x, axis=-1, keepdims=True)
    y = x * lax.rsqrt(ms + RMS_EPS) * g_ref[...]
    h = (y * (1.0 + mod_ref[0, 1:2, :]) + mod_ref[0, 0:1, :]).astype(BF16)

    xa = _dot(h, wa_ref[...])
    hp = hp_ref[...]

    def head_norm(t, g):
        msq = _dot((t * t).astype(BF16), hp)
        return t * lax.rsqrt(msq + RMS_EPS) * g

    emit(head_norm(xa[:, :attn_w], gq_ref[...]) * scale, q_refs)
    emit(head_norm(xa[:, attn_w:2 * attn_w], gk_ref[...]), k_refs)
    emit(xa[:, 2 * attn_w:], v_refs)

    xm = _dot(h, wm_ref[...])
    qkm_ref[0] = xm[:, :2 * ml_w].astype(BF16)
    vm_ref[0] = xm[:, 2 * ml_w:3 * ml_w].astype(BF16)
    og_ref[0] = _sigmoid(xm[:, 3 * ml_w:]).astype(BF16)

    nh = N_MLSTM_HEADS
    zc = _dot(h, wg_ref[...]) + bcol_ref[...]
    lane = lax.broadcasted_iota(jnp.int32, zc.shape, 1)
    gcol_ref[0] = jnp.where(lane < nh, zc, jnp.where(lane < 2 * nh, _log_sigmoid(zc), 0.0))
    zr = _dot_nt(wgt_ref[...], h)[:2 * nh] + brow_ref[...]
    row = lax.broadcasted_iota(jnp.int32, zr.shape, 0)
    grow_ref[0] = jnp.where(row < nh, zr, _log_sigmoid(zr))


def _inproj_call(x, mod, g_mix, wa, wm, wg, wgt, gq, gk, hp, bcol, brow, *, attn_w, ml_w, scale, dilations):
    b, s, d = x.shape
    tm = INPROJ_ROWS
    nh2 = 2 * N_MLSTM_HEADS
    assert all(tm % (dl * BF16_SUBLANES) == 0 for dl in dilations)
    const = lambda shape: pl.BlockSpec(shape, lambda bi, i: (0,) * len(shape))
    tok = lambda w: pl.BlockSpec((1, tm, w), lambda bi, i: (bi, i, 0))
    strided_shapes = tuple(jax.ShapeDtypeStruct(_strided_shape(b, s, attn_w, dl), BF16) for dl in dilations)
    strided_specs = tuple(pl.BlockSpec((1, tm // dl, dl * attn_w), lambda bi, i: (bi, i, 0)) for dl in dilations)
    out_shape = strided_shapes * 3 + (
        jax.ShapeDtypeStruct((b, s, 2 * ml_w), BF16),
        jax.ShapeDtypeStruct((b, s, ml_w), BF16),
        jax.ShapeDtypeStruct((b, s, ml_w), BF16),
        jax.ShapeDtypeStruct((b, s, GATE_LANES), F32),
        jax.ShapeDtypeStruct((b, nh2, s), F32))
    outs = pl.pallas_call(
        functools.partial(_inproj_kernel, attn_w=attn_w, ml_w=ml_w, scale=scale, dilations=dilations),
        out_shape=out_shape,
        grid=(b, s // tm),
        in_specs=[tok(d),
                  pl.BlockSpec((1, N_MOD, d), lambda bi, i: (bi, 0, 0)),
                  const((1, d)), const(wa.shape), const(wm.shape), const(wg.shape), const(wgt.shape),
                  const((1, attn_w)), const((1, attn_w)), const(hp.shape),
                  const((1, GATE_LANES)), const((nh2, 1))],
        out_specs=strided_specs * 3 + (tok(2 * ml_w), tok(ml_w), tok(ml_w), tok(GATE_LANES),
                                       pl.BlockSpec((1, nh2, tm), lambda bi, i: (bi, 0, i))),
        scratch_shapes=[pltpu.VMEM((attn_w // LANES, tm, LANES), F32)],
        compiler_params=pltpu.CompilerParams(dimension_semantics=("arbitrary", "arbitrary"),
                                             vmem_limit_bytes=VMEM_LIMIT_BYTES),
        name="inproj",
    )(x, mod, g_mix, wa, wm, wg, wgt, gq, gk, hp, bcol, brow)
    n_lay = len(dilations)
    return (outs[:n_lay], outs[n_lay:2 * n_lay], outs[2 * n_lay:3 * n_lay]) + tuple(outs[3 * n_lay:])


def _attn_kernel(q_ref, kp_ref, kc_ref, vp_ref, vc_ref, o_ref, lse_ref, k_s, v_s, bias_s, s_s, p_s, m_s,
                 *, n_back, n_heads):
    blk = ATTN_BLOCK
    hd = ATTN_HEAD_DIM
    n_pairs = n_heads // 2
    n = pl.program_id(2)
    lane = lax.broadcasted_iota(jnp.int32, (blk, LANES), 1)
    first_head = lane < hd
    k_s[0:blk, :] = kp_ref[0]
    k_s[blk:, :] = kc_ref[0]
    ind_a = jnp.where(first_head, 1.0, 0.0).astype(BF16)
    ind_b = jnp.where(first_head, 0.0, 1.0).astype(BF16)
    for kb in range(ATTN_SUB_BLOCKS + 1):
        for j in range(n_pairs):
            cols = slice(j * LANES, (j + 1) * LANES)
            src = vp_ref[0, :, cols] if kb == 0 else vc_ref[0, (kb - 1) * blk:kb * blk, cols]
            ra, rb, c0 = 2 * kb * blk, (2 * kb + 1) * blk, 2 * j * LANES
            v_s[ra:ra + blk, c0:c0 + LANES] = jnp.where(first_head, src, 0)
            v_s[ra:ra + blk, c0 + LANES:c0 + 2 * LANES] = ind_a
            v_s[rb:rb + blk, c0:c0 + LANES] = jnp.where(first_head, 0, src)
            v_s[rb:rb + blk, c0 + LANES:c0 + 2 * LANES] = ind_b

    row = lax.broadcasted_iota(jnp.int32, (2 * blk, 2 * blk), 0) & (blk - 1)
    col = lax.broadcasted_iota(jnp.int32, (2 * blk, 2 * blk), 1)
    band = jnp.logical_and(col >= row + (blk - n_back), col <= row + blk)
    bias_s[0] = jnp.where(band, 0.0, NEG_INF)
    bias_s[1] = jnp.where(jnp.logical_and(band, col >= blk), 0.0, NEG_INF)

    units = [(i, j) for i in range(ATTN_SUB_BLOCKS) for j in range(n_pairs)]
    for u, (i, j) in enumerate(units):
        cols = slice(j * LANES, (j + 1) * LANES)
        q = q_ref[0, i * blk:(i + 1) * blk, cols]
        ks = k_s[i * blk:(i + 2) * blk, cols]
        bias = bias_s[jnp.where(n == 0, 1, 0)] if i == 0 else bias_s[0]
        q2 = jnp.concatenate([jnp.where(first_head, q, 0), jnp.where(first_head, 0, q)], axis=0)
        s_s[u] = _dot_nt(q2, ks) + bias
    for u, (i, j) in enumerate(units):
        s = s_s[u]
        m = jnp.max(s, axis=-1, keepdims=True)
        p = jnp.exp2(s - m).astype(BF16)
        p_s[u] = jnp.concatenate([p[:blk, :blk], p[blk:, :blk], p[:blk, blk:], p[blk:, blk:]], axis=1)
        m_s[u] = jnp.broadcast_to(m, (2 * blk, LANES))
    for i in range(ATTN_SUB_BLOCKS):
        outs = []
        stats = jnp.zeros((blk, LANES), F32)
        for j in range(n_pairs):
            u = i * n_pairs + j
            vw = v_s[2 * i * blk:2 * (i + 2) * blk, 2 * j * LANES:(2 * j + 2) * LANES]
            ov = _dot(p_s[u], vw)
            outs.append(ov[:, :LANES] / ov[:, LANES:])
            lse = (jnp.where(first_head, m_s[u, :blk], m_s[u, blk:]) + jnp.log2(ov[:, LANES:])) * LN2
            stats = jnp.where((lane & (hd - 1)) == j, lse, stats)
        o_ref[0, i * blk:(i + 1) * blk, :] = jnp.concatenate(outs, axis=-1).astype(o_ref.dtype)
        lse_ref[0, i * blk:(i + 1) * blk, :] = stats


def _attn_call(q, k, v, *, width, window, dilation):
    b, ls, _ = q.shape
    w = width
    n_back = window // dilation
    blk = ATTN_BLOCK
    assert n_back <= blk
    step_rows = ATTN_SUB_BLOCKS * blk
    assert ls % step_rows == 0
    n_heads = w // ATTN_HEAD_DIM
    assert n_heads <= LANES and n_heads % 2 == 0 and 2 * ATTN_HEAD_DIM == LANES
    n_units = ATTN_SUB_BLOCKS * (n_heads // 2)
    cur =pl.BlockSpec((1, step_rows, w), lambda bi, r, n: (bi, n, r))
    prev = pl.BlockSpec((1, blk, w),
                        lambda bi, r, n: (bi, jnp.maximum(n * ATTN_SUB_BLOCKS - 1, 0), r))
    return pl.pallas_call(
        functools.partial(_attn_kernel, n_back=n_back, n_heads=n_heads),
        out_shape=(jax.ShapeDtypeStruct((b, ls, dilation * w), BF16),
                   jax.ShapeDtypeStruct((b, ls, dilation * LANES), F32)),
        grid=(b, dilation, ls // step_rows),
        in_specs=[cur, prev, cur, prev, cur],
        out_specs=(cur, pl.BlockSpec((1, step_rows, LANES), lambda bi, r, n: (bi, n, r))),
        scratch_shapes=[pltpu.VMEM((blk + step_rows, w), BF16),
                        pltpu.VMEM((2 * (blk + step_rows), 2 * w), BF16),
                        pltpu.VMEM((2, 2 * blk, 2 * blk), F32),
                        pltpu.VMEM((n_units, 2 * blk, 2 * blk), F32),
                        pltpu.VMEM((n_units, blk, 4 * blk), BF16),
                        pltpu.VMEM((n_units, 2 * blk, LANES), F32)],
        compiler_params=pltpu.CompilerParams(
            dimension_semantics=("arbitrary", "arbitrary", "arbitrary"),
            vmem_limit_bytes=VMEM_LIMIT_BYTES),
        name=f"attn_d{dilation}",
    )(q, k, k, v, v)


def _mlstm_kernel(qk_ref, v_ref, og_ref, gcol_ref, grow_ref, wc_ref, bc_ref, gn_ref, tri_ref, hm_ref,
                  o_ref, tail_s, c_s, m_s, *, ml_w):
    L = MLSTM_CHUNK
    nh = N_MLSTM_HEADS
    dh = ml_w // nh

    @pl.when(pl.program_id(1) == 0)
    def _():
        tail_s[...] = jnp.zeros_like(tail_s)
        c_s[...] = jnp.zeros_like(c_s)
        m_s[...] = jnp.zeros_like(m_s)

    tri = tri_ref[...]
    ti = lax.broadcasted_iota(jnp.int32, (L, L), 0)
    si = lax.broadcasted_iota(jnp.int32, (L, L), 1)
    causal = si <= ti
    ones = jnp.ones((L, dh), BF16)
    hp = hm_ref[...]
    for bi in range(qk_ref.shape[0]):
        _mlstm_chunk(bi, qk_ref, v_ref, og_ref, gcol_ref, grow_ref, wc_ref, bc_ref, gn_ref, o_ref,
                     tail_s, c_s, m_s, tri, causal, ones, hp, ml_w=ml_w)


def _mlstm_chunk(bi, qk_ref, v_ref, og_ref, gcol_ref, grow_ref, wc_ref, bc_ref, gn_ref, o_ref,
                 tail_s, c_s, m_s, tri, causal, ones, hp, *, ml_w):
    L = MLSTM_CHUNK
    nh = N_MLSTM_HEADS
    dh = ml_w // nh
    x = qk_ref[bi].astype(F32)
    tail = tail_s[bi]
    rowi = lax.broadcasted_iota(jnp.int32, x.shape, 0)
    acc = bc_ref[...] + x * wc_ref[CONV_WIDTH - 1:CONV_WIDTH, :]
    for back in range(1, CONV_WIDTH):
        shifted = jnp.where(rowi >= back, pltpu.roll(x, back, 0), pltpu.roll(tail, back, 0))
        acc = acc + shifted * wc_ref[CONV_WIDTH - 1 - back:CONV_WIDTH - back, :]
    tail_s[bi] = x
    qk = acc * _sigmoid(acc)

    gcol = gcol_ref[bi]
    grow = grow_ref[bi]
    bcol_all = _dot(tri, gcol, precision=lax.Precision.HIGHEST)
    brow_all = _dot_nt(grow, tri, precision=lax.Precision.HIGHEST)
    v_all = v_ref[bi]

    outs = []
    for h in range(nh):
        st = bi * nh + h
        q = qk[:, h * dh:(h + 1) * dh].astype(BF16)
        k = (qk[:, ml_w + h * dh:ml_w + (h + 1) * dh] * (dh ** -0.5))
        v_ext = jnp.concatenate([v_all[:, h * dh:(h + 1) * dh], ones], axis=-1)
        b_c = jnp.broadcast_to(bcol_all[:, nh + h:nh + h + 1], (L, dh))
        i_c = jnp.broadcast_to(gcol[:, h:h + 1], (L, dh))
        b_r = brow_all[nh + h:nh + h + 1, :]
        i_r = grow[h:h + 1, :]
        m_prev = m_s[st, 0:1, :]
        c_prev = c_s[st]

        log_d = jnp.where(causal, b_c[:, 0:1] - b_r + i_r, NEG_INF)
        m_inter = b_c + m_prev
        m_t = jnp.maximum(m_inter, jnp.max(log_d, axis=-1, keepdims=True))
        d_mat = jnp.exp(log_d - m_t[:, 0:1])
        inter = jnp.exp(m_inter - m_t)
        s_qk = _dot_nt(q, k.astype(BF16)) * d_mat
        ext = jnp.concatenate([inter, inter], axis=-1) * _dot(q, c_prev.astype(BF16)) \
            + _dot(s_qk.astype(BF16), v_ext)
        num = ext[:, :dh]
        nq = ext[:, dh:]
        hh = num / jnp.maximum(jnp.abs(nq), jnp.exp(-m_t))
        msq = _dot((hh * hh).astype(BF16), hp)
        outs.append(hh * lax.rsqrt(msq + RMS_EPS))

        b_last = b_c[L - 1:L, :]
        w_log = b_last - b_c + i_c
        m_new = jnp.maximum(b_last + m_prev, jnp.max(w_log, axis=0, keepdims=True))
        wgt = jnp.exp(w_log - m_new)
        decay = jnp.exp(b_last + m_prev - m_new)
        kw = (k * wgt).astype(BF16)
        c_s[st] = jnp.concatenate([decay, decay], axis=-1) * c_prev + _dot_tn(kw, v_ext)
        m_s[st] = jnp.broadcast_to(m_new, m_s.shape[1:])

    hn = jnp.concatenate(outs, axis=-1) * gn_ref[...]
    o_ref[bi] = (og_ref[bi].astype(F32) * hn).astype(o_ref.dtype)


def _mlstm_call(qkm, vm, og, gcol, grow, w_conv, b_conv, g_norm, tri, hmean):
    b, s, ml_w = vm.shape
    L = MLSTM_CHUNK
    nh = N_MLSTM_HEADS
    dh = ml_w // nh
    bb = MLSTM_BATCH_ROWS
    assert b % bb == 0
    const = lambda shape: pl.BlockSpec(shape, lambda g, i: (0,) * len(shape))
    tok = lambda w: pl.BlockSpec((bb, L, w), lambda g, i: (g, i, 0))
    return pl.pallas_call(
        functools.partial(_mlstm_kernel, ml_w=ml_w),
        out_shape=jax.ShapeDtypeStruct((b, s, ml_w), BF16),
        grid=(b // bb, s // L),
        in_specs=[tok(2 * ml_w), tok(ml_w), tok(ml_w), tok(GATE_LANES),
                  pl.BlockSpec((bb, 2 * nh, L), lambda g, i: (g, 0, i)),
                  const(w_conv.shape), const((1, 2 * ml_w)), const((1, ml_w)),
                  const((L, L)), const(hmean.shape)],
        out_specs=tok(ml_w),
        scratch_shapes=[pltpu.VMEM((bb, L, 2 * ml_w), F32),
                        pltpu.VMEM((bb * nh, dh, 2 * dh), F32),
                        pltpu.VMEM((bb * nh, MOD_ROWS, dh), F32)],
        compiler_params=pltpu.CompilerParams(dimension_semantics=("arbitrary", "arbitrary"),
                                             vmem_limit_bytes=VMEM_LIMIT_BYTES),
        name="mlstm",
    )(qkm, vm, og, gcol, grow, w_conv, b_conv, g_norm, tri, hmean)


def _outffn_kernel(x_ref, mod_ref, g_ref, *rest, dilations):
    n_lay = len(dilations)
    o_refs, l_refs = rest[:n_lay], rest[n_lay:2 * n_lay]
    (hm_ref, ex_ref, woa_ref, wom_ref, wg_ref, wu_ref, wd_ref, out_ref,
     operm_s, lperm_s) = rest[2 * n_lay:]
    tm = x_ref.shape[1]

    def natural(ref, d, scratch):
        if d == 1:
            return ref[0].astype(F32)
        n_cg = scratch.shape[0]
        for r in range(d):
            for cg in range(n_cg):
                c0 = (r * n_cg + cg) * LANES
                scratch[cg, pl.ds(r, tm // d, stride=d), :] = ref[0, :, c0:c0 + LANES].astype(F32)
        return jnp.concatenate([scratch[cg] for cg in range(n_cg)], axis=-1)

    lses = [natural(ref, d, lperm_s) for ref, d in zip(l_refs, dilations)]
    mx = functools.reduce(jnp.maximum, lses)
    es = [jnp.exp(l - mx) for l in lses]
    inv = 1.0 / functools.reduce(jnp.add, es)
    ex = ex_ref[...]
    attn = None
    for e, ref, d in zip(es, o_refs, dilations):
        term = _dot((e * inv).astype(BF16), ex) * natural(ref, d, operm_s)
        attn = term if attn is None else attn + term
    y = _dot(attn.astype(BF16), woa_ref[...]) + _dot(hm_ref[0], wom_ref[...])
    x1 = x_ref[0] + mod_ref[0, 2:3, :] * y
    ms = jnp.mean(x1 * x1, axis=-1, keepdims=True)
    hn = x1 * lax.rsqrt(ms + RMS_EPS) * g_ref[...]
    hb = (hn * (1.0 + mod_ref[0, 4:5, :]) + mod_ref[0, 3:4, :]).astype(BF16)
    g = _dot(hb, wg_ref[...])
    u = _dot(hb, wu_ref[...])
    a = (g * _sigmoid(g) * u).astype(BF16)
    out_ref[0] = x1 + mod_ref[0, 5:6, :] * _dot(a, wd_ref[...])


def _outffn_call(x, mod, g_ffn, os_, lses, hm, expand, woa, wom, wg, wu, wd, *, dilations):
    b, s, d = x.shape
    tm = FFN_ROWS
    assert all(tm % (dl * BF16_SUBLANES) == 0 for dl in dilations)
    aw = woa.shape[0]
    const = lambda shape: pl.BlockSpec(shape, lambda bi, i: (0,) * len(shape))
    tok = lambda w: pl.BlockSpec((1, tm, w), lambda bi, i: (bi, i, 0))
    strided = lambda w: tuple(pl.BlockSpec((1, tm // dl, dl * w), lambda bi, i: (bi, i, 0)) for dl in dilations)
    return pl.pallas_call(
        functools.partial(_outffn_kernel, dilations=dilations),
        out_shape=jax.ShapeDtypeStruct((b, s, d), F32),
        grid=(b, s // tm),
        in_specs=[tok(d), pl.BlockSpec((1, N_MOD, d), lambda bi, i: (bi, 0, 0)), const((1, d)),
                  *strided(aw), *strided(LANES), tok(hm.shape[-1]),
                  const(expand.shape), const(woa.shape), const(wom.shape),
                  const(wg.shape), const(wu.shape), const(wd.shape)],
        out_specs=tok(d),
        scratch_shapes=[pltpu.VMEM((aw // LANES, tm, LANES), F32), pltpu.VMEM((1, tm, LANES), F32)],
        compiler_params=pltpu.CompilerParams(dimension_semantics=("arbitrary", "arbitrary"),
                                             vmem_limit_bytes=VMEM_LIMIT_BYTES),
        name="outffn",
    )(x, mod, g_ffn, *os_, *lses, hm, expand, woa, wom, wg, wu, wd)


def _block_diag_mean(width, group):
    idx = jnp.arange(width) // group
    return jnp.where(idx[:, None] == idx[None, :], 1.0 / group, 0.0).astype(BF16)


def kernel(x, c, g_mix, w_in, w_conv, b_conv, b_igate, b_fgate, q_norm_g, k_norm_g, mlstm_norm_g, w_out,
           g_ffn, w_gate, w_up, w_down, w_ada, b_ada):
    b, s, d = x.shape
    depth = g_mix.shape[0]
    attn_w = d // 2
    ml_w = d - attn_w
    nh = N_MLSTM_HEADS
    n_attn_heads = attn_w // ATTN_HEAD_DIM
    scale = ATTN_HEAD_DIM ** -0.5 * LOG2E
    dilations = tuple(dl for _, dl in DILATED_PATTERNS)
    assert b <= MOD_ROWS and 2 * nh <= GATE_LANES

    c_pad = jnp.zeros((MOD_ROWS, d), F32).at[:b].set(c)
    hmean_attn = _block_diag_mean(attn_w, ATTN_HEAD_DIM)
    hmean_ml = _block_diag_mean(ml_w // nh, ml_w // nh)
    stat_lane = jnp.arange(LANES)
    head_of_stat = jnp.where(stat_lane % ATTN_HEAD_DIM < n_attn_heads // 2,
                             2 * (stat_lane % ATTN_HEAD_DIM) + stat_lane // ATTN_HEAD_DIM, -1)
    head_of_col = jnp.arange(attn_w) // ATTN_HEAD_DIM
    expand = (head_of_stat[:, None] == head_of_col[None, :]).astype(BF16)
    tri = jnp.tril(jnp.ones((MLSTM_CHUNK, MLSTM_CHUNK), F32))

    for l in range(depth):
        mod = _mod_call(c_pad, w_ada[l], b_ada[l][None, :])[:b].reshape(b, N_MOD, d)

        n_main = 3 * attn_w + 4 * ml_w
        wa = w_in[l][:, :3 * attn_w].astype(BF16)
        wm = w_in[l][:, 3 * attn_w:n_main].astype(BF16)
        wg_cols = w_in[l][:, n_main:]
        wg = jnp.zeros((d, GATE_LANES), F32).at[:, :2 * nh].set(wg_cols).astype(BF16)
        wgt = jnp.zeros((BF16_SUBLANES, d), F32).at[:2 * nh].set(wg_cols.T).astype(BF16)
        gate_bias = jnp.concatenate([b_igate[l], b_fgate[l]])
        bcol = jnp.zeros((1, GATE_LANES), F32).at[0, :2 * nh].set(gate_bias)
        brow = gate_bias[:, None]
        gq = jnp.tile(q_norm_g[l], n_attn_heads)[None, :]
        gk = jnp.tile(k_norm_g[l], n_attn_heads)[None, :]

        qs, ks, vs, qkm, vm, og, gcol, grow = _inproj_call(
            x, mod, g_mix[l][None, :], wa, wm, wg, wgt, gq, gk, hmean_attn, bcol, brow,
            attn_w=attn_w, ml_w=ml_w, scale=scale, dilations=dilations)

        os_, lses = [], []
        for (window, dilation), q, k, v in zip(DILATED_PATTERNS, qs, ks, vs):
            o, lse = _attn_call(q, k, v, width=attn_w, window=window, dilation=dilation)
            os_.append(o)
            lses.append(lse)

        hm = _mlstm_call(qkm, vm, og, gcol, grow, w_conv[l], b_conv[l][None, :],
                         mlstm_norm_g[l][None, :], tri, hmean_ml)

        x = _outffn_call(x, mod, g_ffn[l][None, :], os_, lses, hm, expand,
                         w_out[l][:attn_w].astype(BF16), w_out[l][attn_w:].astype(BF16),
                         w_gate[l].astype(BF16), w_up[l].astype(BF16), w_down[l].astype(BF16),
                         dilations=dilations)
    return x
```

```python
import functools

import jax
import jax.numpy as jnp
from jax import lax
from jax.experimental import pallas as pl
from jax.experimental.pallas import tpu as pltpu

F32 = jnp.float32
BF16 = jnp.bfloat16

ATTN_HEAD_DIM = 64
N_MLSTM_HEADS = 4
CONV_WIDTH = 4
DILATED_PATTERNS = ((128, 1), (512, 4), (2048, 16))
ATTN_BLOCK = 128
N_MOD = 6
RMS_EPS = 1e-6

LANES = 128
BF16_SUBLANES = 16
VMEM_LIMIT_BYTES = 56 * 1024 * 1024
MOD_ROWS = 8
GATE_LANES = 128
INPROJ_ROWS = 512
ATTN_SUB_BLOCKS = 4
MLSTM_CHUNK = 128
MLSTM_BATCH_ROWS = 1
FFN_ROWS = 512

NEG_INF = float("-inf")
LOG2E = 1.4426950408889634
LN2 = 0.6931471805599453


def _dot(a, b, **kw):
    return jnp.dot(a, b, preferred_element_type=F32, **kw)


def _dot_nt(a, b, **kw):
    return lax.dot_general(a, b, (((1,), (1,)), ((), ())), preferred_element_type=F32, **kw)


def _dot_tn(a, b):
    return lax.dot_general(a, b, (((0,), (0,)), ((), ())), preferred_element_type=F32)


def _sigmoid(z):
    return 1.0 / (1.0 + jnp.exp(-z))


def _log_sigmoid(z):
    return jnp.minimum(z, 0.0) - jnp.log1p(jnp.exp(-jnp.abs(z)))


def _strided_shape(b, s, w, dilation):
    return (b, s // dilation, dilation * w)


def _mod_kernel(c_ref, w_ref, b_ref, o_ref):
    c = c_ref[...]
    sc = c * _sigmoid(c)
    o_ref[...] = _dot(sc, w_ref[...], precision=lax.Precision.HIGHEST) + b_ref[...]


def _mod_call(c_pad, w_ada, b_ada):
    d, n = w_ada.shape
    tn = n // 4
    return pl.pallas_call(
        _mod_kernel,
        out_shape=jax.ShapeDtypeStruct((MOD_ROWS, n), F32),
        grid=(n // tn,),
        in_specs=[pl.BlockSpec((MOD_ROWS, d), lambda j: (0, 0)),
                  pl.BlockSpec((d, tn), lambda j: (0, j)),
                  pl.BlockSpec((1, tn), lambda j: (0, j))],
        out_specs=pl.BlockSpec((MOD_ROWS, tn), lambda j: (0, j)),
        compiler_params=pltpu.CompilerParams(dimension_semantics=("arbitrary",),
                                             vmem_limit_bytes=VMEM_LIMIT_BYTES),
        name="mod",
    )(c_pad, w_ada, b_ada)


def _inproj_kernel(x_ref, mod_ref, g_ref, wa_ref, wm_ref, wg_ref, wgt_ref, gq_ref, gk_ref, hp_ref,
                   bcol_ref, brow_ref, *rest, attn_w, ml_w, scale, dilations):
    n_lay = len(dilations)
    q_refs, k_refs, v_refs = rest[:n_lay], rest[n_lay:2 * n_lay], rest[2 * n_lay:3 * n_lay]
    qkm_ref, vm_ref, og_ref, gcol_ref, grow_ref, perm_s = rest[3 * n_lay:]
    tm = x_ref.shape[1]

    def emit(val, refs):
        w = val.shape[-1]
        staged = False
        for d, ref in zip(dilations, refs):
            if d == 1:
                ref[0] = val.astype(BF16)
                continue
            if not staged:
                for cg in range(w // LANES):
                    perm_s[cg] = val[:, cg * LANES:(cg + 1) * LANES]
                staged = True
            for r in range(d):
                for cg in range(w // LANES):
                    c0 = r * w + cg * LANES
                    ref[0, :, c0:c0 + LANES] = perm_s[cg, pl.ds(r, tm // d, stride=d), :].astype(BF16)

    x = x_ref[0]
    ms = jnp.mean(x * x, axis=-1, keepdims=True)
    y = x * lax.rsqrt(ms + RMS_EPS) * g_ref[...]
    h = (y * (1.0 + mod_ref[0, 1:2, :]) + mod_ref[0, 0:1, :]).astype(BF16)

    xa = _dot(h, wa_ref[...])
    hp = hp_ref[...]

    def head_norm(t, g):
        msq = _dot((t * t).astype(BF16), hp)
        return t * lax.rsqrt(msq + RMS_EPS) * g

    emit(head_norm(xa[:, :attn_w], gq_ref[...]) * scale, q_refs)
    emit(head_norm(xa[:, attn_w:2 * attn_w], gk_ref[...]), k_refs)
    emit(xa[:, 2 * attn_w:], v_refs)

    xm = _dot(h, wm_ref[...])
    qkm_ref[0] = xm[:, :2 * ml_w].astype(BF16)
    vm_ref[0] = xm[:, 2 * ml_w:3 * ml_w].astype(BF16)
    og_ref[0] = _sigmoid(xm[:, 3 * ml_w:]).astype(BF16)

    nh = N_MLSTM_HEADS
    zc = _dot(h, wg_ref[...]) + bcol_ref[...]
    lane = lax.broadcasted_iota(jnp.int32, zc.shape, 1)
    gcol_ref[0] = jnp.where(lane < nh, zc, jnp.where(lane < 2 * nh, _log_sigmoid(zc), 0.0))
    zr = _dot_nt(wgt_ref[...], h)[:2 * nh] + brow_ref[...]
    row = lax.broadcasted_iota(jnp.int32, zr.shape, 0)
    grow_ref[0] = jnp.where(row < nh, zr, _log_sigmoid(zr))


def _inproj_call(x, mod, g_mix, wa, wm, wg, wgt, gq, gk, hp, bcol, brow, *, attn_w, ml_w, scale, dilations):
    b, s, d = x.shape
    tm = INPROJ_ROWS
    nh2 = 2 * N_MLSTM_HEADS
    assert all(tm % (dl * BF16_SUBLANES) == 0 for dl in dilations)
    const = lambda shape: pl.BlockSpec(shape, lambda bi, i: (0,) * len(shape))
    tok = lambda w: pl.BlockSpec((1, tm, w), lambda bi, i: (bi, i, 0))
    strided_shapes = tuple(jax.ShapeDtypeStruct(_strided_shape(b, s, attn_w, dl), BF16) for dl in dilations)
    strided_specs = tuple(pl.BlockSpec((1, tm // dl, dl * attn_w), lambda bi, i: (bi, i, 0)) for dl in dilations)
    out_shape = strided_shapes * 3 + (
        jax.ShapeDtypeStruct((b, s, 2 * ml_w), BF16),
        jax.ShapeDtypeStruct((b, s, ml_w), BF16),
        jax.ShapeDtypeStruct((b, s, ml_w), BF16),
        jax.ShapeDtypeStruct((b, s, GATE_LANES), F32),
        jax.ShapeDtypeStruct((b, nh2, s), F32))
    outs = pl.pallas_call(
        functools.partial(_inproj_kernel, attn_w=attn_w, ml_w=ml_w, scale=scale, dilations=dilations),
        out_shape=out_shape,
        grid=(b, s // tm),
        in_specs=[tok(d),
                  pl.BlockSpec((1, N_MOD, d), lambda bi, i: (bi, 0, 0)),
                  const((1, d)), const(wa.shape), const(wm.shape), const(wg.shape), const(wgt.shape),
                  const((1, attn_w)), const((1, attn_w)), const(hp.shape),
                  const((1, GATE_LANES)), const((nh2, 1))],
        out_specs=strided_specs * 3 + (tok(2 * ml_w), tok(ml_w), tok(ml_w), tok(GATE_LANES),
                                       pl.BlockSpec((1, nh2, tm), lambda bi, i: (bi, 0, i))),
        scratch_shapes=[pltpu.VMEM((attn_w // LANES, tm, LANES), F32)],
        compiler_params=pltpu.CompilerParams(dimension_semantics=("arbitrary", "arbitrary"),
                                             vmem_limit_bytes=VMEM_LIMIT_BYTES),
        name="inproj",
    )(x, mod, g_mix, wa, wm, wg, wgt, gq, gk, hp, bcol, brow)
    n_lay = len(dilations)
    return (outs[:n_lay], outs[n_lay:2 * n_lay], outs[2 * n_lay:3 * n_lay]) + tuple(outs[3 * n_lay:])


def _attn_kernel(q_ref, kp_ref, kc_ref, vp_ref, vc_ref, o_ref, lse_ref, k_s, v_s, bias_s, s_s, p_s, m_s,
                 *, n_back, n_heads):
    blk = ATTN_BLOCK
    hd = ATTN_HEAD_DIM
    n_pairs = n_heads // 2
    n = pl.program_id(2)
    lane = lax.broadcasted_iota(jnp.int32, (blk, LANES), 1)
    first_head = lane < hd
    k_s[0:blk, :] = kp_ref[0]
    k_s[blk:, :] = kc_ref[0]
    ind_a = jnp.where(first_head, 1.0, 0.0).astype(BF16)
    ind_b = jnp.where(first_head, 0.0, 1.0).astype(BF16)
    for kb in range(ATTN_SUB_BLOCKS + 1):
        for j in range(n_pairs):
            cols = slice(j * LANES, (j + 1) * LANES)
            src = vp_ref[0, :, cols] if kb == 0 else vc_ref[0, (kb - 1) * blk:kb * blk, cols]
            ra, rb, c0 = 2 * kb * blk, (2 * kb + 1) * blk, 2 * j * LANES
            v_s[ra:ra + blk, c0:c0 + LANES] = jnp.where(first_head, src, 0)
            v_s[ra:ra + blk, c0 + LANES:c0 + 2 * LANES] = ind_a
            v_s[rb:rb + blk, c0:c0 + LANES] = jnp.where(first_head, 0, src)
            v_s[rb:rb + blk, c0 + LANES:c0 + 2 * LANES] = ind_b

    row = lax.broadcasted_iota(jnp.int32, (2 * blk, 2 * blk), 0) & (blk - 1)
    col = lax.broadcasted_iota(jnp.int32, (2 * blk, 2 * blk), 1)
    band = jnp.logical_and(col >= row + (blk - n_back), col <= row + blk)
    bias_s[0] = jnp.where(band, 0.0, NEG_INF)
    bias_s[1] = jnp.where(jnp.logical_and(band, col >= blk), 0.0, NEG_INF)

    units = [(i, j) for i in range(ATTN_SUB_BLOCKS) for j in range(n_pairs)]
    for u, (i, j) in enumerate(units):
        cols = slice(j * LANES, (j + 1) * LANES)
        q = q_ref[0, i * blk:(i + 1) * blk, cols]
        ks = k_s[i * blk:(i + 2) * blk, cols]
        bias = bias_s[jnp.where(n == 0, 1, 0)] if i == 0 else bias_s[0]
        q2 = jnp.concatenate([jnp.where(first_head, q, 0), jnp.where(first_head, 0, q)], axis=0)
        s_s[u] = _dot_nt(q2, ks) + bias
    for u, (i, j) in enumerate(units):
        s = s_s[u]
        m = jnp.max(s, axis=-1, keepdims=True)
        p = jnp.exp2(s - m).astype(BF16)
        p_s[u] = jnp.concatenate([p[:blk, :blk], p[blk:, :blk], p[:blk, blk:], p[blk:, blk:]], axis=1)
        m_s[u] = jnp.broadcast_to(m, (2 * blk, LANES))
    for i in range(ATTN_SUB_BLOCKS):
        outs = []
        stats = jnp.zeros((blk, LANES), F32)
        for j in range(n_pairs):
            u = i * n_pairs + j
            vw = v_s[2 * i * blk:2 * (i + 2) * blk, 2 * j * LANES:(2 * j + 2) * LANES]
            ov = _dot(p_s[u], vw)
            outs.append(ov[:, :LANES] / ov[:, LANES:])
            lse = (jnp.where(first_head, m_s[u, :blk], m_s[u, blk:]) + jnp.log2(ov[:, LANES:])) * LN2
            stats = jnp.where((lane & (hd - 1)) == j, lse, stats)
        o_ref[0, i * blk:(i + 1) * blk, :] = jnp.concatenate(outs, axis=-1).astype(o_ref.dtype)
        lse_ref[0, i * blk:(i + 1) * blk, :] = stats


def _attn_call(q, k, v, *, width, window, dilation):
    b, ls, _ = q.shape
    w = width
    n_back = window // dilation
    blk = ATTN_BLOCK
    assert n_back <= blk
    step_rows = ATTN_SUB_BLOCKS * blk
    assert ls % step_rows == 0
    n_heads = w // ATTN_HEAD_DIM
    assert n_heads <= LANES and n_heads % 2 == 0 and 2 * ATTN_HEAD_DIM == LANES
    n_units = ATTN_SUB_BLOCKS * (n_heads // 2)
    cur =pl.BlockSpec((1, step_rows, w), lambda bi, r, n: (bi, n, r))
    prev = pl.BlockSpec((1, blk, w),
                        lambda bi, r, n: (bi, jnp.maximum(n * ATTN_SUB_BLOCKS - 1, 0), r))
    return pl.pallas_call(
        functools.partial(_attn_kernel, n_back=n_back, n_heads=n_heads),
        out_shape=(jax.ShapeDtypeStruct((b, ls, dilation * w), BF16),
                   jax.ShapeDtypeStruct((b, ls, dilation * LANES), F32)),
        grid=(b, dilation, ls // step_rows),
        in_specs=[cur, prev, cur, prev, cur],
        out_specs=(cur, pl.BlockSpec((1, step_rows, LANES), lambda bi, r, n: (bi, n, r))),
        scratch_shapes=[pltpu.VMEM((blk + step_rows, w), BF16),
                        pltpu.VMEM((2 * (blk + step_rows), 2 * w), BF16),
                        pltpu.VMEM((2, 2 * blk, 2 * blk), F32),
                        pltpu.VMEM((n_units, 2 * blk, 2 * blk), F32),
                        pltpu.VMEM((n_units, blk, 4 * blk), BF16),
                        pltpu.VMEM((n_units, 2 * blk, LANES), F32)],
        compiler_params=pltpu.CompilerParams(
            dimension_semantics=("arbitrary", "arbitrary", "arbitrary"),
            vmem_limit_bytes=VMEM_LIMIT_BYTES),
        name=f"attn_d{dilation}",
    )(q, k, k, v, v)


def _mlstm_kernel(qk_ref, v_ref, og_ref, gcol_ref, grow_ref, wc_ref, bc_ref, gn_ref, tri_ref, hm_ref,
                  o_ref, tail_s, c_s, m_s, *, ml_w):
    L = MLSTM_CHUNK
    nh = N_MLSTM_HEADS
    dh = ml_w // nh

    @pl.when(pl.program_id(1) == 0)
    def _():
        tail_s[...] = jnp.zeros_like(tail_s)
        c_s[...] = jnp.zeros_like(c_s)
        m_s[...] = jnp.zeros_like(m_s)

    tri = tri_ref[...]
    ti = lax.broadcasted_iota(jnp.int32, (L, L), 0)
    si = lax.broadcasted_iota(jnp.int32, (L, L), 1)
    causal = si <= ti
    ones = jnp.ones((L, dh), BF16)
    hp = hm_ref[...]
    for bi in range(qk_ref.shape[0]):
        _mlstm_chunk(bi, qk_ref, v_ref, og_ref, gcol_ref, grow_ref, wc_ref, bc_ref, gn_ref, o_ref,
                     tail_s, c_s, m_s, tri, causal, ones, hp, ml_w=ml_w)


def _mlstm_chunk(bi, qk_ref, v_ref, og_ref, gcol_ref, grow_ref, wc_ref, bc_ref, gn_ref, o_ref,
                 tail_s, c_s, m_s, tri, causal, ones, hp, *, ml_w):
    L = MLSTM_CHUNK
    nh = N_MLSTM_HEADS
    dh = ml_w // nh
    x = qk_ref[bi].astype(F32)
    tail = tail_s[bi]
    rowi = lax.broadcasted_iota(jnp.int32, x.shape, 0)
    acc = bc_ref[...] + x * wc_ref[CONV_WIDTH - 1:CONV_WIDTH, :]
    for back in range(1, CONV_WIDTH):
        shifted = jnp.where(rowi >= back, pltpu.roll(x, back, 0), pltpu.roll(tail, back, 0))
        acc = acc + shifted * wc_ref[CONV_WIDTH - 1 - back:CONV_WIDTH - back, :]
    tail_s[bi] = x
    qk = acc * _sigmoid(acc)

    gcol = gcol_ref[bi]
    grow = grow_ref[bi]
    bcol_all = _dot(tri, gcol, precision=lax.Precision.HIGHEST)
    brow_all = _dot_nt(grow, tri, precision=lax.Precision.HIGHEST)

    ks = [qk[:, ml_w + h * dh:ml_w + (h + 1) * dh] * (dh ** -0.5) for h in range(nh)]

    heads = []
    for h in range(nh):
        st = bi * nh + h
        b_c = jnp.broadcast_to(bcol_all[:, nh + h:nh + h + 1], (L, dh))
        i_c = jnp.broadcast_to(gcol[:, h:h + 1], (L, dh))
        b_r = brow_all[nh + h:nh + h + 1, :]
        i_r = grow[h:h + 1, :]
        m_prev = m_s[st, 0:1, :]
        log_d = jnp.where(causal, b_c[:, 0:1] - b_r + i_r, NEG_INF)
        m_inter = b_c + m_prev
        m_t = jnp.maximum(m_inter, jnp.max(log_d, axis=-1, keepdims=True))
        d_mat = jnp.exp(log_d - m_t[:, 0:1])
        inter = jnp.exp(m_inter - m_t)
        b_last = b_c[L - 1:L, :]
        w_log = b_last - b_c + i_c
        m_new = jnp.maximum(b_last + m_prev, jnp.max(w_log, axis=0, keepdims=True))
        wgt = jnp.exp(w_log - m_new)
        decay = jnp.exp(b_last + m_prev - m_new)
        heads.append((m_t, d_mat, inter, m_new, wgt, decay))

    outs = []
    for h, (m_t, d_mat, inter, _, _, _) in enumerate(heads):
        hcols = slice(h * dh, (h + 1) * dh)
        q = qk[:, hcols].astype(BF16)
        v_ext = jnp.concatenate([v_ref[bi, :, hcols], ones], axis=-1)
        c_prev = c_s[bi * nh + h]
        s_qk = _dot_nt(q, ks[h].astype(BF16)) * d_mat
        ext = jnp.concatenate([inter, inter], axis=-1) * _dot(q, c_prev.astype(BF16)) \
            + _dot(s_qk.astype(BF16), v_ext)
        outs.append(ext[:, :dh] / jnp.maximum(jnp.abs(ext[:, dh:]), jnp.exp(-m_t)))
    hh = jnp.concatenate(outs, axis=-1)
    msq = _dot((hh * hh).astype(BF16), hp)
    hn = hh * lax.rsqrt(msq + RMS_EPS) * gn_ref[...]
    o_ref[bi] = (og_ref[bi].astype(F32) * hn).astype(o_ref.dtype)

    for h, (_, _, _, m_new, wgt, decay) in enumerate(heads):
        st = bi * nh + h
        hcols = slice(h * dh, (h + 1) * dh)
        v_ext = jnp.concatenate([v_ref[bi, :, hcols], ones], axis=-1)
        kw = (ks[h] * wgt).astype(BF16)
        c_s[st] = jnp.concatenate([decay, decay], axis=-1) * c_s[st] + _dot_tn(kw, v_ext)
        m_s[st] = jnp.broadcast_to(m_new, m_s.shape[1:])


def _mlstm_call(qkm, vm, og, gcol, grow, w_conv, b_conv, g_norm, tri, hmean):
    b, s, ml_w = vm.shape
    L = MLSTM_CHUNK
    nh = N_MLSTM_HEADS
    dh = ml_w // nh
    bb = MLSTM_BATCH_ROWS
    assert b % bb == 0
    const = lambda shape: pl.BlockSpec(shape, lambda g, i: (0,) * len(shape))
    tok = lambda w: pl.BlockSpec((bb, L, w), lambda g, i: (g, i, 0))
    return pl.pallas_call(
        functools.partial(_mlstm_kernel, ml_w=ml_w),
        out_shape=jax.ShapeDtypeStruct((b, s, ml_w), BF16),
        grid=(b // bb, s // L),
        in_specs=[tok(2 * ml_w), tok(ml_w), tok(ml_w), tok(GATE_LANES),
                  pl.BlockSpec((bb, 2 * nh, L), lambda g, i: (g, 0, i)),
                  const(w_conv.shape), const((1, 2 * ml_w)), const((1, ml_w)),
                  const((L, L)), const(hmean.shape)],
        out_specs=tok(ml_w),
        scratch_shapes=[pltpu.VMEM((bb, L, 2 * ml_w), F32),
                        pltpu.VMEM((bb * nh, dh, 2 * dh), F32),
                        pltpu.VMEM((bb * nh, MOD_ROWS, dh), F32)],
        compiler_params=pltpu.CompilerParams(dimension_semantics=("arbitrary", "arbitrary"),
                                             vmem_limit_bytes=VMEM_LIMIT_BYTES),
        name="mlstm",
    )(qkm, vm, og, gcol, grow, w_conv, b_conv, g_norm, tri, hmean)


def _outffn_kernel(x_ref, mod_ref, g_ref, *rest, dilations):
    n_lay = len(dilations)
    o_refs, l_refs = rest[:n_lay], rest[n_lay:2 * n_lay]
    (hm_ref, ex_ref, woa_ref, wom_ref, wg_ref, wu_ref, wd_ref, out_ref,
     operm_s, lperm_s) = rest[2 * n_lay:]
    tm = x_ref.shape[1]

    def natural(ref, d, scratch):
        if d == 1:
            return ref[0].astype(F32)
        n_cg = scratch.shape[0]
        for r in range(d):
            for cg in range(n_cg):
                c0 = (r * n_cg + cg) * LANES
                scratch[cg, pl.ds(r, tm // d, stride=d), :] = ref[0, :, c0:c0 + LANES].astype(F32)
        return jnp.concatenate([scratch[cg] for cg in range(n_cg)], axis=-1)

    lses = [natural(ref, d, lperm_s) for ref, d in zip(l_refs, dilations)]
    mx = functools.reduce(jnp.maximum, lses)
    es = [jnp.exp(l - mx) for l in lses]
    inv = 1.0 / functools.reduce(jnp.add, es)
    ex = ex_ref[...]
    attn = None
    for e, ref, d in zip(es, o_refs, dilations):
        term = _dot((e * inv).astype(BF16), ex) * natural(ref, d, operm_s)
        attn = term if attn is None else attn + term
    y = _dot(attn.astype(BF16), woa_ref[...]) + _dot(hm_ref[0], wom_ref[...])
    x1 = x_ref[0] + mod_ref[0, 2:3, :] * y
    ms = jnp.mean(x1 * x1, axis=-1, keepdims=True)
    hn = x1 * lax.rsqrt(ms + RMS_EPS) * g_ref[...]
    hb = (hn * (1.0 + mod_ref[0, 4:5, :]) + mod_ref[0, 3:4, :]).astype(BF16)
    g = _dot(hb, wg_ref[...])
    u = _dot(hb, wu_ref[...])
    a = (g * _sigmoid(g) * u).astype(BF16)
    out_ref[0] = x1 + mod_ref[0, 5:6, :] * _dot(a, wd_ref[...])


def _outffn_call(x, mod, g_ffn, os_, lses, hm, expand, woa, wom, wg, wu, wd, *, dilations):
    b, s, d = x.shape
    tm = FFN_ROWS
    assert all(tm % (dl * BF16_SUBLANES) == 0 for dl in dilations)
    aw = woa.shape[0]
    const = lambda shape: pl.BlockSpec(shape, lambda bi, i: (0,) * len(shape))
    tok = lambda w: pl.BlockSpec((1, tm, w), lambda bi, i: (bi, i, 0))
    strided = lambda w: tuple(pl.BlockSpec((1, tm // dl, dl * w), lambda bi, i: (bi, i, 0)) for dl in dilations)
    return pl.pallas_call(
        functools.partial(_outffn_kernel, dilations=dilations),
        out_shape=jax.ShapeDtypeStruct((b, s, d), F32),
        grid=(b, s // tm),
        in_specs=[tok(d), pl.BlockSpec((1, N_MOD, d), lambda bi, i: (bi, 0, 0)), const((1, d)),
                  *strided(aw), *strided(LANES), tok(hm.shape[-1]),
                  const(expand.shape), const(woa.shape), const(wom.shape),
                  const(wg.shape), const(wu.shape), const(wd.shape)],
        out_specs=tok(d),
        scratch_shapes=[pltpu.VMEM((aw // LANES, tm, LANES), F32), pltpu.VMEM((1, tm, LANES), F32)],
        compiler_params=pltpu.CompilerParams(dimension_semantics=("arbitrary", "arbitrary"),
                                             vmem_limit_bytes=VMEM_LIMIT_BYTES),
        name="outffn",
    )(x, mod, g_ffn, *os_, *lses, hm, expand, woa, wom, wg, wu, wd)


def _block_diag_mean(width, group):
    idx = jnp.arange(width) // group
    return jnp.where(idx[:, None] == idx[None, :], 1.0 / group, 0.0).astype(BF16)


def kernel(x, c, g_mix, w_in, w_conv, b_conv, b_igate, b_fgate, q_norm_g, k_norm_g, mlstm_norm_g, w_out,
           g_ffn, w_gate, w_up, w_down, w_ada, b_ada):
    b, s, d = x.shape
    depth = g_mix.shape[0]
    attn_w = d // 2
    ml_w = d - attn_w
    nh = N_MLSTM_HEADS
    n_attn_heads = attn_w // ATTN_HEAD_DIM
    scale = ATTN_HEAD_DIM ** -0.5 * LOG2E
    dilations = tuple(dl for _, dl in DILATED_PATTERNS)
    assert b <= MOD_ROWS and 2 * nh <= GATE_LANES

    c_pad = jnp.zeros((MOD_ROWS, d), F32).at[:b].set(c)
    hmean_attn = _block_diag_mean(attn_w, ATTN_HEAD_DIM)
    hmean_ml = _block_diag_mean(ml_w, ml_w // nh)
    stat_lane = jnp.arange(LANES)
    head_of_stat = jnp.where(stat_lane % ATTN_HEAD_DIM < n_attn_heads // 2,
                             2 * (stat_lane % ATTN_HEAD_DIM) + stat_lane // ATTN_HEAD_DIM, -1)
    head_of_col = jnp.arange(attn_w) // ATTN_HEAD_DIM
    expand = (head_of_stat[:, None] == head_of_col[None, :]).astype(BF16)
    tri = jnp.tril(jnp.ones((MLSTM_CHUNK, MLSTM_CHUNK), F32))

    for l in range(depth):
        mod = _mod_call(c_pad, w_ada[l], b_ada[l][None, :])[:b].reshape(b, N_MOD, d)

        n_main = 3 * attn_w + 4 * ml_w
        wa = w_in[l][:, :3 * attn_w].astype(BF16)
        wm = w_in[l][:, 3 * attn_w:n_main].astype(BF16)
        wg_cols = w_in[l][:, n_main:]
        wg = jnp.zeros((d, GATE_LANES), F32).at[:, :2 * nh].set(wg_cols).astype(BF16)
        wgt = jnp.zeros((BF16_SUBLANES, d), F32).at[:2 * nh].set(wg_cols.T).astype(BF16)
        gate_bias = jnp.concatenate([b_igate[l], b_fgate[l]])
        bcol = jnp.zeros((1, GATE_LANES), F32).at[0, :2 * nh].set(gate_bias)
        brow = gate_bias[:, None]
        gq = jnp.tile(q_norm_g[l], n_attn_heads)[None, :]
        gk = jnp.tile(k_norm_g[l], n_attn_heads)[None, :]

        qs, ks, vs, qkm, vm, og, gcol, grow = _inproj_call(
            x, mod, g_mix[l][None, :], wa, wm, wg, wgt, gq, gk, hmean_attn, bcol, brow,
            attn_w=attn_w, ml_w=ml_w, scale=scale, dilations=dilations)

        os_, lses = [], []
        for (window, dilation), q, k, v in zip(DILATED_PATTERNS, qs, ks, vs):
            o, lse = _attn_call(q, k, v, width=attn_w, window=window, dilation=dilation)
            os_.append(o)
            lses.append(lse)

        hm = _mlstm_call(qkm, vm, og, gcol, grow, w_conv[l], b_conv[l][None, :],
                         mlstm_norm_g[l][None, :], tri, hmean_ml)

        x = _outffn_call(x, mod, g_ffn[l][None, :], os_, lses, hm, expand,
                         w_out[l][:attn_w].astype(BF16), w_out[l][attn_w:].astype(BF16),
                         w_gate[l].astype(BF16), w_up[l].astype(BF16), w_down[l].astype(BF16),
                         dilations=dilations)
    return x
```

```python
import functools

import jax
import jax.numpy as jnp
from jax import lax
from jax.experimental import pallas as pl
from jax.experimental.pallas import tpu as pltpu

F32 = jnp.float32
BF16 = jnp.bfloat16

ATTN_HEAD_DIM = 64
N_MLSTM_HEADS = 4
CONV_WIDTH = 4
DILATED_PATTERNS = ((128, 1), (512, 4), (2048, 16))
ATTN_BLOCK = 128
N_MOD = 6
RMS_EPS = 1e-6

LANES = 128
BF16_SUBLANES = 16
VMEM_LIMIT_BYTES = 56 * 1024 * 1024
MOD_ROWS = 8
GATE_LANES = 128
INPROJ_ROWS = 512
ATTN_SUB_BLOCKS = 4
MLSTM_CHUNK = 256
MLSTM_BATCH_ROWS = 1
FFN_ROWS = 512

NEG_INF = float("-inf")
LOG2E = 1.4426950408889634
LN2 = 0.6931471805599453


def _dot(a, b, **kw):
    return jnp.dot(a, b, preferred_element_type=F32, **kw)


def _dot_nt(a, b, **kw):
    return lax.dot_general(a, b, (((1,), (1,)), ((), ())), preferred_element_type=F32, **kw)


def _dot_tn(a, b):
    return lax.dot_general(a, b, (((0,), (0,)), ((), ())), preferred_element_type=F32)


def _sigmoid(z):
    return 1.0 / (1.0 + jnp.exp(-z))


def _log_sigmoid(z):
    return jnp.minimum(z, 0.0) - jnp.log1p(jnp.exp(-jnp.abs(z)))


def _strided_shape(b, s, w, dilation):
    return (b, s // dilation, dilation * w)


def _mod_kernel(c_ref, w_ref, b_ref, o_ref):
    c = c_ref[...]
    sc = c * _sigmoid(c)
    o_ref[...] = _dot(sc, w_ref[...], precision=lax.Precision.HIGHEST) + b_ref[...]


def _mod_call(c_pad, w_ada, b_ada):
    d, n = w_ada.shape
    tn = n // 4
    return pl.pallas_call(
        _mod_kernel,
        out_shape=jax.ShapeDtypeStruct((MOD_ROWS, n), F32),
        grid=(n // tn,),
        in_specs=[pl.BlockSpec((MOD_ROWS, d), lambda j: (0, 0)),
                  pl.BlockSpec((d, tn), lambda j: (0, j)),
                  pl.BlockSpec((1, tn), lambda j: (0, j))],
        out_specs=pl.BlockSpec((MOD_ROWS, tn), lambda j: (0, j)),
        compiler_params=pltpu.CompilerParams(dimension_semantics=("arbitrary",),
                                             vmem_limit_bytes=VMEM_LIMIT_BYTES),
        name="mod",
    )(c_pad, w_ada, b_ada)


def _inproj_kernel(x_ref, mod_ref, g_ref, wa_ref, wm_ref, wg_ref, wgt_ref, gq_ref, gk_ref, hp_ref,
                   bcol_ref, brow_ref, *rest, attn_w, ml_w, scale, dilations):
    n_lay = len(dilations)
    q_refs, k_refs, v_refs = rest[:n_lay], rest[n_lay:2 * n_lay], rest[2 * n_lay:3 * n_lay]
    qkm_ref, vm_ref, og_ref, gcol_ref, grow_ref, perm_s = rest[3 * n_lay:]
    tm = x_ref.shape[1]

    def emit(val, refs):
        w = val.shape[-1]
        staged = False
        for d, ref in zip(dilations, refs):
            if d == 1:
                ref[0] = val.astype(BF16)
                continue
            if not staged:
                for cg in range(w // LANES):
                    perm_s[cg] = val[:, cg * LANES:(cg + 1) * LANES]
                staged = True
            for r in range(d):
                for cg in range(w // LANES):
                    c0 = r * w + cg * LANES
                    ref[0, :, c0:c0 + LANES] = perm_s[cg, pl.ds(r, tm // d, stride=d), :].astype(BF16)

    x = x_ref[0]
    ms = jnp.mean(x * x, axis=-1, keepdims=True)
    y = x * lax.rsqrt(ms + RMS_EPS) * g_ref[...]
    h = (y * (1.0 + mod_ref[0, 1:2, :]) + mod_ref[0, 0:1, :]).astype(BF16)

    xa = _dot(h, wa_ref[...])
    hp = hp_ref[...]

    def head_norm(t, g):
        msq = _dot((t * t).astype(BF16), hp)
        return t * lax.rsqrt(msq + RMS_EPS) * g

    emit(head_norm(xa[:, :attn_w], gq_ref[...]) * scale, q_refs)
    emit(head_norm(xa[:, attn_w:2 * attn_w], gk_ref[...]), k_refs)
    emit(xa[:, 2 * attn_w:], v_refs)

    xm = _dot(h, wm_ref[...])
    qkm_ref[0] = xm[:, :2 * ml_w].astype(BF16)
    vm_ref[0] = xm[:, 2 * ml_w:3 * ml_w].astype(BF16)
    og_ref[0] = _sigmoid(xm[:, 3 * ml_w:]).astype(BF16)

    nh = N_MLSTM_HEADS
    zc = _dot(h, wg_ref[...]) + bcol_ref[...]
    lane = lax.broadcasted_iota(jnp.int32, zc.shape, 1)
    gcol_ref[0] = jnp.where(lane < nh, zc, jnp.where(lane < 2 * nh, _log_sigmoid(zc), 0.0))
    zr = _dot_nt(wgt_ref[...], h)[:2 * nh] + brow_ref[...]
    row = lax.broadcasted_iota(jnp.int32, zr.shape, 0)
    grow_ref[0] = jnp.where(row < nh, zr, _log_sigmoid(zr))


def _inproj_call(x, mod, g_mix, wa, wm, wg, wgt, gq, gk, hp, bcol, brow, *, attn_w, ml_w, scale, dilations):
    b, s, d = x.shape
    tm = INPROJ_ROWS
    nh2 = 2 * N_MLSTM_HEADS
    assert all(tm % (dl * BF16_SUBLANES) == 0 for dl in dilations)
    const = lambda shape: pl.BlockSpec(shape, lambda bi, i: (0,) * len(shape))
    tok = lambda w: pl.BlockSpec((1, tm, w), lambda bi, i: (bi, i, 0))
    strided_shapes = tuple(jax.ShapeDtypeStruct(_strided_shape(b, s, attn_w, dl), BF16) for dl in dilations)
    strided_specs = tuple(pl.BlockSpec((1, tm // dl, dl * attn_w), lambda bi, i: (bi, i, 0)) for dl in dilations)
    out_shape = strided_shapes * 3 + (
        jax.ShapeDtypeStruct((b, s, 2 * ml_w), BF16),
        jax.ShapeDtypeStruct((b, s, ml_w), BF16),
        jax.ShapeDtypeStruct((b, s, ml_w), BF16),
        jax.ShapeDtypeStruct((b, s, GATE_LANES), F32),
        jax.ShapeDtypeStruct((b, nh2, s), F32))
    outs = pl.pallas_call(
        functools.partial(_inproj_kernel, attn_w=attn_w, ml_w=ml_w, scale=scale, dilations=dilations),
        out_shape=out_shape,
        grid=(b, s // tm),
        in_specs=[tok(d),
                  pl.BlockSpec((1, N_MOD, d), lambda bi, i: (bi, 0, 0)),
                  const((1, d)), const(wa.shape), const(wm.shape), const(wg.shape), const(wgt.shape),
                  const((1, attn_w)), const((1, attn_w)), const(hp.shape),
                  const((1, GATE_LANES)), const((nh2, 1))],
        out_specs=strided_specs * 3 + (tok(2 * ml_w), tok(ml_w), tok(ml_w), tok(GATE_LANES),
                                       pl.BlockSpec((1, nh2, tm), lambda bi, i: (bi, 0, i))),
        scratch_shapes=[pltpu.VMEM((attn_w // LANES, tm, LANES), F32)],
        compiler_params=pltpu.CompilerParams(dimension_semantics=("arbitrary", "arbitrary"),
                                             vmem_limit_bytes=VMEM_LIMIT_BYTES),
        name="inproj",
    )(x, mod, g_mix, wa, wm, wg, wgt, gq, gk, hp, bcol, brow)
    n_lay = len(dilations)
    return (outs[:n_lay], outs[n_lay:2 * n_lay], outs[2 * n_lay:3 * n_lay]) + tuple(outs[3 * n_lay:])


def _attn_kernel(q_ref, kp_ref, kc_ref, vp_ref, vc_ref, o_ref, lse_ref, k_s, v_s, bias_s, s_s, p_s, m_s,
                 *, n_back, n_heads):
    blk = ATTN_BLOCK
    hd = ATTN_HEAD_DIM
    n_pairs = n_heads // 2
    n = pl.program_id(2)
    lane = lax.broadcasted_iota(jnp.int32, (blk, LANES), 1)
    first_head = lane < hd
    k_s[0:blk, :] = kp_ref[0]
    k_s[blk:, :] = kc_ref[0]
    ind_a = jnp.where(first_head, 1.0, 0.0).astype(BF16)
    ind_b = jnp.where(first_head, 0.0, 1.0).astype(BF16)
    for kb in range(ATTN_SUB_BLOCKS + 1):
        for j in range(n_pairs):
            cols = slice(j * LANES, (j + 1) * LANES)
            src = vp_ref[0, :, cols] if kb == 0 else vc_ref[0, (kb - 1) * blk:kb * blk, cols]
            ra, rb, c0 = 2 * kb * blk, (2 * kb + 1) * blk, 2 * j * LANES
            v_s[ra:ra + blk, c0:c0 + LANES] = jnp.where(first_head, src, 0)
            v_s[ra:ra + blk, c0 + LANES:c0 + 2 * LANES] = ind_a
            v_s[rb:rb + blk, c0:c0 + LANES] = jnp.where(first_head, 0, src)
            v_s[rb:rb + blk, c0 + LANES:c0 + 2 * LANES] = ind_b

    row = lax.broadcasted_iota(jnp.int32, (2 * blk, 2 * blk), 0) & (blk - 1)
    col = lax.broadcasted_iota(jnp.int32, (2 * blk, 2 * blk), 1)
    band = jnp.logical_and(col >= row + (blk - n_back), col <= row + blk)
    bias_s[0] = jnp.where(band, 0.0, NEG_INF)
    bias_s[1] = jnp.where(jnp.logical_and(band, col >= blk), 0.0, NEG_INF)

    units = [(i, j) for i in range(ATTN_SUB_BLOCKS) for j in range(n_pairs)]
    for u, (i, j) in enumerate(units):
        cols = slice(j * LANES, (j + 1) * LANES)
        q = q_ref[0, i * blk:(i + 1) * blk, cols]
        ks = k_s[i * blk:(i + 2) * blk, cols]
        bias = bias_s[jnp.where(n == 0, 1, 0)] if i == 0 else bias_s[0]
        q2 = jnp.concatenate([jnp.where(first_head, q, 0), jnp.where(first_head, 0, q)], axis=0)
        s_s[u] = _dot_nt(q2, ks) + bias
    for u, (i, j) in enumerate(units):
        s = s_s[u]
        m = jnp.max(s, axis=-1, keepdims=True)
        p = jnp.exp2(s - m).astype(BF16)
        p_s[u] = jnp.concatenate([p[:blk, :blk], p[blk:, :blk], p[:blk, blk:], p[blk:, blk:]], axis=1)
        m_s[u] = jnp.broadcast_to(m, (2 * blk, LANES))
    for i in range(ATTN_SUB_BLOCKS):
        outs = []
        stats = jnp.zeros((blk, LANES), F32)
        for j in range(n_pairs):
            u = i * n_pairs + j
            vw = v_s[2 * i * blk:2 * (i + 2) * blk, 2 * j * LANES:(2 * j + 2) * LANES]
            ov = _dot(p_s[u], vw)
            outs.append(ov[:, :LANES] / ov[:, LANES:])
            lse = (jnp.where(first_head, m_s[u, :blk], m_s[u, blk:]) + jnp.log2(ov[:, LANES:])) * LN2
            stats = jnp.where((lane & (hd - 1)) == j, lse, stats)
        o_ref[0, i * blk:(i + 1) * blk, :] = jnp.concatenate(outs, axis=-1).astype(o_ref.dtype)
        lse_ref[0, i * blk:(i + 1) * blk, :] = stats


def _attn_call(q, k, v, *, width, window, dilation):
    b, ls, _ = q.shape
    w = width
    n_back = window // dilation
    blk = ATTN_BLOCK
    assert n_back <= blk
    step_rows = ATTN_SUB_BLOCKS * blk
    assert ls % step_rows == 0
    n_heads = w // ATTN_HEAD_DIM
    assert n_heads <= LANES and n_heads % 2 == 0 and 2 * ATTN_HEAD_DIM == LANES
    n_units = ATTN_SUB_BLOCKS * (n_heads // 2)
    cur =pl.BlockSpec((1, step_rows, w), lambda bi, r, n: (bi, n, r))
    prev = pl.BlockSpec((1, blk, w),
                        lambda bi, r, n: (bi, jnp.maximum(n * ATTN_SUB_BLOCKS - 1, 0), r))
    return pl.pallas_call(
        functools.partial(_attn_kernel, n_back=n_back, n_heads=n_heads),
        out_shape=(jax.ShapeDtypeStruct((b, ls, dilation * w), BF16),
                   jax.ShapeDtypeStruct((b, ls, dilation * LANES), F32)),
        grid=(b, dilation, ls // step_rows),
        in_specs=[cur, prev, cur, prev, cur],
        out_specs=(cur, pl.BlockSpec((1, step_rows, LANES), lambda bi, r, n: (bi, n, r))),
        scratch_shapes=[pltpu.VMEM((blk + step_rows, w), BF16),
                        pltpu.VMEM((2 * (blk + step_rows), 2 * w), BF16),
                        pltpu.VMEM((2, 2 * blk, 2 * blk), F32),
                        pltpu.VMEM((n_units, 2 * blk, 2 * blk), F32),
                        pltpu.VMEM((n_units, blk, 4 * blk), BF16),
                        pltpu.VMEM((n_units, 2 * blk, LANES), F32)],
        compiler_params=pltpu.CompilerParams(
            dimension_semantics=("arbitrary", "arbitrary", "arbitrary"),
            vmem_limit_bytes=VMEM_LIMIT_BYTES),
        name=f"attn_d{dilation}",
    )(q, k, k, v, v)


def _mlstm_kernel(qk_ref, v_ref, og_ref, gcol_ref, grow_ref, wc_ref, bc_ref, gn_ref, tri_ref, hm_ref,
                  o_ref, tail_s, c_s, m_s, *, ml_w):
    L = MLSTM_CHUNK
    nh = N_MLSTM_HEADS
    dh = ml_w // nh

    @pl.when(pl.program_id(1) == 0)
    def _():
        tail_s[...] = jnp.zeros_like(tail_s)
        c_s[...] = jnp.zeros_like(c_s)
        m_s[...] = jnp.zeros_like(m_s)

    tri = tri_ref[...]
    ti = lax.broadcasted_iota(jnp.int32, (L, L), 0)
    si = lax.broadcasted_iota(jnp.int32, (L, L), 1)
    causal = si <= ti
    ones = jnp.ones((L, dh), BF16)
    hp = hm_ref[...]
    for bi in range(qk_ref.shape[0]):
        _mlstm_chunk(bi, qk_ref, v_ref, og_ref, gcol_ref, grow_ref, wc_ref, bc_ref, gn_ref, o_ref,
                     tail_s, c_s, m_s, tri, causal, ones, hp, ml_w=ml_w)


def _mlstm_chunk(bi, qk_ref, v_ref, og_ref, gcol_ref, grow_ref, wc_ref, bc_ref, gn_ref, o_ref,
                 tail_s, c_s, m_s, tri, causal, ones, hp, *, ml_w):
    L = MLSTM_CHUNK
    nh = N_MLSTM_HEADS
    dh = ml_w // nh
    x = qk_ref[bi].astype(F32)
    tail = tail_s[bi]
    row8 = lax.broadcasted_iota(jnp.int32, tail.shape, 0)
    acc = bc_ref[...] + x * wc_ref[CONV_WIDTH - 1:CONV_WIDTH, :]
    for back in range(1, CONV_WIDTH):
        rolled = pltpu.roll(x, back, 0)
        top = jnp.where(row8 >= back, rolled[:MOD_ROWS], pltpu.roll(tail, back, 0))
        shifted = jnp.concatenate([top, rolled[MOD_ROWS:]], axis=0)
        acc = acc + shifted * wc_ref[CONV_WIDTH - 1 - back:CONV_WIDTH - back, :]
    tail_s[bi] = x[L - MOD_ROWS:]
    qk = acc * _sigmoid(acc)

    gcol = gcol_ref[bi]
    grow = grow_ref[bi]
    bcol_all = _dot(tri, gcol, precision=lax.Precision.HIGHEST)
    brow_all = _dot_nt(grow, tri, precision=lax.Precision.HIGHEST)

    ks = [qk[:, ml_w + h * dh:ml_w + (h + 1) * dh] * (dh ** -0.5) for h in range(nh)]

    heads = []
    for h in range(nh):
        st = bi * nh + h
        b_c = jnp.broadcast_to(bcol_all[:, nh + h:nh + h + 1], (L, dh))
        i_c = jnp.broadcast_to(gcol[:, h:h + 1], (L, dh))
        b_r = brow_all[nh + h:nh + h + 1, :]
        i_r = grow[h:h + 1, :]
        m_prev = m_s[st, 0:1, :]
        log_d = jnp.where(causal, b_c[:, 0:1] + (i_r - b_r), NEG_INF)
        m_inter = b_c + m_prev
        m_t = jnp.maximum(m_inter, jnp.max(log_d, axis=-1, keepdims=True))
        d_mat = jnp.exp(log_d - m_t[:, 0:1])
        inter = jnp.exp(m_inter - m_t)
        b_last = b_c[L - 1:L, :]
        w_log = b_last - b_c + i_c
        m_new = jnp.maximum(b_last + m_prev, jnp.max(w_log, axis=0, keepdims=True))
        wgt = jnp.exp(w_log - m_new)
        decay = jnp.exp(b_last + m_prev - m_new)
        heads.append((m_t, d_mat, inter, m_new, wgt, decay))

    outs = []
    for h, (m_t, d_mat, inter, _, _, _) in enumerate(heads):
        hcols = slice(h * dh, (h + 1) * dh)
        q = qk[:, hcols].astype(BF16)
        v_ext = jnp.concatenate([v_ref[bi, :, hcols], ones], axis=-1)
        c_prev = c_s[bi * nh + h]
        s_qk = _dot_nt(q, ks[h].astype(BF16)) * d_mat
        ext = jnp.concatenate([inter, inter], axis=-1) * _dot(q, c_prev.astype(BF16)) \
            + _dot(s_qk.astype(BF16), v_ext)
        outs.append(ext[:, :dh] / jnp.maximum(jnp.abs(ext[:, dh:]), jnp.exp(-m_t)))
    hh = jnp.concatenate(outs, axis=-1)
    msq = _dot((hh * hh).astype(BF16), hp)
    hn = hh * lax.rsqrt(msq + RMS_EPS) * gn_ref[...]
    o_ref[bi] = (og_ref[bi].astype(F32) * hn).astype(o_ref.dtype)

    for h, (_, _, _, m_new, wgt, decay) in enumerate(heads):
        st = bi * nh + h
        hcols = slice(h * dh, (h + 1) * dh)
        v_ext = jnp.concatenate([v_ref[bi, :, hcols], ones], axis=-1)
        kw = (ks[h] * wgt).astype(BF16)
        c_s[st] = jnp.concatenate([decay, decay], axis=-1) * c_s[st] + _dot_tn(kw, v_ext)
        m_s[st] = jnp.broadcast_to(m_new, m_s.shape[1:])


def _mlstm_call(qkm, vm, og, gcol, grow, w_conv, b_conv, g_norm, tri, hmean):
    b, s, ml_w = vm.shape
    L = MLSTM_CHUNK
    nh = N_MLSTM_HEADS
    dh = ml_w // nh
    bb = MLSTM_BATCH_ROWS
    assert b % bb == 0
    const = lambda shape: pl.BlockSpec(shape, lambda g, i: (0,) * len(shape))
    tok = lambda w: pl.BlockSpec((bb, L, w), lambda g, i: (g, i, 0))
    return pl.pallas_call(
        functools.partial(_mlstm_kernel, ml_w=ml_w),
        out_shape=jax.ShapeDtypeStruct((b, s, ml_w), BF16),
        grid=(b // bb, s // L),
        in_specs=[tok(2 * ml_w), tok(ml_w), tok(ml_w), tok(GATE_LANES),
                  pl.BlockSpec((bb, 2 * nh, L), lambda g, i: (g, 0, i)),
                  const(w_conv.shape), const((1, 2 * ml_w)), const((1, ml_w)),
                  const((L, L)), const(hmean.shape)],
        out_specs=tok(ml_w),
        scratch_shapes=[pltpu.VMEM((bb, MOD_ROWS, 2 * ml_w), F32),
                        pltpu.VMEM((bb * nh, dh, 2 * dh), F32),
                        pltpu.VMEM((bb * nh, MOD_ROWS, dh), F32)],
        compiler_params=pltpu.CompilerParams(dimension_semantics=("arbitrary", "arbitrary"),
                                             vmem_limit_bytes=VMEM_LIMIT_BYTES),
        name="mlstm",
    )(qkm, vm, og, gcol, grow, w_conv, b_conv, g_norm, tri, hmean)


def _outffn_kernel(x_ref, mod_ref, g_ref, *rest, dilations):
    n_lay = len(dilations)
    o_refs, l_refs = rest[:n_lay], rest[n_lay:2 * n_lay]
    (hm_ref, ex_ref, woa_ref, wom_ref, wg_ref, wu_ref, wd_ref, out_ref,
     operm_s, lperm_s) = rest[2 * n_lay:]
    tm = x_ref.shape[1]

    def natural(ref, d, scratch):
        if d == 1:
            return ref[0].astype(F32)
        n_cg = scratch.shape[0]
        for r in range(d):
            for cg in range(n_cg):
                c0 = (r * n_cg + cg) * LANES
                scratch[cg, pl.ds(r, tm // d, stride=d), :] = ref[0, :, c0:c0 + LANES].astype(F32)
        return jnp.concatenate([scratch[cg] for cg in range(n_cg)], axis=-1)

    lses = [natural(ref, d, lperm_s) for ref, d in zip(l_refs, dilations)]
    mx = functools.reduce(jnp.maximum, lses)
    es = [jnp.exp(l - mx) for l in lses]
    inv = 1.0 / functools.reduce(jnp.add, es)
    ex = ex_ref[...]
    attn = None
    for e, ref, d in zip(es, o_refs, dilations):
        term = _dot((e * inv).astype(BF16), ex) * natural(ref, d, operm_s)
        attn = term if attn is None else attn + term
    y = _dot(attn.astype(BF16), woa_ref[...]) + _dot(hm_ref[0], wom_ref[...])
    x1 = x_ref[0] + mod_ref[0, 2:3, :] * y
    ms = jnp.mean(x1 * x1, axis=-1, keepdims=True)
    hn = x1 * lax.rsqrt(ms + RMS_EPS) * g_ref[...]
    hb = (hn * (1.0 + mod_ref[0, 4:5, :]) + mod_ref[0, 3:4, :]).astype(BF16)
    g = _dot(hb, wg_ref[...])
    u = _dot(hb, wu_ref[...])
    a = (g * _sigmoid(g) * u).astype(BF16)
    out_ref[0] = x1 + mod_ref[0, 5:6, :] * _dot(a, wd_ref[...])


def _outffn_call(x, mod, g_ffn, os_, lses, hm, expand, woa, wom, wg, wu, wd, *, dilations):
    b, s, d = x.shape
    tm = FFN_ROWS
    assert all(tm % (dl * BF16_SUBLANES) == 0 for dl in dilations)
    aw = woa.shape[0]
    const = lambda shape: pl.BlockSpec(shape, lambda bi, i: (0,) * len(shape))
    tok = lambda w: pl.BlockSpec((1, tm, w), lambda bi, i: (bi, i, 0))
    strided = lambda w: tuple(pl.BlockSpec((1, tm // dl, dl * w), lambda bi, i: (bi, i, 0)) for dl in dilations)
    return pl.pallas_call(
        functools.partial(_outffn_kernel, dilations=dilations),
        out_shape=jax.ShapeDtypeStruct((b, s, d), F32),
        grid=(b, s // tm),
        in_specs=[tok(d), pl.BlockSpec((1, N_MOD, d), lambda bi, i: (bi, 0, 0)), const((1, d)),
                  *strided(aw), *strided(LANES), tok(hm.shape[-1]),
                  const(expand.shape), const(woa.shape), const(wom.shape),
                  const(wg.shape), const(wu.shape), const(wd.shape)],
        out_specs=tok(d),
        scratch_shapes=[pltpu.VMEM((aw // LANES, tm, LANES), F32), pltpu.VMEM((1, tm, LANES), F32)],
        compiler_params=pltpu.CompilerParams(dimension_semantics=("arbitrary", "arbitrary"),
                                             vmem_limit_bytes=VMEM_LIMIT_BYTES),
        name="outffn",
    )(x, mod, g_ffn, *os_, *lses, hm, expand, woa, wom, wg, wu, wd)


def _block_diag_mean(width, group):
    idx = jnp.arange(width) // group
    return jnp.where(idx[:, None] == idx[None, :], 1.0 / group, 0.0).astype(BF16)


def kernel(x, c, g_mix, w_in, w_conv, b_conv, b_igate, b_fgate, q_norm_g, k_norm_g, mlstm_norm_g, w_out,
           g_ffn, w_gate, w_up, w_down, w_ada, b_ada):
    b, s, d = x.shape
    depth = g_mix.shape[0]
    attn_w = d // 2
    ml_w = d - attn_w
    nh = N_MLSTM_HEADS
    n_attn_heads = attn_w // ATTN_HEAD_DIM
    scale = ATTN_HEAD_DIM ** -0.5 * LOG2E
    dilations = tuple(dl for _, dl in DILATED_PATTERNS)
    assert b <= MOD_ROWS and 2 * nh <= GATE_LANES

    c_pad = jnp.zeros((MOD_ROWS, d), F32).at[:b].set(c)
    hmean_attn = _block_diag_mean(attn_w, ATTN_HEAD_DIM)
    hmean_ml = _block_diag_mean(ml_w, ml_w // nh)
    stat_lane = jnp.arange(LANES)
    head_of_stat = jnp.where(stat_lane % ATTN_HEAD_DIM < n_attn_heads // 2,
                             2 * (stat_lane % ATTN_HEAD_DIM) + stat_lane // ATTN_HEAD_DIM, -1)
    head_of_col = jnp.arange(attn_w) // ATTN_HEAD_DIM
    expand = (head_of_stat[:, None] == head_of_col[None, :]).astype(BF16)
    tri = jnp.tril(jnp.ones((MLSTM_CHUNK, MLSTM_CHUNK), F32))

    for l in range(depth):
        mod = _mod_call(c_pad, w_ada[l], b_ada[l][None, :])[:b].reshape(b, N_MOD, d)

        n_main = 3 * attn_w + 4 * ml_w
        wa = w_in[l][:, :3 * attn_w].astype(BF16)
        wm = w_in[l][:, 3 * attn_w:n_main].astype(BF16)
        wg_cols = w_in[l][:, n_main:]
        wg = jnp.zeros((d, GATE_LANES), F32).at[:, :2 * nh].set(wg_cols).astype(BF16)
        wgt = jnp.zeros((BF16_SUBLANES, d), F32).at[:2 * nh].set(wg_cols.T).astype(BF16)
        gate_bias = jnp.concatenate([b_igate[l], b_fgate[l]])
        bcol = jnp.zeros((1, GATE_LANES), F32).at[0, :2 * nh].set(gate_bias)
        brow = gate_bias[:, None]
        gq = jnp.tile(q_norm_g[l], n_attn_heads)[None, :]
        gk = jnp.tile(k_norm_g[l], n_attn_heads)[None, :]

        qs, ks, vs, qkm, vm, og, gcol, grow = _inproj_call(
            x, mod, g_mix[l][None, :], wa, wm, wg, wgt, gq, gk, hmean_attn, bcol, brow,
            attn_w=attn_w, ml_w=ml_w, scale=scale, dilations=dilations)

        os_, lses = [], []
        for (window, dilation), q, k, v in zip(DILATED_PATTERNS, qs, ks, vs):
            o, lse = _attn_call(q, k, v, width=attn_w, window=window, dilation=dilation)
            os_.append(o)
            lses.append(lse)

        hm = _mlstm_call(qkm, vm, og, gcol, grow, w_conv[l], b_conv[l][None, :],
                         mlstm_norm_g[l][None, :], tri, hmean_ml)

        x = _outffn_call(x, mod, g_ffn[l][None, :], os_, lses, hm, expand,
                         w_out[l][:attn_w].astype(BF16), w_out[l][attn_w:].astype(BF16),
                         w_gate[l].astype(BF16), w_up[l].astype(BF16), w_down[l].astype(BF16),
                         dilations=dilations)
    return x
```

```python
import functools

import jax
import jax.numpy as jnp
from jax import lax
from jax.experimental import pallas as pl
from jax.experimental.pallas import tpu as pltpu

F32 = jnp.float32
BF16 = jnp.bfloat16

ATTN_HEAD_DIM = 64
N_MLSTM_HEADS = 4
CONV_WIDTH = 4
DILATED_PATTERNS = ((128, 1), (512, 4), (2048, 16))
ATTN_BLOCK = 128
N_MOD = 6
RMS_EPS = 1e-6

LANES = 128
BF16_SUBLANES = 16
VMEM_LIMIT_BYTES = 56 * 1024 * 1024
F32_SUBLANES = 8
MOD_ROWS = F32_SUBLANES
LAYOUT_ROWS = 512
GATE_LANES = 128
INPROJ_ROWS = LAYOUT_ROWS
ATTN_SUB_BLOCKS = 4
MLSTM_CHUNK = 256
MLSTM_BATCH_ROWS = 1
FFN_ROWS = LAYOUT_ROWS

NEG_INF = float("-inf")
LOG2E = 1.4426950408889634
LN2 = 0.6931471805599453


def _dot(a, b, **kw):
    return jnp.dot(a, b, preferred_element_type=F32, **kw)


def _dot_nt(a, b, **kw):
    return lax.dot_general(a, b, (((1,), (1,)), ((), ())), preferred_element_type=F32, **kw)


def _dot_tn(a, b):
    return lax.dot_general(a, b, (((0,), (0,)), ((), ())), preferred_element_type=F32)


def _sigmoid(z):
    return 1.0 / (1.0 + jnp.exp(-z))


def _log_sigmoid(z):
    return jnp.minimum(z, 0.0) - jnp.log1p(jnp.exp(-jnp.abs(z)))


def _strided_shape(b, s, w, dilation):
    return (b, s // dilation, dilation * w)


def _gather_chain(dilations):
    chain, parent = [], 1
    for d in sorted(set(dilations)):
        if d == 1:
            continue
        assert d % parent == 0 and d // parent < F32_SUBLANES
        chain.append((parent, d, d // parent))
        parent = d
    return chain


def _mod_kernel(c_ref, w_ref, b_ref, o_ref):
    c = c_ref[...]
    sc = c * _sigmoid(c)
    o_ref[...] = _dot(sc, w_ref[...], precision=lax.Precision.HIGHEST) + b_ref[...]


def _mod_call(c_pad, w_ada, b_ada):
    d, n = w_ada.shape
    tn = n // 4
    return pl.pallas_call(
        _mod_kernel,
        out_shape=jax.ShapeDtypeStruct((MOD_ROWS, n), F32),
        grid=(n // tn,),
        in_specs=[pl.BlockSpec((MOD_ROWS, d), lambda j: (0, 0)),
                  pl.BlockSpec((d, tn), lambda j: (0, j)),
                  pl.BlockSpec((1, tn), lambda j: (0, j))],
        out_specs=pl.BlockSpec((MOD_ROWS, tn), lambda j: (0, j)),
        compiler_params=pltpu.CompilerParams(dimension_semantics=("arbitrary",),
                                             vmem_limit_bytes=VMEM_LIMIT_BYTES),
        name="mod",
    )(c_pad, w_ada, b_ada)


def _inproj_kernel(x_ref, mod_ref, g_ref, wa_ref, wm_ref, wg_ref, wgt_ref, gq_ref, gk_ref, hp_ref,
                   bcol_ref, brow_ref, *rest, attn_w, ml_w, scale, dilations):
    n_lay = len(dilations)
    q_refs, k_refs, v_refs = rest[:n_lay], rest[n_lay:2 * n_lay], rest[2 * n_lay:3 * n_lay]
    qkm_ref, vm_ref, og_ref, gcol_ref, grow_ref, perm_s = rest[3 * n_lay:]
    tm = x_ref.shape[1]

    def emit(val, refs):
        w = val.shape[-1]
        n_cg = w // LANES
        ref_of = dict(zip(dilations, refs))
        if 1 in ref_of:
            ref_of[1][0] = val.astype(BF16)
        chain = _gather_chain(dilations)
        for cg in range(n_cg):
            perm_s[0, cg] = val[:, cg * LANES:(cg + 1) * LANES]
        for level, (parent, d, f) in enumerate(chain):
            src, dst = perm_s.at[level % 2], perm_s.at[(level + 1) % 2]
            n = tm // d
            for c in range(parent):
                for a in range(f):
                    r = a * parent + c
                    for cg in range(n_cg):
                        rows = src[cg, pl.ds(c * (tm // parent) + a, n, stride=f), :]
                        c0 = r * w + cg * LANES
                        ref_of[d][0, :, c0:c0 + LANES] = rows.astype(BF16)
                        if level + 1 < len(chain):
                            dst[cg, r * n:(r + 1) * n, :] = rows

    x = x_ref[0]
    ms = jnp.mean(x * x, axis=-1, keepdims=True)
    y = x * lax.rsqrt(ms + RMS_EPS) * g_ref[...]
    h = (y * (1.0 + mod_ref[0, 1:2, :]) + mod_ref[0, 0:1, :]).astype(BF16)

    xa = _dot(h, wa_ref[...])
    hp = hp_ref[...]

    def head_norm(t, g):
        msq = _dot((t * t).astype(BF16), hp)
        return t * lax.rsqrt(msq + RMS_EPS) * g

    emit(head_norm(xa[:, :attn_w], gq_ref[...]) * scale, q_refs)
    emit(head_norm(xa[:, attn_w:2 * attn_w], gk_ref[...]), k_refs)
    emit(xa[:, 2 * attn_w:], v_refs)

    xm = _dot(h, wm_ref[...])
    qkm_ref[0] = xm[:, :2 * ml_w].astype(BF16)
    vm_ref[0] = xm[:, 2 * ml_w:3 * ml_w].astype(BF16)
    og_ref[0] = _sigmoid(xm[:, 3 * ml_w:]).astype(BF16)

    nh = N_MLSTM_HEADS
    zc = _dot(h, wg_ref[...]) + bcol_ref[...]
    lane = lax.broadcasted_iota(jnp.int32, zc.shape, 1)
    gcol_ref[0] = jnp.where(lane < nh, zc, jnp.where(lane < 2 * nh, _log_sigmoid(zc), 0.0))
    zr = _dot_nt(wgt_ref[...], h)[:2 * nh] + brow_ref[...]
    row = lax.broadcasted_iota(jnp.int32, zr.shape, 0)
    grow_ref[0] = jnp.where(row < nh, zr, _log_sigmoid(zr))


def _inproj_call(x, mod, g_mix, wa, wm, wg, wgt, gq, gk, hp, bcol, brow, *, attn_w, ml_w, scale, dilations):
    b, s, d = x.shape
    tm = INPROJ_ROWS
    nh2 = 2 * N_MLSTM_HEADS
    assert all(tm % (dl * BF16_SUBLANES) == 0 for dl in dilations)
    const = lambda shape: pl.BlockSpec(shape, lambda bi, i: (0,) * len(shape))
    tok = lambda w: pl.BlockSpec((1, tm, w), lambda bi, i: (bi, i, 0))
    strided_shapes = tuple(jax.ShapeDtypeStruct(_strided_shape(b, s, attn_w, dl), BF16) for dl in dilations)
    strided_specs = tuple(pl.BlockSpec((1, tm // dl, dl * attn_w), lambda bi, i: (bi, i, 0)) for dl in dilations)
    out_shape = strided_shapes * 3 + (
        jax.ShapeDtypeStruct((b, s, 2 * ml_w), BF16),
        jax.ShapeDtypeStruct((b, s, ml_w), BF16),
        jax.ShapeDtypeStruct((b, s, ml_w), BF16),
        jax.ShapeDtypeStruct((b, s, GATE_LANES), F32),
        jax.ShapeDtypeStruct((b, nh2, s), F32))
    outs = pl.pallas_call(
        functools.partial(_inproj_kernel, attn_w=attn_w, ml_w=ml_w, scale=scale, dilations=dilations),
        out_shape=out_shape,
        grid=(b, s // tm),
        in_specs=[tok(d),
                  pl.BlockSpec((1, N_MOD, d), lambda bi, i: (bi, 0, 0)),
                  const((1, d)), const(wa.shape), const(wm.shape), const(wg.shape), const(wgt.shape),
                  const((1, attn_w)), const((1, attn_w)), const(hp.shape),
                  const((1, GATE_LANES)), const((nh2, 1))],
        out_specs=strided_specs * 3 + (tok(2 * ml_w), tok(ml_w), tok(ml_w), tok(GATE_LANES),
                                       pl.BlockSpec((1, nh2, tm), lambda bi, i: (bi, 0, i))),
        scratch_shapes=[pltpu.VMEM((2, attn_w // LANES, tm, LANES), F32)],
        compiler_params=pltpu.CompilerParams(dimension_semantics=("arbitrary", "arbitrary"),
                                             vmem_limit_bytes=VMEM_LIMIT_BYTES),
        name="inproj",
    )(x, mod, g_mix, wa, wm, wg, wgt, gq, gk, hp, bcol, brow)
    n_lay = len(dilations)
    return (outs[:n_lay], outs[n_lay:2 * n_lay], outs[2 * n_lay:3 * n_lay]) + tuple(outs[3 * n_lay:])


def _attn_kernel(q_ref, kp_ref, kc_ref, vp_ref, vc_ref, o_ref, lse_ref, k_s, v_s, bias_s, *, n_back, n_heads):
    blk = ATTN_BLOCK
    hd = ATTN_HEAD_DIM
    n_pairs = n_heads // 2
    n = pl.program_id(2)
    lane = lax.broadcasted_iota(jnp.int32, (blk, LANES), 1)
    first_head = lane < hd
    k_s[0:blk, :] = kp_ref[0]
    k_s[blk:, :] = kc_ref[0]
    ind_a = jnp.where(first_head, 1.0, 0.0).astype(BF16)
    ind_b = jnp.where(first_head, 0.0, 1.0).astype(BF16)
    for kb in range(ATTN_SUB_BLOCKS + 1):
        for j in range(n_pairs):
            cols = slice(j * LANES, (j + 1) * LANES)
            src = vp_ref[0, :, cols] if kb == 0 else vc_ref[0, (kb - 1) * blk:kb * blk, cols]
            ra, rb, c0 = 2 * kb * blk, (2 * kb + 1) * blk, 2 * j * LANES
            v_s[ra:ra + blk, c0:c0 + LANES] = jnp.where(first_head, src, 0)
            v_s[ra:ra + blk, c0 + LANES:c0 + 2 * LANES] = ind_a
            v_s[rb:rb + blk, c0:c0 + LANES] = jnp.where(first_head, 0, src)
            v_s[rb:rb + blk, c0 + LANES:c0 + 2 * LANES] = ind_b

    row = lax.broadcasted_iota(jnp.int32, (2 * blk, 2 * blk), 0) & (blk - 1)
    col = lax.broadcasted_iota(jnp.int32, (2 * blk, 2 * blk), 1)
    band = jnp.logical_and(col >= row + (blk - n_back), col <= row + blk)
    bias_s[0] = jnp.where(band, 0.0, NEG_INF)
    bias_s[1] = jnp.where(jnp.logical_and(band, col >= blk), 0.0, NEG_INF)

    units = [(i, j) for i in range(ATTN_SUB_BLOCKS) for j in range(n_pairs)]
    scores, maxes, probs = [], [], []
    for i, j in units:
        cols = slice(j * LANES, (j + 1) * LANES)
        q = q_ref[0, i * blk:(i + 1) * blk, cols]
        ks = k_s[i * blk:(i + 2) * blk, cols]
        bias = bias_s[jnp.where(n == 0, 1, 0)] if i == 0 else bias_s[0]
        q2 = jnp.concatenate([jnp.where(first_head, q, 0), jnp.where(first_head, 0, q)], axis=0)
        s = _dot_nt(q2, ks) + bias
        scores.append(s)
        maxes.append(jnp.max(s, axis=-1, keepdims=True))
    for s, m in zip(scores, maxes):
        p = jnp.exp2(s - m).astype(BF16)
        probs.append(jnp.concatenate([p[:blk, :blk], p[blk:, :blk], p[:blk, blk:], p[blk:, blk:]], axis=1))
    for i in range(ATTN_SUB_BLOCKS):
        outs = []
        stats = jnp.zeros((blk, LANES), F32)
        for j in range(n_pairs):
            u = i * n_pairs + j
            vw = v_s[2 * i * blk:2 * (i + 2) * blk, 2 * j * LANES:(2 * j + 2) * LANES]
            ov = _dot(probs[u], vw)
            outs.append(ov[:, :LANES] / ov[:, LANES:])
            m = jnp.broadcast_to(maxes[u], (2 * blk, LANES))
            lse = jnp.where(first_head, m[:blk], m[blk:]) * LN2 + jnp.log(ov[:, LANES:])
            stats = jnp.where((lane & (hd - 1)) == j, lse, stats)
        o_ref[0, i * blk:(i + 1) * blk, :] = jnp.concatenate(outs, axis=-1).astype(o_ref.dtype)
        lse_ref[0, i * blk:(i + 1) * blk, :] = stats


def _attn_call(q, k, v, *, width, window, dilation):
    b, ls, _ = q.shape
    w = width
    n_back = window // dilation
    blk = ATTN_BLOCK
    assert n_back <= blk
    step_rows = ATTN_SUB_BLOCKS * blk
    assert ls % step_rows == 0
    n_heads = w // ATTN_HEAD_DIM
    assert n_heads <= LANES and n_heads % 2 == 0 and 2 * ATTN_HEAD_DIM == LANES
    cur =pl.BlockSpec((1, step_rows, w), lambda bi, r, n: (bi, n, r))
    prev = pl.BlockSpec((1, blk, w),
                        lambda bi, r, n: (bi, jnp.maximum(n * ATTN_SUB_BLOCKS - 1, 0), r))
    return pl.pallas_call(
        functools.partial(_attn_kernel, n_back=n_back, n_heads=n_heads),
        out_shape=(jax.ShapeDtypeStruct((b, ls, dilation * w), BF16),
                   jax.ShapeDtypeStruct((b, ls, dilation * LANES), F32)),
        grid=(b, dilation, ls // step_rows),
        in_specs=[cur, prev, cur, prev, cur],
        out_specs=(cur, pl.BlockSpec((1, step_rows, LANES), lambda bi, r, n: (bi, n, r))),
        scratch_shapes=[pltpu.VMEM((blk + step_rows, w), BF16),
                        pltpu.VMEM((2 * (blk + step_rows), 2 * w), BF16),
                        pltpu.VMEM((2, 2 * blk, 2 * blk), F32)],
        compiler_params=pltpu.CompilerParams(
            dimension_semantics=("arbitrary", "arbitrary", "arbitrary"),
            vmem_limit_bytes=VMEM_LIMIT_BYTES),
        name=f"attn_d{dilation}",
    )(q, k, k, v, v)


def _mlstm_kernel(qk_ref, v_ref, og_ref, gcol_ref, grow_ref, wc_ref, bc_ref, gn_ref, tri_ref, hm_ref,
                  o_ref, tail_s, c_s, m_s, *, ml_w):
    L = MLSTM_CHUNK
    nh = N_MLSTM_HEADS
    dh = ml_w // nh

    @pl.when(pl.program_id(1) == 0)
    def _():
        tail_s[...] = jnp.zeros_like(tail_s)
        c_s[...] = jnp.zeros_like(c_s)
        m_s[...] = jnp.zeros_like(m_s)

    tri = tri_ref[...]
    ti = lax.broadcasted_iota(jnp.int32, (L, L), 0)
    si = lax.broadcasted_iota(jnp.int32, (L, L), 1)
    causal = si <= ti
    ones = jnp.ones((L, dh), BF16)
    hp = hm_ref[...]
    for bi in range(qk_ref.shape[0]):
        _mlstm_chunk(bi, qk_ref, v_ref, og_ref, gcol_ref, grow_ref, wc_ref, bc_ref, gn_ref, o_ref,
                     tail_s, c_s, m_s, tri, causal, ones, hp, ml_w=ml_w)


def _mlstm_chunk(bi, qk_ref, v_ref, og_ref, gcol_ref, grow_ref, wc_ref, bc_ref, gn_ref, o_ref,
                 tail_s, c_s, m_s, tri, causal, ones, hp, *, ml_w):
    L = MLSTM_CHUNK
    nh = N_MLSTM_HEADS
    dh = ml_w // nh
    x = qk_ref[bi].astype(F32)
    tail = tail_s[bi]
    row8 = lax.broadcasted_iota(jnp.int32, tail.shape, 0)
    acc = bc_ref[...] + x * wc_ref[CONV_WIDTH - 1:CONV_WIDTH, :]
    for back in range(1, CONV_WIDTH):
        rolled = pltpu.roll(x, back, 0)
        top = jnp.where(row8 >= back, rolled[:MOD_ROWS], pltpu.roll(tail, back, 0))
        shifted = jnp.concatenate([top, rolled[MOD_ROWS:]], axis=0)
        acc = acc + shifted * wc_ref[CONV_WIDTH - 1 - back:CONV_WIDTH - back, :]
    tail_s[bi] = x[L - MOD_ROWS:]
    qk = acc * _sigmoid(acc)

    gcol = gcol_ref[bi]
    grow = grow_ref[bi]
    bcol_all = _dot(tri, gcol, precision=lax.Precision.HIGHEST)
    brow_all = _dot_nt(grow, tri, precision=lax.Precision.HIGHEST)

    ks = [qk[:, ml_w + h * dh:ml_w + (h + 1) * dh] * (dh ** -0.5) for h in range(nh)]

    heads = []
    for h in range(nh):
        st = bi * nh + h
        b_c = jnp.broadcast_to(bcol_all[:, nh + h:nh + h + 1], (L, dh))
        i_c = jnp.broadcast_to(gcol[:, h:h + 1], (L, dh))
        b_r = brow_all[nh + h:nh + h + 1, :]
        i_r = grow[h:h + 1, :]
        m_prev = m_s[st, 0:1, :]
        log_d = jnp.where(causal, b_c[:, 0:1] + (i_r - b_r), NEG_INF)
        m_inter = b_c + m_prev
        m_t = jnp.maximum(m_inter, jnp.max(log_d, axis=-1, keepdims=True))
        d_mat = jnp.exp(log_d - m_t[:, 0:1])
        inter = jnp.exp(m_inter - m_t)
        b_last = b_c[L - 1:L, :]
        w_log = b_last - b_c + i_c
        m_new = jnp.maximum(b_last + m_prev, jnp.max(w_log, axis=0, keepdims=True))
        wgt = jnp.exp(w_log - m_new)
        decay = jnp.exp(b_last + m_prev - m_new)
        heads.append((m_t, d_mat, inter, m_new, wgt, decay))

    outs = []
    for h, (m_t, d_mat, inter, _, _, _) in enumerate(heads):
        hcols = slice(h * dh, (h + 1) * dh)
        q = qk[:, hcols].astype(BF16)
        v_ext = jnp.concatenate([v_ref[bi, :, hcols], ones], axis=-1)
        c_prev = c_s[bi * nh + h]
        s_qk = _dot_nt(q, ks[h].astype(BF16)) * d_mat
        ext = jnp.concatenate([inter, inter], axis=-1) * _dot(q, c_prev.astype(BF16)) \
            + _dot(s_qk.astype(BF16), v_ext)
        outs.append(ext[:, :dh] / jnp.maximum(jnp.abs(ext[:, dh:]), jnp.exp(-m_t)))
    hh = jnp.concatenate(outs, axis=-1)
    msq = _dot((hh * hh).astype(BF16), hp)
    hn = hh * lax.rsqrt(msq + RMS_EPS) * gn_ref[...]
    o_ref[bi] = (og_ref[bi].astype(F32) * hn).astype(o_ref.dtype)

    for h, (_, _, _, m_new, wgt, decay) in enumerate(heads):
        st = bi * nh + h
        hcols = slice(h * dh, (h + 1) * dh)
        v_ext = jnp.concatenate([v_ref[bi, :, hcols], ones], axis=-1)
        kw = (ks[h] * wgt).astype(BF16)
        c_s[st] = jnp.concatenate([decay, decay], axis=-1) * c_s[st] + _dot_tn(kw, v_ext)
        m_s[st] = jnp.broadcast_to(m_new, m_s.shape[1:])


def _mlstm_call(qkm, vm, og, gcol, grow, w_conv, b_conv, g_norm, tri, hmean):
    b, s, ml_w = vm.shape
    L = MLSTM_CHUNK
    nh = N_MLSTM_HEADS
    dh = ml_w // nh
    bb = MLSTM_BATCH_ROWS
    assert b % bb == 0
    const = lambda shape: pl.BlockSpec(shape, lambda g, i: (0,) * len(shape))
    tok = lambda w: pl.BlockSpec((bb, L, w), lambda g, i: (g, i, 0))
    return pl.pallas_call(
        functools.partial(_mlstm_kernel, ml_w=ml_w),
        out_shape=jax.ShapeDtypeStruct((b, s, ml_w), BF16),
        grid=(b // bb, s // L),
        in_specs=[tok(2 * ml_w), tok(ml_w), tok(ml_w), tok(GATE_LANES),
                  pl.BlockSpec((bb, 2 * nh, L), lambda g, i: (g, 0, i)),
                  const(w_conv.shape), const((1, 2 * ml_w)), const((1, ml_w)),
                  const((L, L)), const(hmean.shape)],
        out_specs=tok(ml_w),
        scratch_shapes=[pltpu.VMEM((bb, MOD_ROWS, 2 * ml_w), F32),
                        pltpu.VMEM((bb * nh, dh, 2 * dh), F32),
                        pltpu.VMEM((bb * nh, MOD_ROWS, dh), F32)],
        compiler_params=pltpu.CompilerParams(dimension_semantics=("arbitrary", "arbitrary"),
                                             vmem_limit_bytes=VMEM_LIMIT_BYTES),
        name="mlstm",
    )(qkm, vm, og, gcol, grow, w_conv, b_conv, g_norm, tri, hmean)


def _outffn_kernel(x_ref, mod_ref, g_ref, *rest, dilations):
    n_lay = len(dilations)
    o_refs, l_refs = rest[:n_lay], rest[n_lay:2 * n_lay]
    (hm_ref, ex_ref, woa_ref, wom_ref, wg_ref, wu_ref, wd_ref, out_ref,
     operm_s, lperm_s) = rest[2 * n_lay:]
    tm = x_ref.shape[1]

    def natural(ref, d, scratch):
        if d == 1:
            return ref[0].astype(F32)
        n_cg = scratch.shape[1]
        chain = [step for step in _gather_chain(dilations) if step[1] <= d]
        for level, (parent, dd, f) in enumerate(reversed(chain)):
            dst = scratch.at[level % 2]
            src = scratch.at[(level + 1) % 2]
            n = tm // dd
            for c in range(parent):
                for a in range(f):
                    r = a * parent + c
                    for cg in range(n_cg):
                        if level == 0:
                            c0 = (r * n_cg + cg) * LANES
                            rows = ref[0, :, c0:c0 + LANES].astype(F32)
                        else:
                            rows = src[cg, r * n:(r + 1) * n, :]
                        dst[cg, pl.ds(c * (tm // parent) + a, n, stride=f), :] = rows
        final = scratch.at[(len(chain) - 1) % 2]
        return jnp.concatenate([final[cg] for cg in range(n_cg)], axis=-1)

    lses = [natural(ref, d, lperm_s) for ref, d in zip(l_refs, dilations)]
    mx = functools.reduce(jnp.maximum, lses)
    es = [jnp.exp(l - mx) for l in lses]
    inv = 1.0 / functools.reduce(jnp.add, es)
    ex = ex_ref[...]
    attn = None
    for e, ref, d in zip(es, o_refs, dilations):
        term = _dot((e * inv).astype(BF16), ex) * natural(ref, d, operm_s)
        attn = term if attn is None else attn + term
    y = _dot(attn.astype(BF16), woa_ref[...]) + _dot(hm_ref[0], wom_ref[...])
    x1 = x_ref[0] + mod_ref[0, 2:3, :] * y
    ms = jnp.mean(x1 * x1, axis=-1, keepdims=True)
    hn = x1 * lax.rsqrt(ms + RMS_EPS) * g_ref[...]
    hb = (hn * (1.0 + mod_ref[0, 4:5, :]) + mod_ref[0, 3:4, :]).astype(BF16)
    g = _dot(hb, wg_ref[...])
    u = _dot(hb, wu_ref[...])
    a = (g * _sigmoid(g) * u).astype(BF16)
    out_ref[0] = x1 + mod_ref[0, 5:6, :] * _dot(a, wd_ref[...])


def _outffn_call(x, mod, g_ffn, os_, lses, hm, expand, woa, wom, wg, wu, wd, *, dilations):
    b, s, d = x.shape
    tm = FFN_ROWS
    assert all(tm % (dl * BF16_SUBLANES) == 0 for dl in dilations)
    aw = woa.shape[0]
    const = lambda shape: pl.BlockSpec(shape, lambda bi, i: (0,) * len(shape))
    tok = lambda w: pl.BlockSpec((1, tm, w), lambda bi, i: (bi, i, 0))
    strided = lambda w: tuple(pl.BlockSpec((1, tm // dl, dl * w), lambda bi, i: (bi, i, 0)) for dl in dilations)
    return pl.pallas_call(
        functools.partial(_outffn_kernel, dilations=dilations),
        out_shape=jax.ShapeDtypeStruct((b, s, d), F32),
        grid=(b, s // tm),
        in_specs=[tok(d), pl.BlockSpec((1, N_MOD, d), lambda bi, i: (bi, 0, 0)), const((1, d)),
                  *strided(aw), *strided(LANES), tok(hm.shape[-1]),
                  const(expand.shape), const(woa.shape), const(wom.shape),
                  const(wg.shape), const(wu.shape), const(wd.shape)],
        out_specs=tok(d),
        scratch_shapes=[pltpu.VMEM((2, aw // LANES, tm, LANES), F32), pltpu.VMEM((2, 1, tm, LANES), F32)],
        compiler_params=pltpu.CompilerParams(dimension_semantics=("arbitrary", "arbitrary"),
                                             vmem_limit_bytes=VMEM_LIMIT_BYTES),
        name="outffn",
    )(x, mod, g_ffn, *os_, *lses, hm, expand, woa, wom, wg, wu, wd)


def _block_diag_mean(width, group):
    idx = jnp.arange(width) // group
    return jnp.where(idx[:, None] == idx[None, :], 1.0 / group, 0.0).astype(BF16)


def kernel(x, c, g_mix, w_in, w_conv, b_conv, b_igate, b_fgate, q_norm_g, k_norm_g, mlstm_norm_g, w_out,
           g_ffn, w_gate, w_up, w_down, w_ada, b_ada):
    b, s, d = x.shape
    depth = g_mix.shape[0]
    attn_w = d // 2
    ml_w = d - attn_w
    nh = N_MLSTM_HEADS
    n_attn_heads = attn_w // ATTN_HEAD_DIM
    scale = ATTN_HEAD_DIM ** -0.5 * LOG2E
    dilations = tuple(dl for _, dl in DILATED_PATTERNS)
    assert b <= MOD_ROWS and 2 * nh <= GATE_LANES

    c_pad = jnp.zeros((MOD_ROWS, d), F32).at[:b].set(c)
    hmean_attn = _block_diag_mean(attn_w, ATTN_HEAD_DIM)
    hmean_ml = _block_diag_mean(ml_w, ml_w // nh)
    stat_lane = jnp.arange(LANES)
    head_of_stat = jnp.where(stat_lane % ATTN_HEAD_DIM < n_attn_heads // 2,
                             2 * (stat_lane % ATTN_HEAD_DIM) + stat_lane // ATTN_HEAD_DIM, -1)
    head_of_col = jnp.arange(attn_w) // ATTN_HEAD_DIM
    expand = (head_of_stat[:, None] == head_of_col[None, :]).astype(BF16)
    tri = jnp.tril(jnp.ones((MLSTM_CHUNK, MLSTM_CHUNK), F32))

    for l in range(depth):
        mod = _mod_call(c_pad, w_ada[l], b_ada[l][None, :])[:b].reshape(b, N_MOD, d)

        n_main = 3 * attn_w + 4 * ml_w
        wa = w_in[l][:, :3 * attn_w].astype(BF16)
        wm = w_in[l][:, 3 * attn_w:n_main].astype(BF16)
        wg_cols = w_in[l][:, n_main:]
        wg = jnp.zeros((d, GATE_LANES), F32).at[:, :2 * nh].set(wg_cols).astype(BF16)
        wgt = jnp.zeros((BF16_SUBLANES, d), F32).at[:2 * nh].set(wg_cols.T).astype(BF16)
        gate_bias = jnp.concatenate([b_igate[l], b_fgate[l]])
        bcol = jnp.zeros((1, GATE_LANES), F32).at[0, :2 * nh].set(gate_bias)
        brow = gate_bias[:, None]
        gq = jnp.tile(q_norm_g[l], n_attn_heads)[None, :]
        gk = jnp.tile(k_norm_g[l], n_attn_heads)[None, :]

        qs, ks, vs, qkm, vm, og, gcol, grow = _inproj_call(
            x, mod, g_mix[l][None, :], wa, wm, wg, wgt, gq, gk, hmean_attn, bcol, brow,
            attn_w=attn_w, ml_w=ml_w, scale=scale, dilations=dilations)

        os_, lses = [], []
        for (window, dilation), q, k, v in zip(DILATED_PATTERNS, qs, ks, vs):
            o, lse = _attn_call(q, k, v, width=attn_w, window=window, dilation=dilation)
            os_.append(o)
            lses.append(lse)

        hm = _mlstm_call(qkm, vm, og, gcol, grow, w_conv[l], b_conv[l][None, :],
                         mlstm_norm_g[l][None, :], tri, hmean_ml)

        x = _outffn_call(x, mod, g_ffn[l][None, :], os_, lses, hm, expand,
                         w_out[l][:attn_w].astype(BF16), w_out[l][attn_w:].astype(BF16),
                         w_gate[l].astype(BF16), w_up[l].astype(BF16), w_down[l].astype(BF16),
                         dilations=dilations)
    return x
```

```python
import functools

import jax
import jax.numpy as jnp
from jax import lax
from jax.experimental import pallas as pl
from jax.experimental.pallas import tpu as pltpu

F32 = jnp.float32
BF16 = jnp.bfloat16

ATTN_HEAD_DIM = 64
N_MLSTM_HEADS = 4
CONV_WIDTH = 4
DILATED_PATTERNS = ((128, 1), (512, 4), (2048, 16))
ATTN_BLOCK = 128
N_MOD = 6
RMS_EPS = 1e-6

LANES = 128
BF16_SUBLANES = 16
VMEM_LIMIT_BYTES = 56 * 1024 * 1024
F32_SUBLANES = 8
MOD_ROWS = F32_SUBLANES
LAYOUT_ROWS = 512
GATE_LANES = 128
INPROJ_ROWS = LAYOUT_ROWS
ATTN_SUB_BLOCKS = 4
MLSTM_CHUNK = 256
MLSTM_BATCH_ROWS = 1
FFN_ROWS = LAYOUT_ROWS

NEG_INF = float("-inf")
LOG2E = 1.4426950408889634
LN2 = 0.6931471805599453


def _dot(a, b, **kw):
    return jnp.dot(a, b, preferred_element_type=F32, **kw)


def _dot_nt(a, b, **kw):
    return lax.dot_general(a, b, (((1,), (1,)), ((), ())), preferred_element_type=F32, **kw)


def _dot_tn(a, b):
    return lax.dot_general(a, b, (((0,), (0,)), ((), ())), preferred_element_type=F32)


def _sigmoid(z):
    return 1.0 / (1.0 + jnp.exp(-z))


def _log_sigmoid(z):
    return jnp.minimum(z, 0.0) - jnp.log1p(jnp.exp(-jnp.abs(z)))


def _strided_shape(b, s, w, dilation):
    return (b, s // dilation, dilation * w)


def _gather_chain(dilations):
    chain, parent = [], 1
    for d in sorted(set(dilations)):
        if d == 1:
            continue
        assert d % parent == 0 and d // parent < F32_SUBLANES
        chain.append((parent, d, d // parent))
        parent = d
    return chain


def _mod_kernel(ct_ref, w_ref, b_ref, o_ref, *, n_rows):
    ct = ct_ref[...]
    sc = ct * _sigmoid(ct)
    w = w_ref[...]
    rows = [jnp.sum(w * sc[:, r:r + 1], axis=0, keepdims=True) for r in range(n_rows)]
    rows += [jnp.zeros_like(rows[0])] * (MOD_ROWS - n_rows)
    o_ref[...] = jnp.concatenate(rows, axis=0) + b_ref[...]


def _mod_call(c_t, w_ada, b_ada, *, n_rows):
    d, n = w_ada.shape
    tn = n // 8
    assert tn % LANES == 0
    return pl.pallas_call(
        functools.partial(_mod_kernel, n_rows=n_rows),
        out_shape=jax.ShapeDtypeStruct((MOD_ROWS, n), F32),
        grid=(n // tn,),
        in_specs=[pl.BlockSpec((d, MOD_ROWS), lambda j: (0, 0)),
                  pl.BlockSpec((d, tn), lambda j: (0, j)),
                  pl.BlockSpec((1, tn), lambda j: (0, j))],
        out_specs=pl.BlockSpec((MOD_ROWS, tn), lambda j: (0, j)),
        compiler_params=pltpu.CompilerParams(dimension_semantics=("arbitrary",),
                                             vmem_limit_bytes=VMEM_LIMIT_BYTES),
        name="mod",
    )(c_t, w_ada, b_ada)


def _inproj_kernel(x_ref, mod_ref, g_ref, w_ref, wg_ref, wgt_ref, gq_ref, gk_ref, hp_ref,
                   bcol_ref, brow_ref, *rest, attn_w, ml_w, scale, dilations):
    n_lay = len(dilations)
    q_refs, k_refs, v_refs = rest[:n_lay], rest[n_lay:2 * n_lay], rest[2 * n_lay:3 * n_lay]
    qkm_ref, vm_ref, og_ref, gcol_ref, grow_ref, perm_s = rest[3 * n_lay:]
    tm = x_ref.shape[1]

    def emit(val, refs):
        w = val.shape[-1]
        n_cg = w // LANES
        ref_of = dict(zip(dilations, refs))
        if 1 in ref_of:
            ref_of[1][0] = val.astype(BF16)
        chain = _gather_chain(dilations)
        for cg in range(n_cg):
            perm_s[0, cg] = val[:, cg * LANES:(cg + 1) * LANES]
        for level, (parent, d, f) in enumerate(chain):
            src, dst = perm_s.at[level % 2], perm_s.at[(level + 1) % 2]
            n = tm // d
            for c in range(parent):
                for a in range(f):
                    r = a * parent + c
                    for cg in range(n_cg):
                        rows = src[cg, pl.ds(c * (tm // parent) + a, n, stride=f), :]
                        c0 = r * w + cg * LANES
                        ref_of[d][0, :, c0:c0 + LANES] = rows.astype(BF16)
                        if level + 1 < len(chain):
                            dst[cg, r * n:(r + 1) * n, :] = rows

    x = x_ref[0]
    ms = jnp.mean(x * x, axis=-1, keepdims=True)
    y = x * lax.rsqrt(ms + RMS_EPS) * g_ref[...]
    h = (y * (1.0 + mod_ref[0, 1:2, :]) + mod_ref[0, 0:1, :]).astype(BF16)

    n_attn = 3 * attn_w
    xa = _dot(h, w_ref[:, :n_attn])
    hp = hp_ref[...]

    def head_norm(t, g):
        msq = _dot((t * t).astype(BF16), hp)
        return t * lax.rsqrt(msq + RMS_EPS) * g

    emit(head_norm(xa[:, :attn_w], gq_ref[...]) * scale, q_refs)
    emit(head_norm(xa[:, attn_w:2 * attn_w], gk_ref[...]), k_refs)
    emit(xa[:, 2 * attn_w:], v_refs)

    xm = _dot(h, w_ref[:, n_attn:n_attn + 4 * ml_w])
    qkm_ref[0] = xm[:, :2 * ml_w].astype(BF16)
    vm_ref[0] = xm[:, 2 * ml_w:3 * ml_w].astype(BF16)
    og_ref[0] = _sigmoid(xm[:, 3 * ml_w:]).astype(BF16)

    nh = N_MLSTM_HEADS
    zc = _dot(h, wg_ref[...]) + bcol_ref[...]
    lane = lax.broadcasted_iota(jnp.int32, zc.shape, 1)
    gcol_ref[0] = jnp.where(lane < nh, zc, jnp.where(lane < 2 * nh, _log_sigmoid(zc), 0.0)) * LOG2E
    zr = _dot_nt(wgt_ref[...], h)[:2 * nh] + brow_ref[...]
    row = lax.broadcasted_iota(jnp.int32, zr.shape, 0)
    grow_ref[0] = jnp.where(row < nh, zr, _log_sigmoid(zr)) * LOG2E


def _inproj_call(x, mod, g_mix, w, wg, wgt, gq, gk, hp, bcol, brow, *, attn_w, ml_w, scale, dilations):
    b, s, d = x.shape
    tm = INPROJ_ROWS
    nh2 = 2 * N_MLSTM_HEADS
    assert all(tm % (dl * BF16_SUBLANES) == 0 for dl in dilations)
    const = lambda shape: pl.BlockSpec(shape, lambda bi, i: (0,) * len(shape))
    tok = lambda w: pl.BlockSpec((1, tm, w), lambda bi, i: (bi, i, 0))
    strided_shapes = tuple(jax.ShapeDtypeStruct(_strided_shape(b, s, attn_w, dl), BF16) for dl in dilations)
    strided_specs = tuple(pl.BlockSpec((1, tm // dl, dl * attn_w), lambda bi, i: (bi, i, 0)) for dl in dilations)
    out_shape = strided_shapes * 3 + (
        jax.ShapeDtypeStruct((b, s, 2 * ml_w), BF16),
        jax.ShapeDtypeStruct((b, s, ml_w), BF16),
        jax.ShapeDtypeStruct((b, s, ml_w), BF16),
        jax.ShapeDtypeStruct((b, s, GATE_LANES), F32),
        jax.ShapeDtypeStruct((b, nh2, s), F32))
    outs = pl.pallas_call(
        functools.partial(_inproj_kernel, attn_w=attn_w, ml_w=ml_w, scale=scale, dilations=dilations),
        out_shape=out_shape,
        grid=(b, s // tm),
        in_specs=[tok(d),
                  pl.BlockSpec((1, N_MOD, d), lambda bi, i: (bi, 0, 0)),
                  const((1, d)), const(w.shape), const(wg.shape), const(wgt.shape),
                  const((1, attn_w)), const((1, attn_w)), const(hp.shape),
                  const((1, GATE_LANES)), const((nh2, 1))],
        out_specs=strided_specs * 3 + (tok(2 * ml_w), tok(ml_w), tok(ml_w), tok(GATE_LANES),
                                       pl.BlockSpec((1, nh2, tm), lambda bi, i: (bi, 0, i))),
        scratch_shapes=[pltpu.VMEM((2, attn_w // LANES, tm, LANES), F32)],
        compiler_params=pltpu.CompilerParams(dimension_semantics=("arbitrary", "arbitrary"),
                                             vmem_limit_bytes=VMEM_LIMIT_BYTES),
        name="inproj",
    )(x, mod, g_mix, w, wg, wgt, gq, gk, hp, bcol, brow)
    n_lay = len(dilations)
    return (outs[:n_lay], outs[n_lay:2 * n_lay], outs[2 * n_lay:3 * n_lay]) + tuple(outs[3 * n_lay:])


def _attn_kernel(q_ref, kp_ref, kc_ref, vp_ref, vc_ref, o_ref, lse_ref, k_s, v_s, bias_s, *, n_back, n_heads):
    blk = ATTN_BLOCK
    hd = ATTN_HEAD_DIM
    n_pairs = n_heads // 2
    n = pl.program_id(2)
    lane = lax.broadcasted_iota(jnp.int32, (blk, LANES), 1)
    first_head = lane < hd
    k_s[0:blk, :] = kp_ref[0]
    k_s[blk:, :] = kc_ref[0]
    ind_a = jnp.where(first_head, 1.0, 0.0).astype(BF16)
    ind_b = jnp.where(first_head, 0.0, 1.0).astype(BF16)
    for kb in range(ATTN_SUB_BLOCKS + 1):
        for j in range(n_pairs):
            cols = slice(j * LANES, (j + 1) * LANES)
            src = vp_ref[0, :, cols] if kb == 0 else vc_ref[0, (kb - 1) * blk:kb * blk, cols]
            ra, rb, c0 = 2 * kb * blk, (2 * kb + 1) * blk, 2 * j * LANES
            v_s[ra:ra + blk, c0:c0 + LANES] = jnp.where(first_head, src, 0)
            v_s[ra:ra + blk, c0 + LANES:c0 + 2 * LANES] = ind_a
            v_s[rb:rb + blk, c0:c0 + LANES] = jnp.where(first_head, 0, src)
            v_s[rb:rb + blk, c0 + LANES:c0 + 2 * LANES] = ind_b

    row = lax.broadcasted_iota(jnp.int32, (2 * blk, 2 * blk), 0) & (blk - 1)
    col = lax.broadcasted_iota(jnp.int32, (2 * blk, 2 * blk), 1)
    band = jnp.logical_and(col >= row + (blk - n_back), col <= row + blk)
    bias_s[0] = jnp.where(band, 0.0, NEG_INF)
    bias_s[1] = jnp.where(jnp.logical_and(band, col >= blk), 0.0, NEG_INF)

    units = [(i, j) for i in range(ATTN_SUB_BLOCKS) for j in range(n_pairs)]
    scores, maxes, probs = [], [], []
    for i, j in units:
        cols = slice(j * LANES, (j + 1) * LANES)
        q = q_ref[0, i * blk:(i + 1) * blk, cols]
        ks = k_s[i * blk:(i + 2) * blk, cols]
        bias = bias_s[jnp.where(n == 0, 1, 0)] if i == 0 else bias_s[0]
        q2 = jnp.concatenate([jnp.where(first_head, q, 0), jnp.where(first_head, 0, q)], axis=0)
        s = _dot_nt(q2, ks) + bias
        scores.append(s)
        maxes.append(jnp.max(s, axis=-1, keepdims=True))
    for s, m in zip(scores, maxes):
        p = jnp.exp2(s - m).astype(BF16)
        probs.append(jnp.concatenate([p[:blk, :blk], p[blk:, :blk], p[:blk, blk:], p[blk:, blk:]], axis=1))
    for i in range(ATTN_SUB_BLOCKS):
        outs = []
        stats = jnp.zeros((blk, LANES), F32)
        for j in range(n_pairs):
            u = i * n_pairs + j
            vw = v_s[2 * i * blk:2 * (i + 2) * blk, 2 * j * LANES:(2 * j + 2) * LANES]
            ov = _dot(probs[u], vw)
            outs.append(ov[:, :LANES] / ov[:, LANES:])
            m = jnp.broadcast_to(maxes[u], (2 * blk, LANES))
            lse = jnp.where(first_head, m[:blk], m[blk:]) * LN2 + jnp.log(ov[:, LANES:])
            stats = jnp.where((lane & (hd - 1)) == j, lse, stats)
        o_ref[0, i * blk:(i + 1) * blk, :] = jnp.concatenate(outs, axis=-1).astype(o_ref.dtype)
        lse_ref[0, i * blk:(i + 1) * blk, :] = stats


def _attn_call(q, k, v, *, width, window, dilation):
    b, ls, _ = q.shape
    w = width
    n_back = window // dilation
    blk = ATTN_BLOCK
    assert n_back <= blk
    step_rows = ATTN_SUB_BLOCKS * blk
    assert ls % step_rows == 0
    n_heads = w // ATTN_HEAD_DIM
    assert n_heads <= LANES and n_heads % 2 == 0 and 2 * ATTN_HEAD_DIM == LANES
    cur =pl.BlockSpec((1, step_rows, w), lambda bi, r, n: (bi, n, r))
    prev = pl.BlockSpec((1, blk, w),
                        lambda bi, r, n: (bi, jnp.maximum(n * ATTN_SUB_BLOCKS - 1, 0), r))
    return pl.pallas_call(
        functools.partial(_attn_kernel, n_back=n_back, n_heads=n_heads),
        out_shape=(jax.ShapeDtypeStruct((b, ls, dilation * w), BF16),
                   jax.ShapeDtypeStruct((b, ls, dilation * LANES), F32)),
        grid=(b, dilation, ls // step_rows),
        in_specs=[cur, prev, cur, prev, cur],
        out_specs=(cur, pl.BlockSpec((1, step_rows, LANES), lambda bi, r, n: (bi, n, r))),
        scratch_shapes=[pltpu.VMEM((blk + step_rows, w), BF16),
                        pltpu.VMEM((2 * (blk + step_rows), 2 * w), BF16),
                        pltpu.VMEM((2, 2 * blk, 2 * blk), F32)],
        compiler_params=pltpu.CompilerParams(
            dimension_semantics=("arbitrary", "arbitrary", "arbitrary"),
            vmem_limit_bytes=VMEM_LIMIT_BYTES),
        name=f"attn_d{dilation}",
    )(q, k, k, v, v)


def _mlstm_kernel(qk_ref, v_ref, og_ref, gcol_ref, grow_ref, wc_ref, bc_ref, gn_ref, tri_ref, hm_ref,
                  o_ref, tail_s, c_s, m_s, *, ml_w):
    L = MLSTM_CHUNK
    nh = N_MLSTM_HEADS
    dh = ml_w // nh

    @pl.when(pl.program_id(1) == 0)
    def _():
        tail_s[...] = jnp.zeros_like(tail_s)
        c_s[...] = jnp.zeros_like(c_s)
        m_s[...] = jnp.zeros_like(m_s)

    tri = tri_ref[...]
    ti = lax.broadcasted_iota(jnp.int32, (L, L), 0)
    si = lax.broadcasted_iota(jnp.int32, (L, L), 1)
    causal = si <= ti
    ones = jnp.ones((L, dh), BF16)
    hp = hm_ref[...]
    for bi in range(qk_ref.shape[0]):
        _mlstm_chunk(bi, qk_ref, v_ref, og_ref, gcol_ref, grow_ref, wc_ref, bc_ref, gn_ref, o_ref,
                     tail_s, c_s, m_s, tri, causal, ones, hp, ml_w=ml_w)


def _mlstm_chunk(bi, qk_ref, v_ref, og_ref, gcol_ref, grow_ref, wc_ref, bc_ref, gn_ref, o_ref,
                 tail_s, c_s, m_s, tri, causal, ones, hp, *, ml_w):
    L = MLSTM_CHUNK
    nh = N_MLSTM_HEADS
    dh = ml_w // nh
    x = qk_ref[bi].astype(F32)
    tail = tail_s[bi]
    row8 = lax.broadcasted_iota(jnp.int32, tail.shape, 0)
    acc = bc_ref[...] + x * wc_ref[CONV_WIDTH - 1:CONV_WIDTH, :]
    for back in range(1, CONV_WIDTH):
        rolled = pltpu.roll(x, back, 0)
        top = jnp.where(row8 >= back, rolled[:MOD_ROWS], pltpu.roll(tail, back, 0))
        shifted = jnp.concatenate([top, rolled[MOD_ROWS:]], axis=0)
        acc = acc + shifted * wc_ref[CONV_WIDTH - 1 - back:CONV_WIDTH - back, :]
    tail_s[bi] = x[L - MOD_ROWS:]
    qk = acc * _sigmoid(acc)

    gcol = gcol_ref[bi]
    grow = grow_ref[bi]
    bcol_all = _dot(tri, gcol, precision=lax.Precision.HIGHEST)
    brow_all = _dot_nt(grow, tri, precision=lax.Precision.HIGHEST)

    ks = [qk[:, ml_w + h * dh:ml_w + (h + 1) * dh] * (dh ** -0.5) for h in range(nh)]

    heads = []
    for h in range(nh):
        st = bi * nh + h
        b_c = jnp.broadcast_to(bcol_all[:, nh + h:nh + h + 1], (L, dh))
        i_c = jnp.broadcast_to(gcol[:, h:h + 1], (L, dh))
        b_r = brow_all[nh + h:nh + h + 1, :]
        i_r = grow[h:h + 1, :]
        m_prev = m_s[st, 0:1, :]
        log_d = jnp.where(causal, b_c[:, 0:1] + (i_r - b_r), NEG_INF)
        m_inter = b_c + m_prev
        m_t = jnp.maximum(m_inter, jnp.max(log_d, axis=-1, keepdims=True))
        d_mat = jnp.exp2(log_d - m_t[:, 0:1])
        inter = jnp.exp2(m_inter - m_t)
        b_last = b_c[L - 1:L, :]
        w_log = b_last - b_c + i_c
        m_new = jnp.maximum(b_last + m_prev, jnp.max(w_log, axis=0, keepdims=True))
        wgt = jnp.exp2(w_log - m_new)
        decay = jnp.exp2(b_last + m_prev - m_new)
        heads.append((m_t, d_mat, inter, m_new, wgt, decay))

    outs = []
    for h, (m_t, d_mat, inter, _, _, _) in enumerate(heads):
        hcols = slice(h * dh, (h + 1) * dh)
        q = qk[:, hcols].astype(BF16)
        v_ext = jnp.concatenate([v_ref[bi, :, hcols], ones], axis=-1)
        c_prev = c_s[bi * nh + h]
        s_qk = _dot_nt(q, ks[h].astype(BF16)) * d_mat
        ext = jnp.concatenate([inter, inter], axis=-1) * _dot(q, c_prev.astype(BF16)) \
            + _dot(s_qk.astype(BF16), v_ext)
        outs.append(ext[:, :dh] / jnp.maximum(jnp.abs(ext[:, dh:]), jnp.exp2(-m_t)))
    hh = jnp.concatenate(outs, axis=-1)
    msq = _dot((hh * hh).astype(BF16), hp)
    hn = hh * lax.rsqrt(msq + RMS_EPS) * gn_ref[...]
    o_ref[bi] = (og_ref[bi].astype(F32) * hn).astype(o_ref.dtype)

    for h, (_, _, _, m_new, wgt, decay) in enumerate(heads):
        st = bi * nh + h
        hcols = slice(h * dh, (h + 1) * dh)
        v_ext = jnp.concatenate([v_ref[bi, :, hcols], ones], axis=-1)
        kw = (ks[h] * wgt).astype(BF16)
        c_s[st] = jnp.concatenate([decay, decay], axis=-1) * c_s[st] + _dot_tn(kw, v_ext)
        m_s[st] = jnp.broadcast_to(m_new, m_s.shape[1:])


def _mlstm_call(qkm, vm, og, gcol, grow, w_conv, b_conv, g_norm, tri, hmean):
    b, s, ml_w = vm.shape
    L = MLSTM_CHUNK
    nh = N_MLSTM_HEADS
    dh = ml_w // nh
    bb = MLSTM_BATCH_ROWS
    assert b % bb == 0
    const = lambda shape: pl.BlockSpec(shape, lambda g, i: (0,) * len(shape))
    tok = lambda w: pl.BlockSpec((bb, L, w), lambda g, i: (g, i, 0))
    return pl.pallas_call(
        functools.partial(_mlstm_kernel, ml_w=ml_w),
        out_shape=jax.ShapeDtypeStruct((b, s, ml_w), BF16),
        grid=(b // bb, s // L),
        in_specs=[tok(2 * ml_w), tok(ml_w), tok(ml_w), tok(GATE_LANES),
                  pl.BlockSpec((bb, 2 * nh, L), lambda g, i: (g, 0, i)),
                  const(w_conv.shape), const((1, 2 * ml_w)), const((1, ml_w)),
                  const((L, L)), const(hmean.shape)],
        out_specs=tok(ml_w),
        scratch_shapes=[pltpu.VMEM((bb, MOD_ROWS, 2 * ml_w), F32),
                        pltpu.VMEM((bb * nh, dh, 2 * dh), F32),
                        pltpu.VMEM((bb * nh, MOD_ROWS, dh), F32)],
        compiler_params=pltpu.CompilerParams(dimension_semantics=("arbitrary", "arbitrary"),
                                             vmem_limit_bytes=VMEM_LIMIT_BYTES),
        name="mlstm",
    )(qkm, vm, og, gcol, grow, w_conv, b_conv, g_norm, tri, hmean)


def _outffn_kernel(x_ref, mod_ref, g_ref, *rest, dilations):
    n_lay = len(dilations)
    o_refs, l_refs = rest[:n_lay], rest[n_lay:2 * n_lay]
    (hm_ref, ex_ref, wo_ref, wg_ref, wu_ref, wd_ref, out_ref,
     operm_s, lperm_s) = rest[2 * n_lay:]
    tm = x_ref.shape[1]

    def natural(ref, d, scratch):
        if d == 1:
            return ref[0].astype(F32)
        n_cg = scratch.shape[1]
        chain = [step for step in _gather_chain(dilations) if step[1] <= d]
        for level, (parent, dd, f) in enumerate(reversed(chain)):
            dst = scratch.at[level % 2]
            src = scratch.at[(level + 1) % 2]
            n = tm // dd
            for c in range(parent):
                for a in range(f):
                    r = a * parent + c
                    for cg in range(n_cg):
                        if level == 0:
                            c0 = (r * n_cg + cg) * LANES
                            rows = ref[0, :, c0:c0 + LANES].astype(F32)
                        else:
                            rows = src[cg, r * n:(r + 1) * n, :]
                        dst[cg, pl.ds(c * (tm // parent) + a, n, stride=f), :] = rows
        final = scratch.at[(len(chain) - 1) % 2]
        return jnp.concatenate([final[cg] for cg in range(n_cg)], axis=-1)

    lses = [natural(ref, d, lperm_s) for ref, d in zip(l_refs, dilations)]
    mx = functools.reduce(jnp.maximum, lses)
    es = [jnp.exp(l - mx) for l in lses]
    inv = 1.0 / functools.reduce(jnp.add, es)
    ex = ex_ref[...]
    attn = None
    for e, ref, d in zip(es, o_refs, dilations):
        term = _dot((e * inv).astype(BF16), ex) * natural(ref, d, operm_s)
        attn = term if attn is None else attn + term
    aw = attn.shape[-1]
    y = _dot(attn.astype(BF16), wo_ref[:aw, :]) + _dot(hm_ref[0], wo_ref[aw:, :])
    x1 = x_ref[0] + mod_ref[0, 2:3, :] * y
    ms = jnp.mean(x1 * x1, axis=-1, keepdims=True)
    hn = x1 * lax.rsqrt(ms + RMS_EPS) * g_ref[...]
    hb = (hn * (1.0 + mod_ref[0, 4:5, :]) + mod_ref[0, 3:4, :]).astype(BF16)
    g = _dot(hb, wg_ref[...])
    u = _dot(hb, wu_ref[...])
    a = (g * _sigmoid(g) * u).astype(BF16)
    out_ref[0] = x1 + mod_ref[0, 5:6, :] * _dot(a, wd_ref[...])


def _outffn_call(x, mod, g_ffn, os_, lses, hm, expand, wo, wg, wu, wd, *, dilations):
    b, s, d = x.shape
    tm = FFN_ROWS
    assert all(tm % (dl * BF16_SUBLANES) == 0 for dl in dilations)
    aw = expand.shape[1]
    const = lambda shape: pl.BlockSpec(shape, lambda bi, i: (0,) * len(shape))
    tok = lambda w: pl.BlockSpec((1, tm, w), lambda bi, i: (bi, i, 0))
    strided = lambda w: tuple(pl.BlockSpec((1, tm // dl, dl * w), lambda bi, i: (bi, i, 0)) for dl in dilations)
    return pl.pallas_call(
        functools.partial(_outffn_kernel, dilations=dilations),
        out_shape=jax.ShapeDtypeStruct((b, s, d), F32),
        grid=(b, s // tm),
        in_specs=[tok(d), pl.BlockSpec((1, N_MOD, d), lambda bi, i: (bi, 0, 0)), const((1, d)),
                  *strided(aw), *strided(LANES), tok(hm.shape[-1]),
                  const(expand.shape), const(wo.shape),
                  const(wg.shape), const(wu.shape), const(wd.shape)],
        out_specs=tok(d),
        scratch_shapes=[pltpu.VMEM((2, aw // LANES, tm, LANES), F32), pltpu.VMEM((2, 1, tm, LANES), F32)],
        compiler_params=pltpu.CompilerParams(dimension_semantics=("arbitrary", "arbitrary"),
                                             vmem_limit_bytes=VMEM_LIMIT_BYTES),
        name="outffn",
    )(x, mod, g_ffn, *os_, *lses, hm, expand, wo, wg, wu, wd)


def _block_diag_mean(width, group):
    idx = jnp.arange(width) // group
    return jnp.where(idx[:, None] == idx[None, :], 1.0 / group, 0.0).astype(BF16)


def kernel(x, c, g_mix, w_in, w_conv, b_conv, b_igate, b_fgate, q_norm_g, k_norm_g, mlstm_norm_g, w_out,
           g_ffn, w_gate, w_up, w_down, w_ada, b_ada):
    b, s, d = x.shape
    depth = g_mix.shape[0]
    attn_w = d // 2
    ml_w = d - attn_w
    nh = N_MLSTM_HEADS
    n_attn_heads = attn_w // ATTN_HEAD_DIM
    scale = ATTN_HEAD_DIM ** -0.5 * LOG2E
    dilations = tuple(dl for _, dl in DILATED_PATTERNS)
    assert b <= MOD_ROWS and 2 * nh <= GATE_LANES

    c_t = jnp.zeros((d, MOD_ROWS), F32).at[:, :b].set(c.T)
    hmean_attn = _block_diag_mean(attn_w, ATTN_HEAD_DIM)
    hmean_ml = _block_diag_mean(ml_w, ml_w // nh)
    stat_lane = jnp.arange(LANES)
    head_of_stat = jnp.where(stat_lane % ATTN_HEAD_DIM < n_attn_heads // 2,
                             2 * (stat_lane % ATTN_HEAD_DIM) + stat_lane // ATTN_HEAD_DIM, -1)
    head_of_col = jnp.arange(attn_w) // ATTN_HEAD_DIM
    expand = (head_of_stat[:, None] == head_of_col[None, :]).astype(BF16)
    tri = jnp.tril(jnp.ones((MLSTM_CHUNK, MLSTM_CHUNK), F32))

    for l in range(depth):
        mod = _mod_call(c_t, w_ada[l], b_ada[l][None, :], n_rows=b)[:b].reshape(b, N_MOD, d)

        n_main = 3 * attn_w + 4 * ml_w
        w_in_bf = w_in[l].astype(BF16)
        wg_cols = w_in[l][:, n_main:]
        wg = jnp.zeros((d, GATE_LANES), F32).at[:, :2 * nh].set(wg_cols).astype(BF16)
        wgt = jnp.zeros((BF16_SUBLANES, d), F32).at[:2 * nh].set(wg_cols.T).astype(BF16)
        gate_bias = jnp.concatenate([b_igate[l], b_fgate[l]])
        bcol = jnp.zeros((1, GATE_LANES), F32).at[0, :2 * nh].set(gate_bias)
        brow = gate_bias[:, None]
        gq = jnp.tile(q_norm_g[l], n_attn_heads)[None, :]
        gk = jnp.tile(k_norm_g[l], n_attn_heads)[None, :]

        qs, ks, vs, qkm, vm, og, gcol, grow = _inproj_call(
            x, mod, g_mix[l][None, :], w_in_bf, wg, wgt, gq, gk, hmean_attn, bcol, brow,
            attn_w=attn_w, ml_w=ml_w, scale=scale, dilations=dilations)

        os_, lses = [], []
        for (window, dilation), q, k, v in zip(DILATED_PATTERNS, qs, ks, vs):
            o, lse = _attn_call(q, k, v, width=attn_w, window=window, dilation=dilation)
            os_.append(o)
            lses.append(lse)

        hm = _mlstm_call(qkm, vm, og, gcol, grow, w_conv[l], b_conv[l][None, :],
                         mlstm_norm_g[l][None, :], tri, hmean_ml)

        x = _outffn_call(x, mod, g_ffn[l][None, :], os_, lses, hm, expand,
                         w_out[l].astype(BF16), w_gate[l].astype(BF16), w_up[l].astype(BF16), w_down[l].astype(BF16),
                         dilations=dilations)
    return x
```

```python
import functools

import jax
import jax.numpy as jnp
from jax import lax
from jax.experimental import pallas as pl
from jax.experimental.pallas import tpu as pltpu

F32 = jnp.float32
BF16 = jnp.bfloat16

ATTN_HEAD_DIM = 64
N_MLSTM_HEADS = 4
CONV_WIDTH = 4
DILATED_PATTERNS = ((128, 1), (512, 4), (2048, 16))
ATTN_BLOCK = 128
N_MOD = 6
RMS_EPS = 1e-6

LANES = 128
BF16_SUBLANES = 16
VMEM_LIMIT_BYTES = 56 * 1024 * 1024
F32_SUBLANES = 8
MOD_ROWS = F32_SUBLANES
LAYOUT_ROWS = 512
GATE_LANES = 128
INPROJ_ROWS = LAYOUT_ROWS
ATTN_MAX_SUB_BLOCKS = 8
MLSTM_CHUNK = 256
MLSTM_BATCH_ROWS = 1
FFN_ROWS = LAYOUT_ROWS

NEG_INF = float("-inf")
LOG2E = 1.4426950408889634
LN2 = 0.6931471805599453


def _dot(a, b, **kw):
    return jnp.dot(a, b, preferred_element_type=F32, **kw)


def _dot_nt(a, b, **kw):
    return lax.dot_general(a, b, (((1,), (1,)), ((), ())), preferred_element_type=F32, **kw)


def _dot_tn(a, b):
    return lax.dot_general(a, b, (((0,), (0,)), ((), ())), preferred_element_type=F32)


def _sigmoid(z):
    return 1.0 / (1.0 + jnp.exp(-z))


def _log_sigmoid(z):
    return jnp.minimum(z, 0.0) - jnp.log1p(jnp.exp(-jnp.abs(z)))


def _strided_shape(b, s, w, dilation):
    return (b, s // dilation, dilation * w)


def _gather_chain(dilations):
    chain, parent = [], 1
    for d in sorted(set(dilations)):
        if d == 1:
            continue
        assert d % parent == 0 and d // parent < F32_SUBLANES
        chain.append((parent, d, d // parent))
        parent = d
    return chain


def _mod_kernel(ct_ref, w_ref, b_ref, o_ref, *, n_rows):
    ct = ct_ref[...]
    sc = ct * _sigmoid(ct)
    w = w_ref[...]
    rows = [jnp.sum(w * sc[:, r:r + 1], axis=0, keepdims=True) for r in range(n_rows)]
    rows += [jnp.zeros_like(rows[0])] * (MOD_ROWS - n_rows)
    o_ref[...] = jnp.concatenate(rows, axis=0) + b_ref[...]


def _mod_call(c_t, w_ada, b_ada, *, n_rows):
    d, n = w_ada.shape
    tn = n // 4
    assert tn % LANES == 0
    return pl.pallas_call(
        functools.partial(_mod_kernel, n_rows=n_rows),
        out_shape=jax.ShapeDtypeStruct((MOD_ROWS, n), F32),
        grid=(n // tn,),
        in_specs=[pl.BlockSpec((d, MOD_ROWS), lambda j: (0, 0)),
                  pl.BlockSpec((d, tn), lambda j: (0, j)),
                  pl.BlockSpec((1, tn), lambda j: (0, j))],
        out_specs=pl.BlockSpec((MOD_ROWS, tn), lambda j: (0, j)),
        compiler_params=pltpu.CompilerParams(dimension_semantics=("arbitrary",),
                                             vmem_limit_bytes=VMEM_LIMIT_BYTES),
        name="mod",
    )(c_t, w_ada, b_ada)


def _inproj_kernel(x_ref, mod_ref, g_ref, w_ref, wg_ref, wgt_ref, gq_ref, gk_ref, hp_ref,
                   bcol_ref, brow_ref, *rest, attn_w, ml_w, scale, dilations):
    n_lay = len(dilations)
    q_refs, k_refs, v_refs = rest[:n_lay], rest[n_lay:2 * n_lay], rest[2 * n_lay:3 * n_lay]
    qkm_ref, vm_ref, og_ref, gcol_ref, grow_ref, perm_s = rest[3 * n_lay:]
    tm = x_ref.shape[1]

    def emit(val, refs):
        w = val.shape[-1]
        n_cg = w // LANES
        ref_of = dict(zip(dilations, refs))
        if 1 in ref_of:
            ref_of[1][0] = val.astype(BF16)
        chain = _gather_chain(dilations)
        for cg in range(n_cg):
            perm_s[0, cg] = val[:, cg * LANES:(cg + 1) * LANES]
        for level, (parent, d, f) in enumerate(chain):
            src, dst = perm_s.at[level % 2], perm_s.at[(level + 1) % 2]
            n = tm // d
            for c in range(parent):
                for a in range(f):
                    r = a * parent + c
                    for cg in range(n_cg):
                        rows = src[cg, pl.ds(c * (tm // parent) + a, n, stride=f), :]
                        c0 = r * w + cg * LANES
                        ref_of[d][0, :, c0:c0 + LANES] = rows.astype(BF16)
                        if level + 1 < len(chain):
                            dst[cg, r * n:(r + 1) * n, :] = rows

    x = x_ref[0]
    ms = jnp.mean(x * x, axis=-1, keepdims=True)
    y = x * lax.rsqrt(ms + RMS_EPS) * g_ref[...]
    h = (y * (1.0 + mod_ref[0, 1:2, :]) + mod_ref[0, 0:1, :]).astype(BF16)

    n_attn = 3 * attn_w
    xa = _dot(h, w_ref[:, :n_attn])
    hp = hp_ref[...]

    def head_norm(t, g):
        msq = _dot((t * t).astype(BF16), hp)
        return t * lax.rsqrt(msq + RMS_EPS) * g

    emit(head_norm(xa[:, :attn_w], gq_ref[...]) * scale, q_refs)
    emit(head_norm(xa[:, attn_w:2 * attn_w], gk_ref[...]), k_refs)
    emit(xa[:, 2 * attn_w:], v_refs)

    xm = _dot(h, w_ref[:, n_attn:n_attn + 4 * ml_w])
    qkm_ref[0] = xm[:, :2 * ml_w].astype(BF16)
    vm_ref[0] = xm[:, 2 * ml_w:3 * ml_w].astype(BF16)
    og_ref[0] = _sigmoid(xm[:, 3 * ml_w:]).astype(BF16)

    nh = N_MLSTM_HEADS
    zc = _dot(h, wg_ref[...]) + bcol_ref[...]
    lane = lax.broadcasted_iota(jnp.int32, zc.shape, 1)
    gcol_ref[0] = jnp.where(lane < nh, zc, jnp.where(lane < 2 * nh, _log_sigmoid(zc), 0.0)) * LOG2E
    zr = _dot_nt(wgt_ref[...], h)[:2 * nh] + brow_ref[...]
    row = lax.broadcasted_iota(jnp.int32, zr.shape, 0)
    grow_ref[0] = jnp.where(row < nh, zr, _log_sigmoid(zr)) * LOG2E


def _inproj_call(x, mod, g_mix, w, wg, wgt, gq, gk, hp, bcol, brow, *, attn_w, ml_w, scale, dilations):
    b, s, d = x.shape
    tm = INPROJ_ROWS
    nh2 = 2 * N_MLSTM_HEADS
    assert all(tm % (dl * BF16_SUBLANES) == 0 for dl in dilations)
    const = lambda shape: pl.BlockSpec(shape, lambda bi, i: (0,) * len(shape))
    tok = lambda w: pl.BlockSpec((1, tm, w), lambda bi, i: (bi, i, 0))
    strided_shapes = tuple(jax.ShapeDtypeStruct(_strided_shape(b, s, attn_w, dl), BF16) for dl in dilations)
    strided_specs = tuple(pl.BlockSpec((1, tm // dl, dl * attn_w), lambda bi, i: (bi, i, 0)) for dl in dilations)
    out_shape = strided_shapes * 3 + (
        jax.ShapeDtypeStruct((b, s, 2 * ml_w), BF16),
        jax.ShapeDtypeStruct((b, s, ml_w), BF16),
        jax.ShapeDtypeStruct((b, s, ml_w), BF16),
        jax.ShapeDtypeStruct((b, s, GATE_LANES), F32),
        jax.ShapeDtypeStruct((b, nh2, s), F32))
    outs = pl.pallas_call(
        functools.partial(_inproj_kernel, attn_w=attn_w, ml_w=ml_w, scale=scale, dilations=dilations),
        out_shape=out_shape,
        grid=(b, s // tm),
        in_specs=[tok(d),
                  pl.BlockSpec((1, N_MOD, d), lambda bi, i: (bi, 0, 0)),
                  const((1, d)), const(w.shape), const(wg.shape), const(wgt.shape),
                  const((1, attn_w)), const((1, attn_w)), const(hp.shape),
                  const((1, GATE_LANES)), const((nh2, 1))],
        out_specs=strided_specs * 3 + (tok(2 * ml_w), tok(ml_w), tok(ml_w), tok(GATE_LANES),
                                       pl.BlockSpec((1, nh2, tm), lambda bi, i: (bi, 0, i))),
        scratch_shapes=[pltpu.VMEM((2, attn_w // LANES, tm, LANES), F32)],
        compiler_params=pltpu.CompilerParams(dimension_semantics=("arbitrary", "arbitrary"),
                                             vmem_limit_bytes=VMEM_LIMIT_BYTES),
        name="inproj",
    )(x, mod, g_mix, w, wg, wgt, gq, gk, hp, bcol, brow)
    n_lay = len(dilations)
    return (outs[:n_lay], outs[n_lay:2 * n_lay], outs[2 * n_lay:3 * n_lay]) + tuple(outs[3 * n_lay:])


def _attn_kernel(q_ref, kp_ref, kc_ref, vp_ref, vc_ref, o_ref, lse_ref, k_s, v_s, bias_s, *, n_back, n_heads):
    blk = ATTN_BLOCK
    sub_blocks = q_ref.shape[1] // blk
    hd = ATTN_HEAD_DIM
    n_pairs = n_heads // 2
    n = pl.program_id(2)
    lane = lax.broadcasted_iota(jnp.int32, (blk, LANES), 1)
    first_head = lane < hd
    k_s[0:blk, :] = kp_ref[0]
    k_s[blk:, :] = kc_ref[0]
    ind_a = jnp.where(first_head, 1.0, 0.0).astype(BF16)
    ind_b = jnp.where(first_head, 0.0, 1.0).astype(BF16)
    for kb in range(sub_blocks + 1):
        for j in range(n_pairs):
            cols = slice(j * LANES, (j + 1) * LANES)
            src = vp_ref[0, :, cols] if kb == 0 else vc_ref[0, (kb - 1) * blk:kb * blk, cols]
            ra, rb, c0 = 2 * kb * blk, (2 * kb + 1) * blk, 2 * j * LANES
            v_s[ra:ra + blk, c0:c0 + LANES] = jnp.where(first_head, src, 0)
            v_s[ra:ra + blk, c0 + LANES:c0 + 2 * LANES] = ind_a
            v_s[rb:rb + blk, c0:c0 + LANES] = jnp.where(first_head, 0, src)
            v_s[rb:rb + blk, c0 + LANES:c0 + 2 * LANES] = ind_b

    row = lax.broadcasted_iota(jnp.int32, (2 * blk, 2 * blk), 0) & (blk - 1)
    col = lax.broadcasted_iota(jnp.int32, (2 * blk, 2 * blk), 1)
    band = jnp.logical_and(col >= row + (blk - n_back), col <= row + blk)
    bias_s[0] = jnp.where(band, 0.0, NEG_INF)
    bias_s[1] = jnp.where(jnp.logical_and(band, col >= blk), 0.0, NEG_INF)

    units = [(i, j) for i in range(sub_blocks) for j in range(n_pairs)]
    scores, maxes, probs = [], [], []
    for i, j in units:
        cols = slice(j * LANES, (j + 1) * LANES)
        q = q_ref[0, i * blk:(i + 1) * blk, cols]
        ks = k_s[i * blk:(i + 2) * blk, cols]
        bias = bias_s[jnp.where(n == 0, 1, 0)] if i == 0 else bias_s[0]
        q2 = jnp.concatenate([jnp.where(first_head, q, 0), jnp.where(first_head, 0, q)], axis=0)
        s = _dot_nt(q2, ks) + bias
        scores.append(s)
        maxes.append(jnp.max(s, axis=-1, keepdims=True))
    for s, m in zip(scores, maxes):
        p = jnp.exp2(s - m).astype(BF16)
        probs.append(jnp.concatenate([p[:blk, :blk], p[blk:, :blk], p[:blk, blk:], p[blk:, blk:]], axis=1))
    for i in range(sub_blocks):
        outs = []
        stats = jnp.zeros((blk, LANES), F32)
        for j in range(n_pairs):
            u = i * n_pairs + j
            vw = v_s[2 * i * blk:2 * (i + 2) * blk, 2 * j * LANES:(2 * j + 2) * LANES]
            ov = _dot(probs[u], vw)
            outs.append(ov[:, :LANES] / ov[:, LANES:])
            m = jnp.broadcast_to(maxes[u], (2 * blk, LANES))
            lse = jnp.where(first_head, m[:blk], m[blk:]) * LN2 + jnp.log(ov[:, LANES:])
            stats = jnp.where((lane & (hd - 1)) == j, lse, stats)
        o_ref[0, i * blk:(i + 1) * blk, :] = jnp.concatenate(outs, axis=-1).astype(o_ref.dtype)
        lse_ref[0, i * blk:(i + 1) * blk, :] = stats


def _attn_call(q, k, v, *, width, window, dilation):
    b, ls, _ = q.shape
    w = width
    n_back = window // dilation
    blk = ATTN_BLOCK
    assert n_back <= blk
    sub_blocks = min(ATTN_MAX_SUB_BLOCKS, ls // blk)
    step_rows = sub_blocks * blk
    assert ls % step_rows == 0
    n_heads = w // ATTN_HEAD_DIM
    assert n_heads <= LANES and n_heads % 2 == 0 and 2 * ATTN_HEAD_DIM == LANES
    cur =pl.BlockSpec((1, step_rows, w), lambda bi, r, n: (bi, n, r))
    prev = pl.BlockSpec((1, blk, w),
                        lambda bi, r, n: (bi, jnp.maximum(n * sub_blocks - 1, 0), r))
    return pl.pallas_call(
        functools.partial(_attn_kernel, n_back=n_back, n_heads=n_heads),
        out_shape=(jax.ShapeDtypeStruct((b, ls, dilation * w), BF16),
                   jax.ShapeDtypeStruct((b, ls, dilation * LANES), F32)),
        grid=(b, dilation, ls // step_rows),
        in_specs=[cur, prev, cur, prev, cur],
        out_specs=(cur, pl.BlockSpec((1, step_rows, LANES), lambda bi, r, n: (bi, n, r))),
        scratch_shapes=[pltpu.VMEM((blk + step_rows, w), BF16),
                        pltpu.VMEM((2 * (blk + step_rows), 2 * w), BF16),
                        pltpu.VMEM((2, 2 * blk, 2 * blk), F32)],
        compiler_params=pltpu.CompilerParams(
            dimension_semantics=("arbitrary", "arbitrary", "arbitrary"),
            vmem_limit_bytes=VMEM_LIMIT_BYTES),
        name=f"attn_d{dilation}",
    )(q, k, k, v, v)


def _mlstm_kernel(qk_ref, v_ref, og_ref, gcol_ref, grow_ref, wc_ref, bc_ref, gn_ref, tri_ref, hm_ref,
                  o_ref, tail_s, c_s, m_s, *, ml_w):
    L = MLSTM_CHUNK
    nh = N_MLSTM_HEADS
    dh = ml_w // nh

    @pl.when(pl.program_id(1) == 0)
    def _():
        tail_s[...] = jnp.zeros_like(tail_s)
        c_s[...] = jnp.zeros_like(c_s)
        m_s[...] = jnp.zeros_like(m_s)

    tri = tri_ref[...]
    ti = lax.broadcasted_iota(jnp.int32, (L, L), 0)
    si = lax.broadcasted_iota(jnp.int32, (L, L), 1)
    causal = si <= ti
    ones = jnp.ones((L, dh), BF16)
    hp = hm_ref[...]
    for bi in range(qk_ref.shape[0]):
        _mlstm_chunk(bi, qk_ref, v_ref, og_ref, gcol_ref, grow_ref, wc_ref, bc_ref, gn_ref, o_ref,
                     tail_s, c_s, m_s, tri, causal, ones, hp, ml_w=ml_w)


def _mlstm_chunk(bi, qk_ref, v_ref, og_ref, gcol_ref, grow_ref, wc_ref, bc_ref, gn_ref, o_ref,
                 tail_s, c_s, m_s, tri, causal, ones, hp, *, ml_w):
    L = MLSTM_CHUNK
    nh = N_MLSTM_HEADS
    dh = ml_w // nh
    x = qk_ref[bi].astype(F32)
    tail = tail_s[bi]
    row8 = lax.broadcasted_iota(jnp.int32, tail.shape, 0)
    acc = bc_ref[...] + x * wc_ref[CONV_WIDTH - 1:CONV_WIDTH, :]
    for back in range(1, CONV_WIDTH):
        rolled = pltpu.roll(x, back, 0)
        top = jnp.where(row8 >= back, rolled[:MOD_ROWS], pltpu.roll(tail, back, 0))
        shifted = jnp.concatenate([top, rolled[MOD_ROWS:]], axis=0)
        acc = acc + shifted * wc_ref[CONV_WIDTH - 1 - back:CONV_WIDTH - back, :]
    tail_s[bi] = x[L - MOD_ROWS:]
    qk = acc * _sigmoid(acc)

    gcol = gcol_ref[bi]
    grow = grow_ref[bi]
    bcol_all = _dot(tri, gcol, precision=lax.Precision.HIGHEST)
    brow_all = _dot_nt(grow, tri, precision=lax.Precision.HIGHEST)

    ks = [qk[:, ml_w + h * dh:ml_w + (h + 1) * dh] * (dh ** -0.5) for h in range(nh)]

    heads = []
    for h in range(nh):
        st = bi * nh + h
        b_c = jnp.broadcast_to(bcol_all[:, nh + h:nh + h + 1], (L, dh))
        i_c = jnp.broadcast_to(gcol[:, h:h + 1], (L, dh))
        b_r = brow_all[nh + h:nh + h + 1, :]
        i_r = grow[h:h + 1, :]
        m_prev = m_s[st, 0:1, :]
        log_d = jnp.where(causal, b_c[:, 0:1] + (i_r - b_r), NEG_INF)
        m_inter = b_c + m_prev
        m_t = jnp.maximum(m_inter, jnp.max(log_d, axis=-1, keepdims=True))
        d_mat = jnp.exp2(log_d - m_t[:, 0:1])
        inter = jnp.exp2(m_inter - m_t)
        b_last = b_c[L - 1:L, :]
        w_log = b_last - b_c + i_c
        m_new = jnp.maximum(b_last + m_prev, jnp.max(w_log, axis=0, keepdims=True))
        wgt = jnp.exp2(w_log - m_new)
        decay = jnp.exp2(b_last + m_prev - m_new)
        heads.append((m_t, d_mat, inter, m_new, wgt, decay))

    outs = []
    for h, (m_t, d_mat, inter, _, _, _) in enumerate(heads):
        hcols = slice(h * dh, (h + 1) * dh)
        q = qk[:, hcols].astype(BF16)
        v_ext = jnp.concatenate([v_ref[bi, :, hcols], ones], axis=-1)
        c_prev = c_s[bi * nh + h]
        s_qk = _dot_nt(q, ks[h].astype(BF16)) * d_mat
        ext = jnp.concatenate([inter, inter], axis=-1) * _dot(q, c_prev.astype(BF16)) \
            + _dot(s_qk.astype(BF16), v_ext)
        outs.append(ext[:, :dh] / jnp.maximum(jnp.abs(ext[:, dh:]), jnp.exp2(-m_t)))
    hh = jnp.concatenate(outs, axis=-1)
    msq = _dot((hh * hh).astype(BF16), hp)
    hn = hh * lax.rsqrt(msq + RMS_EPS) * gn_ref[...]
    o_ref[bi] = (og_ref[bi].astype(F32) * hn).astype(o_ref.dtype)

    for h, (_, _, _, m_new, wgt, decay) in enumerate(heads):
        st = bi * nh + h
        hcols = slice(h * dh, (h + 1) * dh)
        v_ext = jnp.concatenate([v_ref[bi, :, hcols], ones], axis=-1)
        kw = (ks[h] * wgt).astype(BF16)
        c_s[st] = jnp.concatenate([decay, decay], axis=-1) * c_s[st] + _dot_tn(kw, v_ext)
        m_s[st] = jnp.broadcast_to(m_new, m_s.shape[1:])


def _mlstm_call(qkm, vm, og, gcol, grow, w_conv, b_conv, g_norm, tri, hmean):
    b, s, ml_w = vm.shape
    L = MLSTM_CHUNK
    nh = N_MLSTM_HEADS
    dh = ml_w // nh
    bb = MLSTM_BATCH_ROWS
    assert b % bb == 0
    const = lambda shape: pl.BlockSpec(shape, lambda g, i: (0,) * len(shape))
    tok = lambda w: pl.BlockSpec((bb, L, w), lambda g, i: (g, i, 0))
    return pl.pallas_call(
        functools.partial(_mlstm_kernel, ml_w=ml_w),
        out_shape=jax.ShapeDtypeStruct((b, s, ml_w), BF16),
        grid=(b // bb, s // L),
        in_specs=[tok(2 * ml_w), tok(ml_w), tok(ml_w), tok(GATE_LANES),
                  pl.BlockSpec((bb, 2 * nh, L), lambda g, i: (g, 0, i)),
                  const(w_conv.shape), const((1, 2 * ml_w)), const((1, ml_w)),
                  const((L, L)), const(hmean.shape)],
        out_specs=tok(ml_w),
        scratch_shapes=[pltpu.VMEM((bb, MOD_ROWS, 2 * ml_w), F32),
                        pltpu.VMEM((bb * nh, dh, 2 * dh), F32),
                        pltpu.VMEM((bb * nh, MOD_ROWS, dh), F32)],
        compiler_params=pltpu.CompilerParams(dimension_semantics=("arbitrary", "arbitrary"),
                                             vmem_limit_bytes=VMEM_LIMIT_BYTES),
        name="mlstm",
    )(qkm, vm, og, gcol, grow, w_conv, b_conv, g_norm, tri, hmean)


def _outffn_kernel(x_ref, mod_ref, g_ref, *rest, dilations):
    n_lay = len(dilations)
    o_refs, l_refs = rest[:n_lay], rest[n_lay:2 * n_lay]
    (hm_ref, ex_ref, wo_ref, wg_ref, wu_ref, wd_ref, out_ref,
     operm_s, lperm_s) = rest[2 * n_lay:]
    tm = x_ref.shape[1]

    def natural(ref, d, scratch):
        if d == 1:
            return ref[0].astype(F32)
        n_cg = scratch.shape[1]
        chain = [step for step in _gather_chain(dilations) if step[1] <= d]
        for level, (parent, dd, f) in enumerate(reversed(chain)):
            dst = scratch.at[level % 2]
            src = scratch.at[(level + 1) % 2]
            n = tm // dd
            for c in range(parent):
                for a in range(f):
                    r = a * parent + c
                    for cg in range(n_cg):
                        if level == 0:
                            c0 = (r * n_cg + cg) * LANES
                            rows = ref[0, :, c0:c0 + LANES].astype(F32)
                        else:
                            rows = src[cg, r * n:(r + 1) * n, :]
                        dst[cg, pl.ds(c * (tm // parent) + a, n, stride=f), :] = rows
        final = scratch.at[(len(chain) - 1) % 2]
        return jnp.concatenate([final[cg] for cg in range(n_cg)], axis=-1)

    lses = [natural(ref, d, lperm_s) for ref, d in zip(l_refs, dilations)]
    mx = functools.reduce(jnp.maximum, lses)
    es = [jnp.exp(l - mx) for l in lses]
    inv = 1.0 / functools.reduce(jnp.add, es)
    ex = ex_ref[...]
    attn = None
    for e, ref, d in zip(es, o_refs, dilations):
        term = _dot((e * inv).astype(BF16), ex) * natural(ref, d, operm_s)
        attn = term if attn is None else attn + term
    aw = attn.shape[-1]
    y = _dot(attn.astype(BF16), wo_ref[:aw, :]) + _dot(hm_ref[0], wo_ref[aw:, :])
    x1 = x_ref[0] + mod_ref[0, 2:3, :] * y
    ms = jnp.mean(x1 * x1, axis=-1, keepdims=True)
    hn = x1 * lax.rsqrt(ms + RMS_EPS) * g_ref[...]
    hb = (hn * (1.0 + mod_ref[0, 4:5, :]) + mod_ref[0, 3:4, :]).astype(BF16)
    g = _dot(hb, wg_ref[...])
    u = _dot(hb, wu_ref[...])
    a = (g * _sigmoid(g) * u).astype(BF16)
    out_ref[0] = x1 + mod_ref[0, 5:6, :] * _dot(a, wd_ref[...])


def _outffn_call(x, mod, g_ffn, os_, lses, hm, expand, wo, wg, wu, wd, *, dilations):
    b, s, d = x.shape
    tm = FFN_ROWS
    assert all(tm % (dl * BF16_SUBLANES) == 0 for dl in dilations)
    aw = expand.shape[1]
    const = lambda shape: pl.BlockSpec(shape, lambda bi, i: (0,) * len(shape))
    tok = lambda w: pl.BlockSpec((1, tm, w), lambda bi, i: (bi, i, 0))
    strided = lambda w: tuple(pl.BlockSpec((1, tm // dl, dl * w), lambda bi, i: (bi, i, 0)) for dl in dilations)
    return pl.pallas_call(
        functools.partial(_outffn_kernel, dilations=dilations),
        out_shape=jax.ShapeDtypeStruct((b, s, d), F32),
        grid=(b, s // tm),
        in_specs=[tok(d), pl.BlockSpec((1, N_MOD, d), lambda bi, i: (bi, 0, 0)), const((1, d)),
                  *strided(aw), *strided(LANES), tok(hm.shape[-1]),
                  const(expand.shape), const(wo.shape),
                  const(wg.shape), const(wu.shape), const(wd.shape)],
        out_specs=tok(d),
        scratch_shapes=[pltpu.VMEM((2, aw // LANES, tm, LANES), F32), pltpu.VMEM((2, 1, tm, LANES), F32)],
        compiler_params=pltpu.CompilerParams(dimension_semantics=("arbitrary", "arbitrary"),
                                             vmem_limit_bytes=VMEM_LIMIT_BYTES),
        name="outffn",
    )(x, mod, g_ffn, *os_, *lses, hm, expand, wo, wg, wu, wd)


def _block_diag_mean(width, group):
    idx = jnp.arange(width) // group
    return jnp.where(idx[:, None] == idx[None, :], 1.0 / group, 0.0).astype(BF16)


def kernel(x, c, g_mix, w_in, w_conv, b_conv, b_igate, b_fgate, q_norm_g, k_norm_g, mlstm_norm_g, w_out,
           g_ffn, w_gate, w_up, w_down, w_ada, b_ada):
    b, s, d = x.shape
    depth = g_mix.shape[0]
    attn_w = d // 2
    ml_w = d - attn_w
    nh = N_MLSTM_HEADS
    n_attn_heads = attn_w // ATTN_HEAD_DIM
    scale = ATTN_HEAD_DIM ** -0.5 * LOG2E
    dilations = tuple(dl for _, dl in DILATED_PATTERNS)
    assert b <= MOD_ROWS and 2 * nh <= GATE_LANES

    c_t = jnp.zeros((d, MOD_ROWS), F32).at[:, :b].set(c.T)
    hmean_attn = _block_diag_mean(attn_w, ATTN_HEAD_DIM)
    hmean_ml = _block_diag_mean(ml_w, ml_w // nh)
    stat_lane = jnp.arange(LANES)
    head_of_stat = jnp.where(stat_lane % ATTN_HEAD_DIM < n_attn_heads // 2,
                             2 * (stat_lane % ATTN_HEAD_DIM) + stat_lane // ATTN_HEAD_DIM, -1)
    head_of_col = jnp.arange(attn_w) // ATTN_HEAD_DIM
    expand = (head_of_stat[:, None] == head_of_col[None, :]).astype(BF16)
    tri = jnp.tril(jnp.ones((MLSTM_CHUNK, MLSTM_CHUNK), F32))

    for l in range(depth):
        mod = _mod_call(c_t, w_ada[l], b_ada[l][None, :], n_rows=b)[:b].reshape(b, N_MOD, d)

        n_main = 3 * attn_w + 4 * ml_w
        w_in_bf = w_in[l].astype(BF16)
        wg_cols = w_in[l][:, n_main:]
        wg = jnp.zeros((d, GATE_LANES), F32).at[:, :2 * nh].set(wg_cols).astype(BF16)
        wgt = jnp.zeros((BF16_SUBLANES, d), F32).at[:2 * nh].set(wg_cols.T).astype(BF16)
        gate_bias = jnp.concatenate([b_igate[l], b_fgate[l]])
        bcol = jnp.zeros((1, GATE_LANES), F32).at[0, :2 * nh].set(gate_bias)
        brow = gate_bias[:, None]
        gq = jnp.tile(q_norm_g[l], n_attn_heads)[None, :]
        gk = jnp.tile(k_norm_g[l], n_attn_heads)[None, :]

        qs, ks, vs, qkm, vm, og, gcol, grow = _inproj_call(
            x, mod, g_mix[l][None, :], w_in_bf, wg, wgt, gq, gk, hmean_attn, bcol, brow,
            attn_w=attn_w, ml_w=ml_w, scale=scale, dilations=dilations)

        os_, lses = [], []
        for (window, dilation), q, k, v in zip(DILATED_PATTERNS, qs, ks, vs):
            o, lse = _attn_call(q, k, v, width=attn_w, window=window, dilation=dilation)
            os_.append(o)
            lses.append(lse)

        hm = _mlstm_call(qkm, vm, og, gcol, grow, w_conv[l], b_conv[l][None, :],
                         mlstm_norm_g[l][None, :], tri, hmean_ml)

        x = _outffn_call(x, mod, g_ffn[l][None, :], os_, lses, hm, expand,
                         w_out[l].astype(BF16), w_gate[l].astype(BF16), w_up[l].astype(BF16), w_down[l].astype(BF16),
                         dilations=dilations)
    return x
```

```python
import functools

import jax
import jax.numpy as jnp
from jax import lax
from jax.experimental import pallas as pl
from jax.experimental.pallas import tpu as pltpu

F32 = jnp.float32
BF16 = jnp.bfloat16

ATTN_HEAD_DIM = 64
N_MLSTM_HEADS = 4
CONV_WIDTH = 4
DILATED_PATTERNS = ((128, 1), (512, 4), (2048, 16))
ATTN_BLOCK = 128
N_MOD = 6
RMS_EPS = 1e-6

LANES = 128
BF16_SUBLANES = 16
VMEM_LIMIT_BYTES = 56 * 1024 * 1024
F32_SUBLANES = 8
MOD_ROWS = F32_SUBLANES
LAYOUT_ROWS = 512
GATE_LANES = 128
INPROJ_ROWS = LAYOUT_ROWS
ATTN_MAX_SUB_BLOCKS = 8
MLSTM_CHUNK = 256
MLSTM_BATCH_ROWS = 1
FFN_ROWS = LAYOUT_ROWS

NEG_INF = float("-inf")
LOG2E = 1.4426950408889634
LN2 = 0.6931471805599453


def _dot(a, b, **kw):
    return jnp.dot(a, b, preferred_element_type=F32, **kw)


def _dot_nt(a, b, **kw):
    return lax.dot_general(a, b, (((1,), (1,)), ((), ())), preferred_element_type=F32, **kw)


def _dot_tn(a, b):
    return lax.dot_general(a, b, (((0,), (0,)), ((), ())), preferred_element_type=F32)


def _sigmoid(z):
    return 1.0 / (1.0 + jnp.exp(-z))


def _log_sigmoid(z):
    return jnp.minimum(z, 0.0) - jnp.log1p(jnp.exp(-jnp.abs(z)))


def _strided_shape(b, s, w, dilation):
    return (b, s // dilation, dilation * w)


def _gather_chain(dilations):
    chain, parent = [], 1
    for d in sorted(set(dilations)):
        if d == 1:
            continue
        assert d % parent == 0 and d // parent < F32_SUBLANES
        chain.append((parent, d, d // parent))
        parent = d
    return chain


def _mod_kernel(ct_ref, w_ref, b_ref, o_ref, *, n_rows):
    ct = ct_ref[...]
    sc = ct * _sigmoid(ct)
    w = w_ref[...]
    rows = [jnp.sum(w * sc[:, r:r + 1], axis=0, keepdims=True) for r in range(n_rows)]
    rows += [jnp.zeros_like(rows[0])] * (MOD_ROWS - n_rows)
    o_ref[...] = jnp.concatenate(rows, axis=0) + b_ref[...]


def _mod_call(c_t, w_ada, b_ada, *, n_rows):
    d, n = w_ada.shape
    tn = n // 4
    assert tn % LANES == 0
    return pl.pallas_call(
        functools.partial(_mod_kernel, n_rows=n_rows),
        out_shape=jax.ShapeDtypeStruct((MOD_ROWS, n), F32),
        grid=(n // tn,),
        in_specs=[pl.BlockSpec((d, MOD_ROWS), lambda j: (0, 0)),
                  pl.BlockSpec((d, tn), lambda j: (0, j)),
                  pl.BlockSpec((1, tn), lambda j: (0, j))],
        out_specs=pl.BlockSpec((MOD_ROWS, tn), lambda j: (0, j)),
        compiler_params=pltpu.CompilerParams(dimension_semantics=("arbitrary",),
                                             vmem_limit_bytes=VMEM_LIMIT_BYTES),
        name="mod",
    )(c_t, w_ada, b_ada)


def _inproj_kernel(x_ref, mod_ref, g_ref, w_ref, wg_ref, wgt_ref, gq_ref, gk_ref, hp_ref,
                   bcol_ref, brow_ref, *rest, attn_w, ml_w, scale, dilations):
    n_lay = len(dilations)
    q_refs, k_refs, v_refs = rest[:n_lay], rest[n_lay:2 * n_lay], rest[2 * n_lay:3 * n_lay]
    qkm_ref, vm_ref, og_ref, gcol_ref, grow_ref, perm_s = rest[3 * n_lay:]
    tm = x_ref.shape[1]

    def emit(val, refs):
        w = val.shape[-1]
        n_cg = w // LANES
        ref_of = dict(zip(dilations, refs))
        if 1 in ref_of:
            ref_of[1][0] = val.astype(BF16)
        chain = _gather_chain(dilations)
        for cg in range(n_cg):
            perm_s[0, cg] = val[:, cg * LANES:(cg + 1) * LANES]
        for level, (parent, d, f) in enumerate(chain):
            src, dst = perm_s.at[level % 2], perm_s.at[(level + 1) % 2]
            n = tm // d
            for c in range(parent):
                for a in range(f):
                    r = a * parent + c
                    for cg in range(n_cg):
                        rows = src[cg, pl.ds(c * (tm // parent) + a, n, stride=f), :]
                        c0 = r * w + cg * LANES
                        ref_of[d][0, :, c0:c0 + LANES] = rows.astype(BF16)
                        if level + 1 < len(chain):
                            dst[cg, r * n:(r + 1) * n, :] = rows

    x = x_ref[0]
    ms = jnp.mean(x * x, axis=-1, keepdims=True)
    y = x * lax.rsqrt(ms + RMS_EPS) * g_ref[...]
    h = (y * (1.0 + mod_ref[0, 1:2, :]) + mod_ref[0, 0:1, :]).astype(BF16)

    n_attn = 3 * attn_w
    xa = _dot(h, w_ref[:, :n_attn])
    hp = hp_ref[...]

    def head_norm(t, g):
        msq = _dot((t * t).astype(BF16), hp)
        return t * lax.rsqrt(msq + RMS_EPS) * g

    emit(head_norm(xa[:, :attn_w], gq_ref[...]) * scale, q_refs)
    emit(head_norm(xa[:, attn_w:2 * attn_w], gk_ref[...]), k_refs)
    emit(xa[:, 2 * attn_w:], v_refs)

    xm = _dot(h, w_ref[:, n_attn:n_attn + 4 * ml_w])
    qkm_ref[0] = xm[:, :2 * ml_w].astype(BF16)
    vm_ref[0] = xm[:, 2 * ml_w:3 * ml_w].astype(BF16)
    og_ref[0] = _sigmoid(xm[:, 3 * ml_w:]).astype(BF16)

    nh = N_MLSTM_HEADS
    zc = _dot(h, wg_ref[...]) + bcol_ref[...]
    lane = lax.broadcasted_iota(jnp.int32, zc.shape, 1)
    gcol_ref[0] = jnp.where(lane < nh, zc, jnp.where(lane < 2 * nh, _log_sigmoid(zc), 0.0)) * LOG2E
    zr = _dot_nt(wgt_ref[...], h)[:2 * nh] + brow_ref[...]
    row = lax.broadcasted_iota(jnp.int32, zr.shape, 0)
    grow_ref[0] = jnp.where(row < nh, zr, _log_sigmoid(zr)) * LOG2E


def _inproj_call(x, mod, g_mix, w, wg, wgt, gq, gk, hp, bcol, brow, *, attn_w, ml_w, scale, dilations):
    b, s, d = x.shape
    tm = INPROJ_ROWS
    nh2 = 2 * N_MLSTM_HEADS
    assert all(tm % (dl * BF16_SUBLANES) == 0 for dl in dilations)
    const = lambda shape: pl.BlockSpec(shape, lambda bi, i: (0,) * len(shape))
    tok = lambda w: pl.BlockSpec((1, tm, w), lambda bi, i: (bi, i, 0))
    strided_shapes = tuple(jax.ShapeDtypeStruct(_strided_shape(b, s, attn_w, dl), BF16) for dl in dilations)
    strided_specs = tuple(pl.BlockSpec((1, tm // dl, dl * attn_w), lambda bi, i: (bi, i, 0)) for dl in dilations)
    out_shape = strided_shapes * 3 + (
        jax.ShapeDtypeStruct((b, s, 2 * ml_w), BF16),
        jax.ShapeDtypeStruct((b, s, ml_w), BF16),
        jax.ShapeDtypeStruct((b, s, ml_w), BF16),
        jax.ShapeDtypeStruct((b, s, GATE_LANES), F32),
        jax.ShapeDtypeStruct((b, nh2, s), F32))
    outs = pl.pallas_call(
        functools.partial(_inproj_kernel, attn_w=attn_w, ml_w=ml_w, scale=scale, dilations=dilations),
        out_shape=out_shape,
        grid=(b, s // tm),
        in_specs=[tok(d),
                  pl.BlockSpec((1, N_MOD, d), lambda bi, i: (bi, 0, 0)),
                  const((1, d)), const(w.shape), const(wg.shape), const(wgt.shape),
                  const((1, attn_w)), const((1, attn_w)), const(hp.shape),
                  const((1, GATE_LANES)), const((nh2, 1))],
        out_specs=strided_specs * 3 + (tok(2 * ml_w), tok(ml_w), tok(ml_w), tok(GATE_LANES),
                                       pl.BlockSpec((1, nh2, tm), lambda bi, i: (bi, 0, i))),
        scratch_shapes=[pltpu.VMEM((2, attn_w // LANES, tm, LANES), F32)],
        compiler_params=pltpu.CompilerParams(dimension_semantics=("arbitrary", "arbitrary"),
                                             vmem_limit_bytes=VMEM_LIMIT_BYTES),
        name="inproj",
    )(x, mod, g_mix, w, wg, wgt, gq, gk, hp, bcol, brow)
    n_lay = len(dilations)
    return (outs[:n_lay], outs[n_lay:2 * n_lay], outs[2 * n_lay:3 * n_lay]) + tuple(outs[3 * n_lay:])


def _attn_kernel(q_ref, kp_ref, kc_ref, vp_ref, vc_ref, o_ref, lse_ref, k_s, v_s, bias_s, *, n_back, n_heads):
    blk = ATTN_BLOCK
    sub_blocks = q_ref.shape[1] // blk
    hd = ATTN_HEAD_DIM
    w = n_heads * hd
    n_res = q_ref.shape[2] // w
    n_pairs = n_heads // 2
    n = pl.program_id(2)
    lane = lax.broadcasted_iota(jnp.int32, (blk, LANES), 1)
    first_head = lane < hd
    ind_a = jnp.where(first_head, 1.0, 0.0).astype(BF16)
    ind_b = jnp.where(first_head, 0.0, 1.0).astype(BF16)
    for res in range(n_res):
        k_s[res, 0:blk, :] = kp_ref[0, :, res * w:(res + 1) * w]
        k_s[res, blk:, :] = kc_ref[0, :, res * w:(res + 1) * w]
        for kb in range(sub_blocks + 1):
            for j in range(n_pairs):
                cols = slice(res * w + j * LANES, res * w + (j + 1) * LANES)
                src = vp_ref[0, :, cols] if kb == 0 else vc_ref[0, (kb - 1) * blk:kb * blk, cols]
                ra, rb, c0 = 2 * kb * blk, (2 * kb + 1) * blk, 2 * j * LANES
                v_s[res, ra:ra + blk, c0:c0 + LANES] = jnp.where(first_head, src, 0)
                v_s[res, ra:ra + blk, c0 + LANES:c0 + 2 * LANES] = ind_a
                v_s[res, rb:rb + blk, c0:c0 + LANES] = jnp.where(first_head, 0, src)
                v_s[res, rb:rb + blk, c0 + LANES:c0 + 2 * LANES] = ind_b

    row = lax.broadcasted_iota(jnp.int32, (2 * blk, 2 * blk), 0) & (blk - 1)
    col = lax.broadcasted_iota(jnp.int32, (2 * blk, 2 * blk), 1)
    band = jnp.logical_and(col >= row + (blk - n_back), col <= row + blk)
    bias_s[0] = jnp.where(band, 0.0, NEG_INF)
    bias_s[1] = jnp.where(jnp.logical_and(band, col >= blk), 0.0, NEG_INF)

    units = [(res, i, j) for res in range(n_res) for i in range(sub_blocks) for j in range(n_pairs)]
    scores, maxes, probs = [], [], []
    for res, i, j in units:
        cols = slice(j * LANES, (j + 1) * LANES)
        q = q_ref[0, i * blk:(i + 1) * blk, res * w + j * LANES:res * w + (j + 1) * LANES]
        ks = k_s[res, i * blk:(i + 2) * blk, cols]
        bias = bias_s[jnp.where(n == 0, 1, 0)] if i == 0 else bias_s[0]
        q2 = jnp.concatenate([jnp.where(first_head, q, 0), jnp.where(first_head, 0, q)], axis=0)
        s = _dot_nt(q2, ks) + bias
        scores.append(s)
        maxes.append(jnp.max(s, axis=-1, keepdims=True))
    for s, m in zip(scores, maxes):
        p = jnp.exp2(s - m).astype(BF16)
        probs.append(jnp.concatenate([p[:blk, :blk], p[blk:, :blk], p[:blk, blk:], p[blk:, blk:]], axis=1))
    for res in range(n_res):
        for i in range(sub_blocks):
            outs = []
            stats = jnp.zeros((blk, LANES), F32)
            for j in range(n_pairs):
                u = (res * sub_blocks + i) * n_pairs + j
                vw = v_s[res, 2 * i * blk:2 * (i + 2) * blk, 2 * j * LANES:(2 * j + 2) * LANES]
                ov = _dot(probs[u], vw)
                outs.append(ov[:, :LANES] / ov[:, LANES:])
                m = jnp.broadcast_to(maxes[u], (2 * blk, LANES))
                lse = jnp.where(first_head, m[:blk], m[blk:]) * LN2 + jnp.log(ov[:, LANES:])
                stats = jnp.where((lane & (hd - 1)) == j, lse, stats)
            rows = slice(i * blk, (i + 1) * blk)
            o_ref[0, rows, res * w:(res + 1) * w] = jnp.concatenate(outs, axis=-1).astype(o_ref.dtype)
            lse_ref[0, rows, res * LANES:(res + 1) * LANES] = stats


def _attn_call(q, k, v, *, width, window, dilation):
    b, ls, _ = q.shape
    w = width
    n_back = window // dilation
    blk = ATTN_BLOCK
    assert n_back <= blk
    sub_blocks = min(ATTN_MAX_SUB_BLOCKS, ls // blk)
    step_rows = sub_blocks * blk
    assert ls % step_rows == 0
    n_heads = w // ATTN_HEAD_DIM
    assert n_heads <= LANES and n_heads % 2 == 0 and 2 * ATTN_HEAD_DIM == LANES
    n_res = max(1, min(dilation, ATTN_MAX_SUB_BLOCKS // sub_blocks))
    assert dilation % n_res == 0
    cur = pl.BlockSpec((1, step_rows, n_res * w), lambda bi, r, n: (bi, n, r))
    prev = pl.BlockSpec((1, blk, n_res * w),
                        lambda bi, r, n: (bi, jnp.maximum(n * sub_blocks - 1, 0), r))
    return pl.pallas_call(
        functools.partial(_attn_kernel, n_back=n_back, n_heads=n_heads),
        out_shape=(jax.ShapeDtypeStruct((b, ls, dilation * w), BF16),
                   jax.ShapeDtypeStruct((b, ls, dilation * LANES), F32)),
        grid=(b, dilation // n_res, ls // step_rows),
        in_specs=[cur, prev, cur, prev, cur],
        out_specs=(cur, pl.BlockSpec((1, step_rows, n_res * LANES), lambda bi, r, n: (bi, n, r))),
        scratch_shapes=[pltpu.VMEM((n_res, blk + step_rows, w), BF16),
                        pltpu.VMEM((n_res, 2 * (blk + step_rows), 2 * w), BF16),
                        pltpu.VMEM((2, 2 * blk, 2 * blk), F32)],
        compiler_params=pltpu.CompilerParams(
            dimension_semantics=("arbitrary", "arbitrary", "arbitrary"),
            vmem_limit_bytes=VMEM_LIMIT_BYTES),
        name=f"attn_d{dilation}",
    )(q, k, k, v, v)


def _mlstm_kernel(qk_ref, v_ref, og_ref, gcol_ref, grow_ref, wc_ref, bc_ref, gn_ref, tri_ref, hm_ref,
                  o_ref, tail_s, c_s, m_s, *, ml_w):
    L = MLSTM_CHUNK
    nh = N_MLSTM_HEADS
    dh = ml_w // nh

    @pl.when(pl.program_id(1) == 0)
    def _():
        tail_s[...] = jnp.zeros_like(tail_s)
        c_s[...] = jnp.zeros_like(c_s)
        m_s[...] = jnp.zeros_like(m_s)

    tri = tri_ref[...]
    ti = lax.broadcasted_iota(jnp.int32, (L, L), 0)
    si = lax.broadcasted_iota(jnp.int32, (L, L), 1)
    causal = si <= ti
    ones = jnp.ones((L, dh), BF16)
    hp = hm_ref[...]
    for bi in range(qk_ref.shape[0]):
        _mlstm_chunk(bi, qk_ref, v_ref, og_ref, gcol_ref, grow_ref, wc_ref, bc_ref, gn_ref, o_ref,
                     tail_s, c_s, m_s, tri, causal, ones, hp, ml_w=ml_w)


def _mlstm_chunk(bi, qk_ref, v_ref, og_ref, gcol_ref, grow_ref, wc_ref, bc_ref, gn_ref, o_ref,
                 tail_s, c_s, m_s, tri, causal, ones, hp, *, ml_w):
    L = MLSTM_CHUNK
    nh = N_MLSTM_HEADS
    dh = ml_w // nh
    x = qk_ref[bi].astype(F32)
    tail = tail_s[bi]
    row8 = lax.broadcasted_iota(jnp.int32, tail.shape, 0)
    acc = bc_ref[...] + x * wc_ref[CONV_WIDTH - 1:CONV_WIDTH, :]
    for back in range(1, CONV_WIDTH):
        rolled = pltpu.roll(x, back, 0)
        top = jnp.where(row8 >= back, rolled[:MOD_ROWS], pltpu.roll(tail, back, 0))
        shifted = jnp.concatenate([top, rolled[MOD_ROWS:]], axis=0)
        acc = acc + shifted * wc_ref[CONV_WIDTH - 1 - back:CONV_WIDTH - back, :]
    tail_s[bi] = x[L - MOD_ROWS:]
    qk = acc * _sigmoid(acc)

    gcol = gcol_ref[bi]
    grow = grow_ref[bi]
    bcol_all = _dot(tri, gcol, precision=lax.Precision.HIGHEST)
    brow_all = _dot_nt(grow, tri, precision=lax.Precision.HIGHEST)

    ks = [qk[:, ml_w + h * dh:ml_w + (h + 1) * dh] * (dh ** -0.5) for h in range(nh)]

    heads = []
    for h in range(nh):
        st = bi * nh + h
        b_c = jnp.broadcast_to(bcol_all[:, nh + h:nh + h + 1], (L, dh))
        i_c = jnp.broadcast_to(gcol[:, h:h + 1], (L, dh))
        b_r = brow_all[nh + h:nh + h + 1, :]
        i_r = grow[h:h + 1, :]
        m_prev = m_s[st, 0:1, :]
        log_d = jnp.where(causal, b_c[:, 0:1] + (i_r - b_r), NEG_INF)
        m_inter = b_c + m_prev
        m_t = jnp.maximum(m_inter, jnp.max(log_d, axis=-1, keepdims=True))
        d_mat = jnp.exp2(log_d - m_t[:, 0:1])
        inter = jnp.exp2(m_inter - m_t)
        b_last = b_c[L - 1:L, :]
        w_log = b_last - b_c + i_c
        m_new = jnp.maximum(b_last + m_prev, jnp.max(w_log, axis=0, keepdims=True))
        wgt = jnp.exp2(w_log - m_new)
        decay = jnp.exp2(b_last + m_prev - m_new)
        heads.append((m_t, d_mat, inter, m_new, wgt, decay))

    outs = []
    for h, (m_t, d_mat, inter, _, _, _) in enumerate(heads):
        hcols = slice(h * dh, (h + 1) * dh)
        q = qk[:, hcols].astype(BF16)
        v_ext = jnp.concatenate([v_ref[bi, :, hcols], ones], axis=-1)
        c_prev = c_s[bi * nh + h]
        s_qk = _dot_nt(q, ks[h].astype(BF16)) * d_mat
        ext = jnp.concatenate([inter, inter], axis=-1) * _dot(q, c_prev.astype(BF16)) \
            + _dot(s_qk.astype(BF16), v_ext)
        outs.append(ext[:, :dh] / jnp.maximum(jnp.abs(ext[:, dh:]), jnp.exp2(-m_t)))
    hh = jnp.concatenate(outs, axis=-1)
    msq = _dot((hh * hh).astype(BF16), hp)
    hn = hh * lax.rsqrt(msq + RMS_EPS) * gn_ref[...]
    o_ref[bi] = (og_ref[bi].astype(F32) * hn).astype(o_ref.dtype)

    for h, (_, _, _, m_new, wgt, decay) in enumerate(heads):
        st = bi * nh + h
        hcols = slice(h * dh, (h + 1) * dh)
        v_ext = jnp.concatenate([v_ref[bi, :, hcols], ones], axis=-1)
        kw = (ks[h] * wgt).astype(BF16)
        c_s[st] = jnp.concatenate([decay, decay], axis=-1) * c_s[st] + _dot_tn(kw, v_ext)
        m_s[st] = jnp.broadcast_to(m_new, m_s.shape[1:])


def _mlstm_call(qkm, vm, og, gcol, grow, w_conv, b_conv, g_norm, tri, hmean):
    b, s, ml_w = vm.shape
    L = MLSTM_CHUNK
    nh = N_MLSTM_HEADS
    dh = ml_w // nh
    bb = MLSTM_BATCH_ROWS
    assert b % bb == 0
    const = lambda shape: pl.BlockSpec(shape, lambda g, i: (0,) * len(shape))
    tok = lambda w: pl.BlockSpec((bb, L, w), lambda g, i: (g, i, 0))
    return pl.pallas_call(
        functools.partial(_mlstm_kernel, ml_w=ml_w),
        out_shape=jax.ShapeDtypeStruct((b, s, ml_w), BF16),
        grid=(b // bb, s // L),
        in_specs=[tok(2 * ml_w), tok(ml_w), tok(ml_w), tok(GATE_LANES),
                  pl.BlockSpec((bb, 2 * nh, L), lambda g, i: (g, 0, i)),
                  const(w_conv.shape), const((1, 2 * ml_w)), const((1, ml_w)),
                  const((L, L)), const(hmean.shape)],
        out_specs=tok(ml_w),
        scratch_shapes=[pltpu.VMEM((bb, MOD_ROWS, 2 * ml_w), F32),
                        pltpu.VMEM((bb * nh, dh, 2 * dh), F32),
                        pltpu.VMEM((bb * nh, MOD_ROWS, dh), F32)],
        compiler_params=pltpu.CompilerParams(dimension_semantics=("arbitrary", "arbitrary"),
                                             vmem_limit_bytes=VMEM_LIMIT_BYTES),
        name="mlstm",
    )(qkm, vm, og, gcol, grow, w_conv, b_conv, g_norm, tri, hmean)


def _outffn_kernel(x_ref, mod_ref, g_ref, *rest, dilations):
    n_lay = len(dilations)
    o_refs, l_refs = rest[:n_lay], rest[n_lay:2 * n_lay]
    (hm_ref, ex_ref, wo_ref, wg_ref, wu_ref, wd_ref, out_ref,
     operm_s, lperm_s) = rest[2 * n_lay:]
    tm = x_ref.shape[1]

    def natural(ref, d, scratch):
        if d == 1:
            return ref[0].astype(F32)
        n_cg = scratch.shape[1]
        chain = [step for step in _gather_chain(dilations) if step[1] <= d]
        for level, (parent, dd, f) in enumerate(reversed(chain)):
            dst = scratch.at[level % 2]
            src = scratch.at[(level + 1) % 2]
            n = tm // dd
            for c in range(parent):
                for a in range(f):
                    r = a * parent + c
                    for cg in range(n_cg):
                        if level == 0:
                            c0 = (r * n_cg + cg) * LANES
                            rows = ref[0, :, c0:c0 + LANES].astype(F32)
                        else:
                            rows = src[cg, r * n:(r + 1) * n, :]
                        dst[cg, pl.ds(c * (tm // parent) + a, n, stride=f), :] = rows
        final = scratch.at[(len(chain) - 1) % 2]
        return jnp.concatenate([final[cg] for cg in range(n_cg)], axis=-1)

    lses = [natural(ref, d, lperm_s) for ref, d in zip(l_refs, dilations)]
    mx = functools.reduce(jnp.maximum, lses)
    es = [jnp.exp(l - mx) for l in lses]
    inv = 1.0 / functools.reduce(jnp.add, es)
    ex = ex_ref[...]
    attn = None
    for e, ref, d in zip(es, o_refs, dilations):
        term = _dot((e * inv).astype(BF16), ex) * natural(ref, d, operm_s)
        attn = term if attn is None else attn + term
    aw = attn.shape[-1]
    y = _dot(attn.astype(BF16), wo_ref[:aw, :]) + _dot(hm_ref[0], wo_ref[aw:, :])
    x1 = x_ref[0] + mod_ref[0, 2:3, :] * y
    ms = jnp.mean(x1 * x1, axis=-1, keepdims=True)
    hn = x1 * lax.rsqrt(ms + RMS_EPS) * g_ref[...]
    hb = (hn * (1.0 + mod_ref[0, 4:5, :]) + mod_ref[0, 3:4, :]).astype(BF16)
    g = _dot(hb, wg_ref[...])
    u = _dot(hb, wu_ref[...])
    a = (g * _sigmoid(g) * u).astype(BF16)
    out_ref[0] = x1 + mod_ref[0, 5:6, :] * _dot(a, wd_ref[...])


def _outffn_call(x, mod, g_ffn, os_, lses, hm, expand, wo, wg, wu, wd, *, dilations):
    b, s, d = x.shape
    tm = FFN_ROWS
    assert all(tm % (dl * BF16_SUBLANES) == 0 for dl in dilations)
    aw = expand.shape[1]
    const = lambda shape: pl.BlockSpec(shape, lambda bi, i: (0,) * len(shape))
    tok = lambda w: pl.BlockSpec((1, tm, w), lambda bi, i: (bi, i, 0))
    strided = lambda w: tuple(pl.BlockSpec((1, tm // dl, dl * w), lambda bi, i: (bi, i, 0)) for dl in dilations)
    return pl.pallas_call(
        functools.partial(_outffn_kernel, dilations=dilations),
        out_shape=jax.ShapeDtypeStruct((b, s, d), F32),
        grid=(b, s // tm),
        in_specs=[tok(d), pl.BlockSpec((1, N_MOD, d), lambda bi, i: (bi, 0, 0)), const((1, d)),
                  *strided(aw), *strided(LANES), tok(hm.shape[-1]),
                  const(expand.shape), const(wo.shape),
                  const(wg.shape), const(wu.shape), const(wd.shape)],
        out_specs=tok(d),
        scratch_shapes=[pltpu.VMEM((2, aw // LANES, tm, LANES), F32), pltpu.VMEM((2, 1, tm, LANES), F32)],
        compiler_params=pltpu.CompilerParams(dimension_semantics=("arbitrary", "arbitrary"),
                                             vmem_limit_bytes=VMEM_LIMIT_BYTES),
        name="outffn",
    )(x, mod, g_ffn, *os_, *lses, hm, expand, wo, wg, wu, wd)


def _block_diag_mean(width, group):
    idx = jnp.arange(width) // group
    return jnp.where(idx[:, None] == idx[None, :], 1.0 / group, 0.0).astype(BF16)


def kernel(x, c, g_mix, w_in, w_conv, b_conv, b_igate, b_fgate, q_norm_g, k_norm_g, mlstm_norm_g, w_out,
           g_ffn, w_gate, w_up, w_down, w_ada, b_ada):
    b, s, d = x.shape
    depth = g_mix.shape[0]
    attn_w = d // 2
    ml_w = d - attn_w
    nh = N_MLSTM_HEADS
    n_attn_heads = attn_w // ATTN_HEAD_DIM
    scale = ATTN_HEAD_DIM ** -0.5 * LOG2E
    dilations = tuple(dl for _, dl in DILATED_PATTERNS)
    assert b <= MOD_ROWS and 2 * nh <= GATE_LANES

    c_t = jnp.zeros((d, MOD_ROWS), F32).at[:, :b].set(c.T)
    hmean_attn = _block_diag_mean(attn_w, ATTN_HEAD_DIM)
    hmean_ml = _block_diag_mean(ml_w, ml_w // nh)
    stat_lane = jnp.arange(LANES)
    head_of_stat = jnp.where(stat_lane % ATTN_HEAD_DIM < n_attn_heads // 2,
                             2 * (stat_lane % ATTN_HEAD_DIM) + stat_lane // ATTN_HEAD_DIM, -1)
    head_of_col = jnp.arange(attn_w) // ATTN_HEAD_DIM
    expand = (head_of_stat[:, None] == head_of_col[None, :]).astype(BF16)
    tri = jnp.tril(jnp.ones((MLSTM_CHUNK, MLSTM_CHUNK), F32))

    for l in range(depth):
        mod = _mod_call(c_t, w_ada[l], b_ada[l][None, :], n_rows=b)[:b].reshape(b, N_MOD, d)

        n_main = 3 * attn_w + 4 * ml_w
        w_in_bf = w_in[l].astype(BF16)
        wg_cols = w_in[l][:, n_main:]
        wg = jnp.zeros((d, GATE_LANES), F32).at[:, :2 * nh].set(wg_cols).astype(BF16)
        wgt = jnp.zeros((BF16_SUBLANES, d), F32).at[:2 * nh].set(wg_cols.T).astype(BF16)
        gate_bias = jnp.concatenate([b_igate[l], b_fgate[l]])
        bcol = jnp.zeros((1, GATE_LANES), F32).at[0, :2 * nh].set(gate_bias)
        brow = gate_bias[:, None]
        gq = jnp.tile(q_norm_g[l], n_attn_heads)[None, :]
        gk = jnp.tile(k_norm_g[l], n_attn_heads)[None, :]

        qs, ks, vs, qkm, vm, og, gcol, grow = _inproj_call(
            x, mod, g_mix[l][None, :], w_in_bf, wg, wgt, gq, gk, hmean_attn, bcol, brow,
            attn_w=attn_w, ml_w=ml_w, scale=scale, dilations=dilations)

        os_, lses = [], []
        for (window, dilation), q, k, v in zip(DILATED_PATTERNS, qs, ks, vs):
            o, lse = _attn_call(q, k, v, width=attn_w, window=window, dilation=dilation)
            os_.append(o)
            lses.append(lse)

        hm = _mlstm_call(qkm, vm, og, gcol, grow, w_conv[l], b_conv[l][None, :],
                         mlstm_norm_g[l][None, :], tri, hmean_ml)

        x = _outffn_call(x, mod, g_ffn[l][None, :], os_, lses, hm, expand,
                         w_out[l].astype(BF16), w_gate[l].astype(BF16), w_up[l].astype(BF16), w_down[l].astype(BF16),
                         dilations=dilations)
    return x
```

```python
import functools

import jax
import jax.numpy as jnp
from jax import lax
from jax.experimental import pallas as pl
from jax.experimental.pallas import tpu as pltpu

F32 = jnp.float32
BF16 = jnp.bfloat16

ATTN_HEAD_DIM = 64
N_MLSTM_HEADS = 4
CONV_WIDTH = 4
DILATED_PATTERNS = ((128, 1), (512, 4), (2048, 16))
ATTN_BLOCK = 128
N_MOD = 6
RMS_EPS = 1e-6

LANES = 128
MXU_WIDTH = 256
BF16_SUBLANES = 16
VMEM_LIMIT_BYTES = 56 * 1024 * 1024
F32_SUBLANES = 8
MOD_ROWS = F32_SUBLANES
LAYOUT_ROWS = 512
GATE_LANES = 128
INPROJ_ROWS = LAYOUT_ROWS
ATTN_MAX_SUB_BLOCKS = 8
MLSTM_CHUNK = 256
MLSTM_BATCH_ROWS = 1
FFN_ROWS = LAYOUT_ROWS

NEG_INF = float("-inf")
LOG2E = 1.4426950408889634
LN2 = 0.6931471805599453


def _dot(a, b, **kw):
    return jnp.dot(a, b, preferred_element_type=F32, **kw)


def _dot_nt(a, b, **kw):
    return lax.dot_general(a, b, (((1,), (1,)), ((), ())), preferred_element_type=F32, **kw)


def _dot_tn(a, b):
    return lax.dot_general(a, b, (((0,), (0,)), ((), ())), preferred_element_type=F32)


def _sigmoid(z):
    return 1.0 / (1.0 + jnp.exp(-z))


def _log_sigmoid(z):
    return jnp.minimum(z, 0.0) - jnp.log1p(jnp.exp(-jnp.abs(z)))


def _strided_shape(b, s, w, dilation):
    return (b, s // dilation, dilation * w)


def _gather_chain(dilations):
    chain, parent = [], 1
    for d in sorted(set(dilations)):
        if d == 1:
            continue
        assert d % parent == 0 and d // parent < F32_SUBLANES
        chain.append((parent, d, d // parent))
        parent = d
    return chain


def _mod_kernel(ct_ref, w_ref, b_ref, o_ref, *, n_rows):
    ct = ct_ref[...]
    sc = ct * _sigmoid(ct)
    w = w_ref[...]
    rows = [jnp.sum(w * sc[:, r:r + 1], axis=0, keepdims=True) for r in range(n_rows)]
    rows += [jnp.zeros_like(rows[0])] * (MOD_ROWS - n_rows)
    o_ref[...] = jnp.concatenate(rows, axis=0) + b_ref[...]


def _mod_call(c_t, w_ada, b_ada, *, n_rows):
    d, n = w_ada.shape
    tn = n // 4
    assert tn % LANES == 0
    return pl.pallas_call(
        functools.partial(_mod_kernel, n_rows=n_rows),
        out_shape=jax.ShapeDtypeStruct((MOD_ROWS, n), F32),
        grid=(n // tn,),
        in_specs=[pl.BlockSpec((d, MOD_ROWS), lambda j: (0, 0)),
                  pl.BlockSpec((d, tn), lambda j: (0, j)),
                  pl.BlockSpec((1, tn), lambda j: (0, j))],
        out_specs=pl.BlockSpec((MOD_ROWS, tn), lambda j: (0, j)),
        compiler_params=pltpu.CompilerParams(dimension_semantics=("arbitrary",),
                                             vmem_limit_bytes=VMEM_LIMIT_BYTES),
        name="mod",
    )(c_t, w_ada, b_ada)


def _inproj_kernel(x_ref, mod_ref, g_ref, w_ref, wg_ref, wgt_ref, gq_ref, gk_ref, hp_ref,
                   bcol_ref, brow_ref, *rest, attn_w, ml_w, scale, dilations):
    n_lay = len(dilations)
    q_refs, k_refs, v_refs = rest[:n_lay], rest[n_lay:2 * n_lay], rest[2 * n_lay:3 * n_lay]
    qkm_ref, vm_ref, og_ref, gcol_ref, grow_ref, perm_s, wb_s = rest[3 * n_lay:]
    tm = x_ref.shape[1]

    @pl.when((pl.program_id(0) == 0) & (pl.program_id(1) == 0))
    def _():
        for c0 in range(0, wb_s.shape[1], MXU_WIDTH):
            wb_s[:, c0:c0 + MXU_WIDTH] = w_ref[:, c0:c0 + MXU_WIDTH].astype(BF16)

    def emit(val, refs):
        w = val.shape[-1]
        n_cg = w // LANES
        ref_of = dict(zip(dilations, refs))
        if 1 in ref_of:
            ref_of[1][0] = val.astype(BF16)
        chain = _gather_chain(dilations)
        for cg in range(n_cg):
            perm_s[0, cg] = val[:, cg * LANES:(cg + 1) * LANES]
        for level, (parent, d, f) in enumerate(chain):
            src, dst = perm_s.at[level % 2], perm_s.at[(level + 1) % 2]
            n = tm // d
            for c in range(parent):
                for a in range(f):
                    r = a * parent + c
                    for cg in range(n_cg):
                        rows = src[cg, pl.ds(c * (tm // parent) + a, n, stride=f), :]
                        c0 = r * w + cg * LANES
                        ref_of[d][0, :, c0:c0 + LANES] = rows.astype(BF16)
                        if level + 1 < len(chain):
                            dst[cg, r * n:(r + 1) * n, :] = rows

    x = x_ref[0]
    ms = jnp.mean(x * x, axis=-1, keepdims=True)
    y = x * lax.rsqrt(ms + RMS_EPS) * g_ref[...]
    h = (y * (1.0 + mod_ref[0, 1:2, :]) + mod_ref[0, 0:1, :]).astype(BF16)

    n_attn = 3 * attn_w
    xa = _dot(h, wb_s[:, :n_attn])
    hp = hp_ref[...]

    def head_norm(t, g):
        t2 = (t * t).astype(BF16)
        gw = hp.shape[0]
        msq = jnp.concatenate([_dot(t2[:, c0:c0 + gw], hp) for c0 in range(0, t.shape[-1], gw)], axis=-1)
        return t * lax.rsqrt(msq + RMS_EPS) * g

    emit(head_norm(xa[:, :attn_w], gq_ref[...]) * scale, q_refs)
    emit(head_norm(xa[:, attn_w:2 * attn_w], gk_ref[...]), k_refs)
    emit(xa[:, 2 * attn_w:], v_refs)

    xm = _dot(h, wb_s[:, n_attn:n_attn + 4 * ml_w])
    qkm_ref[0] = xm[:, :2 * ml_w].astype(BF16)
    vm_ref[0] = xm[:, 2 * ml_w:3 * ml_w].astype(BF16)
    og_ref[0] = _sigmoid(xm[:, 3 * ml_w:]).astype(BF16)

    nh = N_MLSTM_HEADS
    zc = _dot(h, wg_ref[...]) + bcol_ref[...]
    lane = lax.broadcasted_iota(jnp.int32, zc.shape, 1)
    gcol_ref[0] = jnp.where(lane < nh, zc, jnp.where(lane < 2 * nh, _log_sigmoid(zc), 0.0)) * LOG2E
    zr = _dot_nt(wgt_ref[...], h)[:2 * nh] + brow_ref[...]
    row = lax.broadcasted_iota(jnp.int32, zr.shape, 0)
    grow_ref[0] = jnp.where(row < nh, zr, _log_sigmoid(zr)) * LOG2E


def _inproj_call(x, mod, g_mix, w, wg, wgt, gq, gk, hp, bcol, brow, *, attn_w, ml_w, scale, dilations):
    b, s, d = x.shape
    tm = INPROJ_ROWS
    nh2 = 2 * N_MLSTM_HEADS
    assert all(tm % (dl * BF16_SUBLANES) == 0 for dl in dilations)
    const = lambda shape: pl.BlockSpec(shape, lambda bi, i: (0,) * len(shape))
    tok = lambda w: pl.BlockSpec((1, tm, w), lambda bi, i: (bi, i, 0))
    strided_shapes = tuple(jax.ShapeDtypeStruct(_strided_shape(b, s, attn_w, dl), BF16) for dl in dilations)
    strided_specs = tuple(pl.BlockSpec((1, tm // dl, dl * attn_w), lambda bi, i: (bi, i, 0)) for dl in dilations)
    out_shape = strided_shapes * 3 + (
        jax.ShapeDtypeStruct((b, s, 2 * ml_w), BF16),
        jax.ShapeDtypeStruct((b, s, ml_w), BF16),
        jax.ShapeDtypeStruct((b, s, ml_w), BF16),
        jax.ShapeDtypeStruct((b, s, GATE_LANES), F32),
        jax.ShapeDtypeStruct((b, nh2, s), F32))
    outs = pl.pallas_call(
        functools.partial(_inproj_kernel, attn_w=attn_w, ml_w=ml_w, scale=scale, dilations=dilations),
        out_shape=out_shape,
        grid=(b, s // tm),
        in_specs=[tok(d),
                  pl.BlockSpec((1, N_MOD, d), lambda bi, i: (bi, 0, 0)),
                  const((1, d)),
                  pl.BlockSpec(w.shape, lambda bi, i: (0, 0), pipeline_mode=pl.Buffered(1)),
                  const(wg.shape), const(wgt.shape),
                  const((1, attn_w)), const((1, attn_w)), const(hp.shape),
                  const((1, GATE_LANES)), const((nh2, 1))],
        out_specs=strided_specs * 3 + (tok(2 * ml_w), tok(ml_w), tok(ml_w), tok(GATE_LANES),
                                       pl.BlockSpec((1, nh2, tm), lambda bi, i: (bi, 0, i))),
        scratch_shapes=[pltpu.VMEM((2, attn_w // LANES, tm, LANES), F32),
                        pltpu.VMEM((d, 3 * attn_w + 4 * ml_w), BF16)],
        compiler_params=pltpu.CompilerParams(dimension_semantics=("arbitrary", "arbitrary"),
                                             vmem_limit_bytes=VMEM_LIMIT_BYTES),
        name="inproj",
    )(x, mod, g_mix, w, wg, wgt, gq, gk, hp, bcol, brow)
    n_lay = len(dilations)
    return (outs[:n_lay], outs[n_lay:2 * n_lay], outs[2 * n_lay:3 * n_lay]) + tuple(outs[3 * n_lay:])


def _attn_kernel(q_ref, kp_ref, kc_ref, vp_ref, vc_ref, o_ref, lse_ref, k_s, v_s, bias_s, *, n_back, n_heads):
    blk = ATTN_BLOCK
    sub_blocks = q_ref.shape[1] // blk
    hd = ATTN_HEAD_DIM
    w = n_heads * hd
    n_res = q_ref.shape[2] // w
    n_pairs = n_heads // 2
    n = pl.program_id(2)
    lane = lax.broadcasted_iota(jnp.int32, (blk, LANES), 1)
    first_head = lane < hd
    ind_a = jnp.where(first_head, 1.0, 0.0).astype(BF16)
    ind_b = jnp.where(first_head, 0.0, 1.0).astype(BF16)
    for res in range(n_res):
        k_s[res, 0:blk, :] = kp_ref[0, :, res * w:(res + 1) * w]
        k_s[res, blk:, :] = kc_ref[0, :, res * w:(res + 1) * w]
        for kb in range(sub_blocks + 1):
            for j in range(n_pairs):
                cols = slice(res * w + j * LANES, res * w + (j + 1) * LANES)
                src = vp_ref[0, :, cols] if kb == 0 else vc_ref[0, (kb - 1) * blk:kb * blk, cols]
                ra, rb, c0 = 2 * kb * blk, (2 * kb + 1) * blk, 2 * j * LANES
                v_s[res, ra:ra + blk, c0:c0 + LANES] = jnp.where(first_head, src, 0)
                v_s[res, ra:ra + blk, c0 + LANES:c0 + 2 * LANES] = ind_a
                v_s[res, rb:rb + blk, c0:c0 + LANES] = jnp.where(first_head, 0, src)
                v_s[res, rb:rb + blk, c0 + LANES:c0 + 2 * LANES] = ind_b

    row = lax.broadcasted_iota(jnp.int32, (2 * blk, 2 * blk), 0) & (blk - 1)
    col = lax.broadcasted_iota(jnp.int32, (2 * blk, 2 * blk), 1)
    band = jnp.logical_and(col >= row + (blk - n_back), col <= row + blk)
    bias_s[0] = jnp.where(band, 0.0, NEG_INF)
    bias_s[1] = jnp.where(jnp.logical_and(band, col >= blk), 0.0, NEG_INF)

    units = [(res, i, j) for res in range(n_res) for i in range(sub_blocks) for j in range(n_pairs)]
    scores, maxes, probs = [], [], []
    for res, i, j in units:
        cols = slice(j * LANES, (j + 1) * LANES)
        q = q_ref[0, i * blk:(i + 1) * blk, res * w + j * LANES:res * w + (j + 1) * LANES]
        ks = k_s[res, i * blk:(i + 2) * blk, cols]
        bias = bias_s[jnp.where(n == 0, 1, 0)] if i == 0 else bias_s[0]
        q2 = jnp.concatenate([jnp.where(first_head, q, 0), jnp.where(first_head, 0, q)], axis=0)
        s = _dot_nt(q2, ks) + bias
        scores.append(s)
        maxes.append(jnp.max(s, axis=-1, keepdims=True))
    for s, m in zip(scores, maxes):
        p = jnp.exp2(s - m).astype(BF16)
        probs.append(jnp.concatenate([p[:blk, :blk], p[blk:, :blk], p[:blk, blk:], p[blk:, blk:]], axis=1))
    for res in range(n_res):
        for i in range(sub_blocks):
            outs = []
            stats = jnp.zeros((blk, LANES), F32)
            for j in range(n_pairs):
                u = (res * sub_blocks + i) * n_pairs + j
                vw = v_s[res, 2 * i * blk:2 * (i + 2) * blk, 2 * j * LANES:(2 * j + 2) * LANES]
                ov = _dot(probs[u], vw)
                outs.append(ov[:, :LANES] / ov[:, LANES:])
                m = jnp.broadcast_to(maxes[u], (2 * blk, LANES))
                lse = jnp.where(first_head, m[:blk], m[blk:]) * LN2 + jnp.log(ov[:, LANES:])
                stats = jnp.where((lane & (hd - 1)) == j, lse, stats)
            rows = slice(i * blk, (i + 1) * blk)
            o_ref[0, rows, res * w:(res + 1) * w] = jnp.concatenate(outs, axis=-1).astype(o_ref.dtype)
            lse_ref[0, rows, res * LANES:(res + 1) * LANES] = stats


def _attn_call(q, k, v, *, width, window, dilation):
    b, ls, _ = q.shape
    w = width
    n_back = window // dilation
    blk = ATTN_BLOCK
    assert n_back <= blk
    sub_blocks = min(ATTN_MAX_SUB_BLOCKS, ls // blk)
    step_rows = sub_blocks * blk
    assert ls % step_rows == 0
    n_heads = w // ATTN_HEAD_DIM
    assert n_heads <= LANES and n_heads % 2 == 0 and 2 * ATTN_HEAD_DIM == LANES
    n_res = max(1, min(dilation, ATTN_MAX_SUB_BLOCKS // sub_blocks))
    assert dilation % n_res == 0
    cur = pl.BlockSpec((1, step_rows, n_res * w), lambda bi, r, n: (bi, n, r))
    prev = pl.BlockSpec((1, blk, n_res * w),
                        lambda bi, r, n: (bi, jnp.maximum(n * sub_blocks - 1, 0), r))
    return pl.pallas_call(
        functools.partial(_attn_kernel, n_back=n_back, n_heads=n_heads),
        out_shape=(jax.ShapeDtypeStruct((b, ls, dilation * w), BF16),
                   jax.ShapeDtypeStruct((b, ls, dilation * LANES), F32)),
        grid=(b, dilation // n_res, ls // step_rows),
        in_specs=[cur, prev, cur, prev, cur],
        out_specs=(cur, pl.BlockSpec((1, step_rows, n_res * LANES), lambda bi, r, n: (bi, n, r))),
        scratch_shapes=[pltpu.VMEM((n_res, blk + step_rows, w), BF16),
                        pltpu.VMEM((n_res, 2 * (blk + step_rows), 2 * w), BF16),
                        pltpu.VMEM((2, 2 * blk, 2 * blk), F32)],
        compiler_params=pltpu.CompilerParams(
            dimension_semantics=("arbitrary", "arbitrary", "arbitrary"),
            vmem_limit_bytes=VMEM_LIMIT_BYTES),
        name=f"attn_d{dilation}",
    )(q, k, k, v, v)


def _mlstm_kernel(qk_ref, v_ref, og_ref, gcol_ref, grow_ref, wc_ref, bc_ref, gn_ref, tri_ref, hm_ref,
                  o_ref, tail_s, c_s, m_s, *, ml_w):
    L = MLSTM_CHUNK
    nh = N_MLSTM_HEADS
    dh = ml_w // nh

    @pl.when(pl.program_id(1) == 0)
    def _():
        tail_s[...] = jnp.zeros_like(tail_s)
        c_s[...] = jnp.zeros_like(c_s)
        m_s[...] = jnp.zeros_like(m_s)

    tri = tri_ref[...]
    ti = lax.broadcasted_iota(jnp.int32, (L, L), 0)
    si = lax.broadcasted_iota(jnp.int32, (L, L), 1)
    causal = si <= ti
    ones = jnp.ones((L, dh), BF16)
    hp = hm_ref[...]
    for bi in range(qk_ref.shape[0]):
        _mlstm_chunk(bi, qk_ref, v_ref, og_ref, gcol_ref, grow_ref, wc_ref, bc_ref, gn_ref, o_ref,
                     tail_s, c_s, m_s, tri, causal, ones, hp, ml_w=ml_w)


def _mlstm_chunk(bi, qk_ref, v_ref, og_ref, gcol_ref, grow_ref, wc_ref, bc_ref, gn_ref, o_ref,
                 tail_s, c_s, m_s, tri, causal, ones, hp, *, ml_w):
    L = MLSTM_CHUNK
    nh = N_MLSTM_HEADS
    dh = ml_w // nh
    x = qk_ref[bi].astype(F32)
    tail = tail_s[bi]
    row8 = lax.broadcasted_iota(jnp.int32, tail.shape, 0)
    acc = bc_ref[...] + x * wc_ref[CONV_WIDTH - 1:CONV_WIDTH, :]
    for back in range(1, CONV_WIDTH):
        rolled = pltpu.roll(x, back, 0)
        top = jnp.where(row8 >= back, rolled[:MOD_ROWS], pltpu.roll(tail, back, 0))
        shifted = jnp.concatenate([top, rolled[MOD_ROWS:]], axis=0)
        acc = acc + shifted * wc_ref[CONV_WIDTH - 1 - back:CONV_WIDTH - back, :]
    tail_s[bi] = x[L - MOD_ROWS:]
    qk = acc * _sigmoid(acc)

    gcol = gcol_ref[bi]
    grow = grow_ref[bi]
    bcol_all = _dot(tri, gcol, precision=lax.Precision.HIGHEST)
    brow_all = _dot_nt(grow, tri, precision=lax.Precision.HIGHEST)

    ks = [qk[:, ml_w + h * dh:ml_w + (h + 1) * dh] * (dh ** -0.5) for h in range(nh)]

    heads = []
    for h in range(nh):
        st = bi * nh + h
        b_c = jnp.broadcast_to(bcol_all[:, nh + h:nh + h + 1], (L, dh))
        i_c = jnp.broadcast_to(gcol[:, h:h + 1], (L, dh))
        b_r = brow_all[nh + h:nh + h + 1, :]
        i_r = grow[h:h + 1, :]
        m_prev = m_s[st, 0:1, :]
        log_d = jnp.where(causal, b_c[:, 0:1] + (i_r - b_r), NEG_INF)
        m_inter = b_c + m_prev
        m_t = jnp.maximum(m_inter, jnp.max(log_d, axis=-1, keepdims=True))
        d_mat = jnp.exp2(log_d - m_t[:, 0:1])
        inter = jnp.exp2(m_inter - m_t)
        b_last = b_c[L - 1:L, :]
        w_log = b_last - b_c + i_c
        m_new = jnp.maximum(b_last + m_prev, jnp.max(w_log, axis=0, keepdims=True))
        wgt = jnp.exp2(w_log - m_new)
        decay = jnp.exp2(b_last + m_prev - m_new)
        heads.append((m_t, d_mat, inter, m_new, wgt, decay))

    outs = []
    for h, (m_t, d_mat, inter, _, _, _) in enumerate(heads):
        hcols = slice(h * dh, (h + 1) * dh)
        q = qk[:, hcols].astype(BF16)
        v_ext = jnp.concatenate([v_ref[bi, :, hcols], ones], axis=-1)
        c_prev = c_s[bi * nh + h]
        s_qk = _dot_nt(q, ks[h].astype(BF16)) * d_mat
        ext = jnp.concatenate([inter, inter], axis=-1) * _dot(q, c_prev.astype(BF16)) \
            + _dot(s_qk.astype(BF16), v_ext)
        outs.append(ext[:, :dh] / jnp.maximum(jnp.abs(ext[:, dh:]), jnp.exp2(-m_t)))
    hh = jnp.concatenate(outs, axis=-1)
    h2 = (hh * hh).astype(BF16)
    gw = hp.shape[0]
    msq = jnp.concatenate([_dot(h2[:, c0:c0 + gw], hp) for c0 in range(0, ml_w, gw)], axis=-1)
    hn = hh * lax.rsqrt(msq + RMS_EPS) * gn_ref[...]
    o_ref[bi] = (og_ref[bi].astype(F32) * hn).astype(o_ref.dtype)

    for h, (_, _, _, m_new, wgt, decay) in enumerate(heads):
        st = bi * nh + h
        hcols = slice(h * dh, (h + 1) * dh)
        v_ext = jnp.concatenate([v_ref[bi, :, hcols], ones], axis=-1)
        kw = (ks[h] * wgt).astype(BF16)
        c_s[st] = jnp.concatenate([decay, decay], axis=-1) * c_s[st] + _dot_tn(kw, v_ext)
        m_s[st] = jnp.broadcast_to(m_new, m_s.shape[1:])


def _mlstm_call(qkm, vm, og, gcol, grow, w_conv, b_conv, g_norm, tri, hmean):
    b, s, ml_w = vm.shape
    L = MLSTM_CHUNK
    nh = N_MLSTM_HEADS
    dh = ml_w // nh
    bb = MLSTM_BATCH_ROWS
    assert b % bb == 0
    const = lambda shape: pl.BlockSpec(shape, lambda g, i: (0,) * len(shape))
    tok = lambda w: pl.BlockSpec((bb, L, w), lambda g, i: (g, i, 0))
    return pl.pallas_call(
        functools.partial(_mlstm_kernel, ml_w=ml_w),
        out_shape=jax.ShapeDtypeStruct((b, s, ml_w), BF16),
        grid=(b // bb, s // L),
        in_specs=[tok(2 * ml_w), tok(ml_w), tok(ml_w), tok(GATE_LANES),
                  pl.BlockSpec((bb, 2 * nh, L), lambda g, i: (g, 0, i)),
                  const(w_conv.shape), const((1, 2 * ml_w)), const((1, ml_w)),
                  const((L, L)), const(hmean.shape)],
        out_specs=tok(ml_w),
        scratch_shapes=[pltpu.VMEM((bb, MOD_ROWS, 2 * ml_w), F32),
                        pltpu.VMEM((bb * nh, dh, 2 * dh), F32),
                        pltpu.VMEM((bb * nh, MOD_ROWS, dh), F32)],
        compiler_params=pltpu.CompilerParams(dimension_semantics=("arbitrary", "arbitrary"),
                                             vmem_limit_bytes=VMEM_LIMIT_BYTES),
        name="mlstm",
    )(qkm, vm, og, gcol, grow, w_conv, b_conv, g_norm, tri, hmean)


def _outffn_kernel(x_ref, mod_ref, g_ref, *rest, dilations):
    n_lay = len(dilations)
    o_refs, l_refs = rest[:n_lay], rest[n_lay:2 * n_lay]
    (hm_ref, ex_ref, wo_ref, wg_ref, wu_ref, wd_ref, out_ref,
     operm_s, lperm_s) = rest[2 * n_lay:]
    tm = x_ref.shape[1]

    def natural(ref, d, scratch):
        if d == 1:
            return ref[0].astype(F32)
        n_cg = scratch.shape[1]
        chain = [step for step in _gather_chain(dilations) if step[1] <= d]
        for level, (parent, dd, f) in enumerate(reversed(chain)):
            dst = scratch.at[level % 2]
            src = scratch.at[(level + 1) % 2]
            n = tm // dd
            for c in range(parent):
                for a in range(f):
                    r = a * parent + c
                    for cg in range(n_cg):
                        if level == 0:
                            c0 = (r * n_cg + cg) * LANES
                            rows = ref[0, :, c0:c0 + LANES].astype(F32)
                        else:
                            rows = src[cg, r * n:(r + 1) * n, :]
                        dst[cg, pl.ds(c * (tm // parent) + a, n, stride=f), :] = rows
        final = scratch.at[(len(chain) - 1) % 2]
        return jnp.concatenate([final[cg] for cg in range(n_cg)], axis=-1)

    lses = [natural(ref, d, lperm_s) for ref, d in zip(l_refs, dilations)]
    mx = functools.reduce(jnp.maximum, lses)
    es = [jnp.exp(l - mx) for l in lses]
    inv = 1.0 / functools.reduce(jnp.add, es)
    ex = ex_ref[...]
    attn = None
    for e, ref, d in zip(es, o_refs, dilations):
        term = _dot((e * inv).astype(BF16), ex) * natural(ref, d, operm_s)
        attn = term if attn is None else attn + term
    aw = attn.shape[-1]
    y = _dot(attn.astype(BF16), wo_ref[:aw, :]) + _dot(hm_ref[0], wo_ref[aw:, :])
    x1 = x_ref[0] + mod_ref[0, 2:3, :] * y
    ms = jnp.mean(x1 * x1, axis=-1, keepdims=True)
    hn = x1 * lax.rsqrt(ms + RMS_EPS) * g_ref[...]
    hb = (hn * (1.0 + mod_ref[0, 4:5, :]) + mod_ref[0, 3:4, :]).astype(BF16)
    g = _dot(hb, wg_ref[...])
    u = _dot(hb, wu_ref[...])
    a = (g * _sigmoid(g) * u).astype(BF16)
    out_ref[0] = x1 + mod_ref[0, 5:6, :] * _dot(a, wd_ref[...])


def _outffn_call(x, mod, g_ffn, os_, lses, hm, expand, wo, wg, wu, wd, *, dilations):
    b, s, d = x.shape
    tm = FFN_ROWS
    assert all(tm % (dl * BF16_SUBLANES) == 0 for dl in dilations)
    aw = expand.shape[1]
    const = lambda shape: pl.BlockSpec(shape, lambda bi, i: (0,) * len(shape))
    tok = lambda w: pl.BlockSpec((1, tm, w), lambda bi, i: (bi, i, 0))
    strided = lambda w: tuple(pl.BlockSpec((1, tm // dl, dl * w), lambda bi, i: (bi, i, 0)) for dl in dilations)
    return pl.pallas_call(
        functools.partial(_outffn_kernel, dilations=dilations),
        out_shape=jax.ShapeDtypeStruct((b, s, d), F32),
        grid=(b, s // tm),
        in_specs=[tok(d), pl.BlockSpec((1, N_MOD, d), lambda bi, i: (bi, 0, 0)), const((1, d)),
                  *strided(aw), *strided(LANES), tok(hm.shape[-1]),
                  const(expand.shape), const(wo.shape),
                  const(wg.shape), const(wu.shape), const(wd.shape)],
        out_specs=tok(d),
        scratch_shapes=[pltpu.VMEM((2, aw // LANES, tm, LANES), F32), pltpu.VMEM((2, 1, tm, LANES), F32)],
        compiler_params=pltpu.CompilerParams(dimension_semantics=("arbitrary", "arbitrary"),
                                             vmem_limit_bytes=VMEM_LIMIT_BYTES),
        name="outffn",
    )(x, mod, g_ffn, *os_, *lses, hm, expand, wo, wg, wu, wd)


def _block_diag_mean(width, group):
    idx = jnp.arange(width) // group
    return jnp.where(idx[:, None] == idx[None, :], 1.0 / group, 0.0).astype(BF16)


def kernel(x, c, g_mix, w_in, w_conv, b_conv, b_igate, b_fgate, q_norm_g, k_norm_g, mlstm_norm_g, w_out,
           g_ffn, w_gate, w_up, w_down, w_ada, b_ada):
    b, s, d = x.shape
    depth = g_mix.shape[0]
    attn_w = d // 2
    ml_w = d - attn_w
    nh = N_MLSTM_HEADS
    n_attn_heads = attn_w // ATTN_HEAD_DIM
    scale = ATTN_HEAD_DIM ** -0.5 * LOG2E
    dilations = tuple(dl for _, dl in DILATED_PATTERNS)
    assert b <= MOD_ROWS and 2 * nh <= GATE_LANES

    c_t = jnp.zeros((d, MOD_ROWS), F32).at[:, :b].set(c.T)
    assert attn_w % MXU_WIDTH == 0 and MXU_WIDTH % ATTN_HEAD_DIM == 0
    hmean_attn = _block_diag_mean(MXU_WIDTH, ATTN_HEAD_DIM)
    assert ml_w % MXU_WIDTH == 0 and MXU_WIDTH % (ml_w // nh) == 0
    hmean_ml = _block_diag_mean(MXU_WIDTH, ml_w // nh)
    stat_lane = jnp.arange(LANES)
    head_of_stat = jnp.where(stat_lane % ATTN_HEAD_DIM < n_attn_heads // 2,
                             2 * (stat_lane % ATTN_HEAD_DIM) + stat_lane // ATTN_HEAD_DIM, -1)
    head_of_col = jnp.arange(attn_w) // ATTN_HEAD_DIM
    expand = (head_of_stat[:, None] == head_of_col[None, :]).astype(BF16)
    tri = jnp.tril(jnp.ones((MLSTM_CHUNK, MLSTM_CHUNK), F32))

    for l in range(depth):
        mod = _mod_call(c_t, w_ada[l], b_ada[l][None, :], n_rows=b)[:b].reshape(b, N_MOD, d)

        n_main = 3 * attn_w + 4 * ml_w
        assert n_main % MXU_WIDTH == 0
        wg_cols = w_in[l][:, n_main:]
        wg = jnp.zeros((d, GATE_LANES), F32).at[:, :2 * nh].set(wg_cols).astype(BF16)
        wgt = jnp.zeros((BF16_SUBLANES, d), F32).at[:2 * nh].set(wg_cols.T).astype(BF16)
        gate_bias = jnp.concatenate([b_igate[l], b_fgate[l]])
        bcol = jnp.zeros((1, GATE_LANES), F32).at[0, :2 * nh].set(gate_bias)
        brow = gate_bias[:, None]
        gq = jnp.tile(q_norm_g[l], n_attn_heads)[None, :]
        gk = jnp.tile(k_norm_g[l], n_attn_heads)[None, :]

        qs, ks, vs, qkm, vm, og, gcol, grow = _inproj_call(
            x, mod, g_mix[l][None, :], w_in[l], wg, wgt, gq, gk, hmean_attn, bcol, brow,
            attn_w=attn_w, ml_w=ml_w, scale=scale, dilations=dilations)

        os_, lses = [], []
        for (window, dilation), q, k, v in zip(DILATED_PATTERNS, qs, ks, vs):
            o, lse = _attn_call(q, k, v, width=attn_w, window=window, dilation=dilation)
            os_.append(o)
            lses.append(lse)

        hm = _mlstm_call(qkm, vm, og, gcol, grow, w_conv[l], b_conv[l][None, :],
                         mlstm_norm_g[l][None, :], tri, hmean_ml)

        x = _outffn_call(x, mod, g_ffn[l][None, :], os_, lses, hm, expand,
                         w_out[l].astype(BF16), w_gate[l].astype(BF16), w_up[l].astype(BF16), w_down[l].astype(BF16),
                         dilations=dilations)
    return x
```

```python
import functools

import jax
import jax.numpy as jnp
from jax import lax
from jax.experimental import pallas as pl
from jax.experimental.pallas import tpu as pltpu

F32 = jnp.float32
BF16 = jnp.bfloat16

ATTN_HEAD_DIM = 64
N_MLSTM_HEADS = 4
CONV_WIDTH = 4
DILATED_PATTERNS = ((128, 1), (512, 4), (2048, 16))
ATTN_BLOCK = 128
N_MOD = 6
RMS_EPS = 1e-6

LANES = 128
MXU_WIDTH = 256
BF16_SUBLANES = 16
VMEM_LIMIT_BYTES = 56 * 1024 * 1024
F32_SUBLANES = 8
MOD_ROWS = F32_SUBLANES
LAYOUT_ROWS = 512
GATE_LANES = 128
INPROJ_ROWS = LAYOUT_ROWS
ATTN_MAX_SUB_BLOCKS = 8
MLSTM_CHUNK = 256
MLSTM_BATCH_ROWS = 1
FFN_ROWS = LAYOUT_ROWS

NEG_INF = float("-inf")
LOG2E = 1.4426950408889634
LN2 = 0.6931471805599453


def _dot(a, b, **kw):
    return jnp.dot(a, b, preferred_element_type=F32, **kw)


def _dot_nt(a, b, **kw):
    return lax.dot_general(a, b, (((1,), (1,)), ((), ())), preferred_element_type=F32, **kw)


def _dot_tn(a, b):
    return lax.dot_general(a, b, (((0,), (0,)), ((), ())), preferred_element_type=F32)


def _sigmoid(z):
    return 1.0 / (1.0 + jnp.exp(-z))


def _log_sigmoid(z):
    return jnp.minimum(z, 0.0) - jnp.log1p(jnp.exp(-jnp.abs(z)))


def _strided_shape(b, s, w, dilation):
    return (b, s // dilation, dilation * w)


def _gather_chain(dilations):
    chain, parent = [], 1
    for d in sorted(set(dilations)):
        if d == 1:
            continue
        assert d % parent == 0 and d // parent < F32_SUBLANES
        chain.append((parent, d, d // parent))
        parent = d
    return chain


def _mod_kernel(ct_ref, w_ref, b_ref, o_ref, *, n_rows):
    ct = ct_ref[...]
    sc = ct * _sigmoid(ct)
    w = w_ref[...]
    rows = [jnp.sum(w * sc[:, r:r + 1], axis=0, keepdims=True) for r in range(n_rows)]
    rows += [jnp.zeros_like(rows[0])] * (MOD_ROWS - n_rows)
    o_ref[...] = jnp.concatenate(rows, axis=0) + b_ref[...]


def _mod_call(c_t, w_ada, b_ada, *, layer, n_rows):
    _, d, n = w_ada.shape
    tn = n // 4
    assert tn % LANES == 0
    return pl.pallas_call(
        functools.partial(_mod_kernel, n_rows=n_rows),
        out_shape=jax.ShapeDtypeStruct((MOD_ROWS, n), F32),
        grid=(n // tn,),
        in_specs=[pl.BlockSpec((d, MOD_ROWS), lambda j: (0, 0)),
                  pl.BlockSpec((None, d, tn), lambda j: (layer, 0, j)),
                  pl.BlockSpec((1, tn), lambda j: (0, j))],
        out_specs=pl.BlockSpec((MOD_ROWS, tn), lambda j: (0, j)),
        compiler_params=pltpu.CompilerParams(dimension_semantics=("arbitrary",),
                                             vmem_limit_bytes=VMEM_LIMIT_BYTES),
        name="mod",
    )(c_t, w_ada, b_ada)


def _inproj_kernel(x_ref, mod_ref, g_ref, w_ref, wg_ref, wgt_ref, gq_ref, gk_ref, hp_ref,
                   bcol_ref, brow_ref, *rest, attn_w, ml_w, scale, dilations):
    n_lay = len(dilations)
    q_refs, k_refs, v_refs = rest[:n_lay], rest[n_lay:2 * n_lay], rest[2 * n_lay:3 * n_lay]
    qkm_ref, vm_ref, og_ref, gcol_ref, grow_ref, perm_s, wb_s = rest[3 * n_lay:]
    tm = x_ref.shape[1]

    @pl.when((pl.program_id(0) == 0) & (pl.program_id(1) == 0))
    def _():
        for c0 in range(0, wb_s.shape[1], MXU_WIDTH):
            wb_s[:, c0:c0 + MXU_WIDTH] = w_ref[:, c0:c0 + MXU_WIDTH].astype(BF16)

    def emit(val, refs):
        w = val.shape[-1]
        n_cg = w // LANES
        ref_of = dict(zip(dilations, refs))
        if 1 in ref_of:
            ref_of[1][0] = val.astype(BF16)
        chain = _gather_chain(dilations)
        for cg in range(n_cg):
            perm_s[0, cg] = val[:, cg * LANES:(cg + 1) * LANES]
        for level, (parent, d, f) in enumerate(chain):
            src, dst = perm_s.at[level % 2], perm_s.at[(level + 1) % 2]
            n = tm // d
            for c in range(parent):
                for a in range(f):
                    r = a * parent + c
                    for cg in range(n_cg):
                        rows = src[cg, pl.ds(c * (tm // parent) + a, n, stride=f), :]
                        c0 = r * w + cg * LANES
                        ref_of[d][0, :, c0:c0 + LANES] = rows.astype(BF16)
                        if level + 1 < len(chain):
                            dst[cg, r * n:(r + 1) * n, :] = rows

    x = x_ref[0]
    ms = jnp.mean(x * x, axis=-1, keepdims=True)
    y = x * lax.rsqrt(ms + RMS_EPS) * g_ref[...]
    h = (y * (1.0 + mod_ref[0, 1:2, :]) + mod_ref[0, 0:1, :]).astype(BF16)

    n_attn = 3 * attn_w
    xa = _dot(h, wb_s[:, :n_attn])
    hp = hp_ref[...]

    def head_norm(t, g):
        t2 = (t * t).astype(BF16)
        gw = hp.shape[0]
        msq = jnp.concatenate([_dot(t2[:, c0:c0 + gw], hp) for c0 in range(0, t.shape[-1], gw)], axis=-1)
        return t * lax.rsqrt(msq + RMS_EPS) * g

    emit(head_norm(xa[:, :attn_w], gq_ref[...]) * scale, q_refs)
    emit(head_norm(xa[:, attn_w:2 * attn_w], gk_ref[...]), k_refs)
    emit(xa[:, 2 * attn_w:], v_refs)

    xm = _dot(h, wb_s[:, n_attn:n_attn + 4 * ml_w])
    qkm_ref[0] = xm[:, :2 * ml_w].astype(BF16)
    vm_ref[0] = xm[:, 2 * ml_w:3 * ml_w].astype(BF16)
    og_ref[0] = _sigmoid(xm[:, 3 * ml_w:]).astype(BF16)

    nh = N_MLSTM_HEADS
    zc = _dot(h, wg_ref[...]) + bcol_ref[...]
    lane = lax.broadcasted_iota(jnp.int32, zc.shape, 1)
    gcol_ref[0] = jnp.where(lane < nh, zc, jnp.where(lane < 2 * nh, _log_sigmoid(zc), 0.0)) * LOG2E
    zr = _dot_nt(wgt_ref[...], h)[:2 * nh] + brow_ref[...]
    row = lax.broadcasted_iota(jnp.int32, zr.shape, 0)
    grow_ref[0] = jnp.where(row < nh, zr, _log_sigmoid(zr)) * LOG2E


def _inproj_call(x, mod, g_mix, w, wg, wgt, gq, gk, hp, bcol, brow, *, layer, attn_w, ml_w, scale, dilations):
    b, s, d = x.shape
    tm = INPROJ_ROWS
    nh2 = 2 * N_MLSTM_HEADS
    assert all(tm % (dl * BF16_SUBLANES) == 0 for dl in dilations)
    const = lambda shape: pl.BlockSpec(shape, lambda bi, i: (0,) * len(shape))
    tok = lambda w: pl.BlockSpec((1, tm, w), lambda bi, i: (bi, i, 0))
    strided_shapes = tuple(jax.ShapeDtypeStruct(_strided_shape(b, s, attn_w, dl), BF16) for dl in dilations)
    strided_specs = tuple(pl.BlockSpec((1, tm // dl, dl * attn_w), lambda bi, i: (bi, i, 0)) for dl in dilations)
    out_shape = strided_shapes * 3 + (
        jax.ShapeDtypeStruct((b, s, 2 * ml_w), BF16),
        jax.ShapeDtypeStruct((b, s, ml_w), BF16),
        jax.ShapeDtypeStruct((b, s, ml_w), BF16),
        jax.ShapeDtypeStruct((b, s, GATE_LANES), F32),
        jax.ShapeDtypeStruct((b, nh2, s), F32))
    outs = pl.pallas_call(
        functools.partial(_inproj_kernel, attn_w=attn_w, ml_w=ml_w, scale=scale, dilations=dilations),
        out_shape=out_shape,
        grid=(b, s // tm),
        in_specs=[tok(d),
                  pl.BlockSpec((1, N_MOD, d), lambda bi, i: (bi, 0, 0)),
                  const((1, d)),
                  pl.BlockSpec((None,) + w.shape[1:], lambda bi, i: (layer, 0, 0),
                               pipeline_mode=pl.Buffered(1)),
                  const(wg.shape), const(wgt.shape),
                  const((1, attn_w)), const((1, attn_w)), const(hp.shape),
                  const((1, GATE_LANES)), const((nh2, 1))],
        out_specs=strided_specs * 3 + (tok(2 * ml_w), tok(ml_w), tok(ml_w), tok(GATE_LANES),
                                       pl.BlockSpec((1, nh2, tm), lambda bi, i: (bi, 0, i))),
        scratch_shapes=[pltpu.VMEM((2, attn_w // LANES, tm, LANES), F32),
                        pltpu.VMEM((d, 3 * attn_w + 4 * ml_w), BF16)],
        compiler_params=pltpu.CompilerParams(dimension_semantics=("arbitrary", "arbitrary"),
                                             vmem_limit_bytes=VMEM_LIMIT_BYTES),
        name="inproj",
    )(x, mod, g_mix, w, wg, wgt, gq, gk, hp, bcol, brow)
    n_lay = len(dilations)
    return (outs[:n_lay], outs[n_lay:2 * n_lay], outs[2 * n_lay:3 * n_lay]) + tuple(outs[3 * n_lay:])


def _attn_kernel(q_ref, kp_ref, kc_ref, vp_ref, vc_ref, o_ref, lse_ref, k_s, v_s, bias_s, *, n_back, n_heads):
    blk = ATTN_BLOCK
    sub_blocks = q_ref.shape[1] // blk
    hd = ATTN_HEAD_DIM
    w = n_heads * hd
    n_res = q_ref.shape[2] // w
    n_pairs = n_heads // 2
    n = pl.program_id(2)
    lane = lax.broadcasted_iota(jnp.int32, (blk, LANES), 1)
    first_head = lane < hd
    ind_a = jnp.where(first_head, 1.0, 0.0).astype(BF16)
    ind_b = jnp.where(first_head, 0.0, 1.0).astype(BF16)
    for res in range(n_res):
        k_s[res, 0:blk, :] = kp_ref[0, :, res * w:(res + 1) * w]
        k_s[res, blk:, :] = kc_ref[0, :, res * w:(res + 1) * w]
        for kb in range(sub_blocks + 1):
            for j in range(n_pairs):
                cols = slice(res * w + j * LANES, res * w + (j + 1) * LANES)
                src = vp_ref[0, :, cols] if kb == 0 else vc_ref[0, (kb - 1) * blk:kb * blk, cols]
                ra, rb, c0 = 2 * kb * blk, (2 * kb + 1) * blk, 2 * j * LANES
                v_s[res, ra:ra + blk, c0:c0 + LANES] = jnp.where(first_head, src, 0)
                v_s[res, ra:ra + blk, c0 + LANES:c0 + 2 * LANES] = ind_a
                v_s[res, rb:rb + blk, c0:c0 + LANES] = jnp.where(first_head, 0, src)
                v_s[res, rb:rb + blk, c0 + LANES:c0 + 2 * LANES] = ind_b

    row = lax.broadcasted_iota(jnp.int32, (2 * blk, 2 * blk), 0) & (blk - 1)
    col = lax.broadcasted_iota(jnp.int32, (2 * blk, 2 * blk), 1)
    band = jnp.logical_and(col >= row + (blk - n_back), col <= row + blk)
    bias_s[0] = jnp.where(band, 0.0, NEG_INF)
    bias_s[1] = jnp.where(jnp.logical_and(band, col >= blk), 0.0, NEG_INF)

    units = [(res, i, j) for res in range(n_res) for i in range(sub_blocks) for j in range(n_pairs)]
    scores, maxes, probs = [], [], []
    for res, i, j in units:
        cols = slice(j * LANES, (j + 1) * LANES)
        q = q_ref[0, i * blk:(i + 1) * blk, res * w + j * LANES:res * w + (j + 1) * LANES]
        ks = k_s[res, i * blk:(i + 2) * blk, cols]
        bias = bias_s[jnp.where(n == 0, 1, 0)] if i == 0 else bias_s[0]
        q2 = jnp.concatenate([jnp.where(first_head, q, 0), jnp.where(first_head, 0, q)], axis=0)
        s = _dot_nt(q2, ks) + bias
        scores.append(s)
        maxes.append(jnp.max(s, axis=-1, keepdims=True))
    for s, m in zip(scores, maxes):
        p = jnp.exp2(s - m).astype(BF16)
        probs.append(jnp.concatenate([p[:blk, :blk], p[blk:, :blk], p[:blk, blk:], p[blk:, blk:]], axis=1))
    for res in range(n_res):
        for i in range(sub_blocks):
            outs = []
            stats = jnp.zeros((blk, LANES), F32)
            for j in range(n_pairs):
                u = (res * sub_blocks + i) * n_pairs + j
                vw = v_s[res, 2 * i * blk:2 * (i + 2) * blk, 2 * j * LANES:(2 * j + 2) * LANES]
                ov = _dot(probs[u], vw)
                outs.append(ov[:, :LANES] / ov[:, LANES:])
                m = jnp.broadcast_to(maxes[u], (2 * blk, LANES))
                lse = jnp.where(first_head, m[:blk], m[blk:]) * LN2 + jnp.log(ov[:, LANES:])
                stats = jnp.where((lane & (hd - 1)) == j, lse, stats)
            rows = slice(i * blk, (i + 1) * blk)
            o_ref[0, rows, res * w:(res + 1) * w] = jnp.concatenate(outs, axis=-1).astype(o_ref.dtype)
            lse_ref[0, rows, res * LANES:(res + 1) * LANES] = stats


def _attn_call(q, k, v, *, width, window, dilation):
    b, ls, _ = q.shape
    w = width
    n_back = window // dilation
    blk = ATTN_BLOCK
    assert n_back <= blk
    sub_blocks = min(ATTN_MAX_SUB_BLOCKS, ls // blk)
    step_rows = sub_blocks * blk
    assert ls % step_rows == 0
    n_heads = w // ATTN_HEAD_DIM
    assert n_heads <= LANES and n_heads % 2 == 0 and 2 * ATTN_HEAD_DIM == LANES
    n_res = max(1, min(dilation, ATTN_MAX_SUB_BLOCKS // sub_blocks))
    assert dilation % n_res == 0
    cur = pl.BlockSpec((1, step_rows, n_res * w), lambda bi, r, n: (bi, n, r))
    prev = pl.BlockSpec((1, blk, n_res * w),
                        lambda bi, r, n: (bi, jnp.maximum(n * sub_blocks - 1, 0), r))
    return pl.pallas_call(
        functools.partial(_attn_kernel, n_back=n_back, n_heads=n_heads),
        out_shape=(jax.ShapeDtypeStruct((b, ls, dilation * w), BF16),
                   jax.ShapeDtypeStruct((b, ls, dilation * LANES), F32)),
        grid=(b, dilation // n_res, ls // step_rows),
        in_specs=[cur, prev, cur, prev, cur],
        out_specs=(cur, pl.BlockSpec((1, step_rows, n_res * LANES), lambda bi, r, n: (bi, n, r))),
        scratch_shapes=[pltpu.VMEM((n_res, blk + step_rows, w), BF16),
                        pltpu.VMEM((n_res, 2 * (blk + step_rows), 2 * w), BF16),
                        pltpu.VMEM((2, 2 * blk, 2 * blk), F32)],
        compiler_params=pltpu.CompilerParams(
            dimension_semantics=("arbitrary", "arbitrary", "arbitrary"),
            vmem_limit_bytes=VMEM_LIMIT_BYTES),
        name=f"attn_d{dilation}",
    )(q, k, k, v, v)


def _mlstm_kernel(qk_ref, v_ref, og_ref, gcol_ref, grow_ref, wc_ref, bc_ref, gn_ref, tri_ref, hm_ref,
                  o_ref, tail_s, c_s, m_s, *, ml_w):
    L = MLSTM_CHUNK
    nh = N_MLSTM_HEADS
    dh = ml_w // nh

    @pl.when(pl.program_id(1) == 0)
    def _():
        tail_s[...] = jnp.zeros_like(tail_s)
        c_s[...] = jnp.zeros_like(c_s)
        m_s[...] = jnp.zeros_like(m_s)

    tri = tri_ref[...]
    ti = lax.broadcasted_iota(jnp.int32, (L, L), 0)
    si = lax.broadcasted_iota(jnp.int32, (L, L), 1)
    causal = si <= ti
    ones = jnp.ones((L, dh), BF16)
    hp = hm_ref[...]
    for bi in range(qk_ref.shape[0]):
        _mlstm_chunk(bi, qk_ref, v_ref, og_ref, gcol_ref, grow_ref, wc_ref, bc_ref, gn_ref, o_ref,
                     tail_s, c_s, m_s, tri, causal, ones, hp, ml_w=ml_w)


def _mlstm_chunk(bi, qk_ref, v_ref, og_ref, gcol_ref, grow_ref, wc_ref, bc_ref, gn_ref, o_ref,
                 tail_s, c_s, m_s, tri, causal, ones, hp, *, ml_w):
    L = MLSTM_CHUNK
    nh = N_MLSTM_HEADS
    dh = ml_w // nh
    x = qk_ref[bi].astype(F32)
    tail = tail_s[bi]
    row8 = lax.broadcasted_iota(jnp.int32, tail.shape, 0)
    acc = bc_ref[...] + x * wc_ref[CONV_WIDTH - 1:CONV_WIDTH, :]
    for back in range(1, CONV_WIDTH):
        rolled = pltpu.roll(x, back, 0)
        top = jnp.where(row8 >= back, rolled[:MOD_ROWS], pltpu.roll(tail, back, 0))
        shifted = jnp.concatenate([top, rolled[MOD_ROWS:]], axis=0)
        acc = acc + shifted * wc_ref[CONV_WIDTH - 1 - back:CONV_WIDTH - back, :]
    tail_s[bi] = x[L - MOD_ROWS:]
    qk = acc * _sigmoid(acc)

    gcol = gcol_ref[bi]
    grow = grow_ref[bi]
    bcol_all = _dot(tri, gcol, precision=lax.Precision.HIGHEST)
    brow_all = _dot_nt(grow, tri, precision=lax.Precision.HIGHEST)

    ks = [qk[:, ml_w + h * dh:ml_w + (h + 1) * dh] * (dh ** -0.5) for h in range(nh)]

    heads = []
    for h in range(nh):
        st = bi * nh + h
        b_c = jnp.broadcast_to(bcol_all[:, nh + h:nh + h + 1], (L, dh))
        i_c = jnp.broadcast_to(gcol[:, h:h + 1], (L, dh))
        b_r = brow_all[nh + h:nh + h + 1, :]
        i_r = grow[h:h + 1, :]
        m_prev = m_s[st, 0:1, :]
        log_d = jnp.where(causal, b_c[:, 0:1] + (i_r - b_r), NEG_INF)
        m_inter = b_c + m_prev
        m_t = jnp.maximum(m_inter, jnp.max(log_d, axis=-1, keepdims=True))
        d_mat = jnp.exp2(log_d - m_t[:, 0:1])
        inter = jnp.exp2(m_inter - m_t)
        b_last = b_c[L - 1:L, :]
        w_log = b_last - b_c + i_c
        m_new = jnp.maximum(b_last + m_prev, jnp.max(w_log, axis=0, keepdims=True))
        wgt = jnp.exp2(w_log - m_new)
        decay = jnp.exp2(b_last + m_prev - m_new)
        heads.append((m_t, d_mat, inter, m_new, wgt, decay))

    outs = []
    for h, (m_t, d_mat, inter, _, _, _) in enumerate(heads):
        hcols = slice(h * dh, (h + 1) * dh)
        q = qk[:, hcols].astype(BF16)
        v_ext = jnp.concatenate([v_ref[bi, :, hcols], ones], axis=-1)
        c_prev = c_s[bi * nh + h]
        s_qk = _dot_nt(q, ks[h].astype(BF16)) * d_mat
        ext = jnp.concatenate([inter, inter], axis=-1) * _dot(q, c_prev.astype(BF16)) \
            + _dot(s_qk.astype(BF16), v_ext)
        outs.append(ext[:, :dh] / jnp.maximum(jnp.abs(ext[:, dh:]), jnp.exp2(-m_t)))
    hh = jnp.concatenate(outs, axis=-1)
    h2 = (hh * hh).astype(BF16)
    gw = hp.shape[0]
    msq = jnp.concatenate([_dot(h2[:, c0:c0 + gw], hp) for c0 in range(0, ml_w, gw)], axis=-1)
    hn = hh * lax.rsqrt(msq + RMS_EPS) * gn_ref[...]
    o_ref[bi] = (og_ref[bi].astype(F32) * hn).astype(o_ref.dtype)

    for h, (_, _, _, m_new, wgt, decay) in enumerate(heads):
        st = bi * nh + h
        hcols = slice(h * dh, (h + 1) * dh)
        v_ext = jnp.concatenate([v_ref[bi, :, hcols], ones], axis=-1)
        kw = (ks[h] * wgt).astype(BF16)
        c_s[st] = jnp.concatenate([decay, decay], axis=-1) * c_s[st] + _dot_tn(kw, v_ext)
        m_s[st] = jnp.broadcast_to(m_new, m_s.shape[1:])


def _mlstm_call(qkm, vm, og, gcol, grow, w_conv, b_conv, g_norm, tri, hmean):
    b, s, ml_w = vm.shape
    L = MLSTM_CHUNK
    nh = N_MLSTM_HEADS
    dh = ml_w // nh
    bb = MLSTM_BATCH_ROWS
    assert b % bb == 0
    const = lambda shape: pl.BlockSpec(shape, lambda g, i: (0,) * len(shape))
    tok = lambda w: pl.BlockSpec((bb, L, w), lambda g, i: (g, i, 0))
    return pl.pallas_call(
        functools.partial(_mlstm_kernel, ml_w=ml_w),
        out_shape=jax.ShapeDtypeStruct((b, s, ml_w), BF16),
        grid=(b // bb, s // L),
        in_specs=[tok(2 * ml_w), tok(ml_w), tok(ml_w), tok(GATE_LANES),
                  pl.BlockSpec((bb, 2 * nh, L), lambda g, i: (g, 0, i)),
                  const(w_conv.shape), const((1, 2 * ml_w)), const((1, ml_w)),
                  const((L, L)), const(hmean.shape)],
        out_specs=tok(ml_w),
        scratch_shapes=[pltpu.VMEM((bb, MOD_ROWS, 2 * ml_w), F32),
                        pltpu.VMEM((bb * nh, dh, 2 * dh), F32),
                        pltpu.VMEM((bb * nh, MOD_ROWS, dh), F32)],
        compiler_params=pltpu.CompilerParams(dimension_semantics=("arbitrary", "arbitrary"),
                                             vmem_limit_bytes=VMEM_LIMIT_BYTES),
        name="mlstm",
    )(qkm, vm, og, gcol, grow, w_conv, b_conv, g_norm, tri, hmean)


def _outffn_kernel(x_ref, mod_ref, g_ref, *rest, dilations):
    n_lay = len(dilations)
    o_refs, l_refs = rest[:n_lay], rest[n_lay:2 * n_lay]
    (hm_ref, ex_ref, wo_ref, wg_ref, wu_ref, wd_ref, out_ref,
     operm_s, lperm_s) = rest[2 * n_lay:]
    tm = x_ref.shape[1]

    def natural(ref, d, scratch):
        if d == 1:
            return ref[0].astype(F32)
        n_cg = scratch.shape[1]
        chain = [step for step in _gather_chain(dilations) if step[1] <= d]
        for level, (parent, dd, f) in enumerate(reversed(chain)):
            dst = scratch.at[level % 2]
            src = scratch.at[(level + 1) % 2]
            n = tm // dd
            for c in range(parent):
                for a in range(f):
                    r = a * parent + c
                    for cg in range(n_cg):
                        if level == 0:
                            c0 = (r * n_cg + cg) * LANES
                            rows = ref[0, :, c0:c0 + LANES].astype(F32)
                        else:
                            rows = src[cg, r * n:(r + 1) * n, :]
                        dst[cg, pl.ds(c * (tm // parent) + a, n, stride=f), :] = rows
        final = scratch.at[(len(chain) - 1) % 2]
        return jnp.concatenate([final[cg] for cg in range(n_cg)], axis=-1)

    lses = [natural(ref, d, lperm_s) for ref, d in zip(l_refs, dilations)]
    mx = functools.reduce(jnp.maximum, lses)
    es = [jnp.exp(l - mx) for l in lses]
    inv = 1.0 / functools.reduce(jnp.add, es)
    ex = ex_ref[...]
    attn = None
    for e, ref, d in zip(es, o_refs, dilations):
        term = _dot((e * inv).astype(BF16), ex) * natural(ref, d, operm_s)
        attn = term if attn is None else attn + term
    aw = attn.shape[-1]
    y = _dot(attn.astype(BF16), wo_ref[:aw, :]) + _dot(hm_ref[0], wo_ref[aw:, :])
    x1 = x_ref[0] + mod_ref[0, 2:3, :] * y
    ms = jnp.mean(x1 * x1, axis=-1, keepdims=True)
    hn = x1 * lax.rsqrt(ms + RMS_EPS) * g_ref[...]
    hb = (hn * (1.0 + mod_ref[0, 4:5, :]) + mod_ref[0, 3:4, :]).astype(BF16)
    g = _dot(hb, wg_ref[...])
    u = _dot(hb, wu_ref[...])
    a = (g * _sigmoid(g) * u).astype(BF16)
    out_ref[0] = x1 + mod_ref[0, 5:6, :] * _dot(a, wd_ref[...])


def _outffn_call(x, mod, g_ffn, os_, lses, hm, expand, wo, wg, wu, wd, *, dilations):
    b, s, d = x.shape
    tm = FFN_ROWS
    assert all(tm % (dl * BF16_SUBLANES) == 0 for dl in dilations)
    aw = expand.shape[1]
    const = lambda shape: pl.BlockSpec(shape, lambda bi, i: (0,) * len(shape))
    tok = lambda w: pl.BlockSpec((1, tm, w), lambda bi, i: (bi, i, 0))
    strided = lambda w: tuple(pl.BlockSpec((1, tm // dl, dl * w), lambda bi, i: (bi, i, 0)) for dl in dilations)
    return pl.pallas_call(
        functools.partial(_outffn_kernel, dilations=dilations),
        out_shape=jax.ShapeDtypeStruct((b, s, d), F32),
        grid=(b, s // tm),
        in_specs=[tok(d), pl.BlockSpec((1, N_MOD, d), lambda bi, i: (bi, 0, 0)), const((1, d)),
                  *strided(aw), *strided(LANES), tok(hm.shape[-1]),
                  const(expand.shape), const(wo.shape),
                  const(wg.shape), const(wu.shape), const(wd.shape)],
        out_specs=tok(d),
        scratch_shapes=[pltpu.VMEM((2, aw // LANES, tm, LANES), F32), pltpu.VMEM((2, 1, tm, LANES), F32)],
        compiler_params=pltpu.CompilerParams(dimension_semantics=("arbitrary", "arbitrary"),
                                             vmem_limit_bytes=VMEM_LIMIT_BYTES),
        name="outffn",
    )(x, mod, g_ffn, *os_, *lses, hm, expand, wo, wg, wu, wd)


def _block_diag_mean(width, group):
    idx = jnp.arange(width) // group
    return jnp.where(idx[:, None] == idx[None, :], 1.0 / group, 0.0).astype(BF16)


def kernel(x, c, g_mix, w_in, w_conv, b_conv, b_igate, b_fgate, q_norm_g, k_norm_g, mlstm_norm_g, w_out,
           g_ffn, w_gate, w_up, w_down, w_ada, b_ada):
    b, s, d = x.shape
    depth = g_mix.shape[0]
    attn_w = d // 2
    ml_w = d - attn_w
    nh = N_MLSTM_HEADS
    n_attn_heads = attn_w // ATTN_HEAD_DIM
    scale = ATTN_HEAD_DIM ** -0.5 * LOG2E
    dilations = tuple(dl for _, dl in DILATED_PATTERNS)
    assert b <= MOD_ROWS and 2 * nh <= GATE_LANES

    c_t = jnp.zeros((d, MOD_ROWS), F32).at[:, :b].set(c.T)
    assert attn_w % MXU_WIDTH == 0 and MXU_WIDTH % ATTN_HEAD_DIM == 0
    hmean_attn = _block_diag_mean(MXU_WIDTH, ATTN_HEAD_DIM)
    assert ml_w % MXU_WIDTH == 0 and MXU_WIDTH % (ml_w // nh) == 0
    hmean_ml = _block_diag_mean(MXU_WIDTH, ml_w // nh)
    stat_lane = jnp.arange(LANES)
    head_of_stat = jnp.where(stat_lane % ATTN_HEAD_DIM < n_attn_heads // 2,
                             2 * (stat_lane % ATTN_HEAD_DIM) + stat_lane // ATTN_HEAD_DIM, -1)
    head_of_col = jnp.arange(attn_w) // ATTN_HEAD_DIM
    expand = (head_of_stat[:, None] == head_of_col[None, :]).astype(BF16)
    tri = jnp.tril(jnp.ones((MLSTM_CHUNK, MLSTM_CHUNK), F32))

    for l in range(depth):
        mod = _mod_call(c_t, w_ada, b_ada[l][None, :], layer=l, n_rows=b)[:b].reshape(b, N_MOD, d)

        n_main = 3 * attn_w + 4 * ml_w
        assert n_main % MXU_WIDTH == 0
        wg_cols = w_in[l][:, n_main:]
        wg = jnp.zeros((d, GATE_LANES), F32).at[:, :2 * nh].set(wg_cols).astype(BF16)
        wgt = jnp.zeros((BF16_SUBLANES, d), F32).at[:2 * nh].set(wg_cols.T).astype(BF16)
        gate_bias = jnp.concatenate([b_igate[l], b_fgate[l]])
        bcol = jnp.zeros((1, GATE_LANES), F32).at[0, :2 * nh].set(gate_bias)
        brow = gate_bias[:, None]
        gq = jnp.tile(q_norm_g[l], n_attn_heads)[None, :]
        gk = jnp.tile(k_norm_g[l], n_attn_heads)[None, :]

        qs, ks, vs, qkm, vm, og, gcol, grow = _inproj_call(
            x, mod, g_mix[l][None, :], w_in, wg, wgt, gq, gk, hmean_attn, bcol, brow,
            layer=l, attn_w=attn_w, ml_w=ml_w, scale=scale, dilations=dilations)

        os_, lses = [], []
        for (window, dilation), q, k, v in zip(DILATED_PATTERNS, qs, ks, vs):
            o, lse = _attn_call(q, k, v, width=attn_w, window=window, dilation=dilation)
            os_.append(o)
            lses.append(lse)

        hm = _mlstm_call(qkm, vm, og, gcol, grow, w_conv[l], b_conv[l][None, :],
                         mlstm_norm_g[l][None, :], tri, hmean_ml)

        x = _outffn_call(x, mod, g_ffn[l][None, :], os_, lses, hm, expand,
                         w_out[l].astype(BF16), w_gate[l].astype(BF16), w_up[l].astype(BF16), w_down[l].astype(BF16),
                         dilations=dilations)
    return x
```

```python
import functools

import jax
import jax.numpy as jnp
from jax import lax
from jax.experimental import pallas as pl
from jax.experimental.pallas import tpu as pltpu

F32 = jnp.float32
BF16 = jnp.bfloat16

ATTN_HEAD_DIM = 64
N_MLSTM_HEADS = 4
CONV_WIDTH = 4
DILATED_PATTERNS = ((128, 1), (512, 4), (2048, 16))
ATTN_BLOCK = 128
N_MOD = 6
RMS_EPS = 1e-6

LANES = 128
MXU_WIDTH = 256
BF16_SUBLANES = 16
VMEM_LIMIT_BYTES = 56 * 1024 * 1024
F32_SUBLANES = 8
MOD_ROWS = F32_SUBLANES
LAYOUT_ROWS = 512
GATE_LANES = 128
INPROJ_ROWS = LAYOUT_ROWS
ATTN_MAX_SUB_BLOCKS = 8
MLSTM_CHUNK = 256
MLSTM_BATCH_ROWS = 1
FFN_ROWS = LAYOUT_ROWS

NEG_INF = float("-inf")
LOG2E = 1.4426950408889634
LN2 = 0.6931471805599453


def _dot(a, b, **kw):
    return jnp.dot(a, b, preferred_element_type=F32, **kw)


def _dot_nt(a, b, **kw):
    return lax.dot_general(a, b, (((1,), (1,)), ((), ())), preferred_element_type=F32, **kw)


def _dot_tn(a, b):
    return lax.dot_general(a, b, (((0,), (0,)), ((), ())), preferred_element_type=F32)


def _sigmoid(z):
    return 1.0 / (1.0 + jnp.exp(-z))


def _log_sigmoid(z):
    return jnp.minimum(z, 0.0) - jnp.log1p(jnp.exp(-jnp.abs(z)))


def _strided_shape(b, s, w, dilation):
    return (b, s // dilation, dilation * w)


def _gather_chain(dilations):
    chain, parent = [], 1
    for d in sorted(set(dilations)):
        if d == 1:
            continue
        assert d % parent == 0 and d // parent < F32_SUBLANES
        chain.append((parent, d, d // parent))
        parent = d
    return chain


def _mod_kernel(ct_ref, w_ref, b_ref, o_ref, *, n_rows):
    ct = ct_ref[...]
    sc = ct * _sigmoid(ct)
    w = w_ref[...]
    rows = [jnp.sum(w * sc[:, r:r + 1], axis=0, keepdims=True) for r in range(n_rows)]
    rows += [jnp.zeros_like(rows[0])] * (MOD_ROWS - n_rows)
    o_ref[...] = jnp.concatenate(rows, axis=0) + b_ref[...]


def _mod_call(c_t, w_ada, b_ada, *, layer, n_rows):
    _, d, n = w_ada.shape
    tn = n // 4
    assert tn % LANES == 0
    return pl.pallas_call(
        functools.partial(_mod_kernel, n_rows=n_rows),
        out_shape=jax.ShapeDtypeStruct((MOD_ROWS, n), F32),
        grid=(n // tn,),
        in_specs=[pl.BlockSpec((d, MOD_ROWS), lambda j: (0, 0)),
                  pl.BlockSpec((None, d, tn), lambda j: (layer, 0, j)),
                  pl.BlockSpec((1, tn), lambda j: (0, j))],
        out_specs=pl.BlockSpec((MOD_ROWS, tn), lambda j: (0, j)),
        compiler_params=pltpu.CompilerParams(dimension_semantics=("arbitrary",),
                                             vmem_limit_bytes=VMEM_LIMIT_BYTES),
        name="mod",
    )(c_t, w_ada, b_ada)


def _inproj_kernel(x_ref, mod_ref, g_ref, w_ref, wgt_ref, gq_ref, gk_ref, hp_ref,
                   brow_ref, *rest, attn_w, ml_w, scale, dilations):
    n_lay = len(dilations)
    q_refs, k_refs, v_refs = rest[:n_lay], rest[n_lay:2 * n_lay], rest[2 * n_lay:3 * n_lay]
    qkm_ref, vm_ref, og_ref, gcol_ref, grow_ref, perm_s, wb_s = rest[3 * n_lay:]
    tm = x_ref.shape[1]

    @pl.when((pl.program_id(0) == 0) & (pl.program_id(1) == 0))
    def _():
        for c0 in range(0, wb_s.shape[1], MXU_WIDTH):
            wb_s[:, c0:c0 + MXU_WIDTH] = w_ref[:, c0:c0 + MXU_WIDTH].astype(BF16)

    def emit(val, refs):
        w = val.shape[-1]
        n_cg = w // LANES
        ref_of = dict(zip(dilations, refs))
        if 1 in ref_of:
            ref_of[1][0] = val.astype(BF16)
        chain = _gather_chain(dilations)
        for cg in range(n_cg):
            perm_s[0, cg] = val[:, cg * LANES:(cg + 1) * LANES]
        for level, (parent, d, f) in enumerate(chain):
            src, dst = perm_s.at[level % 2], perm_s.at[(level + 1) % 2]
            n = tm // d
            for c in range(parent):
                for a in range(f):
                    r = a * parent + c
                    for cg in range(n_cg):
                        rows = src[cg, pl.ds(c * (tm // parent) + a, n, stride=f), :]
                        c0 = r * w + cg * LANES
                        ref_of[d][0, :, c0:c0 + LANES] = rows.astype(BF16)
                        if level + 1 < len(chain):
                            dst[cg, r * n:(r + 1) * n, :] = rows

    x = x_ref[0]
    ms = jnp.mean(x * x, axis=-1, keepdims=True)
    y = x * lax.rsqrt(ms + RMS_EPS) * g_ref[...]
    h = (y * (1.0 + mod_ref[0, 1:2, :]) + mod_ref[0, 0:1, :]).astype(BF16)

    n_attn = 3 * attn_w
    xa = _dot(h, wb_s[:, :n_attn])
    hp = hp_ref[...]

    def head_norm(t, g):
        t2 = (t * t).astype(BF16)
        gw = hp.shape[0]
        msq = jnp.concatenate([_dot(t2[:, c0:c0 + gw], hp) for c0 in range(0, t.shape[-1], gw)], axis=-1)
        return t * lax.rsqrt(msq + RMS_EPS) * g

    emit(head_norm(xa[:, :attn_w], gq_ref[...]) * scale, q_refs)
    emit(head_norm(xa[:, attn_w:2 * attn_w], gk_ref[...]), k_refs)
    emit(xa[:, 2 * attn_w:], v_refs)

    xm = _dot(h, wb_s[:, n_attn:n_attn + 4 * ml_w])
    qkm_ref[0] = xm[:, :2 * ml_w].astype(BF16)
    vm_ref[0] = xm[:, 2 * ml_w:3 * ml_w].astype(BF16)
    og_ref[0] = _sigmoid(xm[:, 3 * ml_w:]).astype(BF16)

    nh = N_MLSTM_HEADS
    zr = _dot_nt(wgt_ref[...], h)[:2 * nh] + brow_ref[...]
    row = lax.broadcasted_iota(jnp.int32, zr.shape, 0)
    gates = jnp.where(row < nh, zr, _log_sigmoid(zr)) * LOG2E
    grow_ref[0] = gates
    padded = jnp.concatenate([gates, jnp.zeros((GATE_LANES - 2 * nh, tm), F32)], axis=0)
    gcol_ref[0] = padded.T


def _inproj_call(x, mod, g_mix, w, wgt, gq, gk, hp, brow, *, layer, attn_w, ml_w, scale, dilations):
    b, s, d = x.shape
    tm = INPROJ_ROWS
    nh2 = 2 * N_MLSTM_HEADS
    assert all(tm % (dl * BF16_SUBLANES) == 0 for dl in dilations)
    const = lambda shape: pl.BlockSpec(shape, lambda bi, i: (0,) * len(shape))
    tok = lambda w: pl.BlockSpec((1, tm, w), lambda bi, i: (bi, i, 0))
    strided_shapes = tuple(jax.ShapeDtypeStruct(_strided_shape(b, s, attn_w, dl), BF16) for dl in dilations)
    strided_specs = tuple(pl.BlockSpec((1, tm // dl, dl * attn_w), lambda bi, i: (bi, i, 0)) for dl in dilations)
    out_shape = strided_shapes * 3 + (
        jax.ShapeDtypeStruct((b, s, 2 * ml_w), BF16),
        jax.ShapeDtypeStruct((b, s, ml_w), BF16),
        jax.ShapeDtypeStruct((b, s, ml_w), BF16),
        jax.ShapeDtypeStruct((b, s, GATE_LANES), F32),
        jax.ShapeDtypeStruct((b, nh2, s), F32))
    outs = pl.pallas_call(
        functools.partial(_inproj_kernel, attn_w=attn_w, ml_w=ml_w, scale=scale, dilations=dilations),
        out_shape=out_shape,
        grid=(b, s // tm),
        in_specs=[tok(d),
                  pl.BlockSpec((1, N_MOD, d), lambda bi, i: (bi, 0, 0)),
                  const((1, d)),
                  pl.BlockSpec((None,) + w.shape[1:], lambda bi, i: (layer, 0, 0),
                               pipeline_mode=pl.Buffered(1)),
                  const(wgt.shape),
                  const((1, attn_w)), const((1, attn_w)), const(hp.shape),
                  const((nh2, 1))],
        out_specs=strided_specs * 3 + (tok(2 * ml_w), tok(ml_w), tok(ml_w), tok(GATE_LANES),
                                       pl.BlockSpec((1, nh2, tm), lambda bi, i: (bi, 0, i))),
        scratch_shapes=[pltpu.VMEM((2, attn_w // LANES, tm, LANES), F32),
                        pltpu.VMEM((d, 3 * attn_w + 4 * ml_w), BF16)],
        compiler_params=pltpu.CompilerParams(dimension_semantics=("arbitrary", "arbitrary"),
                                             vmem_limit_bytes=VMEM_LIMIT_BYTES),
        name="inproj",
    )(x, mod, g_mix, w, wgt, gq, gk, hp, brow)
    n_lay = len(dilations)
    return (outs[:n_lay], outs[n_lay:2 * n_lay], outs[2 * n_lay:3 * n_lay]) + tuple(outs[3 * n_lay:])


def _attn_kernel(q_ref, kp_ref, kc_ref, vp_ref, vc_ref, o_ref, lse_ref, k_s, v_s, bias_s, *, n_back, n_heads):
    blk = ATTN_BLOCK
    sub_blocks = q_ref.shape[1] // blk
    hd = ATTN_HEAD_DIM
    w = n_heads * hd
    n_res = q_ref.shape[2] // w
    n_pairs = n_heads // 2
    n = pl.program_id(2)
    lane = lax.broadcasted_iota(jnp.int32, (blk, LANES), 1)
    first_head = lane < hd
    ind_a = jnp.where(first_head, 1.0, 0.0).astype(BF16)
    ind_b = jnp.where(first_head, 0.0, 1.0).astype(BF16)
    for res in range(n_res):
        k_s[res, 0:blk, :] = kp_ref[0, :, res * w:(res + 1) * w]
        k_s[res, blk:, :] = kc_ref[0, :, res * w:(res + 1) * w]
        for kb in range(sub_blocks + 1):
            for j in range(n_pairs):
                cols = slice(res * w + j * LANES, res * w + (j + 1) * LANES)
                src = vp_ref[0, :, cols] if kb == 0 else vc_ref[0, (kb - 1) * blk:kb * blk, cols]
                ra, rb, c0 = 2 * kb * blk, (2 * kb + 1) * blk, 2 * j * LANES
                v_s[res, ra:ra + blk, c0:c0 + LANES] = jnp.where(first_head, src, 0)
                v_s[res, ra:ra + blk, c0 + LANES:c0 + 2 * LANES] = ind_a
                v_s[res, rb:rb + blk, c0:c0 + LANES] = jnp.where(first_head, 0, src)
                v_s[res, rb:rb + blk, c0 + LANES:c0 + 2 * LANES] = ind_b

    row = lax.broadcasted_iota(jnp.int32, (2 * blk, 2 * blk), 0) & (blk - 1)
    col = lax.broadcasted_iota(jnp.int32, (2 * blk, 2 * blk), 1)
    band = jnp.logical_and(col >= row + (blk - n_back), col <= row + blk)
    bias_s[0] = jnp.where(band, 0.0, NEG_INF)
    bias_s[1] = jnp.where(jnp.logical_and(band, col >= blk), 0.0, NEG_INF)

    units = [(res, i, j) for res in range(n_res) for i in range(sub_blocks) for j in range(n_pairs)]
    scores, maxes, probs = [], [], []
    for res, i, j in units:
        cols = slice(j * LANES, (j + 1) * LANES)
        q = q_ref[0, i * blk:(i + 1) * blk, res * w + j * LANES:res * w + (j + 1) * LANES]
        ks = k_s[res, i * blk:(i + 2) * blk, cols]
        bias = bias_s[jnp.where(n == 0, 1, 0)] if i == 0 else bias_s[0]
        q2 = jnp.concatenate([jnp.where(first_head, q, 0), jnp.where(first_head, 0, q)], axis=0)
        s = _dot_nt(q2, ks) + bias
        scores.append(s)
        maxes.append(jnp.max(s, axis=-1, keepdims=True))
    for s, m in zip(scores, maxes):
        p = jnp.exp2(s - m).astype(BF16)
        probs.append(jnp.concatenate([p[:blk, :blk], p[blk:, :blk], p[:blk, blk:], p[blk:, blk:]], axis=1))
    for res in range(n_res):
        for i in range(sub_blocks):
            outs = []
            stats = jnp.zeros((blk, LANES), F32)
            for j in range(n_pairs):
                u = (res * sub_blocks + i) * n_pairs + j
                vw = v_s[res, 2 * i * blk:2 * (i + 2) * blk, 2 * j * LANES:(2 * j + 2) * LANES]
                ov = _dot(probs[u], vw)
                outs.append(ov[:, :LANES] / ov[:, LANES:])
                m = jnp.broadcast_to(maxes[u], (2 * blk, LANES))
                lse = jnp.where(first_head, m[:blk], m[blk:]) * LN2 + jnp.log(ov[:, LANES:])
                stats = jnp.where((lane & (hd - 1)) == j, lse, stats)
            rows = slice(i * blk, (i + 1) * blk)
            o_ref[0, rows, res * w:(res + 1) * w] = jnp.concatenate(outs, axis=-1).astype(o_ref.dtype)
            lse_ref[0, rows, res * LANES:(res + 1) * LANES] = stats


def _attn_call(q, k, v, *, width, window, dilation):
    b, ls, _ = q.shape
    w = width
    n_back = window // dilation
    blk = ATTN_BLOCK
    assert n_back <= blk
    sub_blocks = min(ATTN_MAX_SUB_BLOCKS, ls // blk)
    step_rows = sub_blocks * blk
    assert ls % step_rows == 0
    n_heads = w // ATTN_HEAD_DIM
    assert n_heads <= LANES and n_heads % 2 == 0 and 2 * ATTN_HEAD_DIM == LANES
    n_res = max(1, min(dilation, ATTN_MAX_SUB_BLOCKS // sub_blocks))
    assert dilation % n_res == 0
    cur = pl.BlockSpec((1, step_rows, n_res * w), lambda bi, r, n: (bi, n, r))
    prev = pl.BlockSpec((1, blk, n_res * w),
                        lambda bi, r, n: (bi, jnp.maximum(n * sub_blocks - 1, 0), r))
    return pl.pallas_call(
        functools.partial(_attn_kernel, n_back=n_back, n_heads=n_heads),
        out_shape=(jax.ShapeDtypeStruct((b, ls, dilation * w), BF16),
                   jax.ShapeDtypeStruct((b, ls, dilation * LANES), F32)),
        grid=(b, dilation // n_res, ls // step_rows),
        in_specs=[cur, prev, cur, prev, cur],
        out_specs=(cur, pl.BlockSpec((1, step_rows, n_res * LANES), lambda bi, r, n: (bi, n, r))),
        scratch_shapes=[pltpu.VMEM((n_res, blk + step_rows, w), BF16),
                        pltpu.VMEM((n_res, 2 * (blk + step_rows), 2 * w), BF16),
                        pltpu.VMEM((2, 2 * blk, 2 * blk), F32)],
        compiler_params=pltpu.CompilerParams(
            dimension_semantics=("arbitrary", "arbitrary", "arbitrary"),
            vmem_limit_bytes=VMEM_LIMIT_BYTES),
        name=f"attn_d{dilation}",
    )(q, k, k, v, v)


def _mlstm_kernel(qk_ref, v_ref, og_ref, gcol_ref, grow_ref, wc_ref, bc_ref, gn_ref, tri_ref, hm_ref,
                  o_ref, tail_s, c_s, m_s, *, ml_w):
    L = MLSTM_CHUNK
    nh = N_MLSTM_HEADS
    dh = ml_w // nh

    @pl.when(pl.program_id(1) == 0)
    def _():
        tail_s[...] = jnp.zeros_like(tail_s)
        c_s[...] = jnp.zeros_like(c_s)
        m_s[...] = jnp.zeros_like(m_s)

    tri = tri_ref[...]
    ti = lax.broadcasted_iota(jnp.int32, (L, L), 0)
    si = lax.broadcasted_iota(jnp.int32, (L, L), 1)
    causal = si <= ti
    ones = jnp.ones((L, dh), BF16)
    hp = hm_ref[...]
    for bi in range(qk_ref.shape[0]):
        _mlstm_chunk(bi, qk_ref, v_ref, og_ref, gcol_ref, grow_ref, wc_ref, bc_ref, gn_ref, o_ref,
                     tail_s, c_s, m_s, tri, causal, ones, hp, ml_w=ml_w)


def _mlstm_chunk(bi, qk_ref, v_ref, og_ref, gcol_ref, grow_ref, wc_ref, bc_ref, gn_ref, o_ref,
                 tail_s, c_s, m_s, tri, causal, ones, hp, *, ml_w):
    L = MLSTM_CHUNK
    nh = N_MLSTM_HEADS
    dh = ml_w // nh
    x = qk_ref[bi].astype(F32)
    tail = tail_s[bi]
    row8 = lax.broadcasted_iota(jnp.int32, tail.shape, 0)
    acc = bc_ref[...] + x * wc_ref[CONV_WIDTH - 1:CONV_WIDTH, :]
    for back in range(1, CONV_WIDTH):
        rolled = pltpu.roll(x, back, 0)
        top = jnp.where(row8 >= back, rolled[:MOD_ROWS], pltpu.roll(tail, back, 0))
        shifted = jnp.concatenate([top, rolled[MOD_ROWS:]], axis=0)
        acc = acc + shifted * wc_ref[CONV_WIDTH - 1 - back:CONV_WIDTH - back, :]
    tail_s[bi] = x[L - MOD_ROWS:]
    qk = acc * _sigmoid(acc)

    gcol = gcol_ref[bi]
    grow = grow_ref[bi]
    bcol_all = _dot(tri, gcol, precision=lax.Precision.HIGHEST)
    brow_all = _dot_nt(grow, tri, precision=lax.Precision.HIGHEST)

    ks = [qk[:, ml_w + h * dh:ml_w + (h + 1) * dh] * (dh ** -0.5) for h in range(nh)]

    heads = []
    for h in range(nh):
        st = bi * nh + h
        b_c = jnp.broadcast_to(bcol_all[:, nh + h:nh + h + 1], (L, dh))
        i_c = jnp.broadcast_to(gcol[:, h:h + 1], (L, dh))
        b_r = brow_all[nh + h:nh + h + 1, :]
        i_r = grow[h:h + 1, :]
        m_prev = m_s[st, 0:1, :]
        log_d = jnp.where(causal, b_c[:, 0:1] + (i_r - b_r), NEG_INF)
        m_inter = b_c + m_prev
        m_t = jnp.maximum(m_inter, jnp.max(log_d, axis=-1, keepdims=True))
        d_mat = jnp.exp2(log_d - m_t[:, 0:1])
        inter = jnp.exp2(m_inter - m_t)
        b_last = b_c[L - 1:L, :]
        w_log = b_last - b_c + i_c
        m_new = jnp.maximum(b_last + m_prev, jnp.max(w_log, axis=0, keepdims=True))
        wgt = jnp.exp2(w_log - m_new)
        decay = jnp.exp2(b_last + m_prev - m_new)
        heads.append((m_t, d_mat, inter, m_new, wgt, decay))

    outs = []
    for h, (m_t, d_mat, inter, _, _, _) in enumerate(heads):
        hcols = slice(h * dh, (h + 1) * dh)
        q = qk[:, hcols].astype(BF16)
        v_ext = jnp.concatenate([v_ref[bi, :, hcols], ones], axis=-1)
        c_prev = c_s[bi * nh + h]
        s_qk = _dot_nt(q, ks[h].astype(BF16)) * d_mat
        ext = jnp.concatenate([inter, inter], axis=-1) * _dot(q, c_prev.astype(BF16)) \
            + _dot(s_qk.astype(BF16), v_ext)
        outs.append(ext[:, :dh] / jnp.maximum(jnp.abs(ext[:, dh:]), jnp.exp2(-m_t)))
    hh = jnp.concatenate(outs, axis=-1)
    h2 = (hh * hh).astype(BF16)
    gw = hp.shape[0]
    msq = jnp.concatenate([_dot(h2[:, c0:c0 + gw], hp) for c0 in range(0, ml_w, gw)], axis=-1)
    hn = hh * lax.rsqrt(msq + RMS_EPS) * gn_ref[...]
    o_ref[bi] = (og_ref[bi].astype(F32) * hn).astype(o_ref.dtype)

    for h, (_, _, _, m_new, wgt, decay) in enumerate(heads):
        st = bi * nh + h
        hcols = slice(h * dh, (h + 1) * dh)
        v_ext = jnp.concatenate([v_ref[bi, :, hcols], ones], axis=-1)
        kw = (ks[h] * wgt).astype(BF16)
        c_s[st] = jnp.concatenate([decay, decay], axis=-1) * c_s[st] + _dot_tn(kw, v_ext)
        m_s[st] = jnp.broadcast_to(m_new, m_s.shape[1:])


def _mlstm_call(qkm, vm, og, gcol, grow, w_conv, b_conv, g_norm, tri, hmean):
    b, s, ml_w = vm.shape
    L = MLSTM_CHUNK
    nh = N_MLSTM_HEADS
    dh = ml_w // nh
    bb = MLSTM_BATCH_ROWS
    assert b % bb == 0
    const = lambda shape: pl.BlockSpec(shape, lambda g, i: (0,) * len(shape))
    tok = lambda w: pl.BlockSpec((bb, L, w), lambda g, i: (g, i, 0))
    return pl.pallas_call(
        functools.partial(_mlstm_kernel, ml_w=ml_w),
        out_shape=jax.ShapeDtypeStruct((b, s, ml_w), BF16),
        grid=(b // bb, s // L),
        in_specs=[tok(2 * ml_w), tok(ml_w), tok(ml_w), tok(GATE_LANES),
                  pl.BlockSpec((bb, 2 * nh, L), lambda g, i: (g, 0, i)),
                  const(w_conv.shape), const((1, 2 * ml_w)), const((1, ml_w)),
                  const((L, L)), const(hmean.shape)],
        out_specs=tok(ml_w),
        scratch_shapes=[pltpu.VMEM((bb, MOD_ROWS, 2 * ml_w), F32),
                        pltpu.VMEM((bb * nh, dh, 2 * dh), F32),
                        pltpu.VMEM((bb * nh, MOD_ROWS, dh), F32)],
        compiler_params=pltpu.CompilerParams(dimension_semantics=("arbitrary", "arbitrary"),
                                             vmem_limit_bytes=VMEM_LIMIT_BYTES),
        name="mlstm",
    )(qkm, vm, og, gcol, grow, w_conv, b_conv, g_norm, tri, hmean)


def _outffn_kernel(x_ref, mod_ref, g_ref, *rest, dilations):
    n_lay = len(dilations)
    o_refs, l_refs = rest[:n_lay], rest[n_lay:2 * n_lay]
    (hm_ref, ex_ref, wo_ref, wg_ref, wu_ref, wd_ref, out_ref,
     operm_s, lperm_s) = rest[2 * n_lay:]
    tm = x_ref.shape[1]

    def natural(ref, d, scratch):
        if d == 1:
            return ref[0].astype(F32)
        n_cg = scratch.shape[1]
        chain = [step for step in _gather_chain(dilations) if step[1] <= d]
        for level, (parent, dd, f) in enumerate(reversed(chain)):
            dst = scratch.at[level % 2]
            src = scratch.at[(level + 1) % 2]
            n = tm // dd
            for c in range(parent):
                for a in range(f):
                    r = a * parent + c
                    for cg in range(n_cg):
                        if level == 0:
                            c0 = (r * n_cg + cg) * LANES
                            rows = ref[0, :, c0:c0 + LANES].astype(F32)
                        else:
                            rows = src[cg, r * n:(r + 1) * n, :]
                        dst[cg, pl.ds(c * (tm // parent) + a, n, stride=f), :] = rows
        final = scratch.at[(len(chain) - 1) % 2]
        return jnp.concatenate([final[cg] for cg in range(n_cg)], axis=-1)

    lses = [natural(ref, d, lperm_s) for ref, d in zip(l_refs, dilations)]
    mx = functools.reduce(jnp.maximum, lses)
    es = [jnp.exp(l - mx) for l in lses]
    inv = 1.0 / functools.reduce(jnp.add, es)
    ex = ex_ref[...]
    attn = None
    for e, ref, d in zip(es, o_refs, dilations):
        term = _dot((e * inv).astype(BF16), ex) * natural(ref, d, operm_s)
        attn = term if attn is None else attn + term
    aw = attn.shape[-1]
    y = _dot(attn.astype(BF16), wo_ref[:aw, :]) + _dot(hm_ref[0], wo_ref[aw:, :])
    x1 = x_ref[0] + mod_ref[0, 2:3, :] * y
    ms = jnp.mean(x1 * x1, axis=-1, keepdims=True)
    hn = x1 * lax.rsqrt(ms + RMS_EPS) * g_ref[...]
    hb = (hn * (1.0 + mod_ref[0, 4:5, :]) + mod_ref[0, 3:4, :]).astype(BF16)
    g = _dot(hb, wg_ref[...])
    u = _dot(hb, wu_ref[...])
    a = (g * _sigmoid(g) * u).astype(BF16)
    out_ref[0] = x1 + mod_ref[0, 5:6, :] * _dot(a, wd_ref[...])


def _outffn_call(x, mod, g_ffn, os_, lses, hm, expand, wo, wg, wu, wd, *, dilations):
    b, s, d = x.shape
    tm = FFN_ROWS
    assert all(tm % (dl * BF16_SUBLANES) == 0 for dl in dilations)
    aw = expand.shape[1]
    const = lambda shape: pl.BlockSpec(shape, lambda bi, i: (0,) * len(shape))
    tok = lambda w: pl.BlockSpec((1, tm, w), lambda bi, i: (bi, i, 0))
    strided = lambda w: tuple(pl.BlockSpec((1, tm // dl, dl * w), lambda bi, i: (bi, i, 0)) for dl in dilations)
    return pl.pallas_call(
        functools.partial(_outffn_kernel, dilations=dilations),
        out_shape=jax.ShapeDtypeStruct((b, s, d), F32),
        grid=(b, s // tm),
        in_specs=[tok(d), pl.BlockSpec((1, N_MOD, d), lambda bi, i: (bi, 0, 0)), const((1, d)),
                  *strided(aw), *strided(LANES), tok(hm.shape[-1]),
                  const(expand.shape), const(wo.shape),
                  const(wg.shape), const(wu.shape), const(wd.shape)],
        out_specs=tok(d),
        scratch_shapes=[pltpu.VMEM((2, aw // LANES, tm, LANES), F32), pltpu.VMEM((2, 1, tm, LANES), F32)],
        compiler_params=pltpu.CompilerParams(dimension_semantics=("arbitrary", "arbitrary"),
                                             vmem_limit_bytes=VMEM_LIMIT_BYTES),
        name="outffn",
    )(x, mod, g_ffn, *os_, *lses, hm, expand, wo, wg, wu, wd)


def _block_diag_mean(width, group):
    idx = jnp.arange(width) // group
    return jnp.where(idx[:, None] == idx[None, :], 1.0 / group, 0.0).astype(BF16)


def kernel(x, c, g_mix, w_in, w_conv, b_conv, b_igate, b_fgate, q_norm_g, k_norm_g, mlstm_norm_g, w_out,
           g_ffn, w_gate, w_up, w_down, w_ada, b_ada):
    b, s, d = x.shape
    depth = g_mix.shape[0]
    attn_w = d // 2
    ml_w = d - attn_w
    nh = N_MLSTM_HEADS
    n_attn_heads = attn_w // ATTN_HEAD_DIM
    scale = ATTN_HEAD_DIM ** -0.5 * LOG2E
    dilations = tuple(dl for _, dl in DILATED_PATTERNS)
    assert b <= MOD_ROWS and 2 * nh <= GATE_LANES

    c_t = jnp.zeros((d, MOD_ROWS), F32).at[:, :b].set(c.T)
    assert attn_w % MXU_WIDTH == 0 and MXU_WIDTH % ATTN_HEAD_DIM == 0
    hmean_attn = _block_diag_mean(MXU_WIDTH, ATTN_HEAD_DIM)
    assert ml_w % MXU_WIDTH == 0 and MXU_WIDTH % (ml_w // nh) == 0
    hmean_ml = _block_diag_mean(MXU_WIDTH, ml_w // nh)
    stat_lane = jnp.arange(LANES)
    head_of_stat = jnp.where(stat_lane % ATTN_HEAD_DIM < n_attn_heads // 2,
                             2 * (stat_lane % ATTN_HEAD_DIM) + stat_lane // ATTN_HEAD_DIM, -1)
    head_of_col = jnp.arange(attn_w) // ATTN_HEAD_DIM
    expand = (head_of_stat[:, None] == head_of_col[None, :]).astype(BF16)
    tri = jnp.tril(jnp.ones((MLSTM_CHUNK, MLSTM_CHUNK), F32))

    for l in range(depth):
        mod = _mod_call(c_t, w_ada, b_ada[l][None, :], layer=l, n_rows=b)[:b].reshape(b, N_MOD, d)

        n_main = 3 * attn_w + 4 * ml_w
        assert n_main % MXU_WIDTH == 0
        wg_cols = w_in[l][:, n_main:]
        wgt = jnp.zeros((BF16_SUBLANES, d), F32).at[:2 * nh].set(wg_cols.T).astype(BF16)
        brow = jnp.concatenate([b_igate[l], b_fgate[l]])[:, None]
        gq = jnp.tile(q_norm_g[l], n_attn_heads)[None, :]
        gk = jnp.tile(k_norm_g[l], n_attn_heads)[None, :]

        qs, ks, vs, qkm, vm, og, gcol, grow = _inproj_call(
            x, mod, g_mix[l][None, :], w_in, wgt, gq, gk, hmean_attn, brow,
            layer=l, attn_w=attn_w, ml_w=ml_w, scale=scale, dilations=dilations)

        os_, lses = [], []
        for (window, dilation), q, k, v in zip(DILATED_PATTERNS, qs, ks, vs):
            o, lse = _attn_call(q, k, v, width=attn_w, window=window, dilation=dilation)
            os_.append(o)
            lses.append(lse)

        hm = _mlstm_call(qkm, vm, og, gcol, grow, w_conv[l], b_conv[l][None, :],
                         mlstm_norm_g[l][None, :], tri, hmean_ml)

        x = _outffn_call(x, mod, g_ffn[l][None, :], os_, lses, hm, expand,
                         w_out[l].astype(BF16), w_gate[l].astype(BF16), w_up[l].astype(BF16), w_down[l].astype(BF16),
                         dilations=dilations)
    return x
```

```python
import functools

import jax
import jax.numpy as jnp
from jax import lax
from jax.experimental import pallas as pl
from jax.experimental.pallas import tpu as pltpu

F32 = jnp.float32
BF16 = jnp.bfloat16

ATTN_HEAD_DIM = 64
N_MLSTM_HEADS = 4
CONV_WIDTH = 4
DILATED_PATTERNS = ((128, 1), (512, 4), (2048, 16))
ATTN_BLOCK = 128
N_MOD = 6
RMS_EPS = 1e-6

LANES = 128
MXU_WIDTH = 256
BF16_SUBLANES = 16
VMEM_LIMIT_BYTES = 56 * 1024 * 1024
F32_SUBLANES = 8
MOD_ROWS = F32_SUBLANES
LAYOUT_ROWS = 512
GATE_LANES = 128
INPROJ_ROWS = LAYOUT_ROWS
ATTN_MAX_SUB_BLOCKS = 8
ATTN_BLOCKS_PER_STAGE = 2
MLSTM_CHUNK = 256
MLSTM_BATCH_ROWS = 1
FFN_ROWS = LAYOUT_ROWS

NEG_INF = float("-inf")
LOG2E = 1.4426950408889634
LN2 = 0.6931471805599453


def _dot(a, b, **kw):
    return jnp.dot(a, b, preferred_element_type=F32, **kw)


def _dot_nt(a, b, **kw):
    return lax.dot_general(a, b, (((1,), (1,)), ((), ())), preferred_element_type=F32, **kw)


def _dot_tn(a, b):
    return lax.dot_general(a, b, (((0,), (0,)), ((), ())), preferred_element_type=F32)


def _sigmoid(z):
    return 1.0 / (1.0 + jnp.exp(-z))


def _log_sigmoid(z):
    return jnp.minimum(z, 0.0) - jnp.log1p(jnp.exp(-jnp.abs(z)))


def _strided_shape(b, s, w, dilation):
    return (b, s // dilation, dilation * w)


def _gather_chain(dilations):
    chain, parent = [], 1
    for d in sorted(set(dilations)):
        if d == 1:
            continue
        assert d % parent == 0 and d // parent < F32_SUBLANES
        chain.append((parent, d, d // parent))
        parent = d
    return chain


def _mod_kernel(ct_ref, w_ref, b_ref, o_ref, *, n_rows):
    ct = ct_ref[...]
    sc = ct * _sigmoid(ct)
    w = w_ref[...]
    rows = [jnp.sum(w * sc[:, r:r + 1], axis=0, keepdims=True) for r in range(n_rows)]
    rows += [jnp.zeros_like(rows[0])] * (MOD_ROWS - n_rows)
    o_ref[...] = jnp.concatenate(rows, axis=0) + b_ref[...]


def _mod_call(c_t, w_ada, b_ada, *, layer, n_rows):
    _, d, n = w_ada.shape
    tn = n // 4
    assert tn % LANES == 0
    return pl.pallas_call(
        functools.partial(_mod_kernel, n_rows=n_rows),
        out_shape=jax.ShapeDtypeStruct((MOD_ROWS, n), F32),
        grid=(n // tn,),
        in_specs=[pl.BlockSpec((d, MOD_ROWS), lambda j: (0, 0)),
                  pl.BlockSpec((None, d, tn), lambda j: (layer, 0, j)),
                  pl.BlockSpec((1, tn), lambda j: (0, j))],
        out_specs=pl.BlockSpec((MOD_ROWS, tn), lambda j: (0, j)),
        compiler_params=pltpu.CompilerParams(dimension_semantics=("arbitrary",),
                                             vmem_limit_bytes=VMEM_LIMIT_BYTES),
        name="mod",
    )(c_t, w_ada, b_ada)


def _inproj_kernel(x_ref, mod_ref, g_ref, w_ref, wgt_ref, gq_ref, gk_ref, hp_ref,
                   brow_ref, *rest, attn_w, ml_w, scale, dilations):
    n_lay = len(dilations)
    q_refs, k_refs, v_refs = rest[:n_lay], rest[n_lay:2 * n_lay], rest[2 * n_lay:3 * n_lay]
    qkm_ref, vm_ref, og_ref, gcol_ref, grow_ref, perm_s, wb_s = rest[3 * n_lay:]
    tm = x_ref.shape[1]

    @pl.when((pl.program_id(0) == 0) & (pl.program_id(1) == 0))
    def _():
        for c0 in range(0, wb_s.shape[1], MXU_WIDTH):
            wb_s[:, c0:c0 + MXU_WIDTH] = w_ref[:, c0:c0 + MXU_WIDTH].astype(BF16)

    def emit(val, refs):
        w = val.shape[-1]
        n_cg = w // LANES
        ref_of = dict(zip(dilations, refs))
        if 1 in ref_of:
            ref_of[1][0] = val.astype(BF16)
        chain = _gather_chain(dilations)
        for cg in range(n_cg):
            perm_s[0, cg] = val[:, cg * LANES:(cg + 1) * LANES]
        for level, (parent, d, f) in enumerate(chain):
            src, dst = perm_s.at[level % 2], perm_s.at[(level + 1) % 2]
            n = tm // d
            for c in range(parent):
                for a in range(f):
                    r = a * parent + c
                    for cg in range(n_cg):
                        rows = src[cg, pl.ds(c * (tm // parent) + a, n, stride=f), :]
                        c0 = r * w + cg * LANES
                        ref_of[d][0, :, c0:c0 + LANES] = rows.astype(BF16)
                        if level + 1 < len(chain):
                            dst[cg, r * n:(r + 1) * n, :] = rows

    x = x_ref[0]
    ms = jnp.mean(x * x, axis=-1, keepdims=True)
    y = x * lax.rsqrt(ms + RMS_EPS) * g_ref[...]
    h = (y * (1.0 + mod_ref[0, 1:2, :]) + mod_ref[0, 0:1, :]).astype(BF16)

    n_attn = 3 * attn_w
    xa = _dot(h, wb_s[:, :n_attn])
    hp = hp_ref[...]

    def head_norm(t, g):
        t2 = (t * t).astype(BF16)
        gw = hp.shape[0]
        msq = jnp.concatenate([_dot(t2[:, c0:c0 + gw], hp) for c0 in range(0, t.shape[-1], gw)], axis=-1)
        return t * lax.rsqrt(msq + RMS_EPS) * g

    emit(head_norm(xa[:, :attn_w], gq_ref[...]) * scale, q_refs)
    emit(head_norm(xa[:, attn_w:2 * attn_w], gk_ref[...]), k_refs)
    emit(xa[:, 2 * attn_w:], v_refs)

    xm = _dot(h, wb_s[:, n_attn:n_attn + 4 * ml_w])
    qkm_ref[0] = xm[:, :2 * ml_w].astype(BF16)
    vm_ref[0] = xm[:, 2 * ml_w:3 * ml_w].astype(BF16)
    og_ref[0] = _sigmoid(xm[:, 3 * ml_w:]).astype(BF16)

    nh = N_MLSTM_HEADS
    zr = _dot_nt(wgt_ref[...], h)[:2 * nh] + brow_ref[...]
    row = lax.broadcasted_iota(jnp.int32, zr.shape, 0)
    gates = jnp.where(row < nh, zr, _log_sigmoid(zr)) * LOG2E
    grow_ref[0] = gates
    padded = jnp.concatenate([gates, jnp.zeros((GATE_LANES - 2 * nh, tm), F32)], axis=0)
    gcol_ref[0] = padded.T


def _inproj_call(x, mod, g_mix, w, wgt, gq, gk, hp, brow, *, layer, attn_w, ml_w, scale, dilations):
    b, s, d = x.shape
    tm = INPROJ_ROWS
    nh2 = 2 * N_MLSTM_HEADS
    assert all(tm % (dl * BF16_SUBLANES) == 0 for dl in dilations)
    const = lambda shape: pl.BlockSpec(shape, lambda bi, i: (0,) * len(shape))
    tok = lambda w: pl.BlockSpec((1, tm, w), lambda bi, i: (bi, i, 0))
    strided_shapes = tuple(jax.ShapeDtypeStruct(_strided_shape(b, s, attn_w, dl), BF16) for dl in dilations)
    strided_specs = tuple(pl.BlockSpec((1, tm // dl, dl * attn_w), lambda bi, i: (bi, i, 0)) for dl in dilations)
    out_shape = strided_shapes * 3 + (
        jax.ShapeDtypeStruct((b, s, 2 * ml_w), BF16),
        jax.ShapeDtypeStruct((b, s, ml_w), BF16),
        jax.ShapeDtypeStruct((b, s, ml_w), BF16),
        jax.ShapeDtypeStruct((b, s, GATE_LANES), F32),
        jax.ShapeDtypeStruct((b, nh2, s), F32))
    outs = pl.pallas_call(
        functools.partial(_inproj_kernel, attn_w=attn_w, ml_w=ml_w, scale=scale, dilations=dilations),
        out_shape=out_shape,
        grid=(b, s // tm),
        in_specs=[tok(d),
                  pl.BlockSpec((1, N_MOD, d), lambda bi, i: (bi, 0, 0)),
                  const((1, d)),
                  pl.BlockSpec((None,) + w.shape[1:], lambda bi, i: (layer, 0, 0),
                               pipeline_mode=pl.Buffered(1)),
                  const(wgt.shape),
                  const((1, attn_w)), const((1, attn_w)), const(hp.shape),
                  const((nh2, 1))],
        out_specs=strided_specs * 3 + (tok(2 * ml_w), tok(ml_w), tok(ml_w), tok(GATE_LANES),
                                       pl.BlockSpec((1, nh2, tm), lambda bi, i: (bi, 0, i))),
        scratch_shapes=[pltpu.VMEM((2, attn_w // LANES, tm, LANES), F32),
                        pltpu.VMEM((d, 3 * attn_w + 4 * ml_w), BF16)],
        compiler_params=pltpu.CompilerParams(dimension_semantics=("arbitrary", "arbitrary"),
                                             vmem_limit_bytes=VMEM_LIMIT_BYTES),
        name="inproj",
    )(x, mod, g_mix, w, wgt, gq, gk, hp, brow)
    n_lay = len(dilations)
    return (outs[:n_lay], outs[n_lay:2 * n_lay], outs[2 * n_lay:3 * n_lay]) + tuple(outs[3 * n_lay:])


def _attn_kernel(q_ref, kp_ref, kc_ref, vp_ref, vc_ref, o_ref, lse_ref, k_s, v_s, bias_s, p_s, m_s,
                 *, n_back, n_heads):
    blk = ATTN_BLOCK
    sub_blocks = q_ref.shape[1] // blk
    hd = ATTN_HEAD_DIM
    w = n_heads * hd
    n_res = q_ref.shape[2] // w
    n_pairs = n_heads // 2
    n = pl.program_id(2)
    lane = lax.broadcasted_iota(jnp.int32, (blk, LANES), 1)
    first_head = lane < hd
    ind_a = jnp.where(first_head, 1.0, 0.0).astype(BF16)
    ind_b = jnp.where(first_head, 0.0, 1.0).astype(BF16)
    for res in range(n_res):
        k_s[res, 0:blk, :] = kp_ref[0, :, res * w:(res + 1) * w]
        k_s[res, blk:, :] = kc_ref[0, :, res * w:(res + 1) * w]
        for kb in range(sub_blocks + 1):
            for j in range(n_pairs):
                cols = slice(res * w + j * LANES, res * w + (j + 1) * LANES)
                src = vp_ref[0, :, cols] if kb == 0 else vc_ref[0, (kb - 1) * blk:kb * blk, cols]
                ra, rb, c0 = 2 * kb * blk, (2 * kb + 1) * blk, 2 * j * LANES
                v_s[res, ra:ra + blk, c0:c0 + LANES] = jnp.where(first_head, src, 0)
                v_s[res, ra:ra + blk, c0 + LANES:c0 + 2 * LANES] = ind_a
                v_s[res, rb:rb + blk, c0:c0 + LANES] = jnp.where(first_head, 0, src)
                v_s[res, rb:rb + blk, c0 + LANES:c0 + 2 * LANES] = ind_b

    row = lax.broadcasted_iota(jnp.int32, (2 * blk, 2 * blk), 0) & (blk - 1)
    col = lax.broadcasted_iota(jnp.int32, (2 * blk, 2 * blk), 1)
    band = jnp.logical_and(col >= row + (blk - n_back), col <= row + blk)
    bias_s[0] = jnp.where(band, 0.0, NEG_INF)
    bias_s[1] = jnp.where(jnp.logical_and(band, col >= blk), 0.0, NEG_INF)

    def block_rows(i, scale, count):
        start = scale * i * blk
        return pl.ds(start if isinstance(i, int) else pl.multiple_of(start, blk), count * blk)

    span = min(ATTN_BLOCKS_PER_STAGE, sub_blocks)
    n_groups = sub_blocks // span

    def staged(b, res, j):
        return (b * n_res + res) * n_pairs + j

    def score_stage(g, first_bias=None):
        slot = g % 2
        for b in range(span):
            i = g * span + b
            bias = first_bias if (b == 0 and first_bias is not None) else bias_s[0]
            for res in range(n_res):
                for j in range(n_pairs):
                    q = q_ref[0, block_rows(i, 1, 1), res * w + j * LANES:res * w + (j + 1) * LANES]
                    ks = k_s[res, block_rows(i, 1, 2), j * LANES:(j + 1) * LANES]
                    q2 = jnp.concatenate([jnp.where(first_head, q, 0), jnp.where(first_head, 0, q)], axis=0)
                    s = _dot_nt(q2, ks) + bias
                    m = jnp.max(s, axis=-1, keepdims=True)
                    p = jnp.exp2(s - m).astype(BF16)
                    p_s[slot, staged(b, res, j)] = jnp.concatenate(
                        [p[:blk, :blk], p[blk:, :blk], p[:blk, blk:], p[blk:, blk:]], axis=1)
                    m_s[slot, staged(b, res, j)] = jnp.broadcast_to(m, (2 * blk, LANES))

    def value_stage(g):
        slot = g % 2
        for b in range(span):
            i = g * span + b
            rows = block_rows(i, 1, 1)
            for res in range(n_res):
                outs = []
                stats = jnp.zeros((blk, LANES), F32)
                for j in range(n_pairs):
                    vw = v_s[res, block_rows(i, 2, 4), 2 * j * LANES:(2 * j + 2) * LANES]
                    ov = _dot(p_s[slot, staged(b, res, j)], vw)
                    outs.append(ov[:, :LANES] / ov[:, LANES:])
                    m = m_s[slot, staged(b, res, j)]
                    lse = jnp.where(first_head, m[:blk], m[blk:]) * LN2 + jnp.log(ov[:, LANES:])
                    stats = jnp.where((lane & (hd - 1)) == j, lse, stats)
                o_ref[0, rows, res * w:(res + 1) * w] = jnp.concatenate(outs, axis=-1).astype(o_ref.dtype)
                lse_ref[0, rows, res * LANES:(res + 1) * LANES] = stats

    score_stage(0, bias_s[jnp.where(n == 0, 1, 0)])

    def pipeline_trip(g, carry):
        value_stage(g - 1)
        score_stage(g)
        return carry

    lax.fori_loop(1, n_groups, pipeline_trip, 0)
    value_stage(n_groups - 1)


def _attn_call(q, k, v, *, width, window, dilation):
    b, ls, _ = q.shape
    w = width
    n_back = window // dilation
    blk = ATTN_BLOCK
    assert n_back <= blk
    sub_blocks = min(ATTN_MAX_SUB_BLOCKS, ls // blk)
    step_rows = sub_blocks * blk
    assert ls % step_rows == 0
    n_heads = w // ATTN_HEAD_DIM
    assert n_heads <= LANES and n_heads % 2 == 0 and 2 * ATTN_HEAD_DIM == LANES
    n_res = max(1, min(dilation, ATTN_MAX_SUB_BLOCKS // sub_blocks))
    assert dilation % n_res == 0
    span = min(ATTN_BLOCKS_PER_STAGE, sub_blocks)
    assert sub_blocks % span == 0
    n_staged = span * n_res * (n_heads // 2)
    cur =pl.BlockSpec((1, step_rows, n_res * w), lambda bi, r, n: (bi, n, r))
    prev = pl.BlockSpec((1, blk, n_res * w),
                        lambda bi, r, n: (bi, jnp.maximum(n * sub_blocks - 1, 0), r))
    return pl.pallas_call(
        functools.partial(_attn_kernel, n_back=n_back, n_heads=n_heads),
        out_shape=(jax.ShapeDtypeStruct((b, ls, dilation * w), BF16),
                   jax.ShapeDtypeStruct((b, ls, dilation * LANES), F32)),
        grid=(b, dilation // n_res, ls // step_rows),
        in_specs=[cur, prev, cur, prev, cur],
        out_specs=(cur, pl.BlockSpec((1, step_rows, n_res * LANES), lambda bi, r, n: (bi, n, r))),
        scratch_shapes=[pltpu.VMEM((n_res, blk + step_rows, w), BF16),
                        pltpu.VMEM((n_res, 2 * (blk + step_rows), 2 * w), BF16),
                        pltpu.VMEM((2, 2 * blk, 2 * blk), F32),
                        pltpu.VMEM((2, n_staged, blk, 4 * blk), BF16),
                        pltpu.VMEM((2, n_staged, 2 * blk, LANES), F32)],
        compiler_params=pltpu.CompilerParams(
            dimension_semantics=("arbitrary", "arbitrary", "arbitrary"),
            vmem_limit_bytes=VMEM_LIMIT_BYTES),
        name=f"attn_d{dilation}",
    )(q, k, k, v, v)


def _mlstm_kernel(qk_ref, v_ref, og_ref, gcol_ref, grow_ref, wc_ref, bc_ref, gn_ref, tri_ref, hm_ref,
                  o_ref, tail_s, c_s, m_s, *, ml_w):
    L = MLSTM_CHUNK
    nh = N_MLSTM_HEADS
    dh = ml_w // nh

    @pl.when(pl.program_id(1) == 0)
    def _():
        tail_s[...] = jnp.zeros_like(tail_s)
        c_s[...] = jnp.zeros_like(c_s)
        m_s[...] = jnp.zeros_like(m_s)

    tri = tri_ref[...]
    ti = lax.broadcasted_iota(jnp.int32, (L, L), 0)
    si = lax.broadcasted_iota(jnp.int32, (L, L), 1)
    causal = si <= ti
    ones = jnp.ones((L, dh), BF16)
    hp = hm_ref[...]
    for bi in range(qk_ref.shape[0]):
        _mlstm_chunk(bi, qk_ref, v_ref, og_ref, gcol_ref, grow_ref, wc_ref, bc_ref, gn_ref, o_ref,
                     tail_s, c_s, m_s, tri, causal, ones, hp, ml_w=ml_w)


def _mlstm_chunk(bi, qk_ref, v_ref, og_ref, gcol_ref, grow_ref, wc_ref, bc_ref, gn_ref, o_ref,
                 tail_s, c_s, m_s, tri, causal, ones, hp, *, ml_w):
    L = MLSTM_CHUNK
    nh = N_MLSTM_HEADS
    dh = ml_w // nh
    x = qk_ref[bi].astype(F32)
    tail = tail_s[bi]
    row8 = lax.broadcasted_iota(jnp.int32, tail.shape, 0)
    acc = bc_ref[...] + x * wc_ref[CONV_WIDTH - 1:CONV_WIDTH, :]
    for back in range(1, CONV_WIDTH):
        rolled = pltpu.roll(x, back, 0)
        top = jnp.where(row8 >= back, rolled[:MOD_ROWS], pltpu.roll(tail, back, 0))
        shifted = jnp.concatenate([top, rolled[MOD_ROWS:]], axis=0)
        acc = acc + shifted * wc_ref[CONV_WIDTH - 1 - back:CONV_WIDTH - back, :]
    tail_s[bi] = x[L - MOD_ROWS:]
    qk = acc * _sigmoid(acc)

    gcol = gcol_ref[bi]
    grow = grow_ref[bi]
    bcol_all = _dot(tri, gcol, precision=lax.Precision.HIGHEST)
    brow_all = _dot_nt(grow, tri, precision=lax.Precision.HIGHEST)

    ks = [qk[:, ml_w + h * dh:ml_w + (h + 1) * dh] * (dh ** -0.5) for h in range(nh)]

    heads = []
    for h in range(nh):
        st = bi * nh + h
        b_c = jnp.broadcast_to(bcol_all[:, nh + h:nh + h + 1], (L, dh))
        i_c = jnp.broadcast_to(gcol[:, h:h + 1], (L, dh))
        b_r = brow_all[nh + h:nh + h + 1, :]
        i_r = grow[h:h + 1, :]
        m_prev = m_s[st, 0:1, :]
        log_d = jnp.where(causal, b_c[:, 0:1] + (i_r - b_r), NEG_INF)
        m_inter = b_c + m_prev
        m_t = jnp.maximum(m_inter, jnp.max(log_d, axis=-1, keepdims=True))
        d_mat = jnp.exp2(log_d - m_t[:, 0:1])
        inter = jnp.exp2(m_inter - m_t)
        b_last = b_c[L - 1:L, :]
        w_log = b_last - b_c + i_c
        m_new = jnp.maximum(b_last + m_prev, jnp.max(w_log, axis=0, keepdims=True))
        wgt = jnp.exp2(w_log - m_new)
        decay = jnp.exp2(b_last + m_prev - m_new)
        heads.append((m_t, d_mat, inter, m_new, wgt, decay))

    outs = []
    for h, (m_t, d_mat, inter, _, _, _) in enumerate(heads):
        hcols = slice(h * dh, (h + 1) * dh)
        q = qk[:, hcols].astype(BF16)
        v_ext = jnp.concatenate([v_ref[bi, :, hcols], ones], axis=-1)
        c_prev = c_s[bi * nh + h]
        s_qk = _dot_nt(q, ks[h].astype(BF16)) * d_mat
        ext = jnp.concatenate([inter, inter], axis=-1) * _dot(q, c_prev.astype(BF16)) \
            + _dot(s_qk.astype(BF16), v_ext)
        outs.append(ext[:, :dh] / jnp.maximum(jnp.abs(ext[:, dh:]), jnp.exp2(-m_t)))
    hh = jnp.concatenate(outs, axis=-1)
    h2 = (hh * hh).astype(BF16)
    gw = hp.shape[0]
    msq = jnp.concatenate([_dot(h2[:, c0:c0 + gw], hp) for c0 in range(0, ml_w, gw)], axis=-1)
    hn = hh * lax.rsqrt(msq + RMS_EPS) * gn_ref[...]
    o_ref[bi] = (og_ref[bi].astype(F32) * hn).astype(o_ref.dtype)

    for h, (_, _, _, m_new, wgt, decay) in enumerate(heads):
        st = bi * nh + h
        hcols = slice(h * dh, (h + 1) * dh)
        v_ext = jnp.concatenate([v_ref[bi, :, hcols], ones], axis=-1)
        kw = (ks[h] * wgt).astype(BF16)
        c_s[st] = jnp.concatenate([decay, decay], axis=-1) * c_s[st] + _dot_tn(kw, v_ext)
        m_s[st] = jnp.broadcast_to(m_new, m_s.shape[1:])


def _mlstm_call(qkm, vm, og, gcol, grow, w_conv, b_conv, g_norm, tri, hmean):
    b, s, ml_w = vm.shape
    L = MLSTM_CHUNK
    nh = N_MLSTM_HEADS
    dh = ml_w // nh
    bb = MLSTM_BATCH_ROWS
    assert b % bb == 0
    const = lambda shape: pl.BlockSpec(shape, lambda g, i: (0,) * len(shape))
    tok = lambda w: pl.BlockSpec((bb, L, w), lambda g, i: (g, i, 0))
    return pl.pallas_call(
        functools.partial(_mlstm_kernel, ml_w=ml_w),
        out_shape=jax.ShapeDtypeStruct((b, s, ml_w), BF16),
        grid=(b // bb, s // L),
        in_specs=[tok(2 * ml_w), tok(ml_w), tok(ml_w), tok(GATE_LANES),
                  pl.BlockSpec((bb, 2 * nh, L), lambda g, i: (g, 0, i)),
                  const(w_conv.shape), const((1, 2 * ml_w)), const((1, ml_w)),
                  const((L, L)), const(hmean.shape)],
        out_specs=tok(ml_w),
        scratch_shapes=[pltpu.VMEM((bb, MOD_ROWS, 2 * ml_w), F32),
                        pltpu.VMEM((bb * nh, dh, 2 * dh), F32),
                        pltpu.VMEM((bb * nh, MOD_ROWS, dh), F32)],
        compiler_params=pltpu.CompilerParams(dimension_semantics=("arbitrary", "arbitrary"),
                                             vmem_limit_bytes=VMEM_LIMIT_BYTES),
        name="mlstm",
    )(qkm, vm, og, gcol, grow, w_conv, b_conv, g_norm, tri, hmean)


def _outffn_kernel(x_ref, mod_ref, g_ref, *rest, dilations):
    n_lay = len(dilations)
    o_refs, l_refs = rest[:n_lay], rest[n_lay:2 * n_lay]
    (hm_ref, ex_ref, wo_ref, wg_ref, wu_ref, wd_ref, out_ref,
     operm_s, lperm_s) = rest[2 * n_lay:]
    tm = x_ref.shape[1]

    def natural(ref, d, scratch):
        if d == 1:
            return ref[0].astype(F32)
        n_cg = scratch.shape[1]
        chain = [step for step in _gather_chain(dilations) if step[1] <= d]
        for level, (parent, dd, f) in enumerate(reversed(chain)):
            dst = scratch.at[level % 2]
            src = scratch.at[(level + 1) % 2]
            n = tm // dd
            for c in range(parent):
                for a in range(f):
                    r = a * parent + c
                    for cg in range(n_cg):
                        if level == 0:
                            c0 = (r * n_cg + cg) * LANES
                            rows = ref[0, :, c0:c0 + LANES].astype(F32)
                        else:
                            rows = src[cg, r * n:(r + 1) * n, :]
                        dst[cg, pl.ds(c * (tm // parent) + a, n, stride=f), :] = rows
        final = scratch.at[(len(chain) - 1) % 2]
        return jnp.concatenate([final[cg] for cg in range(n_cg)], axis=-1)

    lses = [natural(ref, d, lperm_s) for ref, d in zip(l_refs, dilations)]
    mx = functools.reduce(jnp.maximum, lses)
    es = [jnp.exp(l - mx) for l in lses]
    inv = 1.0 / functools.reduce(jnp.add, es)
    ex = ex_ref[...]
    attn = None
    for e, ref, d in zip(es, o_refs, dilations):
        term = _dot((e * inv).astype(BF16), ex) * natural(ref, d, operm_s)
        attn = term if attn is None else attn + term
    aw = attn.shape[-1]
    y = _dot(attn.astype(BF16), wo_ref[:aw, :]) + _dot(hm_ref[0], wo_ref[aw:, :])
    x1 = x_ref[0] + mod_ref[0, 2:3, :] * y
    ms = jnp.mean(x1 * x1, axis=-1, keepdims=True)
    hn = x1 * lax.rsqrt(ms + RMS_EPS) * g_ref[...]
    hb = (hn * (1.0 + mod_ref[0, 4:5, :]) + mod_ref[0, 3:4, :]).astype(BF16)
    g = _dot(hb, wg_ref[...])
    u = _dot(hb, wu_ref[...])
    a = (g * _sigmoid(g) * u).astype(BF16)
    out_ref[0] = x1 + mod_ref[0, 5:6, :] * _dot(a, wd_ref[...])


def _outffn_call(x, mod, g_ffn, os_, lses, hm, expand, wo, wg, wu, wd, *, dilations):
    b, s, d = x.shape
    tm = FFN_ROWS
    assert all(tm % (dl * BF16_SUBLANES) == 0 for dl in dilations)
    aw = expand.shape[1]
    const = lambda shape: pl.BlockSpec(shape, lambda bi, i: (0,) * len(shape))
    tok = lambda w: pl.BlockSpec((1, tm, w), lambda bi, i: (bi, i, 0))
    strided = lambda w: tuple(pl.BlockSpec((1, tm // dl, dl * w), lambda bi, i: (bi, i, 0)) for dl in dilations)
    return pl.pallas_call(
        functools.partial(_outffn_kernel, dilations=dilations),
        out_shape=jax.ShapeDtypeStruct((b, s, d), F32),
        grid=(b, s // tm),
        in_specs=[tok(d), pl.BlockSpec((1, N_MOD, d), lambda bi, i: (bi, 0, 0)), const((1, d)),
                  *strided(aw), *strided(LANES), tok(hm.shape[-1]),
                  const(expand.shape), const(wo.shape),
                  const(wg.shape), const(wu.shape), const(wd.shape)],
        out_specs=tok(d),
        scratch_shapes=[pltpu.VMEM((2, aw // LANES, tm, LANES), F32), pltpu.VMEM((2, 1, tm, LANES), F32)],
        compiler_params=pltpu.CompilerParams(dimension_semantics=("arbitrary", "arbitrary"),
                                             vmem_limit_bytes=VMEM_LIMIT_BYTES),
        name="outffn",
    )(x, mod, g_ffn, *os_, *lses, hm, expand, wo, wg, wu, wd)


def _block_diag_mean(width, group):
    idx = jnp.arange(width) // group
    return jnp.where(idx[:, None] == idx[None, :], 1.0 / group, 0.0).astype(BF16)


def kernel(x, c, g_mix, w_in, w_conv, b_conv, b_igate, b_fgate, q_norm_g, k_norm_g, mlstm_norm_g, w_out,
           g_ffn, w_gate, w_up, w_down, w_ada, b_ada):
    b, s, d = x.shape
    depth = g_mix.shape[0]
    attn_w = d // 2
    ml_w = d - attn_w
    nh = N_MLSTM_HEADS
    n_attn_heads = attn_w // ATTN_HEAD_DIM
    scale = ATTN_HEAD_DIM ** -0.5 * LOG2E
    dilations = tuple(dl for _, dl in DILATED_PATTERNS)
    assert b <= MOD_ROWS and 2 * nh <= GATE_LANES

    c_t = jnp.zeros((d, MOD_ROWS), F32).at[:, :b].set(c.T)
    assert attn_w % MXU_WIDTH == 0 and MXU_WIDTH % ATTN_HEAD_DIM == 0
    hmean_attn = _block_diag_mean(MXU_WIDTH, ATTN_HEAD_DIM)
    assert ml_w % MXU_WIDTH == 0 and MXU_WIDTH % (ml_w // nh) == 0
    hmean_ml = _block_diag_mean(MXU_WIDTH, ml_w // nh)
    stat_lane = jnp.arange(LANES)
    head_of_stat = jnp.where(stat_lane % ATTN_HEAD_DIM < n_attn_heads // 2,
                             2 * (stat_lane % ATTN_HEAD_DIM) + stat_lane // ATTN_HEAD_DIM, -1)
    head_of_col = jnp.arange(attn_w) // ATTN_HEAD_DIM
    expand = (head_of_stat[:, None] == head_of_col[None, :]).astype(BF16)
    tri = jnp.tril(jnp.ones((MLSTM_CHUNK, MLSTM_CHUNK), F32))

    for l in range(depth):
        mod = _mod_call(c_t, w_ada, b_ada[l][None, :], layer=l, n_rows=b)[:b].reshape(b, N_MOD, d)

        n_main = 3 * attn_w + 4 * ml_w
        assert n_main % MXU_WIDTH == 0
        wg_cols = w_in[l][:, n_main:]
        wgt = jnp.zeros((BF16_SUBLANES, d), F32).at[:2 * nh].set(wg_cols.T).astype(BF16)
        brow = jnp.concatenate([b_igate[l], b_fgate[l]])[:, None]
        gq = jnp.tile(q_norm_g[l], n_attn_heads)[None, :]
        gk = jnp.tile(k_norm_g[l], n_attn_heads)[None, :]

        qs, ks, vs, qkm, vm, og, gcol, grow = _inproj_call(
            x, mod, g_mix[l][None, :], w_in, wgt, gq, gk, hmean_attn, brow,
            layer=l, attn_w=attn_w, ml_w=ml_w, scale=scale, dilations=dilations)

        os_, lses = [], []
        for (window, dilation), q, k, v in zip(DILATED_PATTERNS, qs, ks, vs):
            o, lse = _attn_call(q, k, v, width=attn_w, window=window, dilation=dilation)
            os_.append(o)
            lses.append(lse)

        hm = _mlstm_call(qkm, vm, og, gcol, grow, w_conv[l], b_conv[l][None, :],
                         mlstm_norm_g[l][None, :], tri, hmean_ml)

        x = _outffn_call(x, mod, g_ffn[l][None, :], os_, lses, hm, expand,
                         w_out[l].astype(BF16), w_gate[l].astype(BF16), w_up[l].astype(BF16), w_down[l].astype(BF16),
                         dilations=dilations)
    return x
```

```python
import functools

import jax
import jax.numpy as jnp
from jax import lax
from jax.experimental import pallas as pl
from jax.experimental.pallas import tpu as pltpu

F32 = jnp.float32
BF16 = jnp.bfloat16

ATTN_HEAD_DIM = 64
N_MLSTM_HEADS = 4
CONV_WIDTH = 4
DILATED_PATTERNS = ((128, 1), (512, 4), (2048, 16))
ATTN_BLOCK = 128
N_MOD = 6
RMS_EPS = 1e-6

LANES = 128
MXU_WIDTH = 256
BF16_SUBLANES = 16
VMEM_LIMIT_BYTES = 56 * 1024 * 1024
F32_SUBLANES = 8
MOD_ROWS = F32_SUBLANES
LAYOUT_ROWS = 512
GATE_LANES = 128
INPROJ_ROWS = LAYOUT_ROWS
ATTN_MAX_SUB_BLOCKS = 8
MLSTM_CHUNK = 256
MLSTM_BATCH_ROWS = 1
FFN_ROWS = LAYOUT_ROWS

NEG_INF = float("-inf")
LOG2E = 1.4426950408889634
LN2 = 0.6931471805599453


def _dot(a, b, **kw):
    return jnp.dot(a, b, preferred_element_type=F32, **kw)


def _dot_nt(a, b, **kw):
    return lax.dot_general(a, b, (((1,), (1,)), ((), ())), preferred_element_type=F32, **kw)


def _dot_tn(a, b):
    return lax.dot_general(a, b, (((0,), (0,)), ((), ())), preferred_element_type=F32)


def _sigmoid(z):
    return 1.0 / (1.0 + jnp.exp(-z))


def _log_sigmoid(z):
    return jnp.minimum(z, 0.0) - jnp.log1p(jnp.exp(-jnp.abs(z)))


def _strided_shape(b, s, w, dilation):
    return (b, s // dilation, dilation * w)


def _gather_chain(dilations):
    chain, parent = [], 1
    for d in sorted(set(dilations)):
        if d == 1:
            continue
        assert d % parent == 0 and d // parent < F32_SUBLANES
        chain.append((parent, d, d // parent))
        parent = d
    return chain


def _mod_kernel(ct_ref, w_ref, b_ref, o_ref, *, n_rows):
    ct = ct_ref[...]
    sc = ct * _sigmoid(ct)
    w = w_ref[...]
    rows = [jnp.sum(w * sc[:, r:r + 1], axis=0, keepdims=True) for r in range(n_rows)]
    rows += [jnp.zeros_like(rows[0])] * (MOD_ROWS - n_rows)
    o_ref[...] = jnp.concatenate(rows, axis=0) + b_ref[...]


def _mod_call(c_t, w_ada, b_ada, *, layer, n_rows):
    _, d, n = w_ada.shape
    tn = n // 4
    assert tn % LANES == 0
    return pl.pallas_call(
        functools.partial(_mod_kernel, n_rows=n_rows),
        out_shape=jax.ShapeDtypeStruct((MOD_ROWS, n), F32),
        grid=(n // tn,),
        in_specs=[pl.BlockSpec((d, MOD_ROWS), lambda j: (0, 0)),
                  pl.BlockSpec((None, d, tn), lambda j: (layer, 0, j)),
                  pl.BlockSpec((1, tn), lambda j: (0, j))],
        out_specs=pl.BlockSpec((MOD_ROWS, tn), lambda j: (0, j)),
        compiler_params=pltpu.CompilerParams(dimension_semantics=("arbitrary",),
                                             vmem_limit_bytes=VMEM_LIMIT_BYTES),
        name="mod",
    )(c_t, w_ada, b_ada)


def _inproj_kernel(x_ref, mod_ref, g_ref, w_ref, wgt_ref, gq_ref, gk_ref, hp_ref,
                   brow_ref, *rest, attn_w, ml_w, scale, dilations):
    n_lay = len(dilations)
    q_refs, k_refs, v_refs = rest[:n_lay], rest[n_lay:2 * n_lay], rest[2 * n_lay:3 * n_lay]
    qkm_ref, vm_ref, og_ref, gcol_ref, grow_ref, perm_s, wb_s = rest[3 * n_lay:]
    tm = x_ref.shape[1]

    @pl.when((pl.program_id(0) == 0) & (pl.program_id(1) == 0))
    def _():
        for c0 in range(0, wb_s.shape[1], MXU_WIDTH):
            wb_s[:, c0:c0 + MXU_WIDTH] = w_ref[:, c0:c0 + MXU_WIDTH].astype(BF16)

    def emit(val, refs):
        w = val.shape[-1]
        n_cg = w // LANES
        ref_of = dict(zip(dilations, refs))
        if 1 in ref_of:
            ref_of[1][0] = val.astype(BF16)
        chain = _gather_chain(dilations)
        for cg in range(n_cg):
            perm_s[0, cg] = val[:, cg * LANES:(cg + 1) * LANES]
        for level, (parent, d, f) in enumerate(chain):
            src, dst = perm_s.at[level % 2], perm_s.at[(level + 1) % 2]
            n = tm // d
            for c in range(parent):
                for a in range(f):
                    r = a * parent + c
                    for cg in range(n_cg):
                        rows = src[cg, pl.ds(c * (tm // parent) + a, n, stride=f), :]
                        c0 = r * w + cg * LANES
                        ref_of[d][0, :, c0:c0 + LANES] = rows.astype(BF16)
                        if level + 1 < len(chain):
                            dst[cg, r * n:(r + 1) * n, :] = rows

    x = x_ref[0]
    ms = jnp.mean(x * x, axis=-1, keepdims=True)
    y = x * lax.rsqrt(ms + RMS_EPS) * g_ref[...]
    h = (y * (1.0 + mod_ref[0, 1:2, :]) + mod_ref[0, 0:1, :]).astype(BF16)

    n_attn = 3 * attn_w
    xa = _dot(h, wb_s[:, :n_attn])
    hp = hp_ref[...]

    def head_norm(t, g):
        t2 = (t * t).astype(BF16)
        gw = hp.shape[0]
        msq = jnp.concatenate([_dot(t2[:, c0:c0 + gw], hp) for c0 in range(0, t.shape[-1], gw)], axis=-1)
        return t * lax.rsqrt(msq + RMS_EPS) * g

    emit(head_norm(xa[:, :attn_w], gq_ref[...]) * scale, q_refs)
    emit(head_norm(xa[:, attn_w:2 * attn_w], gk_ref[...]), k_refs)
    emit(xa[:, 2 * attn_w:], v_refs)

    xm = _dot(h, wb_s[:, n_attn:n_attn + 4 * ml_w])
    qkm_ref[0] = xm[:, :2 * ml_w].astype(BF16)
    vm_ref[0] = xm[:, 2 * ml_w:3 * ml_w].astype(BF16)
    og_ref[0] = _sigmoid(xm[:, 3 * ml_w:]).astype(BF16)

    nh = N_MLSTM_HEADS
    zr = _dot_nt(wgt_ref[...], h)[:2 * nh] + brow_ref[...]
    row = lax.broadcasted_iota(jnp.int32, zr.shape, 0)
    gates = jnp.where(row < nh, zr, _log_sigmoid(zr)) * LOG2E
    grow_ref[0] = gates
    padded = jnp.concatenate([gates, jnp.zeros((GATE_LANES - 2 * nh, tm), F32)], axis=0)
    gcol_ref[0] = padded.T


def _inproj_call(x, mod, g_mix, w, wgt, gq, gk, hp, brow, *, layer, attn_w, ml_w, scale, dilations):
    b, s, d = x.shape
    tm = INPROJ_ROWS
    nh2 = 2 * N_MLSTM_HEADS
    assert all(tm % (dl * BF16_SUBLANES) == 0 for dl in dilations)
    const = lambda shape: pl.BlockSpec(shape, lambda bi, i: (0,) * len(shape))
    tok = lambda w: pl.BlockSpec((1, tm, w), lambda bi, i: (bi, i, 0))
    strided_shapes = tuple(jax.ShapeDtypeStruct(_strided_shape(b, s, attn_w, dl), BF16) for dl in dilations)
    strided_specs = tuple(pl.BlockSpec((1, tm // dl, dl * attn_w), lambda bi, i: (bi, i, 0)) for dl in dilations)
    out_shape = strided_shapes * 3 + (
        jax.ShapeDtypeStruct((b, s, 2 * ml_w), BF16),
        jax.ShapeDtypeStruct((b, s, ml_w), BF16),
        jax.ShapeDtypeStruct((b, s, ml_w), BF16),
        jax.ShapeDtypeStruct((b, s, GATE_LANES), F32),
        jax.ShapeDtypeStruct((b, nh2, s), F32))
    outs = pl.pallas_call(
        functools.partial(_inproj_kernel, attn_w=attn_w, ml_w=ml_w, scale=scale, dilations=dilations),
        out_shape=out_shape,
        grid=(b, s // tm),
        in_specs=[tok(d),
                  pl.BlockSpec((1, N_MOD, d), lambda bi, i: (bi, 0, 0)),
                  const((1, d)),
                  pl.BlockSpec((None,) + w.shape[1:], lambda bi, i: (layer, 0, 0),
                               pipeline_mode=pl.Buffered(1)),
                  const(wgt.shape),
                  const((1, attn_w)), const((1, attn_w)), const(hp.shape),
                  const((nh2, 1))],
        out_specs=strided_specs * 3 + (tok(2 * ml_w), tok(ml_w), tok(ml_w), tok(GATE_LANES),
                                       pl.BlockSpec((1, nh2, tm), lambda bi, i: (bi, 0, i))),
        scratch_shapes=[pltpu.VMEM((2, attn_w // LANES, tm, LANES), F32),
                        pltpu.VMEM((d, 3 * attn_w + 4 * ml_w), BF16)],
        compiler_params=pltpu.CompilerParams(dimension_semantics=("arbitrary", "arbitrary"),
                                             vmem_limit_bytes=VMEM_LIMIT_BYTES),
        name="inproj",
    )(x, mod, g_mix, w, wgt, gq, gk, hp, brow)
    n_lay = len(dilations)
    return (outs[:n_lay], outs[n_lay:2 * n_lay], outs[2 * n_lay:3 * n_lay]) + tuple(outs[3 * n_lay:])


def _attn_kernel(q_ref, kp_ref, kc_ref, vp_ref, vc_ref, bias_ref, ind_ref, o_ref, lse_ref, k0_s, v_s,
                 *, n_heads):
    blk = ATTN_BLOCK
    sub_blocks = q_ref.shape[1] // blk
    hd = ATTN_HEAD_DIM
    w = n_heads * hd
    n_res = q_ref.shape[2] // w
    n_pairs = n_heads // 2
    n = pl.program_id(2)
    lane = lax.broadcasted_iota(jnp.int32, (blk, LANES), 1)
    first_head = lane < hd
    for res in range(n_res):
        k0_s[res, 0:blk, :] = kp_ref[0, :, res * w:(res + 1) * w]
        k0_s[res, blk:, :] = kc_ref[0, 0:blk, res * w:(res + 1) * w]
        for kb in range(sub_blocks + 1):
            for j in range(n_pairs):
                cols = slice(res * w + j * LANES, res * w + (j + 1) * LANES)
                src = vp_ref[0, :, cols] if kb == 0 else vc_ref[0, (kb - 1) * blk:kb * blk, cols]
                ra, rb = 2 * kb * blk, (2 * kb + 1) * blk
                v_s[res, ra:ra + blk, j * LANES:(j + 1) * LANES] = jnp.where(first_head, src, 0)
                v_s[res, rb:rb + blk, j * LANES:(j + 1) * LANES] = jnp.where(first_head, 0, src)

    units = [(res, i, j) for res in range(n_res) for i in range(sub_blocks) for j in range(n_pairs)]
    scores, maxes, probs = [], [], []
    for res, i, j in units:
        cols = slice(j * LANES, (j + 1) * LANES)
        q = q_ref[0, i * blk:(i + 1) * blk, res * w + j * LANES:res * w + (j + 1) * LANES]
        if i == 0:
            ks = k0_s[res, :, cols]
            bias = bias_ref[jnp.where(n == 0, 1, 0)]
        else:
            ks = kc_ref[0, (i - 1) * blk:(i + 1) * blk, res * w + j * LANES:res * w + (j + 1) * LANES]
            bias = bias_ref[0]
        q2 = jnp.concatenate([jnp.where(first_head, q, 0), jnp.where(first_head, 0, q)], axis=0)
        s = _dot_nt(q2, ks) + bias
        scores.append(s)
        maxes.append(jnp.max(s, axis=-1, keepdims=True))
    for s, m in zip(scores, maxes):
        p = jnp.exp2(s - m).astype(BF16)
        probs.append(jnp.concatenate([p[:blk, :blk], p[blk:, :blk], p[:blk, blk:], p[blk:, blk:]], axis=1))
    for res in range(n_res):
        for i in range(sub_blocks):
            outs = []
            stats = jnp.zeros((blk, LANES), F32)
            for j in range(n_pairs):
                u = (res * sub_blocks + i) * n_pairs + j
                vw = jnp.concatenate([v_s[res, 2 * i * blk:2 * (i + 2) * blk, j * LANES:(j + 1) * LANES],
                                      ind_ref[...]], axis=1)
                ov = _dot(probs[u], vw)
                outs.append(ov[:, :LANES] / ov[:, LANES:])
                m = jnp.broadcast_to(maxes[u], (2 * blk, LANES))
                lse = jnp.where(first_head, m[:blk], m[blk:]) * LN2 + jnp.log(ov[:, LANES:])
                stats = jnp.where((lane & (hd - 1)) == j, lse, stats)
            rows = slice(i * blk, (i + 1) * blk)
            o_ref[0, rows, res * w:(res + 1) * w] = jnp.concatenate(outs, axis=-1).astype(o_ref.dtype)
            lse_ref[0, rows, res * LANES:(res + 1) * LANES] = stats


def _attn_band_masks(n_back):
    blk = ATTN_BLOCK
    row = jnp.arange(2 * blk)[:, None] % blk
    col = jnp.arange(2 * blk)[None, :]
    band = (col >= row + (blk - n_back)) & (col <= row + blk)
    return jnp.stack([jnp.where(band, 0.0, NEG_INF), jnp.where(band & (col >= blk), 0.0, NEG_INF)]).astype(F32)


def _attn_indicators():
    first = (jnp.arange(LANES) < ATTN_HEAD_DIM)[None, :]
    plane = jnp.concatenate([jnp.broadcast_to(first, (ATTN_BLOCK, LANES)),
                             jnp.broadcast_to(~first, (ATTN_BLOCK, LANES))], axis=0)
    return jnp.concatenate([plane, plane], axis=0).astype(BF16)


def _attn_call(q, k, v, *, width, window, dilation):
    b, ls, _ = q.shape
    w = width
    n_back = window // dilation
    blk = ATTN_BLOCK
    assert n_back <= blk
    sub_blocks = min(ATTN_MAX_SUB_BLOCKS, ls // blk)
    step_rows = sub_blocks * blk
    assert ls % step_rows == 0
    n_heads = w // ATTN_HEAD_DIM
    assert n_heads <= LANES and n_heads % 2 == 0 and 2 * ATTN_HEAD_DIM == LANES
    n_res = max(1, min(dilation, ATTN_MAX_SUB_BLOCKS // sub_blocks))
    assert dilation % n_res == 0
    cur = pl.BlockSpec((1, step_rows, n_res * w), lambda bi, r, n: (bi, n, r))
    prev = pl.BlockSpec((1, blk, n_res * w),
                        lambda bi, r, n: (bi, jnp.maximum(n * sub_blocks - 1, 0), r))
    masks = _attn_band_masks(n_back)
    indicators = _attn_indicators()
    const = lambda shape: pl.BlockSpec(shape, lambda bi, r, n: (0,) * len(shape))
    return pl.pallas_call(
        functools.partial(_attn_kernel, n_heads=n_heads),
        out_shape=(jax.ShapeDtypeStruct((b, ls, dilation * w), BF16),
                   jax.ShapeDtypeStruct((b, ls, dilation * LANES), F32)),
        grid=(b, dilation // n_res, ls // step_rows),
        in_specs=[cur, prev, cur, prev, cur, const(masks.shape), const(indicators.shape)],
        out_specs=(cur, pl.BlockSpec((1, step_rows, n_res * LANES), lambda bi, r, n: (bi, n, r))),
        scratch_shapes=[pltpu.VMEM((n_res, 2 * blk, w), BF16),
                        pltpu.VMEM((n_res, 2 * (blk + step_rows), w), BF16)],
        compiler_params=pltpu.CompilerParams(
            dimension_semantics=("arbitrary", "arbitrary", "arbitrary"),
            vmem_limit_bytes=VMEM_LIMIT_BYTES),
        name=f"attn_d{dilation}",
    )(q, k, k, v, v, masks, indicators)


def _mlstm_kernel(qk_ref, v_ref, og_ref, gcol_ref, grow_ref, wc_ref, bc_ref, gn_ref, tri_ref, hm_ref,
                  o_ref, tail_s, c_s, m_s, *, ml_w):
    L = MLSTM_CHUNK
    nh = N_MLSTM_HEADS
    dh = ml_w // nh

    @pl.when(pl.program_id(1) == 0)
    def _():
        tail_s[...] = jnp.zeros_like(tail_s)
        c_s[...] = jnp.zeros_like(c_s)
        m_s[...] = jnp.zeros_like(m_s)

    tri = tri_ref[...]
    ti = lax.broadcasted_iota(jnp.int32, (L, L), 0)
    si = lax.broadcasted_iota(jnp.int32, (L, L), 1)
    causal = si <= ti
    ones = jnp.ones((L, dh), BF16)
    hp = hm_ref[...]
    for bi in range(qk_ref.shape[0]):
        _mlstm_chunk(bi, qk_ref, v_ref, og_ref, gcol_ref, grow_ref, wc_ref, bc_ref, gn_ref, o_ref,
                     tail_s, c_s, m_s, tri, causal, ones, hp, ml_w=ml_w)


def _mlstm_chunk(bi, qk_ref, v_ref, og_ref, gcol_ref, grow_ref, wc_ref, bc_ref, gn_ref, o_ref,
                 tail_s, c_s, m_s, tri, causal, ones, hp, *, ml_w):
    L = MLSTM_CHUNK
    nh = N_MLSTM_HEADS
    dh = ml_w // nh
    x = qk_ref[bi].astype(F32)
    tail = tail_s[bi]
    row8 = lax.broadcasted_iota(jnp.int32, tail.shape, 0)
    acc = bc_ref[...] + x * wc_ref[CONV_WIDTH - 1:CONV_WIDTH, :]
    for back in range(1, CONV_WIDTH):
        rolled = pltpu.roll(x, back, 0)
        top = jnp.where(row8 >= back, rolled[:MOD_ROWS], pltpu.roll(tail, back, 0))
        shifted = jnp.concatenate([top, rolled[MOD_ROWS:]], axis=0)
        acc = acc + shifted * wc_ref[CONV_WIDTH - 1 - back:CONV_WIDTH - back, :]
    tail_s[bi] = x[L - MOD_ROWS:]
    qk = acc * _sigmoid(acc)

    gcol = gcol_ref[bi]
    grow = grow_ref[bi]
    bcol_all = _dot(tri, gcol, precision=lax.Precision.HIGHEST)
    brow_all = _dot_nt(grow, tri, precision=lax.Precision.HIGHEST)

    ks = [qk[:, ml_w + h * dh:ml_w + (h + 1) * dh] * (dh ** -0.5) for h in range(nh)]

    heads = []
    for h in range(nh):
        st = bi * nh + h
        b_c = jnp.broadcast_to(bcol_all[:, nh + h:nh + h + 1], (L, dh))
        i_c = jnp.broadcast_to(gcol[:, h:h + 1], (L, dh))
        b_r = brow_all[nh + h:nh + h + 1, :]
        i_r = grow[h:h + 1, :]
        m_prev = m_s[st, 0:1, :]
        log_d = jnp.where(causal, b_c[:, 0:1] + (i_r - b_r), NEG_INF)
        m_inter = b_c + m_prev
        m_t = jnp.maximum(m_inter, jnp.max(log_d, axis=-1, keepdims=True))
        d_mat = jnp.exp2(log_d - m_t[:, 0:1])
        inter = jnp.exp2(m_inter - m_t)
        b_last = b_c[L - 1:L, :]
        w_log = b_last - b_c + i_c
        m_new = jnp.maximum(b_last + m_prev, jnp.max(w_log, axis=0, keepdims=True))
        wgt = jnp.exp2(w_log - m_new)
        decay = jnp.exp2(b_last + m_prev - m_new)
        heads.append((m_t, d_mat, inter, m_new, wgt, decay))

    outs = []
    for h, (m_t, d_mat, inter, _, _, _) in enumerate(heads):
        hcols = slice(h * dh, (h + 1) * dh)
        q = qk[:, hcols].astype(BF16)
        v_ext = jnp.concatenate([v_ref[bi, :, hcols], ones], axis=-1)
        c_prev = c_s[bi * nh + h]
        s_qk = _dot_nt(q, ks[h].astype(BF16)) * d_mat
        ext = jnp.concatenate([inter, inter], axis=-1) * _dot(q, c_prev.astype(BF16)) \
            + _dot(s_qk.astype(BF16), v_ext)
        outs.append(ext[:, :dh] / jnp.maximum(jnp.abs(ext[:, dh:]), jnp.exp2(-m_t)))
    hh = jnp.concatenate(outs, axis=-1)
    h2 = (hh * hh).astype(BF16)
    gw = hp.shape[0]
    msq = jnp.concatenate([_dot(h2[:, c0:c0 + gw], hp) for c0 in range(0, ml_w, gw)], axis=-1)
    hn = hh * lax.rsqrt(msq + RMS_EPS) * gn_ref[...]
    o_ref[bi] = (og_ref[bi].astype(F32) * hn).astype(o_ref.dtype)

    for h, (_, _, _, m_new, wgt, decay) in enumerate(heads):
        st = bi * nh + h
        hcols = slice(h * dh, (h + 1) * dh)
        v_ext = jnp.concatenate([v_ref[bi, :, hcols], ones], axis=-1)
        kw = (ks[h] * wgt).astype(BF16)
        c_s[st] = jnp.concatenate([decay, decay], axis=-1) * c_s[st] + _dot_tn(kw, v_ext)
        m_s[st] = jnp.broadcast_to(m_new, m_s.shape[1:])


def _mlstm_call(qkm, vm, og, gcol, grow, w_conv, b_conv, g_norm, tri, hmean):
    b, s, ml_w = vm.shape
    L = MLSTM_CHUNK
    nh = N_MLSTM_HEADS
    dh = ml_w // nh
    bb = MLSTM_BATCH_ROWS
    assert b % bb == 0
    const = lambda shape: pl.BlockSpec(shape, lambda g, i: (0,) * len(shape))
    tok = lambda w: pl.BlockSpec((bb, L, w), lambda g, i: (g, i, 0))
    return pl.pallas_call(
        functools.partial(_mlstm_kernel, ml_w=ml_w),
        out_shape=jax.ShapeDtypeStruct((b, s, ml_w), BF16),
        grid=(b // bb, s // L),
        in_specs=[tok(2 * ml_w), tok(ml_w), tok(ml_w), tok(GATE_LANES),
                  pl.BlockSpec((bb, 2 * nh, L), lambda g, i: (g, 0, i)),
                  const(w_conv.shape), const((1, 2 * ml_w)), const((1, ml_w)),
                  const((L, L)), const(hmean.shape)],
        out_specs=tok(ml_w),
        scratch_shapes=[pltpu.VMEM((bb, MOD_ROWS, 2 * ml_w), F32),
                        pltpu.VMEM((bb * nh, dh, 2 * dh), F32),
                        pltpu.VMEM((bb * nh, MOD_ROWS, dh), F32)],
        compiler_params=pltpu.CompilerParams(dimension_semantics=("arbitrary", "arbitrary"),
                                             vmem_limit_bytes=VMEM_LIMIT_BYTES),
        name="mlstm",
    )(qkm, vm, og, gcol, grow, w_conv, b_conv, g_norm, tri, hmean)


def _outffn_kernel(x_ref, mod_ref, g_ref, *rest, dilations):
    n_lay = len(dilations)
    o_refs, l_refs = rest[:n_lay], rest[n_lay:2 * n_lay]
    (hm_ref, ex_ref, wo_ref, wg_ref, wu_ref, wd_ref, out_ref,
     operm_s, lperm_s) = rest[2 * n_lay:]
    tm = x_ref.shape[1]

    def natural(ref, d, scratch):
        if d == 1:
            return ref[0].astype(F32)
        n_cg = scratch.shape[1]
        chain = [step for step in _gather_chain(dilations) if step[1] <= d]
        for level, (parent, dd, f) in enumerate(reversed(chain)):
            dst = scratch.at[level % 2]
            src = scratch.at[(level + 1) % 2]
            n = tm // dd
            for c in range(parent):
                for a in range(f):
                    r = a * parent + c
                    for cg in range(n_cg):
                        if level == 0:
                            c0 = (r * n_cg + cg) * LANES
                            rows = ref[0, :, c0:c0 + LANES].astype(F32)
                        else:
                            rows = src[cg, r * n:(r + 1) * n, :]
                        dst[cg, pl.ds(c * (tm // parent) + a, n, stride=f), :] = rows
        final = scratch.at[(len(chain) - 1) % 2]
        return jnp.concatenate([final[cg] for cg in range(n_cg)], axis=-1)

    lses = [natural(ref, d, lperm_s) for ref, d in zip(l_refs, dilations)]
    mx = functools.reduce(jnp.maximum, lses)
    es = [jnp.exp(l - mx) for l in lses]
    inv = 1.0 / functools.reduce(jnp.add, es)
    ex = ex_ref[...]
    attn = None
    for e, ref, d in zip(es, o_refs, dilations):
        term = _dot((e * inv).astype(BF16), ex) * natural(ref, d, operm_s)
        attn = term if attn is None else attn + term
    aw = attn.shape[-1]
    y = _dot(attn.astype(BF16), wo_ref[:aw, :]) + _dot(hm_ref[0], wo_ref[aw:, :])
    x1 = x_ref[0] + mod_ref[0, 2:3, :] * y
    ms = jnp.mean(x1 * x1, axis=-1, keepdims=True)
    hn = x1 * lax.rsqrt(ms + RMS_EPS) * g_ref[...]
    hb = (hn * (1.0 + mod_ref[0, 4:5, :]) + mod_ref[0, 3:4, :]).astype(BF16)
    g = _dot(hb, wg_ref[...])
    u = _dot(hb, wu_ref[...])
    a = (g * _sigmoid(g) * u).astype(BF16)
    out_ref[0] = x1 + mod_ref[0, 5:6, :] * _dot(a, wd_ref[...])


def _outffn_call(x, mod, g_ffn, os_, lses, hm, expand, wo, wg, wu, wd, *, dilations):
    b, s, d = x.shape
    tm = FFN_ROWS
    assert all(tm % (dl * BF16_SUBLANES) == 0 for dl in dilations)
    aw = expand.shape[1]
    const = lambda shape: pl.BlockSpec(shape, lambda bi, i: (0,) * len(shape))
    tok = lambda w: pl.BlockSpec((1, tm, w), lambda bi, i: (bi, i, 0))
    strided = lambda w: tuple(pl.BlockSpec((1, tm // dl, dl * w), lambda bi, i: (bi, i, 0)) for dl in dilations)
    return pl.pallas_call(
        functools.partial(_outffn_kernel, dilations=dilations),
        out_shape=jax.ShapeDtypeStruct((b, s, d), F32),
        grid=(b, s // tm),
        in_specs=[tok(d), pl.BlockSpec((1, N_MOD, d), lambda bi, i: (bi, 0, 0)), const((1, d)),
                  *strided(aw), *strided(LANES), tok(hm.shape[-1]),
                  const(expand.shape), const(wo.shape),
                  const(wg.shape), const(wu.shape), const(wd.shape)],
        out_specs=tok(d),
        scratch_shapes=[pltpu.VMEM((2, aw // LANES, tm, LANES), F32), pltpu.VMEM((2, 1, tm, LANES), F32)],
        compiler_params=pltpu.CompilerParams(dimension_semantics=("arbitrary", "arbitrary"),
                                             vmem_limit_bytes=VMEM_LIMIT_BYTES),
        name="outffn",
    )(x, mod, g_ffn, *os_, *lses, hm, expand, wo, wg, wu, wd)


def _block_diag_mean(width, group):
    idx = jnp.arange(width) // group
    return jnp.where(idx[:, None] == idx[None, :], 1.0 / group, 0.0).astype(BF16)


def kernel(x, c, g_mix, w_in, w_conv, b_conv, b_igate, b_fgate, q_norm_g, k_norm_g, mlstm_norm_g, w_out,
           g_ffn, w_gate, w_up, w_down, w_ada, b_ada):
    b, s, d = x.shape
    depth = g_mix.shape[0]
    attn_w = d // 2
    ml_w = d - attn_w
    nh = N_MLSTM_HEADS
    n_attn_heads = attn_w // ATTN_HEAD_DIM
    scale = ATTN_HEAD_DIM ** -0.5 * LOG2E
    dilations = tuple(dl for _, dl in DILATED_PATTERNS)
    assert b <= MOD_ROWS and 2 * nh <= GATE_LANES

    c_t = jnp.zeros((d, MOD_ROWS), F32).at[:, :b].set(c.T)
    assert attn_w % MXU_WIDTH == 0 and MXU_WIDTH % ATTN_HEAD_DIM == 0
    hmean_attn = _block_diag_mean(MXU_WIDTH, ATTN_HEAD_DIM)
    assert ml_w % MXU_WIDTH == 0 and MXU_WIDTH % (ml_w // nh) == 0
    hmean_ml = _block_diag_mean(MXU_WIDTH, ml_w // nh)
    stat_lane = jnp.arange(LANES)
    head_of_stat = jnp.where(stat_lane % ATTN_HEAD_DIM < n_attn_heads // 2,
                             2 * (stat_lane % ATTN_HEAD_DIM) + stat_lane // ATTN_HEAD_DIM, -1)
    head_of_col = jnp.arange(attn_w) // ATTN_HEAD_DIM
    expand = (head_of_stat[:, None] == head_of_col[None, :]).astype(BF16)
    tri = jnp.tril(jnp.ones((MLSTM_CHUNK, MLSTM_CHUNK), F32))

    for l in range(depth):
        mod = _mod_call(c_t, w_ada, b_ada[l][None, :], layer=l, n_rows=b)[:b].reshape(b, N_MOD, d)

        n_main = 3 * attn_w + 4 * ml_w
        assert n_main % MXU_WIDTH == 0
        wg_cols = w_in[l][:, n_main:]
        wgt = jnp.zeros((BF16_SUBLANES, d), F32).at[:2 * nh].set(wg_cols.T).astype(BF16)
        brow = jnp.concatenate([b_igate[l], b_fgate[l]])[:, None]
        gq = jnp.tile(q_norm_g[l], n_attn_heads)[None, :]
        gk = jnp.tile(k_norm_g[l], n_attn_heads)[None, :]

        qs, ks, vs, qkm, vm, og, gcol, grow = _inproj_call(
            x, mod, g_mix[l][None, :], w_in, wgt, gq, gk, hmean_attn, brow,
            layer=l, attn_w=attn_w, ml_w=ml_w, scale=scale, dilations=dilations)

        os_, lses = [], []
        for (window, dilation), q, k, v in zip(DILATED_PATTERNS, qs, ks, vs):
            o, lse = _attn_call(q, k, v, width=attn_w, window=window, dilation=dilation)
            os_.append(o)
            lses.append(lse)

        hm = _mlstm_call(qkm, vm, og, gcol, grow, w_conv[l], b_conv[l][None, :],
                         mlstm_norm_g[l][None, :], tri, hmean_ml)

        x = _outffn_call(x, mod, g_ffn[l][None, :], os_, lses, hm, expand,
                         w_out[l].astype(BF16), w_gate[l].astype(BF16), w_up[l].astype(BF16), w_down[l].astype(BF16),
                         dilations=dilations)
    return x
```

```python
import functools

import jax
import jax.numpy as jnp
from jax import lax
from jax.experimental import pallas as pl
from jax.experimental.pallas import tpu as pltpu

F32 = jnp.float32
BF16 = jnp.bfloat16

ATTN_HEAD_DIM = 64
N_MLSTM_HEADS = 4
CONV_WIDTH = 4
DILATED_PATTERNS = ((128, 1), (512, 4), (2048, 16))
ATTN_BLOCK = 128
N_MOD = 6
RMS_EPS = 1e-6

LANES = 128
MXU_WIDTH = 256
BF16_SUBLANES = 16
VMEM_LIMIT_BYTES = 56 * 1024 * 1024
F32_SUBLANES = 8
MOD_ROWS = F32_SUBLANES
LAYOUT_ROWS = 512
GATE_LANES = 128
INPROJ_ROWS = LAYOUT_ROWS
ATTN_MAX_SUB_BLOCKS = 8
MLSTM_CHUNK = 256
MLSTM_BATCH_ROWS = 1
FFN_ROWS = LAYOUT_ROWS

NEG_INF = float("-inf")
LOG2E = 1.4426950408889634
LN2 = 0.6931471805599453


def _dot(a, b, **kw):
    return jnp.dot(a, b, preferred_element_type=F32, **kw)


def _dot_nt(a, b, **kw):
    return lax.dot_general(a, b, (((1,), (1,)), ((), ())), preferred_element_type=F32, **kw)


def _dot_tn(a, b):
    return lax.dot_general(a, b, (((0,), (0,)), ((), ())), preferred_element_type=F32)


def _sigmoid(z):
    return 1.0 / (1.0 + jnp.exp(-z))


def _log_sigmoid(z):
    return jnp.minimum(z, 0.0) - jnp.log1p(jnp.exp(-jnp.abs(z)))


def _strided_shape(b, s, w, dilation):
    return (b, s // dilation, dilation * w)


def _gather_chain(dilations):
    chain, parent = [], 1
    for d in sorted(set(dilations)):
        if d == 1:
            continue
        assert d % parent == 0 and d // parent < F32_SUBLANES
        chain.append((parent, d, d // parent))
        parent = d
    return chain


def _mod_kernel(ct_ref, w_ref, b_ref, o_ref, *, n_rows):
    ct = ct_ref[...]
    sc = ct * _sigmoid(ct)
    w = w_ref[...]
    rows = [jnp.sum(w * sc[:, r:r + 1], axis=0, keepdims=True) for r in range(n_rows)]
    rows += [jnp.zeros_like(rows[0])] * (MOD_ROWS - n_rows)
    o_ref[...] = jnp.concatenate(rows, axis=0) + b_ref[...]


def _mod_call(c_t, w_ada, b_ada, *, layer, n_rows):
    _, d, n = w_ada.shape
    tn = n // 4
    assert tn % LANES == 0
    return pl.pallas_call(
        functools.partial(_mod_kernel, n_rows=n_rows),
        out_shape=jax.ShapeDtypeStruct((MOD_ROWS, n), F32),
        grid=(n // tn,),
        in_specs=[pl.BlockSpec((d, MOD_ROWS), lambda j: (0, 0)),
                  pl.BlockSpec((None, d, tn), lambda j: (layer, 0, j)),
                  pl.BlockSpec((1, tn), lambda j: (0, j))],
        out_specs=pl.BlockSpec((MOD_ROWS, tn), lambda j: (0, j)),
        compiler_params=pltpu.CompilerParams(dimension_semantics=("arbitrary",),
                                             vmem_limit_bytes=VMEM_LIMIT_BYTES),
        name="mod",
    )(c_t, w_ada, b_ada)


def _inproj_kernel(x_ref, mod_ref, g_ref, w_ref, wgt_ref, gq_ref, gk_ref, hp_ref,
                   brow_ref, *rest, attn_w, ml_w, scale, dilations):
    n_lay = len(dilations)
    q_refs, k_refs, v_refs = rest[:n_lay], rest[n_lay:2 * n_lay], rest[2 * n_lay:3 * n_lay]
    qkm_ref, vm_ref, og_ref, gcol_ref, grow_ref, perm_s, wb_s = rest[3 * n_lay:]
    tm = x_ref.shape[1]

    @pl.when((pl.program_id(0) == 0) & (pl.program_id(1) == 0))
    def _():
        for c0 in range(0, wb_s.shape[1], MXU_WIDTH):
            wb_s[:, c0:c0 + MXU_WIDTH] = w_ref[:, c0:c0 + MXU_WIDTH].astype(BF16)

    def emit(val, refs):
        w = val.shape[-1]
        n_cg = w // LANES
        ref_of = dict(zip(dilations, refs))
        if 1 in ref_of:
            ref_of[1][0] = val.astype(BF16)
        chain = _gather_chain(dilations)
        for cg in range(n_cg):
            perm_s[0, cg] = val[:, cg * LANES:(cg + 1) * LANES]
        for level, (parent, d, f) in enumerate(chain):
            src, dst = perm_s.at[level % 2], perm_s.at[(level + 1) % 2]
            n = tm // d
            for c in range(parent):
                for a in range(f):
                    r = a * parent + c
                    for cg in range(n_cg):
                        rows = src[cg, pl.ds(c * (tm // parent) + a, n, stride=f), :]
                        c0 = r * w + cg * LANES
                        ref_of[d][0, :, c0:c0 + LANES] = rows.astype(BF16)
                        if level + 1 < len(chain):
                            dst[cg, r * n:(r + 1) * n, :] = rows

    x = x_ref[0]
    ms = jnp.mean(x * x, axis=-1, keepdims=True)
    y = x * lax.rsqrt(ms + RMS_EPS) * g_ref[...]
    h = (y * (1.0 + mod_ref[0, 1:2, :]) + mod_ref[0, 0:1, :]).astype(BF16)

    n_attn = 3 * attn_w
    xa = _dot(h, wb_s[:, :n_attn])
    hp = hp_ref[...]

    def head_norm(t, g):
        t2 = (t * t).astype(BF16)
        gw = hp.shape[0]
        msq = jnp.concatenate([_dot(t2[:, c0:c0 + gw], hp) for c0 in range(0, t.shape[-1], gw)], axis=-1)
        return t * lax.rsqrt(msq + RMS_EPS) * g

    emit(head_norm(xa[:, :attn_w], gq_ref[...]) * scale, q_refs)
    emit(head_norm(xa[:, attn_w:2 * attn_w], gk_ref[...]), k_refs)
    emit(xa[:, 2 * attn_w:], v_refs)

    xm = _dot(h, wb_s[:, n_attn:n_attn + 4 * ml_w])
    qkm_ref[0] = xm[:, :2 * ml_w].astype(BF16)
    vm_ref[0] = xm[:, 2 * ml_w:3 * ml_w].astype(BF16)
    og_ref[0] = _sigmoid(xm[:, 3 * ml_w:]).astype(BF16)

    nh = N_MLSTM_HEADS
    zr = _dot_nt(wgt_ref[...], h)[:2 * nh] + brow_ref[...]
    row = lax.broadcasted_iota(jnp.int32, zr.shape, 0)
    gates = jnp.where(row < nh, zr, _log_sigmoid(zr)) * LOG2E
    grow_ref[0] = gates
    padded = jnp.concatenate([gates, jnp.zeros((GATE_LANES - 2 * nh, tm), F32)], axis=0)
    gcol_ref[0] = padded.T


def _inproj_call(x, mod, g_mix, w, wgt, gq, gk, hp, brow, *, layer, attn_w, ml_w, scale, dilations):
    b, s, d = x.shape
    tm = INPROJ_ROWS
    nh2 = 2 * N_MLSTM_HEADS
    assert all(tm % (dl * BF16_SUBLANES) == 0 for dl in dilations)
    const = lambda shape: pl.BlockSpec(shape, lambda bi, i: (0,) * len(shape))
    tok = lambda w: pl.BlockSpec((1, tm, w), lambda bi, i: (bi, i, 0))
    strided_shapes = tuple(jax.ShapeDtypeStruct(_strided_shape(b, s, attn_w, dl), BF16) for dl in dilations)
    strided_specs = tuple(pl.BlockSpec((1, tm // dl, dl * attn_w), lambda bi, i: (bi, i, 0)) for dl in dilations)
    out_shape = strided_shapes * 3 + (
        jax.ShapeDtypeStruct((b, s, 2 * ml_w), BF16),
        jax.ShapeDtypeStruct((b, s, ml_w), BF16),
        jax.ShapeDtypeStruct((b, s, ml_w), BF16),
        jax.ShapeDtypeStruct((b, s, GATE_LANES), F32),
        jax.ShapeDtypeStruct((b, nh2, s), F32))
    outs = pl.pallas_call(
        functools.partial(_inproj_kernel, attn_w=attn_w, ml_w=ml_w, scale=scale, dilations=dilations),
        out_shape=out_shape,
        grid=(b, s // tm),
        in_specs=[tok(d),
                  pl.BlockSpec((1, N_MOD, d), lambda bi, i: (bi, 0, 0)),
                  const((1, d)),
                  pl.BlockSpec((None,) + w.shape[1:], lambda bi, i: (layer, 0, 0),
                               pipeline_mode=pl.Buffered(1)),
                  const(wgt.shape),
                  const((1, attn_w)), const((1, attn_w)), const(hp.shape),
                  const((nh2, 1))],
        out_specs=strided_specs * 3 + (tok(2 * ml_w), tok(ml_w), tok(ml_w), tok(GATE_LANES),
                                       pl.BlockSpec((1, nh2, tm), lambda bi, i: (bi, 0, i))),
        scratch_shapes=[pltpu.VMEM((2, attn_w // LANES, tm, LANES), F32),
                        pltpu.VMEM((d, 3 * attn_w + 4 * ml_w), BF16)],
        compiler_params=pltpu.CompilerParams(dimension_semantics=("arbitrary", "arbitrary"),
                                             vmem_limit_bytes=VMEM_LIMIT_BYTES),
        name="inproj",
    )(x, mod, g_mix, w, wgt, gq, gk, hp, brow)
    n_lay = len(dilations)
    return (outs[:n_lay], outs[n_lay:2 * n_lay], outs[2 * n_lay:3 * n_lay]) + tuple(outs[3 * n_lay:])


def _attn_kernel(q_ref, kp_ref, kc_ref, vp_ref, vc_ref, bias_ref, ind_ref, o_ref, lse_ref, k0_s, *, n_heads):
    blk = ATTN_BLOCK
    sub_blocks = q_ref.shape[1] // blk
    hd = ATTN_HEAD_DIM
    w = n_heads * hd
    n_res = q_ref.shape[2] // w
    n_pairs = n_heads // 2
    n = pl.program_id(2)
    lane = lax.broadcasted_iota(jnp.int32, (blk, LANES), 1)
    first_head = lane < hd
    for res in range(n_res):
        k0_s[res, 0:blk, :] = kp_ref[0, :, res * w:(res + 1) * w]
        k0_s[res, blk:, :] = kc_ref[0, 0:blk, res * w:(res + 1) * w]

    def value_planes(res, kb, j):
        cols = slice(res * w + j * LANES, res * w + (j + 1) * LANES)
        src = vp_ref[0, :, cols] if kb == 0 else vc_ref[0, (kb - 1) * blk:kb * blk, cols]
        return [jnp.where(first_head, src, 0), jnp.where(first_head, 0, src)]

    units = [(res, i, j) for res in range(n_res) for i in range(sub_blocks) for j in range(n_pairs)]
    scores, maxes, probs = [], [], []
    for res, i, j in units:
        cols = slice(j * LANES, (j + 1) * LANES)
        q = q_ref[0, i * blk:(i + 1) * blk, res * w + j * LANES:res * w + (j + 1) * LANES]
        if i == 0:
            ks = k0_s[res, :, cols]
            bias = bias_ref[jnp.where(n == 0, 1, 0)]
        else:
            ks = kc_ref[0, (i - 1) * blk:(i + 1) * blk, res * w + j * LANES:res * w + (j + 1) * LANES]
            bias = bias_ref[0]
        q2 = jnp.concatenate([jnp.where(first_head, q, 0), jnp.where(first_head, 0, q)], axis=0)
        s = _dot_nt(q2, ks) + bias
        scores.append(s)
        maxes.append(jnp.max(s, axis=-1, keepdims=True))
    for s, m in zip(scores, maxes):
        p = jnp.exp2(s - m).astype(BF16)
        probs.append(jnp.concatenate([p[:blk, :blk], p[blk:, :blk], p[:blk, blk:], p[blk:, blk:]], axis=1))
    for res in range(n_res):
        for i in range(sub_blocks):
            outs = []
            stats = jnp.zeros((blk, LANES), F32)
            for j in range(n_pairs):
                u = (res * sub_blocks + i) * n_pairs + j
                planes = jnp.concatenate(value_planes(res, i, j) + value_planes(res, i + 1, j), axis=0)
                vw = jnp.concatenate([planes, ind_ref[...]], axis=1)
                ov = _dot(probs[u], vw)
                outs.append(ov[:, :LANES] / ov[:, LANES:])
                m = jnp.broadcast_to(maxes[u], (2 * blk, LANES))
                lse = jnp.where(first_head, m[:blk], m[blk:]) * LN2 + jnp.log(ov[:, LANES:])
                stats = jnp.where((lane & (hd - 1)) == j, lse, stats)
            rows = slice(i * blk, (i + 1) * blk)
            o_ref[0, rows, res * w:(res + 1) * w] = jnp.concatenate(outs, axis=-1).astype(o_ref.dtype)
            lse_ref[0, rows, res * LANES:(res + 1) * LANES] = stats


def _attn_band_masks(n_back):
    blk = ATTN_BLOCK
    row = jnp.arange(2 * blk)[:, None] % blk
    col = jnp.arange(2 * blk)[None, :]
    band = (col >= row + (blk - n_back)) & (col <= row + blk)
    return jnp.stack([jnp.where(band, 0.0, NEG_INF), jnp.where(band & (col >= blk), 0.0, NEG_INF)]).astype(F32)


def _attn_indicators():
    first = (jnp.arange(LANES) < ATTN_HEAD_DIM)[None, :]
    plane = jnp.concatenate([jnp.broadcast_to(first, (ATTN_BLOCK, LANES)),
                             jnp.broadcast_to(~first, (ATTN_BLOCK, LANES))], axis=0)
    return jnp.concatenate([plane, plane], axis=0).astype(BF16)


def _attn_call(q, k, v, *, width, window, dilation):
    b, ls, _ = q.shape
    w = width
    n_back = window // dilation
    blk = ATTN_BLOCK
    assert n_back <= blk
    sub_blocks = min(ATTN_MAX_SUB_BLOCKS, ls // blk)
    step_rows = sub_blocks * blk
    assert ls % step_rows == 0
    n_heads = w // ATTN_HEAD_DIM
    assert n_heads <= LANES and n_heads % 2 == 0 and 2 * ATTN_HEAD_DIM == LANES
    n_res = max(1, min(dilation, ATTN_MAX_SUB_BLOCKS // sub_blocks))
    assert dilation % n_res == 0
    cur = pl.BlockSpec((1, step_rows, n_res * w), lambda bi, r, n: (bi, n, r))
    prev = pl.BlockSpec((1, blk, n_res * w),
                        lambda bi, r, n: (bi, jnp.maximum(n * sub_blocks - 1, 0), r))
    masks = _attn_band_masks(n_back)
    indicators = _attn_indicators()
    const = lambda shape: pl.BlockSpec(shape, lambda bi, r, n: (0,) * len(shape))
    return pl.pallas_call(
        functools.partial(_attn_kernel, n_heads=n_heads),
        out_shape=(jax.ShapeDtypeStruct((b, ls, dilation * w), BF16),
                   jax.ShapeDtypeStruct((b, ls, dilation * LANES), F32)),
        grid=(b, dilation // n_res, ls // step_rows),
        in_specs=[cur, prev, cur, prev, cur, const(masks.shape), const(indicators.shape)],
        out_specs=(cur, pl.BlockSpec((1, step_rows, n_res * LANES), lambda bi, r, n: (bi, n, r))),
        scratch_shapes=[pltpu.VMEM((n_res, 2 * blk, w), BF16)],
        compiler_params=pltpu.CompilerParams(
            dimension_semantics=("arbitrary", "arbitrary", "arbitrary"),
            vmem_limit_bytes=VMEM_LIMIT_BYTES),
        name=f"attn_d{dilation}",
    )(q, k, k, v, v, masks, indicators)


def _mlstm_kernel(qk_ref, v_ref, og_ref, gcol_ref, grow_ref, wc_ref, bc_ref, gn_ref, tri_ref, hm_ref,
                  o_ref, tail_s, c_s, m_s, *, ml_w):
    L = MLSTM_CHUNK
    nh = N_MLSTM_HEADS
    dh = ml_w // nh

    @pl.when(pl.program_id(1) == 0)
    def _():
        tail_s[...] = jnp.zeros_like(tail_s)
        c_s[...] = jnp.zeros_like(c_s)
        m_s[...] = jnp.zeros_like(m_s)

    tri = tri_ref[...]
    ti = lax.broadcasted_iota(jnp.int32, (L, L), 0)
    si = lax.broadcasted_iota(jnp.int32, (L, L), 1)
    causal = si <= ti
    ones = jnp.ones((L, dh), BF16)
    hp = hm_ref[...]
    for bi in range(qk_ref.shape[0]):
        _mlstm_chunk(bi, qk_ref, v_ref, og_ref, gcol_ref, grow_ref, wc_ref, bc_ref, gn_ref, o_ref,
                     tail_s, c_s, m_s, tri, causal, ones, hp, ml_w=ml_w)


def _mlstm_chunk(bi, qk_ref, v_ref, og_ref, gcol_ref, grow_ref, wc_ref, bc_ref, gn_ref, o_ref,
                 tail_s, c_s, m_s, tri, causal, ones, hp, *, ml_w):
    L = MLSTM_CHUNK
    nh = N_MLSTM_HEADS
    dh = ml_w // nh
    x = qk_ref[bi].astype(F32)
    tail = tail_s[bi]
    row8 = lax.broadcasted_iota(jnp.int32, tail.shape, 0)
    acc = bc_ref[...] + x * wc_ref[CONV_WIDTH - 1:CONV_WIDTH, :]
    for back in range(1, CONV_WIDTH):
        rolled = pltpu.roll(x, back, 0)
        top = jnp.where(row8 >= back, rolled[:MOD_ROWS], pltpu.roll(tail, back, 0))
        shifted = jnp.concatenate([top, rolled[MOD_ROWS:]], axis=0)
        acc = acc + shifted * wc_ref[CONV_WIDTH - 1 - back:CONV_WIDTH - back, :]
    tail_s[bi] = x[L - MOD_ROWS:]
    qk = acc * _sigmoid(acc)

    gcol = gcol_ref[bi]
    grow = grow_ref[bi]
    bcol_all = _dot(tri, gcol, precision=lax.Precision.HIGHEST)
    brow_all = _dot_nt(grow, tri, precision=lax.Precision.HIGHEST)

    ks = [qk[:, ml_w + h * dh:ml_w + (h + 1) * dh] * (dh ** -0.5) for h in range(nh)]

    heads = []
    for h in range(nh):
        st = bi * nh + h
        b_c = jnp.broadcast_to(bcol_all[:, nh + h:nh + h + 1], (L, dh))
        i_c = jnp.broadcast_to(gcol[:, h:h + 1], (L, dh))
        b_r = brow_all[nh + h:nh + h + 1, :]
        i_r = grow[h:h + 1, :]
        m_prev = m_s[st, 0:1, :]
        log_d = jnp.where(causal, b_c[:, 0:1] + (i_r - b_r), NEG_INF)
        m_inter = b_c + m_prev
        m_t = jnp.maximum(m_inter, jnp.max(log_d, axis=-1, keepdims=True))
        d_mat = jnp.exp2(log_d - m_t[:, 0:1])
        inter = jnp.exp2(m_inter - m_t)
        b_last = b_c[L - 1:L, :]
        w_log = b_last - b_c + i_c
        m_new = jnp.maximum(b_last + m_prev, jnp.max(w_log, axis=0, keepdims=True))
        wgt = jnp.exp2(w_log - m_new)
        decay = jnp.exp2(b_last + m_prev - m_new)
        heads.append((m_t, d_mat, inter, m_new, wgt, decay))

    outs = []
    for h, (m_t, d_mat, inter, _, _, _) in enumerate(heads):
        hcols = slice(h * dh, (h + 1) * dh)
        q = qk[:, hcols].astype(BF16)
        v_ext = jnp.concatenate([v_ref[bi, :, hcols], ones], axis=-1)
        c_prev = c_s[bi * nh + h]
        s_qk = _dot_nt(q, ks[h].astype(BF16)) * d_mat
        ext = jnp.concatenate([inter, inter], axis=-1) * _dot(q, c_prev.astype(BF16)) \
            + _dot(s_qk.astype(BF16), v_ext)
        outs.append(ext[:, :dh] / jnp.maximum(jnp.abs(ext[:, dh:]), jnp.exp2(-m_t)))
    hh = jnp.concatenate(outs, axis=-1)
    h2 = (hh * hh).astype(BF16)
    gw = hp.shape[0]
    msq = jnp.concatenate([_dot(h2[:, c0:c0 + gw], hp) for c0 in range(0, ml_w, gw)], axis=-1)
    hn = hh * lax.rsqrt(msq + RMS_EPS) * gn_ref[...]
    o_ref[bi] = (og_ref[bi].astype(F32) * hn).astype(o_ref.dtype)

    for h, (_, _, _, m_new, wgt, decay) in enumerate(heads):
        st = bi * nh + h
        hcols = slice(h * dh, (h + 1) * dh)
        v_ext = jnp.concatenate([v_ref[bi, :, hcols], ones], axis=-1)
        kw = (ks[h] * wgt).astype(BF16)
        c_s[st] = jnp.concatenate([decay, decay], axis=-1) * c_s[st] + _dot_tn(kw, v_ext)
        m_s[st] = jnp.broadcast_to(m_new, m_s.shape[1:])


def _mlstm_call(qkm, vm, og, gcol, grow, w_conv, b_conv, g_norm, tri, hmean):
    b, s, ml_w = vm.shape
    L = MLSTM_CHUNK
    nh = N_MLSTM_HEADS
    dh = ml_w // nh
    bb = MLSTM_BATCH_ROWS
    assert b % bb == 0
    const = lambda shape: pl.BlockSpec(shape, lambda g, i: (0,) * len(shape))
    tok = lambda w: pl.BlockSpec((bb, L, w), lambda g, i: (g, i, 0))
    return pl.pallas_call(
        functools.partial(_mlstm_kernel, ml_w=ml_w),
        out_shape=jax.ShapeDtypeStruct((b, s, ml_w), BF16),
        grid=(b // bb, s // L),
        in_specs=[tok(2 * ml_w), tok(ml_w), tok(ml_w), tok(GATE_LANES),
                  pl.BlockSpec((bb, 2 * nh, L), lambda g, i: (g, 0, i)),
                  const(w_conv.shape), const((1, 2 * ml_w)), const((1, ml_w)),
                  const((L, L)), const(hmean.shape)],
        out_specs=tok(ml_w),
        scratch_shapes=[pltpu.VMEM((bb, MOD_ROWS, 2 * ml_w), F32),
                        pltpu.VMEM((bb * nh, dh, 2 * dh), F32),
                        pltpu.VMEM((bb * nh, MOD_ROWS, dh), F32)],
        compiler_params=pltpu.CompilerParams(dimension_semantics=("arbitrary", "arbitrary"),
                                             vmem_limit_bytes=VMEM_LIMIT_BYTES),
        name="mlstm",
    )(qkm, vm, og, gcol, grow, w_conv, b_conv, g_norm, tri, hmean)


def _outffn_kernel(x_ref, mod_ref, g_ref, *rest, dilations):
    n_lay = len(dilations)
    o_refs, l_refs = rest[:n_lay], rest[n_lay:2 * n_lay]
    (hm_ref, ex_ref, wo_ref, wg_ref, wu_ref, wd_ref, out_ref,
     operm_s, lperm_s) = rest[2 * n_lay:]
    tm = x_ref.shape[1]

    def natural(ref, d, scratch):
        if d == 1:
            return ref[0].astype(F32)
        n_cg = scratch.shape[1]
        chain = [step for step in _gather_chain(dilations) if step[1] <= d]
        for level, (parent, dd, f) in enumerate(reversed(chain)):
            dst = scratch.at[level % 2]
            src = scratch.at[(level + 1) % 2]
            n = tm // dd
            for c in range(parent):
                for a in range(f):
                    r = a * parent + c
                    for cg in range(n_cg):
                        if level == 0:
                            c0 = (r * n_cg + cg) * LANES
                            rows = ref[0, :, c0:c0 + LANES].astype(F32)
                        else:
                            rows = src[cg, r * n:(r + 1) * n, :]
                        dst[cg, pl.ds(c * (tm // parent) + a, n, stride=f), :] = rows
        final = scratch.at[(len(chain) - 1) % 2]
        return jnp.concatenate([final[cg] for cg in range(n_cg)], axis=-1)

    lses = [natural(ref, d, lperm_s) for ref, d in zip(l_refs, dilations)]
    mx = functools.reduce(jnp.maximum, lses)
    es = [jnp.exp(l - mx) for l in lses]
    inv = 1.0 / functools.reduce(jnp.add, es)
    ex = ex_ref[...]
    attn = None
    for e, ref, d in zip(es, o_refs, dilations):
        term = _dot((e * inv).astype(BF16), ex) * natural(ref, d, operm_s)
        attn = term if attn is None else attn + term
    aw = attn.shape[-1]
    y = _dot(attn.astype(BF16), wo_ref[:aw, :]) + _dot(hm_ref[0], wo_ref[aw:, :])
    x1 = x_ref[0] + mod_ref[0, 2:3, :] * y
    ms = jnp.mean(x1 * x1, axis=-1, keepdims=True)
    hn = x1 * lax.rsqrt(ms + RMS_EPS) * g_ref[...]
    hb = (hn * (1.0 + mod_ref[0, 4:5, :]) + mod_ref[0, 3:4, :]).astype(BF16)
    g = _dot(hb, wg_ref[...])
    u = _dot(hb, wu_ref[...])
    a = (g * _sigmoid(g) * u).astype(BF16)
    out_ref[0] = x1 + mod_ref[0, 5:6, :] * _dot(a, wd_ref[...])


def _outffn_call(x, mod, g_ffn, os_, lses, hm, expand, wo, wg, wu, wd, *, dilations):
    b, s, d = x.shape
    tm = FFN_ROWS
    assert all(tm % (dl * BF16_SUBLANES) == 0 for dl in dilations)
    aw = expand.shape[1]
    const = lambda shape: pl.BlockSpec(shape, lambda bi, i: (0,) * len(shape))
    tok = lambda w: pl.BlockSpec((1, tm, w), lambda bi, i: (bi, i, 0))
    strided = lambda w: tuple(pl.BlockSpec((1, tm // dl, dl * w), lambda bi, i: (bi, i, 0)) for dl in dilations)
    return pl.pallas_call(
        functools.partial(_outffn_kernel, dilations=dilations),
        out_shape=jax.ShapeDtypeStruct((b, s, d), F32),
        grid=(b, s // tm),
        in_specs=[tok(d), pl.BlockSpec((1, N_MOD, d), lambda bi, i: (bi, 0, 0)), const((1, d)),
                  *strided(aw), *strided(LANES), tok(hm.shape[-1]),
                  const(expand.shape), const(wo.shape),
                  const(wg.shape), const(wu.shape), const(wd.shape)],
        out_specs=tok(d),
        scratch_shapes=[pltpu.VMEM((2, aw // LANES, tm, LANES), F32), pltpu.VMEM((2, 1, tm, LANES), F32)],
        compiler_params=pltpu.CompilerParams(dimension_semantics=("arbitrary", "arbitrary"),
                                             vmem_limit_bytes=VMEM_LIMIT_BYTES),
        name="outffn",
    )(x, mod, g_ffn, *os_, *lses, hm, expand, wo, wg, wu, wd)


def _block_diag_mean(width, group):
    idx = jnp.arange(width) // group
    return jnp.where(idx[:, None] == idx[None, :], 1.0 / group, 0.0).astype(BF16)


def kernel(x, c, g_mix, w_in, w_conv, b_conv, b_igate, b_fgate, q_norm_g, k_norm_g, mlstm_norm_g, w_out,
           g_ffn, w_gate, w_up, w_down, w_ada, b_ada):
    b, s, d = x.shape
    depth = g_mix.shape[0]
    attn_w = d // 2
    ml_w = d - attn_w
    nh = N_MLSTM_HEADS
    n_attn_heads = attn_w // ATTN_HEAD_DIM
    scale = ATTN_HEAD_DIM ** -0.5 * LOG2E
    dilations = tuple(dl for _, dl in DILATED_PATTERNS)
    assert b <= MOD_ROWS and 2 * nh <= GATE_LANES

    c_t = jnp.zeros((d, MOD_ROWS), F32).at[:, :b].set(c.T)
    assert attn_w % MXU_WIDTH == 0 and MXU_WIDTH % ATTN_HEAD_DIM == 0
    hmean_attn = _block_diag_mean(MXU_WIDTH, ATTN_HEAD_DIM)
    assert ml_w % MXU_WIDTH == 0 and MXU_WIDTH % (ml_w // nh) == 0
    hmean_ml = _block_diag_mean(MXU_WIDTH, ml_w // nh)
    stat_lane = jnp.arange(LANES)
    head_of_stat = jnp.where(stat_lane % ATTN_HEAD_DIM < n_attn_heads // 2,
                             2 * (stat_lane % ATTN_HEAD_DIM) + stat_lane // ATTN_HEAD_DIM, -1)
    head_of_col = jnp.arange(attn_w) // ATTN_HEAD_DIM
    expand = (head_of_stat[:, None] == head_of_col[None, :]).astype(BF16)
    tri = jnp.tril(jnp.ones((MLSTM_CHUNK, MLSTM_CHUNK), F32))

    for l in range(depth):
        mod = _mod_call(c_t, w_ada, b_ada[l][None, :], layer=l, n_rows=b)[:b].reshape(b, N_MOD, d)

        n_main = 3 * attn_w + 4 * ml_w
        assert n_main % MXU_WIDTH == 0
        wg_cols = w_in[l][:, n_main:]
        wgt = jnp.zeros((BF16_SUBLANES, d), F32).at[:2 * nh].set(wg_cols.T).astype(BF16)
        brow = jnp.concatenate([b_igate[l], b_fgate[l]])[:, None]
        gq = jnp.tile(q_norm_g[l], n_attn_heads)[None, :]
        gk = jnp.tile(k_norm_g[l], n_attn_heads)[None, :]

        qs, ks, vs, qkm, vm, og, gcol, grow = _inproj_call(
            x, mod, g_mix[l][None, :], w_in, wgt, gq, gk, hmean_attn, brow,
            layer=l, attn_w=attn_w, ml_w=ml_w, scale=scale, dilations=dilations)

        os_, lses = [], []
        for (window, dilation), q, k, v in zip(DILATED_PATTERNS, qs, ks, vs):
            o, lse = _attn_call(q, k, v, width=attn_w, window=window, dilation=dilation)
            os_.append(o)
            lses.append(lse)

        hm = _mlstm_call(qkm, vm, og, gcol, grow, w_conv[l], b_conv[l][None, :],
                         mlstm_norm_g[l][None, :], tri, hmean_ml)

        x = _outffn_call(x, mod, g_ffn[l][None, :], os_, lses, hm, expand,
                         w_out[l].astype(BF16), w_gate[l].astype(BF16), w_up[l].astype(BF16), w_down[l].astype(BF16),
                         dilations=dilations)
    return x
```

```python
import functools

import jax
import jax.numpy as jnp
from jax import lax
from jax.experimental import pallas as pl
from jax.experimental.pallas import tpu as pltpu

F32 = jnp.float32
BF16 = jnp.bfloat16

ATTN_HEAD_DIM = 64
N_MLSTM_HEADS = 4
CONV_WIDTH = 4
DILATED_PATTERNS = ((128, 1), (512, 4), (2048, 16))
ATTN_BLOCK = 128
N_MOD = 6
RMS_EPS = 1e-6

LANES = 128
MXU_WIDTH = 256
BF16_SUBLANES = 16
VMEM_LIMIT_BYTES = 56 * 1024 * 1024
F32_SUBLANES = 8
MOD_ROWS = F32_SUBLANES
LAYOUT_ROWS = 512
GATE_LANES = 128
INPROJ_ROWS = LAYOUT_ROWS
ATTN_MAX_SUB_BLOCKS = 16
MLSTM_CHUNK = 256
MLSTM_BATCH_ROWS = 1
FFN_ROWS = LAYOUT_ROWS

NEG_INF = float("-inf")
LOG2E = 1.4426950408889634
LN2 = 0.6931471805599453


def _dot(a, b, **kw):
    return jnp.dot(a, b, preferred_element_type=F32, **kw)


def _dot_nt(a, b, **kw):
    return lax.dot_general(a, b, (((1,), (1,)), ((), ())), preferred_element_type=F32, **kw)


def _dot_tn(a, b):
    return lax.dot_general(a, b, (((0,), (0,)), ((), ())), preferred_element_type=F32)


def _sigmoid(z):
    return 1.0 / (1.0 + jnp.exp(-z))


def _log_sigmoid(z):
    return jnp.minimum(z, 0.0) - jnp.log1p(jnp.exp(-jnp.abs(z)))


def _strided_shape(b, s, w, dilation):
    return (b, s // dilation, dilation * w)


def _gather_chain(dilations):
    chain, parent = [], 1
    for d in sorted(set(dilations)):
        if d == 1:
            continue
        assert d % parent == 0 and d // parent < F32_SUBLANES
        chain.append((parent, d, d // parent))
        parent = d
    return chain


def _mod_kernel(ct_ref, w_ref, b_ref, o_ref, *, n_rows):
    ct = ct_ref[...]
    sc = ct * _sigmoid(ct)
    w = w_ref[...]
    rows = [jnp.sum(w * sc[:, r:r + 1], axis=0, keepdims=True) for r in range(n_rows)]
    rows += [jnp.zeros_like(rows[0])] * (MOD_ROWS - n_rows)
    o_ref[...] = jnp.concatenate(rows, axis=0) + b_ref[...]


def _mod_call(c_t, w_ada, b_ada, *, layer, n_rows):
    _, d, n = w_ada.shape
    tn = n // 4
    assert tn % LANES == 0
    return pl.pallas_call(
        functools.partial(_mod_kernel, n_rows=n_rows),
        out_shape=jax.ShapeDtypeStruct((MOD_ROWS, n), F32),
        grid=(n // tn,),
        in_specs=[pl.BlockSpec((d, MOD_ROWS), lambda j: (0, 0)),
                  pl.BlockSpec((None, d, tn), lambda j: (layer, 0, j)),
                  pl.BlockSpec((1, tn), lambda j: (0, j))],
        out_specs=pl.BlockSpec((MOD_ROWS, tn), lambda j: (0, j)),
        compiler_params=pltpu.CompilerParams(dimension_semantics=("arbitrary",),
                                             vmem_limit_bytes=VMEM_LIMIT_BYTES),
        name="mod",
    )(c_t, w_ada, b_ada)


def _inproj_kernel(x_ref, mod_ref, g_ref, w_ref, wgt_ref, gq_ref, gk_ref, hp_ref,
                   brow_ref, *rest, attn_w, ml_w, scale, dilations):
    n_lay = len(dilations)
    q_refs, k_refs, v_refs = rest[:n_lay], rest[n_lay:2 * n_lay], rest[2 * n_lay:3 * n_lay]
    qkm_ref, vm_ref, og_ref, gcol_ref, grow_ref, perm_s, wb_s = rest[3 * n_lay:]
    tm = x_ref.shape[1]

    @pl.when((pl.program_id(0) == 0) & (pl.program_id(1) == 0))
    def _():
        for c0 in range(0, wb_s.shape[1], MXU_WIDTH):
            wb_s[:, c0:c0 + MXU_WIDTH] = w_ref[:, c0:c0 + MXU_WIDTH].astype(BF16)

    def emit(val, refs):
        w = val.shape[-1]
        n_cg = w // LANES
        ref_of = dict(zip(dilations, refs))
        if 1 in ref_of:
            ref_of[1][0] = val.astype(BF16)
        chain = _gather_chain(dilations)
        for cg in range(n_cg):
            perm_s[0, cg] = val[:, cg * LANES:(cg + 1) * LANES]
        for level, (parent, d, f) in enumerate(chain):
            src, dst = perm_s.at[level % 2], perm_s.at[(level + 1) % 2]
            n = tm // d
            for c in range(parent):
                for a in range(f):
                    r = a * parent + c
                    for cg in range(n_cg):
                        rows = src[cg, pl.ds(c * (tm // parent) + a, n, stride=f), :]
                        c0 = r * w + cg * LANES
                        ref_of[d][0, :, c0:c0 + LANES] = rows.astype(BF16)
                        if level + 1 < len(chain):
                            dst[cg, r * n:(r + 1) * n, :] = rows

    x = x_ref[0]
    ms = jnp.mean(x * x, axis=-1, keepdims=True)
    y = x * lax.rsqrt(ms + RMS_EPS) * g_ref[...]
    h = (y * (1.0 + mod_ref[0, 1:2, :]) + mod_ref[0, 0:1, :]).astype(BF16)

    n_attn = 3 * attn_w
    xa = _dot(h, wb_s[:, :n_attn])
    hp = hp_ref[...]

    def head_norm(t, g):
        t2 = (t * t).astype(BF16)
        gw = hp.shape[0]
        msq = jnp.concatenate([_dot(t2[:, c0:c0 + gw], hp) for c0 in range(0, t.shape[-1], gw)], axis=-1)
        return t * lax.rsqrt(msq + RMS_EPS) * g

    emit(head_norm(xa[:, :attn_w], gq_ref[...]) * scale, q_refs)
    emit(head_norm(xa[:, attn_w:2 * attn_w], gk_ref[...]), k_refs)
    emit(xa[:, 2 * attn_w:], v_refs)

    xm = _dot(h, wb_s[:, n_attn:n_attn + 4 * ml_w])
    qkm_ref[0] = xm[:, :2 * ml_w].astype(BF16)
    vm_ref[0] = xm[:, 2 * ml_w:3 * ml_w].astype(BF16)
    og_ref[0] = _sigmoid(xm[:, 3 * ml_w:]).astype(BF16)

    nh = N_MLSTM_HEADS
    zr = _dot_nt(wgt_ref[...], h)[:2 * nh] + brow_ref[...]
    row = lax.broadcasted_iota(jnp.int32, zr.shape, 0)
    gates = jnp.where(row < nh, zr, _log_sigmoid(zr)) * LOG2E
    grow_ref[0] = gates
    padded = jnp.concatenate([gates, jnp.zeros((GATE_LANES - 2 * nh, tm), F32)], axis=0)
    gcol_ref[0] = padded.T


def _inproj_call(x, mod, g_mix, w, wgt, gq, gk, hp, brow, *, layer, attn_w, ml_w, scale, dilations):
    b, s, d = x.shape
    tm = INPROJ_ROWS
    nh2 = 2 * N_MLSTM_HEADS
    assert all(tm % (dl * BF16_SUBLANES) == 0 for dl in dilations)
    const = lambda shape: pl.BlockSpec(shape, lambda bi, i: (0,) * len(shape))
    tok = lambda w: pl.BlockSpec((1, tm, w), lambda bi, i: (bi, i, 0))
    strided_shapes = tuple(jax.ShapeDtypeStruct(_strided_shape(b, s, attn_w, dl), BF16) for dl in dilations)
    strided_specs = tuple(pl.BlockSpec((1, tm // dl, dl * attn_w), lambda bi, i: (bi, i, 0)) for dl in dilations)
    out_shape = strided_shapes * 3 + (
        jax.ShapeDtypeStruct((b, s, 2 * ml_w), BF16),
        jax.ShapeDtypeStruct((b, s, ml_w), BF16),
        jax.ShapeDtypeStruct((b, s, ml_w), BF16),
        jax.ShapeDtypeStruct((b, s, GATE_LANES), F32),
        jax.ShapeDtypeStruct((b, nh2, s), F32))
    outs = pl.pallas_call(
        functools.partial(_inproj_kernel, attn_w=attn_w, ml_w=ml_w, scale=scale, dilations=dilations),
        out_shape=out_shape,
        grid=(b, s // tm),
        in_specs=[tok(d),
                  pl.BlockSpec((1, N_MOD, d), lambda bi, i: (bi, 0, 0)),
                  const((1, d)),
                  pl.BlockSpec((None,) + w.shape[1:], lambda bi, i: (layer, 0, 0),
                               pipeline_mode=pl.Buffered(1)),
                  const(wgt.shape),
                  const((1, attn_w)), const((1, attn_w)), const(hp.shape),
                  const((nh2, 1))],
        out_specs=strided_specs * 3 + (tok(2 * ml_w), tok(ml_w), tok(ml_w), tok(GATE_LANES),
                                       pl.BlockSpec((1, nh2, tm), lambda bi, i: (bi, 0, i))),
        scratch_shapes=[pltpu.VMEM((2, attn_w // LANES, tm, LANES), F32),
                        pltpu.VMEM((d, 3 * attn_w + 4 * ml_w), BF16)],
        compiler_params=pltpu.CompilerParams(dimension_semantics=("arbitrary", "arbitrary"),
                                             vmem_limit_bytes=VMEM_LIMIT_BYTES),
        name="inproj",
    )(x, mod, g_mix, w, wgt, gq, gk, hp, brow)
    n_lay = len(dilations)
    return (outs[:n_lay], outs[n_lay:2 * n_lay], outs[2 * n_lay:3 * n_lay]) + tuple(outs[3 * n_lay:])


def _attn_kernel(q_ref, kp_ref, kc_ref, vp_ref, vc_ref, bias_ref, ind_ref, o_ref, lse_ref, k0_s, *, n_heads):
    blk = ATTN_BLOCK
    sub_blocks = q_ref.shape[1] // blk
    hd = ATTN_HEAD_DIM
    w = n_heads * hd
    n_res = q_ref.shape[2] // w
    n_pairs = n_heads // 2
    n = pl.program_id(2)
    lane = lax.broadcasted_iota(jnp.int32, (blk, LANES), 1)
    first_head = lane < hd
    for res in range(n_res):
        k0_s[res, 0:blk, :] = kp_ref[0, :, res * w:(res + 1) * w]
        k0_s[res, blk:, :] = kc_ref[0, 0:blk, res * w:(res + 1) * w]

    def value_planes(res, kb, j):
        cols = slice(res * w + j * LANES, res * w + (j + 1) * LANES)
        src = vp_ref[0, :, cols] if kb == 0 else vc_ref[0, (kb - 1) * blk:kb * blk, cols]
        return [jnp.where(first_head, src, 0), jnp.where(first_head, 0, src)]

    units = [(res, i, j) for res in range(n_res) for i in range(sub_blocks) for j in range(n_pairs)]
    scores, maxes, probs = [], [], []
    for res, i, j in units:
        cols = slice(j * LANES, (j + 1) * LANES)
        q = q_ref[0, i * blk:(i + 1) * blk, res * w + j * LANES:res * w + (j + 1) * LANES]
        if i == 0:
            ks = k0_s[res, :, cols]
            bias = bias_ref[jnp.where(n == 0, 1, 0)]
        else:
            ks = kc_ref[0, (i - 1) * blk:(i + 1) * blk, res * w + j * LANES:res * w + (j + 1) * LANES]
            bias = bias_ref[0]
        q2 = jnp.concatenate([jnp.where(first_head, q, 0), jnp.where(first_head, 0, q)], axis=0)
        s = _dot_nt(q2, ks) + bias
        scores.append(s)
        maxes.append(jnp.max(s, axis=-1, keepdims=True))
    for s, m in zip(scores, maxes):
        p = jnp.exp2(s - m).astype(BF16)
        probs.append(jnp.concatenate([p[:blk, :blk], p[blk:, :blk], p[:blk, blk:], p[blk:, blk:]], axis=1))
    for res in range(n_res):
        for i in range(sub_blocks):
            outs = []
            stats = jnp.zeros((blk, LANES), F32)
            for j in range(n_pairs):
                u = (res * sub_blocks + i) * n_pairs + j
                planes = jnp.concatenate(value_planes(res, i, j) + value_planes(res, i + 1, j), axis=0)
                vw = jnp.concatenate([planes, ind_ref[...]], axis=1)
                ov = _dot(probs[u], vw)
                outs.append(ov[:, :LANES] / ov[:, LANES:])
                m = jnp.broadcast_to(maxes[u], (2 * blk, LANES))
                lse = jnp.where(first_head, m[:blk], m[blk:]) * LN2 + jnp.log(ov[:, LANES:])
                stats = jnp.where((lane & (hd - 1)) == j, lse, stats)
            rows = slice(i * blk, (i + 1) * blk)
            o_ref[0, rows, res * w:(res + 1) * w] = jnp.concatenate(outs, axis=-1).astype(o_ref.dtype)
            lse_ref[0, rows, res * LANES:(res + 1) * LANES] = stats


def _attn_band_masks(n_back):
    blk = ATTN_BLOCK
    row = jnp.arange(2 * blk)[:, None] % blk
    col = jnp.arange(2 * blk)[None, :]
    band = (col >= row + (blk - n_back)) & (col <= row + blk)
    return jnp.stack([jnp.where(band, 0.0, NEG_INF), jnp.where(band & (col >= blk), 0.0, NEG_INF)]).astype(F32)


def _attn_indicators():
    first = (jnp.arange(LANES) < ATTN_HEAD_DIM)[None, :]
    plane = jnp.concatenate([jnp.broadcast_to(first, (ATTN_BLOCK, LANES)),
                             jnp.broadcast_to(~first, (ATTN_BLOCK, LANES))], axis=0)
    return jnp.concatenate([plane, plane], axis=0).astype(BF16)


def _attn_call(q, k, v, *, width, window, dilation):
    b, ls, _ = q.shape
    w = width
    n_back = window // dilation
    blk = ATTN_BLOCK
    assert n_back <= blk
    sub_blocks = min(ATTN_MAX_SUB_BLOCKS, ls // blk)
    step_rows = sub_blocks * blk
    assert ls % step_rows == 0
    n_heads = w // ATTN_HEAD_DIM
    assert n_heads <= LANES and n_heads % 2 == 0 and 2 * ATTN_HEAD_DIM == LANES
    n_res = max(1, min(dilation, ATTN_MAX_SUB_BLOCKS // sub_blocks))
    assert dilation % n_res == 0
    cur = pl.BlockSpec((1, step_rows, n_res * w), lambda bi, r, n: (bi, n, r))
    prev = pl.BlockSpec((1, blk, n_res * w),
                        lambda bi, r, n: (bi, jnp.maximum(n * sub_blocks - 1, 0), r))
    masks = _attn_band_masks(n_back)
    indicators = _attn_indicators()
    const = lambda shape: pl.BlockSpec(shape, lambda bi, r, n: (0,) * len(shape))
    return pl.pallas_call(
        functools.partial(_attn_kernel, n_heads=n_heads),
        out_shape=(jax.ShapeDtypeStruct((b, ls, dilation * w), BF16),
                   jax.ShapeDtypeStruct((b, ls, dilation * LANES), F32)),
        grid=(b, dilation // n_res, ls // step_rows),
        in_specs=[cur, prev, cur, prev, cur, const(masks.shape), const(indicators.shape)],
        out_specs=(cur, pl.BlockSpec((1, step_rows, n_res * LANES), lambda bi, r, n: (bi, n, r))),
        scratch_shapes=[pltpu.VMEM((n_res, 2 * blk, w), BF16)],
        compiler_params=pltpu.CompilerParams(
            dimension_semantics=("arbitrary", "arbitrary", "arbitrary"),
            vmem_limit_bytes=VMEM_LIMIT_BYTES),
        name=f"attn_d{dilation}",
    )(q, k, k, v, v, masks, indicators)


def _mlstm_kernel(qk_ref, v_ref, og_ref, gcol_ref, grow_ref, wc_ref, bc_ref, gn_ref, tri_ref, hm_ref,
                  o_ref, tail_s, c_s, m_s, *, ml_w):
    L = MLSTM_CHUNK
    nh = N_MLSTM_HEADS
    dh = ml_w // nh

    @pl.when(pl.program_id(1) == 0)
    def _():
        tail_s[...] = jnp.zeros_like(tail_s)
        c_s[...] = jnp.zeros_like(c_s)
        m_s[...] = jnp.zeros_like(m_s)

    tri = tri_ref[...]
    ti = lax.broadcasted_iota(jnp.int32, (L, L), 0)
    si = lax.broadcasted_iota(jnp.int32, (L, L), 1)
    causal = si <= ti
    ones = jnp.ones((L, dh), BF16)
    hp = hm_ref[...]
    for bi in range(qk_ref.shape[0]):
        _mlstm_chunk(bi, qk_ref, v_ref, og_ref, gcol_ref, grow_ref, wc_ref, bc_ref, gn_ref, o_ref,
                     tail_s, c_s, m_s, tri, causal, ones, hp, ml_w=ml_w)


def _mlstm_chunk(bi, qk_ref, v_ref, og_ref, gcol_ref, grow_ref, wc_ref, bc_ref, gn_ref, o_ref,
                 tail_s, c_s, m_s, tri, causal, ones, hp, *, ml_w):
    L = MLSTM_CHUNK
    nh = N_MLSTM_HEADS
    dh = ml_w // nh
    x = qk_ref[bi].astype(F32)
    tail = tail_s[bi]
    row8 = lax.broadcasted_iota(jnp.int32, tail.shape, 0)
    acc = bc_ref[...] + x * wc_ref[CONV_WIDTH - 1:CONV_WIDTH, :]
    for back in range(1, CONV_WIDTH):
        rolled = pltpu.roll(x, back, 0)
        top = jnp.where(row8 >= back, rolled[:MOD_ROWS], pltpu.roll(tail, back, 0))
        shifted = jnp.concatenate([top, rolled[MOD_ROWS:]], axis=0)
        acc = acc + shifted * wc_ref[CONV_WIDTH - 1 - back:CONV_WIDTH - back, :]
    tail_s[bi] = x[L - MOD_ROWS:]
    qk = acc * _sigmoid(acc)

    gcol = gcol_ref[bi]
    grow = grow_ref[bi]
    bcol_all = _dot(tri, gcol, precision=lax.Precision.HIGHEST)
    brow_all = _dot_nt(grow, tri, precision=lax.Precision.HIGHEST)

    ks = [qk[:, ml_w + h * dh:ml_w + (h + 1) * dh] * (dh ** -0.5) for h in range(nh)]

    heads = []
    for h in range(nh):
        st = bi * nh + h
        b_c = jnp.broadcast_to(bcol_all[:, nh + h:nh + h + 1], (L, dh))
        i_c = jnp.broadcast_to(gcol[:, h:h + 1], (L, dh))
        b_r = brow_all[nh + h:nh + h + 1, :]
        i_r = grow[h:h + 1, :]
        m_prev = m_s[st, 0:1, :]
        log_d = jnp.where(causal, b_c[:, 0:1] + (i_r - b_r), NEG_INF)
        m_inter = b_c + m_prev
        m_t = jnp.maximum(m_inter, jnp.max(log_d, axis=-1, keepdims=True))
        d_mat = jnp.exp2(log_d - m_t[:, 0:1])
        inter = jnp.exp2(m_inter - m_t)
        b_last = b_c[L - 1:L, :]
        w_log = b_last - b_c + i_c
        m_new = jnp.maximum(b_last + m_prev, jnp.max(w_log, axis=0, keepdims=True))
        wgt = jnp.exp2(w_log - m_new)
        decay = jnp.exp2(b_last + m_prev - m_new)
        heads.append((m_t, d_mat, inter, m_new, wgt, decay))

    outs = []
    for h, (m_t, d_mat, inter, _, _, _) in enumerate(heads):
        hcols = slice(h * dh, (h + 1) * dh)
        q = qk[:, hcols].astype(BF16)
        v_ext = jnp.concatenate([v_ref[bi, :, hcols], ones], axis=-1)
        c_prev = c_s[bi * nh + h]
        s_qk = _dot_nt(q, ks[h].astype(BF16)) * d_mat
        ext = jnp.concatenate([inter, inter], axis=-1) * _dot(q, c_prev.astype(BF16)) \
            + _dot(s_qk.astype(BF16), v_ext)
        outs.append(ext[:, :dh] / jnp.maximum(jnp.abs(ext[:, dh:]), jnp.exp2(-m_t)))
    hh = jnp.concatenate(outs, axis=-1)
    h2 = (hh * hh).astype(BF16)
    gw = hp.shape[0]
    msq = jnp.concatenate([_dot(h2[:, c0:c0 + gw], hp) for c0 in range(0, ml_w, gw)], axis=-1)
    hn = hh * lax.rsqrt(msq + RMS_EPS) * gn_ref[...]
    o_ref[bi] = (og_ref[bi].astype(F32) * hn).astype(o_ref.dtype)

    for h, (_, _, _, m_new, wgt, decay) in enumerate(heads):
        st = bi * nh + h
        hcols = slice(h * dh, (h + 1) * dh)
        v_ext = jnp.concatenate([v_ref[bi, :, hcols], ones], axis=-1)
        kw = (ks[h] * wgt).astype(BF16)
        c_s[st] = jnp.concatenate([decay, decay], axis=-1) * c_s[st] + _dot_tn(kw, v_ext)
        m_s[st] = jnp.broadcast_to(m_new, m_s.shape[1:])


def _mlstm_call(qkm, vm, og, gcol, grow, w_conv, b_conv, g_norm, tri, hmean):
    b, s, ml_w = vm.shape
    L = MLSTM_CHUNK
    nh = N_MLSTM_HEADS
    dh = ml_w // nh
    bb = MLSTM_BATCH_ROWS
    assert b % bb == 0
    const = lambda shape: pl.BlockSpec(shape, lambda g, i: (0,) * len(shape))
    tok = lambda w: pl.BlockSpec((bb, L, w), lambda g, i: (g, i, 0))
    return pl.pallas_call(
        functools.partial(_mlstm_kernel, ml_w=ml_w),
        out_shape=jax.ShapeDtypeStruct((b, s, ml_w), BF16),
        grid=(b // bb, s // L),
        in_specs=[tok(2 * ml_w), tok(ml_w), tok(ml_w), tok(GATE_LANES),
                  pl.BlockSpec((bb, 2 * nh, L), lambda g, i: (g, 0, i)),
                  const(w_conv.shape), const((1, 2 * ml_w)), const((1, ml_w)),
                  const((L, L)), const(hmean.shape)],
        out_specs=tok(ml_w),
        scratch_shapes=[pltpu.VMEM((bb, MOD_ROWS, 2 * ml_w), F32),
                        pltpu.VMEM((bb * nh, dh, 2 * dh), F32),
                        pltpu.VMEM((bb * nh, MOD_ROWS, dh), F32)],
        compiler_params=pltpu.CompilerParams(dimension_semantics=("arbitrary", "arbitrary"),
                                             vmem_limit_bytes=VMEM_LIMIT_BYTES),
        name="mlstm",
    )(qkm, vm, og, gcol, grow, w_conv, b_conv, g_norm, tri, hmean)


def _outffn_kernel(x_ref, mod_ref, g_ref, *rest, dilations):
    n_lay = len(dilations)
    o_refs, l_refs = rest[:n_lay], rest[n_lay:2 * n_lay]
    (hm_ref, ex_ref, wo_ref, wg_ref, wu_ref, wd_ref, out_ref,
     operm_s, lperm_s) = rest[2 * n_lay:]
    tm = x_ref.shape[1]

    def natural(ref, d, scratch):
        if d == 1:
            return ref[0].astype(F32)
        n_cg = scratch.shape[1]
        chain = [step for step in _gather_chain(dilations) if step[1] <= d]
        for level, (parent, dd, f) in enumerate(reversed(chain)):
            dst = scratch.at[level % 2]
            src = scratch.at[(level + 1) % 2]
            n = tm // dd
            for c in range(parent):
                for a in range(f):
                    r = a * parent + c
                    for cg in range(n_cg):
                        if level == 0:
                            c0 = (r * n_cg + cg) * LANES
                            rows = ref[0, :, c0:c0 + LANES].astype(F32)
                        else:
                            rows = src[cg, r * n:(r + 1) * n, :]
                        dst[cg, pl.ds(c * (tm // parent) + a, n, stride=f), :] = rows
        final = scratch.at[(len(chain) - 1) % 2]
        return jnp.concatenate([final[cg] for cg in range(n_cg)], axis=-1)

    lses = [natural(ref, d, lperm_s) for ref, d in zip(l_refs, dilations)]
    mx = functools.reduce(jnp.maximum, lses)
    es = [jnp.exp(l - mx) for l in lses]
    inv = 1.0 / functools.reduce(jnp.add, es)
    ex = ex_ref[...]
    attn = None
    for e, ref, d in zip(es, o_refs, dilations):
        term = _dot((e * inv).astype(BF16), ex) * natural(ref, d, operm_s)
        attn = term if attn is None else attn + term
    aw = attn.shape[-1]
    y = _dot(attn.astype(BF16), wo_ref[:aw, :]) + _dot(hm_ref[0], wo_ref[aw:, :])
    x1 = x_ref[0] + mod_ref[0, 2:3, :] * y
    ms = jnp.mean(x1 * x1, axis=-1, keepdims=True)
    hn = x1 * lax.rsqrt(ms + RMS_EPS) * g_ref[...]
    hb = (hn * (1.0 + mod_ref[0, 4:5, :]) + mod_ref[0, 3:4, :]).astype(BF16)
    g = _dot(hb, wg_ref[...])
    u = _dot(hb, wu_ref[...])
    a = (g * _sigmoid(g) * u).astype(BF16)
    out_ref[0] = x1 + mod_ref[0, 5:6, :] * _dot(a, wd_ref[...])


def _outffn_call(x, mod, g_ffn, os_, lses, hm, expand, wo, wg, wu, wd, *, dilations):
    b, s, d = x.shape
    tm = FFN_ROWS
    assert all(tm % (dl * BF16_SUBLANES) == 0 for dl in dilations)
    aw = expand.shape[1]
    const = lambda shape: pl.BlockSpec(shape, lambda bi, i: (0,) * len(shape))
    tok = lambda w: pl.BlockSpec((1, tm, w), lambda bi, i: (bi, i, 0))
    strided = lambda w: tuple(pl.BlockSpec((1, tm // dl, dl * w), lambda bi, i: (bi, i, 0)) for dl in dilations)
    return pl.pallas_call(
        functools.partial(_outffn_kernel, dilations=dilations),
        out_shape=jax.ShapeDtypeStruct((b, s, d), F32),
        grid=(b, s // tm),
        in_specs=[tok(d), pl.BlockSpec((1, N_MOD, d), lambda bi, i: (bi, 0, 0)), const((1, d)),
                  *strided(aw), *strided(LANES), tok(hm.shape[-1]),
                  const(expand.shape), const(wo.shape),
                  const(wg.shape), const(wu.shape), const(wd.shape)],
        out_specs=tok(d),
        scratch_shapes=[pltpu.VMEM((2, aw // LANES, tm, LANES), F32), pltpu.VMEM((2, 1, tm, LANES), F32)],
        compiler_params=pltpu.CompilerParams(dimension_semantics=("arbitrary", "arbitrary"),
                                             vmem_limit_bytes=VMEM_LIMIT_BYTES),
        name="outffn",
    )(x, mod, g_ffn, *os_, *lses, hm, expand, wo, wg, wu, wd)


def _block_diag_mean(width, group):
    idx = jnp.arange(width) // group
    return jnp.where(idx[:, None] == idx[None, :], 1.0 / group, 0.0).astype(BF16)


def kernel(x, c, g_mix, w_in, w_conv, b_conv, b_igate, b_fgate, q_norm_g, k_norm_g, mlstm_norm_g, w_out,
           g_ffn, w_gate, w_up, w_down, w_ada, b_ada):
    b, s, d = x.shape
    depth = g_mix.shape[0]
    attn_w = d // 2
    ml_w = d - attn_w
    nh = N_MLSTM_HEADS
    n_attn_heads = attn_w // ATTN_HEAD_DIM
    scale = ATTN_HEAD_DIM ** -0.5 * LOG2E
    dilations = tuple(dl for _, dl in DILATED_PATTERNS)
    assert b <= MOD_ROWS and 2 * nh <= GATE_LANES

    c_t = jnp.zeros((d, MOD_ROWS), F32).at[:, :b].set(c.T)
    assert attn_w % MXU_WIDTH == 0 and MXU_WIDTH % ATTN_HEAD_DIM == 0
    hmean_attn = _block_diag_mean(MXU_WIDTH, ATTN_HEAD_DIM)
    assert ml_w % MXU_WIDTH == 0 and MXU_WIDTH % (ml_w // nh) == 0
    hmean_ml = _block_diag_mean(MXU_WIDTH, ml_w // nh)
    stat_lane = jnp.arange(LANES)
    head_of_stat = jnp.where(stat_lane % ATTN_HEAD_DIM < n_attn_heads // 2,
                             2 * (stat_lane % ATTN_HEAD_DIM) + stat_lane // ATTN_HEAD_DIM, -1)
    head_of_col = jnp.arange(attn_w) // ATTN_HEAD_DIM
    expand = (head_of_stat[:, None] == head_of_col[None, :]).astype(BF16)
    tri = jnp.tril(jnp.ones((MLSTM_CHUNK, MLSTM_CHUNK), F32))

    for l in range(depth):
        mod = _mod_call(c_t, w_ada, b_ada[l][None, :], layer=l, n_rows=b)[:b].reshape(b, N_MOD, d)

        n_main = 3 * attn_w + 4 * ml_w
        assert n_main % MXU_WIDTH == 0
        wg_cols = w_in[l][:, n_main:]
        wgt = jnp.zeros((BF16_SUBLANES, d), F32).at[:2 * nh].set(wg_cols.T).astype(BF16)
        brow = jnp.concatenate([b_igate[l], b_fgate[l]])[:, None]
        gq = jnp.tile(q_norm_g[l], n_attn_heads)[None, :]
        gk = jnp.tile(k_norm_g[l], n_attn_heads)[None, :]

        qs, ks, vs, qkm, vm, og, gcol, grow = _inproj_call(
            x, mod, g_mix[l][None, :], w_in, wgt, gq, gk, hmean_attn, brow,
            layer=l, attn_w=attn_w, ml_w=ml_w, scale=scale, dilations=dilations)

        os_, lses = [], []
        for (window, dilation), q, k, v in zip(DILATED_PATTERNS, qs, ks, vs):
            o, lse = _attn_call(q, k, v, width=attn_w, window=window, dilation=dilation)
            os_.append(o)
            lses.append(lse)

        hm = _mlstm_call(qkm, vm, og, gcol, grow, w_conv[l], b_conv[l][None, :],
                         mlstm_norm_g[l][None, :], tri, hmean_ml)

        x = _outffn_call(x, mod, g_ffn[l][None, :], os_, lses, hm, expand,
                         w_out[l].astype(BF16), w_gate[l].astype(BF16), w_up[l].astype(BF16), w_down[l].astype(BF16),
                         dilations=dilations)
    return x
```

```python
import functools

import jax
import jax.numpy as jnp
from jax import lax
from jax.experimental import pallas as pl
from jax.experimental.pallas import tpu as pltpu

F32 = jnp.float32
BF16 = jnp.bfloat16

ATTN_HEAD_DIM = 64
N_MLSTM_HEADS = 4
CONV_WIDTH = 4
DILATED_PATTERNS = ((128, 1), (512, 4), (2048, 16))
ATTN_BLOCK = 128
N_MOD = 6
RMS_EPS = 1e-6

LANES = 128
MXU_WIDTH = 256
BF16_SUBLANES = 16
VMEM_LIMIT_BYTES = 56 * 1024 * 1024
F32_SUBLANES = 8
MOD_ROWS = F32_SUBLANES
MOD_COL_STEPS = 4
LAYOUT_ROWS = 512
GATE_LANES = 128
INPROJ_ROWS = LAYOUT_ROWS
ATTN_MAX_SUB_BLOCKS = 16
MLSTM_CHUNK = 256
MLSTM_BATCH_ROWS = 2
FFN_ROWS = LAYOUT_ROWS

NEG_INF = float("-inf")
LOG2E = 1.4426950408889634
LN2 = 0.6931471805599453


def _dot(a, b, **kw):
    return jnp.dot(a, b, preferred_element_type=F32, **kw)


def _dot_nt(a, b, **kw):
    return lax.dot_general(a, b, (((1,), (1,)), ((), ())), preferred_element_type=F32, **kw)


def _dot_tn(a, b):
    return lax.dot_general(a, b, (((0,), (0,)), ((), ())), preferred_element_type=F32)


def _sigmoid(z):
    return 1.0 / (1.0 + jnp.exp(-z))


def _log_sigmoid(z):
    return jnp.minimum(z, 0.0) - jnp.log1p(jnp.exp(-jnp.abs(z)))


def _strided_shape(b, s, w, dilation):
    return (b, s // dilation, dilation * w)


def _gather_chain(dilations):
    chain, parent = [], 1
    for d in sorted(set(dilations)):
        if d == 1:
            continue
        assert d % parent == 0 and d // parent < F32_SUBLANES
        chain.append((parent, d, d // parent))
        parent = d
    return chain


def _mod_kernel(ct_ref, w_ref, b_ref, o_ref, *, n_rows):
    ct = ct_ref[...]
    sc = ct * _sigmoid(ct)
    w = w_ref[...]
    rows = [jnp.sum(w * sc[:, r:r + 1], axis=0, keepdims=True) for r in range(n_rows)]
    rows += [jnp.zeros_like(rows[0])] * (MOD_ROWS - n_rows)
    o_ref[...] = jnp.concatenate(rows, axis=0) + b_ref[...]


def _mod_call(c_t, w_ada, b_ada, *, layer, n_rows):
    _, d, n = w_ada.shape
    assert n % (MOD_COL_STEPS * LANES) == 0
    tn = n // MOD_COL_STEPS
    return pl.pallas_call(
        functools.partial(_mod_kernel, n_rows=n_rows),
        out_shape=jax.ShapeDtypeStruct((MOD_ROWS, n), F32),
        grid=(n // tn,),
        in_specs=[pl.BlockSpec((d, MOD_ROWS), lambda j: (0, 0)),
                  pl.BlockSpec((None, d, tn), lambda j: (layer, 0, j)),
                  pl.BlockSpec((1, tn), lambda j: (0, j))],
        out_specs=pl.BlockSpec((MOD_ROWS, tn), lambda j: (0, j)),
        compiler_params=pltpu.CompilerParams(dimension_semantics=("arbitrary",),
                                             vmem_limit_bytes=VMEM_LIMIT_BYTES),
        name="mod",
    )(c_t, w_ada, b_ada)


def _inproj_kernel(x_ref, mod_ref, g_ref, w_ref, wgt_ref, gq_ref, gk_ref, hp_ref,
                   brow_ref, *rest, attn_w, ml_w, scale, dilations):
    n_lay = len(dilations)
    q_refs, k_refs, v_refs = rest[:n_lay], rest[n_lay:2 * n_lay], rest[2 * n_lay:3 * n_lay]
    qkm_ref, vm_ref, og_ref, gcol_ref, grow_ref, perm_s, wb_s = rest[3 * n_lay:]
    tm = x_ref.shape[1]

    @pl.when((pl.program_id(0) == 0) & (pl.program_id(1) == 0))
    def _():
        for c0 in range(0, wb_s.shape[1], MXU_WIDTH):
            wb_s[:, c0:c0 + MXU_WIDTH] = w_ref[:, c0:c0 + MXU_WIDTH].astype(BF16)

    def emit(val, refs):
        w = val.shape[-1]
        n_cg = w // LANES
        ref_of = dict(zip(dilations, refs))
        if 1 in ref_of:
            ref_of[1][0] = val.astype(BF16)
        chain = _gather_chain(dilations)
        for cg in range(n_cg):
            perm_s[0, cg] = val[:, cg * LANES:(cg + 1) * LANES]
        for level, (parent, d, f) in enumerate(chain):
            src, dst = perm_s.at[level % 2], perm_s.at[(level + 1) % 2]
            n = tm // d
            for c in range(parent):
                for a in range(f):
                    r = a * parent + c
                    for cg in range(n_cg):
                        rows = src[cg, pl.ds(c * (tm // parent) + a, n, stride=f), :]
                        c0 = r * w + cg * LANES
                        ref_of[d][0, :, c0:c0 + LANES] = rows.astype(BF16)
                        if level + 1 < len(chain):
                            dst[cg, r * n:(r + 1) * n, :] = rows

    x = x_ref[0]
    ms = jnp.mean(x * x, axis=-1, keepdims=True)
    y = x * lax.rsqrt(ms + RMS_EPS) * g_ref[...]
    h = (y * (1.0 + mod_ref[0, 1:2, :]) + mod_ref[0, 0:1, :]).astype(BF16)

    n_attn = 3 * attn_w
    xa = _dot(h, wb_s[:, :n_attn])
    hp = hp_ref[...]

    def head_norm(t, g):
        t2 = (t * t).astype(BF16)
        gw = hp.shape[0]
        msq = jnp.concatenate([_dot(t2[:, c0:c0 + gw], hp) for c0 in range(0, t.shape[-1], gw)], axis=-1)
        return t * lax.rsqrt(msq + RMS_EPS) * g

    emit(head_norm(xa[:, :attn_w], gq_ref[...]) * scale, q_refs)
    emit(head_norm(xa[:, attn_w:2 * attn_w], gk_ref[...]), k_refs)
    emit(xa[:, 2 * attn_w:], v_refs)

    xm = _dot(h, wb_s[:, n_attn:n_attn + 4 * ml_w])
    qkm_ref[0] = xm[:, :2 * ml_w].astype(BF16)
    vm_ref[0] = xm[:, 2 * ml_w:3 * ml_w].astype(BF16)
    og_ref[0] = _sigmoid(xm[:, 3 * ml_w:]).astype(BF16)

    nh = N_MLSTM_HEADS
    zr = _dot_nt(wgt_ref[...], h)[:2 * nh] + brow_ref[...]
    row = lax.broadcasted_iota(jnp.int32, zr.shape, 0)
    gates = jnp.where(row < nh, zr, _log_sigmoid(zr)) * LOG2E
    grow_ref[0] = gates
    padded = jnp.concatenate([gates, jnp.zeros((GATE_LANES - 2 * nh, tm), F32)], axis=0)
    gcol_ref[0] = padded.T


def _inproj_call(x, mod, g_mix, w, wgt, gq, gk, hp, brow, *, layer, attn_w, ml_w, scale, dilations):
    b, s, d = x.shape
    tm = INPROJ_ROWS
    nh2 = 2 * N_MLSTM_HEADS
    assert all(tm % (dl * BF16_SUBLANES) == 0 for dl in dilations)
    const = lambda shape: pl.BlockSpec(shape, lambda bi, i: (0,) * len(shape))
    tok = lambda w: pl.BlockSpec((1, tm, w), lambda bi, i: (bi, i, 0))
    strided_shapes = tuple(jax.ShapeDtypeStruct(_strided_shape(b, s, attn_w, dl), BF16) for dl in dilations)
    strided_specs = tuple(pl.BlockSpec((1, tm // dl, dl * attn_w), lambda bi, i: (bi, i, 0)) for dl in dilations)
    out_shape = strided_shapes * 3 + (
        jax.ShapeDtypeStruct((b, s, 2 * ml_w), BF16),
        jax.ShapeDtypeStruct((b, s, ml_w), BF16),
        jax.ShapeDtypeStruct((b, s, ml_w), BF16),
        jax.ShapeDtypeStruct((b, s, GATE_LANES), F32),
        jax.ShapeDtypeStruct((b, nh2, s), F32))
    outs = pl.pallas_call(
        functools.partial(_inproj_kernel, attn_w=attn_w, ml_w=ml_w, scale=scale, dilations=dilations),
        out_shape=out_shape,
        grid=(b, s // tm),
        in_specs=[tok(d),
                  pl.BlockSpec((1, N_MOD, d), lambda bi, i: (bi, 0, 0)),
                  const((1, d)),
                  pl.BlockSpec((None,) + w.shape[1:], lambda bi, i: (layer, 0, 0),
                               pipeline_mode=pl.Buffered(1)),
                  const(wgt.shape),
                  const((1, attn_w)), const((1, attn_w)), const(hp.shape),
                  const((nh2, 1))],
        out_specs=strided_specs * 3 + (tok(2 * ml_w), tok(ml_w), tok(ml_w), tok(GATE_LANES),
                                       pl.BlockSpec((1, nh2, tm), lambda bi, i: (bi, 0, i))),
        scratch_shapes=[pltpu.VMEM((2, attn_w // LANES, tm, LANES), F32),
                        pltpu.VMEM((d, 3 * attn_w + 4 * ml_w), BF16)],
        compiler_params=pltpu.CompilerParams(dimension_semantics=("arbitrary", "arbitrary"),
                                             vmem_limit_bytes=VMEM_LIMIT_BYTES),
        name="inproj",
    )(x, mod, g_mix, w, wgt, gq, gk, hp, brow)
    n_lay = len(dilations)
    return (outs[:n_lay], outs[n_lay:2 * n_lay], outs[2 * n_lay:3 * n_lay]) + tuple(outs[3 * n_lay:])


def _attn_kernel(q_ref, kp_ref, kc_ref, vp_ref, vc_ref, bias_ref, ind_ref, o_ref, lse_ref, k0_s, *, n_heads):
    blk = ATTN_BLOCK
    sub_blocks = q_ref.shape[1] // blk
    hd = ATTN_HEAD_DIM
    w = n_heads * hd
    n_res = q_ref.shape[2] // w
    n_pairs = n_heads // 2
    n = pl.program_id(2)
    lane = lax.broadcasted_iota(jnp.int32, (blk, LANES), 1)
    first_head = lane < hd
    for res in range(n_res):
        k0_s[res, 0:blk, :] = kp_ref[0, :, res * w:(res + 1) * w]
        k0_s[res, blk:, :] = kc_ref[0, 0:blk, res * w:(res + 1) * w]

    def value_planes(res, kb, j):
        cols = slice(res * w + j * LANES, res * w + (j + 1) * LANES)
        src = vp_ref[0, :, cols] if kb == 0 else vc_ref[0, (kb - 1) * blk:kb * blk, cols]
        return [jnp.where(first_head, src, 0), jnp.where(first_head, 0, src)]

    units = [(res, i, j) for res in range(n_res) for i in range(sub_blocks) for j in range(n_pairs)]
    scores, maxes, probs = [], [], []
    for res, i, j in units:
        cols = slice(j * LANES, (j + 1) * LANES)
        q = q_ref[0, i * blk:(i + 1) * blk, res * w + j * LANES:res * w + (j + 1) * LANES]
        if i == 0:
            ks = k0_s[res, :, cols]
            bias = bias_ref[jnp.where(n == 0, 1, 0)]
        else:
            ks = kc_ref[0, (i - 1) * blk:(i + 1) * blk, res * w + j * LANES:res * w + (j + 1) * LANES]
            bias = bias_ref[0]
        q2 = jnp.concatenate([jnp.where(first_head, q, 0), jnp.where(first_head, 0, q)], axis=0)
        s = _dot_nt(q2, ks) + bias
        scores.append(s)
        maxes.append(jnp.max(s, axis=-1, keepdims=True))
    for s, m in zip(scores, maxes):
        p = jnp.exp2(s - m).astype(BF16)
        probs.append(jnp.concatenate([p[:blk, :blk], p[blk:, :blk], p[:blk, blk:], p[blk:, blk:]], axis=1))
    for res in range(n_res):
        for i in range(sub_blocks):
            outs = []
            stats = jnp.zeros((blk, LANES), F32)
            for j in range(n_pairs):
                u = (res * sub_blocks + i) * n_pairs + j
                planes = jnp.concatenate(value_planes(res, i, j) + value_planes(res, i + 1, j), axis=0)
                vw = jnp.concatenate([planes, ind_ref[...]], axis=1)
                ov = _dot(probs[u], vw)
                outs.append(ov[:, :LANES] / ov[:, LANES:])
                m = jnp.broadcast_to(maxes[u], (2 * blk, LANES))
                lse = jnp.where(first_head, m[:blk], m[blk:]) * LN2 + jnp.log(ov[:, LANES:])
                stats = jnp.where((lane & (hd - 1)) == j, lse, stats)
            rows = slice(i * blk, (i + 1) * blk)
            o_ref[0, rows, res * w:(res + 1) * w] = jnp.concatenate(outs, axis=-1).astype(o_ref.dtype)
            lse_ref[0, rows, res * LANES:(res + 1) * LANES] = stats


def _attn_band_masks(n_back):
    blk = ATTN_BLOCK
    row = jnp.arange(2 * blk)[:, None] % blk
    col = jnp.arange(2 * blk)[None, :]
    band = (col >= row + (blk - n_back)) & (col <= row + blk)
    return jnp.stack([jnp.where(band, 0.0, NEG_INF), jnp.where(band & (col >= blk), 0.0, NEG_INF)]).astype(F32)


def _attn_indicators():
    first = (jnp.arange(LANES) < ATTN_HEAD_DIM)[None, :]
    plane = jnp.concatenate([jnp.broadcast_to(first, (ATTN_BLOCK, LANES)),
                             jnp.broadcast_to(~first, (ATTN_BLOCK, LANES))], axis=0)
    return jnp.concatenate([plane, plane], axis=0).astype(BF16)


def _attn_call(q, k, v, *, width, window, dilation):
    b, ls, _ = q.shape
    w = width
    n_back = window // dilation
    blk = ATTN_BLOCK
    assert n_back <= blk
    sub_blocks = min(ATTN_MAX_SUB_BLOCKS, ls // blk)
    step_rows = sub_blocks * blk
    assert ls % step_rows == 0
    n_heads = w // ATTN_HEAD_DIM
    assert n_heads <= LANES and n_heads % 2 == 0 and 2 * ATTN_HEAD_DIM == LANES
    n_res = max(1, min(dilation, ATTN_MAX_SUB_BLOCKS // sub_blocks))
    assert dilation % n_res == 0
    cur = pl.BlockSpec((1, step_rows, n_res * w), lambda bi, r, n: (bi, n, r))
    prev = pl.BlockSpec((1, blk, n_res * w),
                        lambda bi, r, n: (bi, jnp.maximum(n * sub_blocks - 1, 0), r))
    masks = _attn_band_masks(n_back)
    indicators = _attn_indicators()
    const = lambda shape: pl.BlockSpec(shape, lambda bi, r, n: (0,) * len(shape))
    return pl.pallas_call(
        functools.partial(_attn_kernel, n_heads=n_heads),
        out_shape=(jax.ShapeDtypeStruct((b, ls, dilation * w), BF16),
                   jax.ShapeDtypeStruct((b, ls, dilation * LANES), F32)),
        grid=(b, dilation // n_res, ls // step_rows),
        in_specs=[cur, prev, cur, prev, cur, const(masks.shape), const(indicators.shape)],
        out_specs=(cur, pl.BlockSpec((1, step_rows, n_res * LANES), lambda bi, r, n: (bi, n, r))),
        scratch_shapes=[pltpu.VMEM((n_res, 2 * blk, w), BF16)],
        compiler_params=pltpu.CompilerParams(
            dimension_semantics=("arbitrary", "arbitrary", "arbitrary"),
            vmem_limit_bytes=VMEM_LIMIT_BYTES),
        name=f"attn_d{dilation}",
    )(q, k, k, v, v, masks, indicators)


def _mlstm_kernel(qk_ref, v_ref, og_ref, gcol_ref, grow_ref, wc_ref, bc_ref, gn_ref, tri_ref, hm_ref,
                  o_ref, tail_s, c_s, m_s, *, ml_w):
    L = MLSTM_CHUNK
    nh = N_MLSTM_HEADS
    dh = ml_w // nh

    @pl.when(pl.program_id(1) == 0)
    def _():
        tail_s[...] = jnp.zeros_like(tail_s)
        c_s[...] = jnp.zeros_like(c_s)
        m_s[...] = jnp.zeros_like(m_s)

    tri = tri_ref[...]
    ti = lax.broadcasted_iota(jnp.int32, (L, L), 0)
    si = lax.broadcasted_iota(jnp.int32, (L, L), 1)
    causal = si <= ti
    ones = jnp.ones((L, dh), BF16)
    hp = hm_ref[...]
    for bi in range(qk_ref.shape[0]):
        _mlstm_chunk(bi, qk_ref, v_ref, og_ref, gcol_ref, grow_ref, wc_ref, bc_ref, gn_ref, o_ref,
                     tail_s, c_s, m_s, tri, causal, ones, hp, ml_w=ml_w)


def _mlstm_chunk(bi, qk_ref, v_ref, og_ref, gcol_ref, grow_ref, wc_ref, bc_ref, gn_ref, o_ref,
                 tail_s, c_s, m_s, tri, causal, ones, hp, *, ml_w):
    L = MLSTM_CHUNK
    nh = N_MLSTM_HEADS
    dh = ml_w // nh
    x = qk_ref[bi].astype(F32)
    tail = tail_s[bi]
    row8 = lax.broadcasted_iota(jnp.int32, tail.shape, 0)
    acc = bc_ref[...] + x * wc_ref[CONV_WIDTH - 1:CONV_WIDTH, :]
    for back in range(1, CONV_WIDTH):
        rolled = pltpu.roll(x, back, 0)
        top = jnp.where(row8 >= back, rolled[:MOD_ROWS], pltpu.roll(tail, back, 0))
        shifted = jnp.concatenate([top, rolled[MOD_ROWS:]], axis=0)
        acc = acc + shifted * wc_ref[CONV_WIDTH - 1 - back:CONV_WIDTH - back, :]
    tail_s[bi] = x[L - MOD_ROWS:]
    qk = acc * _sigmoid(acc)

    gcol = gcol_ref[bi]
    grow = grow_ref[bi]
    bcol_all = _dot(tri, gcol, precision=lax.Precision.HIGHEST)
    brow_all = _dot_nt(grow, tri, precision=lax.Precision.HIGHEST)

    ks = [qk[:, ml_w + h * dh:ml_w + (h + 1) * dh] * (dh ** -0.5) for h in range(nh)]

    heads = []
    for h in range(nh):
        st = bi * nh + h
        b_c = jnp.broadcast_to(bcol_all[:, nh + h:nh + h + 1], (L, dh))
        i_c = jnp.broadcast_to(gcol[:, h:h + 1], (L, dh))
        b_r = brow_all[nh + h:nh + h + 1, :]
        i_r = grow[h:h + 1, :]
        m_prev = m_s[st, 0:1, :]
        log_d = jnp.where(causal, b_c[:, 0:1] + (i_r - b_r), NEG_INF)
        m_inter = b_c + m_prev
        m_t = jnp.maximum(m_inter, jnp.max(log_d, axis=-1, keepdims=True))
        d_mat = jnp.exp2(log_d - m_t[:, 0:1])
        inter = jnp.exp2(m_inter - m_t)
        b_last = b_c[L - 1:L, :]
        w_log = b_last - b_c + i_c
        m_new = jnp.maximum(b_last + m_prev, jnp.max(w_log, axis=0, keepdims=True))
        wgt = jnp.exp2(w_log - m_new)
        decay = jnp.exp2(b_last + m_prev - m_new)
        heads.append((m_t, d_mat, inter, m_new, wgt, decay))

    outs = []
    for h, (m_t, d_mat, inter, _, _, _) in enumerate(heads):
        hcols = slice(h * dh, (h + 1) * dh)
        q = qk[:, hcols].astype(BF16)
        v_ext = jnp.concatenate([v_ref[bi, :, hcols], ones], axis=-1)
        c_prev = c_s[bi * nh + h]
        s_qk = _dot_nt(q, ks[h].astype(BF16)) * d_mat
        ext = jnp.concatenate([inter, inter], axis=-1) * _dot(q, c_prev.astype(BF16)) \
            + _dot(s_qk.astype(BF16), v_ext)
        outs.append(ext[:, :dh] / jnp.maximum(jnp.abs(ext[:, dh:]), jnp.exp2(-m_t)))
    hh = jnp.concatenate(outs, axis=-1)
    h2 = (hh * hh).astype(BF16)
    gw = hp.shape[0]
    msq = jnp.concatenate([_dot(h2[:, c0:c0 + gw], hp) for c0 in range(0, ml_w, gw)], axis=-1)
    hn = hh * lax.rsqrt(msq + RMS_EPS) * gn_ref[...]
    o_ref[bi] = (og_ref[bi].astype(F32) * hn).astype(o_ref.dtype)

    for h, (_, _, _, m_new, wgt, decay) in enumerate(heads):
        st = bi * nh + h
        hcols = slice(h * dh, (h + 1) * dh)
        v_ext = jnp.concatenate([v_ref[bi, :, hcols], ones], axis=-1)
        kw = (ks[h] * wgt).astype(BF16)
        c_s[st] = jnp.concatenate([decay, decay], axis=-1) * c_s[st] + _dot_tn(kw, v_ext)
        m_s[st] = jnp.broadcast_to(m_new, m_s.shape[1:])


def _mlstm_call(qkm, vm, og, gcol, grow, w_conv, b_conv, g_norm, tri, hmean):
    b, s, ml_w = vm.shape
    L = MLSTM_CHUNK
    nh = N_MLSTM_HEADS
    dh = ml_w // nh
    bb = MLSTM_BATCH_ROWS
    assert b % bb == 0
    const = lambda shape: pl.BlockSpec(shape, lambda g, i: (0,) * len(shape))
    tok = lambda w: pl.BlockSpec((bb, L, w), lambda g, i: (g, i, 0))
    return pl.pallas_call(
        functools.partial(_mlstm_kernel, ml_w=ml_w),
        out_shape=jax.ShapeDtypeStruct((b, s, ml_w), BF16),
        grid=(b // bb, s // L),
        in_specs=[tok(2 * ml_w), tok(ml_w), tok(ml_w), tok(GATE_LANES),
                  pl.BlockSpec((bb, 2 * nh, L), lambda g, i: (g, 0, i)),
                  const(w_conv.shape), const((1, 2 * ml_w)), const((1, ml_w)),
                  const((L, L)), const(hmean.shape)],
        out_specs=tok(ml_w),
        scratch_shapes=[pltpu.VMEM((bb, MOD_ROWS, 2 * ml_w), F32),
                        pltpu.VMEM((bb * nh, dh, 2 * dh), F32),
                        pltpu.VMEM((bb * nh, MOD_ROWS, dh), F32)],
        compiler_params=pltpu.CompilerParams(dimension_semantics=("arbitrary", "arbitrary"),
                                             vmem_limit_bytes=VMEM_LIMIT_BYTES),
        name="mlstm",
    )(qkm, vm, og, gcol, grow, w_conv, b_conv, g_norm, tri, hmean)


def _outffn_kernel(x_ref, mod_ref, g_ref, *rest, dilations):
    n_lay = len(dilations)
    o_refs, l_refs = rest[:n_lay], rest[n_lay:2 * n_lay]
    (hm_ref, ex_ref, wo_ref, wg_ref, wu_ref, wd_ref, out_ref,
     operm_s, lperm_s) = rest[2 * n_lay:]
    tm = x_ref.shape[1]

    def natural(ref, d, scratch):
        if d == 1:
            return ref[0].astype(F32)
        n_cg = scratch.shape[1]
        chain = [step for step in _gather_chain(dilations) if step[1] <= d]
        for level, (parent, dd, f) in enumerate(reversed(chain)):
            dst = scratch.at[level % 2]
            src = scratch.at[(level + 1) % 2]
            n = tm // dd
            for c in range(parent):
                for a in range(f):
                    r = a * parent + c
                    for cg in range(n_cg):
                        if level == 0:
                            c0 = (r * n_cg + cg) * LANES
                            rows = ref[0, :, c0:c0 + LANES].astype(F32)
                        else:
                            rows = src[cg, r * n:(r + 1) * n, :]
                        dst[cg, pl.ds(c * (tm // parent) + a, n, stride=f), :] = rows
        final = scratch.at[(len(chain) - 1) % 2]
        return jnp.concatenate([final[cg] for cg in range(n_cg)], axis=-1)

    lses = [natural(ref, d, lperm_s) for ref, d in zip(l_refs, dilations)]
    mx = functools.reduce(jnp.maximum, lses)
    es = [jnp.exp(l - mx) for l in lses]
    inv = 1.0 / functools.reduce(jnp.add, es)
    ex = ex_ref[...]
    attn = None
    for e, ref, d in zip(es, o_refs, dilations):
        term = _dot((e * inv).astype(BF16), ex) * natural(ref, d, operm_s)
        attn = term if attn is None else attn + term
    aw = attn.shape[-1]
    y = _dot(attn.astype(BF16), wo_ref[:aw, :]) + _dot(hm_ref[0], wo_ref[aw:, :])
    x1 = x_ref[0] + mod_ref[0, 2:3, :] * y
    ms = jnp.mean(x1 * x1, axis=-1, keepdims=True)
    hn = x1 * lax.rsqrt(ms + RMS_EPS) * g_ref[...]
    hb = (hn * (1.0 + mod_ref[0, 4:5, :]) + mod_ref[0, 3:4, :]).astype(BF16)
    g = _dot(hb, wg_ref[...])
    u = _dot(hb, wu_ref[...])
    a = (g * _sigmoid(g) * u).astype(BF16)
    out_ref[0] = x1 + mod_ref[0, 5:6, :] * _dot(a, wd_ref[...])


def _outffn_call(x, mod, g_ffn, os_, lses, hm, expand, wo, wg, wu, wd, *, dilations):
    b, s, d = x.shape
    tm = FFN_ROWS
    assert all(tm % (dl * BF16_SUBLANES) == 0 for dl in dilations)
    aw = expand.shape[1]
    const = lambda shape: pl.BlockSpec(shape, lambda bi, i: (0,) * len(shape))
    tok = lambda w: pl.BlockSpec((1, tm, w), lambda bi, i: (bi, i, 0))
    strided = lambda w: tuple(pl.BlockSpec((1, tm // dl, dl * w), lambda bi, i: (bi, i, 0)) for dl in dilations)
    return pl.pallas_call(
        functools.partial(_outffn_kernel, dilations=dilations),
        out_shape=jax.ShapeDtypeStruct((b, s, d), F32),
        grid=(b, s // tm),
        in_specs=[tok(d), pl.BlockSpec((1, N_MOD, d), lambda bi, i: (bi, 0, 0)), const((1, d)),
                  *strided(aw), *strided(LANES), tok(hm.shape[-1]),
                  const(expand.shape), const(wo.shape),
                  const(wg.shape), const(wu.shape), const(wd.shape)],
        out_specs=tok(d),
        scratch_shapes=[pltpu.VMEM((2, aw // LANES, tm, LANES), F32), pltpu.VMEM((2, 1, tm, LANES), F32)],
        compiler_params=pltpu.CompilerParams(dimension_semantics=("arbitrary", "arbitrary"),
                                             vmem_limit_bytes=VMEM_LIMIT_BYTES),
        name="outffn",
    )(x, mod, g_ffn, *os_, *lses, hm, expand, wo, wg, wu, wd)


def _block_diag_mean(width, group):
    idx = jnp.arange(width) // group
    return jnp.where(idx[:, None] == idx[None, :], 1.0 / group, 0.0).astype(BF16)


def kernel(x, c, g_mix, w_in, w_conv, b_conv, b_igate, b_fgate, q_norm_g, k_norm_g, mlstm_norm_g, w_out,
           g_ffn, w_gate, w_up, w_down, w_ada, b_ada):
    b, s, d = x.shape
    depth = g_mix.shape[0]
    attn_w = d // 2
    ml_w = d - attn_w
    nh = N_MLSTM_HEADS
    n_attn_heads = attn_w // ATTN_HEAD_DIM
    scale = ATTN_HEAD_DIM ** -0.5 * LOG2E
    dilations = tuple(dl for _, dl in DILATED_PATTERNS)
    assert b <= MOD_ROWS and 2 * nh <= GATE_LANES

    c_t = jnp.zeros((d, MOD_ROWS), F32).at[:, :b].set(c.T)
    assert attn_w % MXU_WIDTH == 0 and MXU_WIDTH % ATTN_HEAD_DIM == 0
    hmean_attn = _block_diag_mean(MXU_WIDTH, ATTN_HEAD_DIM)
    assert ml_w % MXU_WIDTH == 0 and MXU_WIDTH % (ml_w // nh) == 0
    hmean_ml = _block_diag_mean(MXU_WIDTH, ml_w // nh)
    stat_lane = jnp.arange(LANES)
    head_of_stat = jnp.where(stat_lane % ATTN_HEAD_DIM < n_attn_heads // 2,
                             2 * (stat_lane % ATTN_HEAD_DIM) + stat_lane // ATTN_HEAD_DIM, -1)
    head_of_col = jnp.arange(attn_w) // ATTN_HEAD_DIM
    expand = (head_of_stat[:, None] == head_of_col[None, :]).astype(BF16)
    tri = jnp.tril(jnp.ones((MLSTM_CHUNK, MLSTM_CHUNK), F32))

    for l in range(depth):
        mod = _mod_call(c_t, w_ada, b_ada[l][None, :], layer=l, n_rows=b)[:b].reshape(b, N_MOD, d)

        n_main = 3 * attn_w + 4 * ml_w
        assert n_main % MXU_WIDTH == 0
        wg_cols = w_in[l][:, n_main:]
        wgt = jnp.zeros((BF16_SUBLANES, d), F32).at[:2 * nh].set(wg_cols.T).astype(BF16)
        brow = jnp.concatenate([b_igate[l], b_fgate[l]])[:, None]
        gq = jnp.tile(q_norm_g[l], n_attn_heads)[None, :]
        gk = jnp.tile(k_norm_g[l], n_attn_heads)[None, :]

        qs, ks, vs, qkm, vm, og, gcol, grow = _inproj_call(
            x, mod, g_mix[l][None, :], w_in, wgt, gq, gk, hmean_attn, brow,
            layer=l, attn_w=attn_w, ml_w=ml_w, scale=scale, dilations=dilations)

        os_, lses = [], []
        for (window, dilation), q, k, v in zip(DILATED_PATTERNS, qs, ks, vs):
            o, lse = _attn_call(q, k, v, width=attn_w, window=window, dilation=dilation)
            os_.append(o)
            lses.append(lse)

        hm = _mlstm_call(qkm, vm, og, gcol, grow, w_conv[l], b_conv[l][None, :],
                         mlstm_norm_g[l][None, :], tri, hmean_ml)

        x = _outffn_call(x, mod, g_ffn[l][None, :], os_, lses, hm, expand,
                         w_out[l].astype(BF16), w_gate[l].astype(BF16), w_up[l].astype(BF16), w_down[l].astype(BF16),
                         dilations=dilations)
    return x
```

```python
import functools

import jax
import jax.numpy as jnp
from jax import lax
from jax.experimental import pallas as pl
from jax.experimental.pallas import tpu as pltpu

F32 = jnp.float32
BF16 = jnp.bfloat16

ATTN_HEAD_DIM = 64
N_MLSTM_HEADS = 4
CONV_WIDTH = 4
DILATED_PATTERNS = ((128, 1), (512, 4), (2048, 16))
ATTN_BLOCK = 128
N_MOD = 6
RMS_EPS = 1e-6

LANES = 128
MXU_WIDTH = 256
BF16_SUBLANES = 16
VMEM_LIMIT_BYTES = 56 * 1024 * 1024
F32_SUBLANES = 8
MOD_ROWS = F32_SUBLANES
MOD_COL_STEPS = 4
LAYOUT_ROWS = 512
GATE_LANES = 128
INPROJ_ROWS = LAYOUT_ROWS
ATTN_MAX_SUB_BLOCKS = 16
MLSTM_CHUNK = 256
MLSTM_BATCH_ROWS = 1
FFN_ROWS = LAYOUT_ROWS

NEG_INF = float("-inf")
LOG2E = 1.4426950408889634
LN2 = 0.6931471805599453


def _dot(a, b, **kw):
    return jnp.dot(a, b, preferred_element_type=F32, **kw)


def _dot_nt(a, b, **kw):
    return lax.dot_general(a, b, (((1,), (1,)), ((), ())), preferred_element_type=F32, **kw)


def _dot_tn(a, b):
    return lax.dot_general(a, b, (((0,), (0,)), ((), ())), preferred_element_type=F32)


def _sigmoid(z):
    return 1.0 / (1.0 + jnp.exp(-z))


def _log_sigmoid(z):
    return jnp.minimum(z, 0.0) - jnp.log1p(jnp.exp(-jnp.abs(z)))


def _strided_shape(b, s, w, dilation):
    return (b, s // dilation, dilation * w)


def _gather_chain(dilations):
    chain, parent = [], 1
    for d in sorted(set(dilations)):
        if d == 1:
            continue
        assert d % parent == 0 and d // parent < F32_SUBLANES
        chain.append((parent, d, d // parent))
        parent = d
    return chain


def _mod_kernel(ct_ref, w_ref, b_ref, o_ref, *, n_rows):
    ct = ct_ref[...]
    sc = ct * _sigmoid(ct)
    w = w_ref[...]
    rows = [jnp.sum(w * sc[:, r:r + 1], axis=0, keepdims=True) for r in range(n_rows)]
    rows += [jnp.zeros_like(rows[0])] * (MOD_ROWS - n_rows)
    o_ref[...] = jnp.concatenate(rows, axis=0) + b_ref[...]


def _mod_call(c_t, w_ada, b_ada, *, layer, n_rows):
    _, d, n = w_ada.shape
    assert n % (MOD_COL_STEPS * LANES) == 0
    tn = n // MOD_COL_STEPS
    return pl.pallas_call(
        functools.partial(_mod_kernel, n_rows=n_rows),
        out_shape=jax.ShapeDtypeStruct((MOD_ROWS, n), F32),
        grid=(n // tn,),
        in_specs=[pl.BlockSpec((d, MOD_ROWS), lambda j: (0, 0)),
                  pl.BlockSpec((None, d, tn), lambda j: (layer, 0, j)),
                  pl.BlockSpec((1, tn), lambda j: (0, j))],
        out_specs=pl.BlockSpec((MOD_ROWS, tn), lambda j: (0, j)),
        compiler_params=pltpu.CompilerParams(dimension_semantics=("arbitrary",),
                                             vmem_limit_bytes=VMEM_LIMIT_BYTES),
        name="mod",
    )(c_t, w_ada, b_ada)


def _inproj_kernel(x_ref, mod_ref, g_ref, w_ref, wgt_ref, gq_ref, gk_ref, hp_ref,
                   brow_ref, *rest, attn_w, ml_w, scale, dilations):
    n_lay = len(dilations)
    q_refs, k_refs, v_refs = rest[:n_lay], rest[n_lay:2 * n_lay], rest[2 * n_lay:3 * n_lay]
    qkm_ref, vm_ref, og_ref, gcol_ref, grow_ref, perm_s, wb_s = rest[3 * n_lay:]
    tm = x_ref.shape[1]

    @pl.when((pl.program_id(0) == 0) & (pl.program_id(1) == 0))
    def _():
        for c0 in range(0, wb_s.shape[1], MXU_WIDTH):
            wb_s[:, c0:c0 + MXU_WIDTH] = w_ref[:, c0:c0 + MXU_WIDTH].astype(BF16)

    def emit(val, refs):
        w = val.shape[-1]
        n_cg = w // LANES
        ref_of = dict(zip(dilations, refs))
        if 1 in ref_of:
            ref_of[1][0] = val.astype(BF16)
        chain = _gather_chain(dilations)
        for cg in range(n_cg):
            perm_s[0, cg] = val[:, cg * LANES:(cg + 1) * LANES]
        for level, (parent, d, f) in enumerate(chain):
            src, dst = perm_s.at[level % 2], perm_s.at[(level + 1) % 2]
            n = tm // d
            for c in range(parent):
                for a in range(f):
                    r = a * parent + c
                    for cg in range(n_cg):
                        rows = src[cg, pl.ds(c * (tm // parent) + a, n, stride=f), :]
                        c0 = r * w + cg * LANES
                        ref_of[d][0, :, c0:c0 + LANES] = rows.astype(BF16)
                        if level + 1 < len(chain):
                            dst[cg, r * n:(r + 1) * n, :] = rows

    x = x_ref[0]
    ms = jnp.mean(x * x, axis=-1, keepdims=True)
    y = x * lax.rsqrt(ms + RMS_EPS) * g_ref[...]
    h = (y * (1.0 + mod_ref[0, 1:2, :]) + mod_ref[0, 0:1, :]).astype(BF16)

    n_attn = 3 * attn_w
    xa = _dot(h, wb_s[:, :n_attn])
    hp = hp_ref[...]

    def head_norm(t, g):
        t2 = (t * t).astype(BF16)
        gw = hp.shape[0]
        msq = jnp.concatenate([_dot(t2[:, c0:c0 + gw], hp) for c0 in range(0, t.shape[-1], gw)], axis=-1)
        return t * lax.rsqrt(msq + RMS_EPS) * g

    emit(head_norm(xa[:, :attn_w], gq_ref[...]) * scale, q_refs)
    emit(head_norm(xa[:, attn_w:2 * attn_w], gk_ref[...]), k_refs)
    emit(xa[:, 2 * attn_w:], v_refs)

    xm = _dot(h, wb_s[:, n_attn:n_attn + 4 * ml_w])
    qkm_ref[0] = xm[:, :2 * ml_w].astype(BF16)
    vm_ref[0] = xm[:, 2 * ml_w:3 * ml_w].astype(BF16)
    og_ref[0] = _sigmoid(xm[:, 3 * ml_w:]).astype(BF16)

    nh = N_MLSTM_HEADS
    zr = _dot_nt(wgt_ref[...], h)[:2 * nh] + brow_ref[...]
    row = lax.broadcasted_iota(jnp.int32, zr.shape, 0)
    gates = jnp.where(row < nh, zr, _log_sigmoid(zr)) * LOG2E
    grow_ref[0] = gates
    padded = jnp.concatenate([gates, jnp.zeros((GATE_LANES - 2 * nh, tm), F32)], axis=0)
    gcol_ref[0] = padded.T


def _inproj_call(x, mod, g_mix, w, wgt, gq, gk, hp, brow, *, layer, attn_w, ml_w, scale, dilations):
    b, s, d = x.shape
    tm = INPROJ_ROWS
    nh2 = 2 * N_MLSTM_HEADS
    assert all(tm % (dl * BF16_SUBLANES) == 0 for dl in dilations)
    const = lambda shape: pl.BlockSpec(shape, lambda bi, i: (0,) * len(shape))
    tok = lambda w: pl.BlockSpec((1, tm, w), lambda bi, i: (bi, i, 0))
    strided_shapes = tuple(jax.ShapeDtypeStruct(_strided_shape(b, s, attn_w, dl), BF16) for dl in dilations)
    strided_specs = tuple(pl.BlockSpec((1, tm // dl, dl * attn_w), lambda bi, i: (bi, i, 0)) for dl in dilations)
    out_shape = strided_shapes * 3 + (
        jax.ShapeDtypeStruct((b, s, 2 * ml_w), BF16),
        jax.ShapeDtypeStruct((b, s, ml_w), BF16),
        jax.ShapeDtypeStruct((b, s, ml_w), BF16),
        jax.ShapeDtypeStruct((b, s, GATE_LANES), F32),
        jax.ShapeDtypeStruct((b, nh2, s), F32))
    outs = pl.pallas_call(
        functools.partial(_inproj_kernel, attn_w=attn_w, ml_w=ml_w, scale=scale, dilations=dilations),
        out_shape=out_shape,
        grid=(b, s // tm),
        in_specs=[tok(d),
                  pl.BlockSpec((1, N_MOD, d), lambda bi, i: (bi, 0, 0)),
                  const((1, d)),
                  pl.BlockSpec((None,) + w.shape[1:], lambda bi, i: (layer, 0, 0),
                               pipeline_mode=pl.Buffered(1)),
                  const(wgt.shape),
                  const((1, attn_w)), const((1, attn_w)), const(hp.shape),
                  const((nh2, 1))],
        out_specs=strided_specs * 3 + (tok(2 * ml_w), tok(ml_w), tok(ml_w), tok(GATE_LANES),
                                       pl.BlockSpec((1, nh2, tm), lambda bi, i: (bi, 0, i))),
        scratch_shapes=[pltpu.VMEM((2, attn_w // LANES, tm, LANES), F32),
                        pltpu.VMEM((d, 3 * attn_w + 4 * ml_w), BF16)],
        compiler_params=pltpu.CompilerParams(dimension_semantics=("arbitrary", "arbitrary"),
                                             vmem_limit_bytes=VMEM_LIMIT_BYTES),
        name="inproj",
    )(x, mod, g_mix, w, wgt, gq, gk, hp, brow)
    n_lay = len(dilations)
    return (outs[:n_lay], outs[n_lay:2 * n_lay], outs[2 * n_lay:3 * n_lay]) + tuple(outs[3 * n_lay:])


def _attn_kernel(q_ref, kp_ref, kc_ref, vp_ref, vc_ref, bias_ref, ind_ref, o_ref, lse_ref, k0_s, *, n_heads):
    blk = ATTN_BLOCK
    sub_blocks = q_ref.shape[1] // blk
    hd = ATTN_HEAD_DIM
    w = n_heads * hd
    n_res = q_ref.shape[2] // w
    n_pairs = n_heads // 2
    n = pl.program_id(2)
    lane = lax.broadcasted_iota(jnp.int32, (blk, LANES), 1)
    first_head = lane < hd
    for res in range(n_res):
        k0_s[res, 0:blk, :] = kp_ref[0, :, res * w:(res + 1) * w]
        k0_s[res, blk:, :] = kc_ref[0, 0:blk, res * w:(res + 1) * w]

    def value_planes(res, kb, j):
        cols = slice(res * w + j * LANES, res * w + (j + 1) * LANES)
        src = vp_ref[0, :, cols] if kb == 0 else vc_ref[0, (kb - 1) * blk:kb * blk, cols]
        return [jnp.where(first_head, src, 0), jnp.where(first_head, 0, src)]

    units = [(res, i, j) for res in range(n_res) for i in range(sub_blocks) for j in range(n_pairs)]
    scores, maxes, probs = [], [], []
    for res, i, j in units:
        cols = slice(j * LANES, (j + 1) * LANES)
        q = q_ref[0, i * blk:(i + 1) * blk, res * w + j * LANES:res * w + (j + 1) * LANES]
        if i == 0:
            ks = k0_s[res, :, cols]
            bias = bias_ref[jnp.where(n == 0, 1, 0)]
        else:
            ks = kc_ref[0, (i - 1) * blk:(i + 1) * blk, res * w + j * LANES:res * w + (j + 1) * LANES]
            bias = bias_ref[0]
        q2 = jnp.concatenate([jnp.where(first_head, q, 0), jnp.where(first_head, 0, q)], axis=0)
        s = _dot_nt(q2, ks) + bias
        scores.append(s)
        maxes.append(jnp.max(s, axis=-1, keepdims=True))
    for s, m in zip(scores, maxes):
        p = jnp.exp2(s - m).astype(BF16)
        probs.append(jnp.concatenate([p[:blk, :blk], p[blk:, :blk], p[:blk, blk:], p[blk:, blk:]], axis=1))
    for res in range(n_res):
        for i in range(sub_blocks):
            outs = []
            stats = jnp.zeros((blk, LANES), F32)
            for j in range(n_pairs):
                u = (res * sub_blocks + i) * n_pairs + j
                planes = jnp.concatenate(value_planes(res, i, j) + value_planes(res, i + 1, j), axis=0)
                vw = jnp.concatenate([planes, ind_ref[...]], axis=1)
                ov = _dot(probs[u], vw)
                outs.append(ov[:, :LANES] / ov[:, LANES:])
                m = jnp.broadcast_to(maxes[u], (2 * blk, LANES))
                lse = jnp.where(first_head, m[:blk], m[blk:]) * LN2 + jnp.log(ov[:, LANES:])
                stats = jnp.where((lane & (hd - 1)) == j, lse, stats)
            rows = slice(i * blk, (i + 1) * blk)
            o_ref[0, rows, res * w:(res + 1) * w] = jnp.concatenate(outs, axis=-1).astype(o_ref.dtype)
            lse_ref[0, rows, res * LANES:(res + 1) * LANES] = stats


def _attn_band_masks(n_back):
    blk = ATTN_BLOCK
    row = jnp.arange(2 * blk)[:, None] % blk
    col = jnp.arange(2 * blk)[None, :]
    band = (col >= row + (blk - n_back)) & (col <= row + blk)
    return jnp.stack([jnp.where(band, 0.0, NEG_INF), jnp.where(band & (col >= blk), 0.0, NEG_INF)]).astype(F32)


def _attn_indicators():
    first = (jnp.arange(LANES) < ATTN_HEAD_DIM)[None, :]
    plane = jnp.concatenate([jnp.broadcast_to(first, (ATTN_BLOCK, LANES)),
                             jnp.broadcast_to(~first, (ATTN_BLOCK, LANES))], axis=0)
    return jnp.concatenate([plane, plane], axis=0).astype(BF16)


def _attn_call(q, k, v, *, width, window, dilation):
    b, ls, _ = q.shape
    w = width
    n_back = window // dilation
    blk = ATTN_BLOCK
    assert n_back <= blk
    sub_blocks = min(ATTN_MAX_SUB_BLOCKS, ls // blk)
    step_rows = sub_blocks * blk
    assert ls % step_rows == 0
    n_heads = w // ATTN_HEAD_DIM
    assert n_heads <= LANES and n_heads % 2 == 0 and 2 * ATTN_HEAD_DIM == LANES
    n_res = max(1, min(dilation, ATTN_MAX_SUB_BLOCKS // sub_blocks))
    assert dilation % n_res == 0
    cur = pl.BlockSpec((1, step_rows, n_res * w), lambda bi, r, n: (bi, n, r))
    prev = pl.BlockSpec((1, blk, n_res * w),
                        lambda bi, r, n: (bi, jnp.maximum(n * sub_blocks - 1, 0), r))
    masks = _attn_band_masks(n_back)
    indicators = _attn_indicators()
    const = lambda shape: pl.BlockSpec(shape, lambda bi, r, n: (0,) * len(shape))
    return pl.pallas_call(
        functools.partial(_attn_kernel, n_heads=n_heads),
        out_shape=(jax.ShapeDtypeStruct((b, ls, dilation * w), BF16),
                   jax.ShapeDtypeStruct((b, ls, dilation * LANES), F32)),
        grid=(b, dilation // n_res, ls // step_rows),
        in_specs=[cur, prev, cur, prev, cur, const(masks.shape), const(indicators.shape)],
        out_specs=(cur, pl.BlockSpec((1, step_rows, n_res * LANES), lambda bi, r, n: (bi, n, r))),
        scratch_shapes=[pltpu.VMEM((n_res, 2 * blk, w), BF16)],
        compiler_params=pltpu.CompilerParams(
            dimension_semantics=("arbitrary", "arbitrary", "arbitrary"),
            vmem_limit_bytes=VMEM_LIMIT_BYTES),
        name=f"attn_d{dilation}",
    )(q, k, k, v, v, masks, indicators)


def _mlstm_kernel(qk_ref, v_ref, og_ref, gcol_ref, grow_ref, wc_ref, bc_ref, gn_ref, tri_ref, hm_ref,
                  o_ref, tail_s, c_s, m_s, *, ml_w):
    L = MLSTM_CHUNK
    nh = N_MLSTM_HEADS
    dh = ml_w // nh

    @pl.when(pl.program_id(1) == 0)
    def _():
        tail_s[...] = jnp.zeros_like(tail_s)
        c_s[...] = jnp.zeros_like(c_s)
        m_s[...] = jnp.zeros_like(m_s)

    tri = tri_ref[...]
    ti = lax.broadcasted_iota(jnp.int32, (L, L), 0)
    si = lax.broadcasted_iota(jnp.int32, (L, L), 1)
    causal = si <= ti
    ones = jnp.ones((L, dh), BF16)
    hp = hm_ref[...]
    for bi in range(qk_ref.shape[0]):
        _mlstm_chunk(bi, qk_ref, v_ref, og_ref, gcol_ref, grow_ref, wc_ref, bc_ref, gn_ref, o_ref,
                     tail_s, c_s, m_s, tri, causal, ones, hp, ml_w=ml_w)


def _mlstm_chunk(bi, qk_ref, v_ref, og_ref, gcol_ref, grow_ref, wc_ref, bc_ref, gn_ref, o_ref,
                 tail_s, c_s, m_s, tri, causal, ones, hp, *, ml_w):
    L = MLSTM_CHUNK
    nh = N_MLSTM_HEADS
    dh = ml_w // nh
    x = qk_ref[bi].astype(F32)
    tail = tail_s[bi]
    row8 = lax.broadcasted_iota(jnp.int32, tail.shape, 0)
    acc = bc_ref[...] + x * wc_ref[CONV_WIDTH - 1:CONV_WIDTH, :]
    for back in range(1, CONV_WIDTH):
        rolled = pltpu.roll(x, back, 0)
        top = jnp.where(row8 >= back, rolled[:MOD_ROWS], pltpu.roll(tail, back, 0))
        shifted = jnp.concatenate([top, rolled[MOD_ROWS:]], axis=0)
        acc = acc + shifted * wc_ref[CONV_WIDTH - 1 - back:CONV_WIDTH - back, :]
    tail_s[bi] = x[L - MOD_ROWS:]
    qk = acc * _sigmoid(acc)

    gcol = gcol_ref[bi]
    grow = grow_ref[bi]
    bcol_all = _dot(tri, gcol, precision=lax.Precision.HIGHEST)
    brow_all = _dot_nt(grow, tri, precision=lax.Precision.HIGHEST)

    ks = [qk[:, ml_w + h * dh:ml_w + (h + 1) * dh] * (dh ** -0.5) for h in range(nh)]

    heads = []
    for h in range(nh):
        st = bi * nh + h
        b_c = jnp.broadcast_to(bcol_all[:, nh + h:nh + h + 1], (L, dh))
        i_c = jnp.broadcast_to(gcol[:, h:h + 1], (L, dh))
        b_r = brow_all[nh + h:nh + h + 1, :]
        i_r = grow[h:h + 1, :]
        m_prev = m_s[st, 0:1, :]
        log_d = jnp.where(causal, b_c[:, 0:1] + (i_r - b_r), NEG_INF)
        m_inter = b_c + m_prev
        m_t = jnp.maximum(m_inter, jnp.max(log_d, axis=-1, keepdims=True))
        d_mat = jnp.exp2(log_d - m_t[:, 0:1])
        inter = jnp.exp2(m_inter - m_t)
        b_last = b_c[L - 1:L, :]
        w_log = b_last - b_c + i_c
        m_new = jnp.maximum(b_last + m_prev, jnp.max(w_log, axis=0, keepdims=True))
        wgt = jnp.exp2(w_log - m_new)
        decay = jnp.exp2(b_last + m_prev - m_new)
        heads.append((m_t, d_mat, inter, m_new, wgt, decay))

    outs = []
    for h, (m_t, d_mat, inter, _, _, _) in enumerate(heads):
        hcols = slice(h * dh, (h + 1) * dh)
        q = qk[:, hcols].astype(BF16)
        v_ext = jnp.concatenate([v_ref[bi, :, hcols], ones], axis=-1)
        c_prev = c_s[bi * nh + h]
        s_qk = _dot_nt(q, ks[h].astype(BF16)) * d_mat
        ext = jnp.concatenate([inter, inter], axis=-1) * _dot(q, c_prev.astype(BF16)) \
            + _dot(s_qk.astype(BF16), v_ext)
        outs.append(ext[:, :dh] / jnp.maximum(jnp.abs(ext[:, dh:]), jnp.exp2(-m_t)))
    hh = jnp.concatenate(outs, axis=-1)
    h2 = (hh * hh).astype(BF16)
    gw = hp.shape[0]
    msq = jnp.concatenate([_dot(h2[:, c0:c0 + gw], hp) for c0 in range(0, ml_w, gw)], axis=-1)
    hn = hh * lax.rsqrt(msq + RMS_EPS) * gn_ref[...]
    o_ref[bi] = (og_ref[bi].astype(F32) * hn).astype(o_ref.dtype)

    for h, (_, _, _, m_new, wgt, decay) in enumerate(heads):
        st = bi * nh + h
        hcols = slice(h * dh, (h + 1) * dh)
        v_ext = jnp.concatenate([v_ref[bi, :, hcols], ones], axis=-1)
        kw = (ks[h] * wgt).astype(BF16)
        c_s[st] = jnp.concatenate([decay, decay], axis=-1) * c_s[st] + _dot_tn(kw, v_ext)
        m_s[st] = jnp.broadcast_to(m_new, m_s.shape[1:])


def _mlstm_call(qkm, vm, og, gcol, grow, w_conv, b_conv, g_norm, tri, hmean):
    b, s, ml_w = vm.shape
    L = MLSTM_CHUNK
    nh = N_MLSTM_HEADS
    dh = ml_w // nh
    bb = MLSTM_BATCH_ROWS
    assert b % bb == 0
    const = lambda shape: pl.BlockSpec(shape, lambda g, i: (0,) * len(shape))
    tok = lambda w: pl.BlockSpec((bb, L, w), lambda g, i: (g, i, 0))
    return pl.pallas_call(
        functools.partial(_mlstm_kernel, ml_w=ml_w),
        out_shape=jax.ShapeDtypeStruct((b, s, ml_w), BF16),
        grid=(b // bb, s // L),
        in_specs=[tok(2 * ml_w), tok(ml_w), tok(ml_w), tok(GATE_LANES),
                  pl.BlockSpec((bb, 2 * nh, L), lambda g, i: (g, 0, i)),
                  const(w_conv.shape), const((1, 2 * ml_w)), const((1, ml_w)),
                  const((L, L)), const(hmean.shape)],
        out_specs=tok(ml_w),
        scratch_shapes=[pltpu.VMEM((bb, MOD_ROWS, 2 * ml_w), F32),
                        pltpu.VMEM((bb * nh, dh, 2 * dh), F32),
                        pltpu.VMEM((bb * nh, MOD_ROWS, dh), F32)],
        compiler_params=pltpu.CompilerParams(dimension_semantics=("arbitrary", "arbitrary"),
                                             vmem_limit_bytes=VMEM_LIMIT_BYTES),
        name="mlstm",
    )(qkm, vm, og, gcol, grow, w_conv, b_conv, g_norm, tri, hmean)


def _outffn_kernel(x_ref, mod_ref, g_ref, *rest, dilations):
    n_lay = len(dilations)
    o_refs, l_refs = rest[:n_lay], rest[n_lay:2 * n_lay]
    (hm_ref, ex_ref, wo_ref, wg_ref, wu_ref, wd_ref, out_ref,
     operm_s, lperm_s) = rest[2 * n_lay:]
    tm = x_ref.shape[1]

    def natural(ref, d, scratch):
        if d == 1:
            return ref[0].astype(F32)
        n_cg = scratch.shape[1]
        chain = [step for step in _gather_chain(dilations) if step[1] <= d]
        for level, (parent, dd, f) in enumerate(reversed(chain)):
            dst = scratch.at[level % 2]
            src = scratch.at[(level + 1) % 2]
            n = tm // dd
            for c in range(parent):
                for a in range(f):
                    r = a * parent + c
                    for cg in range(n_cg):
                        if level == 0:
                            c0 = (r * n_cg + cg) * LANES
                            rows = ref[0, :, c0:c0 + LANES].astype(F32)
                        else:
                            rows = src[cg, r * n:(r + 1) * n, :]
                        dst[cg, pl.ds(c * (tm // parent) + a, n, stride=f), :] = rows
        final = scratch.at[(len(chain) - 1) % 2]
        return jnp.concatenate([final[cg] for cg in range(n_cg)], axis=-1)

    lses = [natural(ref, d, lperm_s) for ref, d in zip(l_refs, dilations)]
    mx = functools.reduce(jnp.maximum, lses)
    es = [jnp.exp(l - mx) for l in lses]
    inv = 1.0 / functools.reduce(jnp.add, es)
    ex = ex_ref[...]
    attn = None
    for e, ref, d in zip(es, o_refs, dilations):
        term = _dot((e * inv).astype(BF16), ex) * natural(ref, d, operm_s)
        attn = term if attn is None else attn + term
    aw = attn.shape[-1]
    y = _dot(attn.astype(BF16), wo_ref[:aw, :]) + _dot(hm_ref[0], wo_ref[aw:, :])
    x1 = x_ref[0] + mod_ref[0, 2:3, :] * y
    ms = jnp.mean(x1 * x1, axis=-1, keepdims=True)
    hn = x1 * lax.rsqrt(ms + RMS_EPS) * g_ref[...]
    hb = (hn * (1.0 + mod_ref[0, 4:5, :]) + mod_ref[0, 3:4, :]).astype(BF16)
    g = _dot(hb, wg_ref[...])
    u = _dot(hb, wu_ref[...])
    a = (g * _sigmoid(g) * u).astype(BF16)
    out_ref[0] = x1 + mod_ref[0, 5:6, :] * _dot(a, wd_ref[...])


def _outffn_call(x, mod, g_ffn, os_, lses, hm, expand, wo, wg, wu, wd, *, dilations):
    b, s, d = x.shape
    tm = FFN_ROWS
    assert all(tm % (dl * BF16_SUBLANES) == 0 for dl in dilations)
    aw = expand.shape[1]
    const = lambda shape: pl.BlockSpec(shape, lambda bi, i: (0,) * len(shape))
    tok = lambda w: pl.BlockSpec((1, tm, w), lambda bi, i: (bi, i, 0))
    strided = lambda w: tuple(pl.BlockSpec((1, tm // dl, dl * w), lambda bi, i: (bi, i, 0)) for dl in dilations)
    return pl.pallas_call(
        functools.partial(_outffn_kernel, dilations=dilations),
        out_shape=jax.ShapeDtypeStruct((b, s, d), F32),
        grid=(b, s // tm),
        in_specs=[tok(d), pl.BlockSpec((1, N_MOD, d), lambda bi, i: (bi, 0, 0)), const((1, d)),
                  *strided(aw), *strided(LANES), tok(hm.shape[-1]),
                  const(expand.shape), const(wo.shape),
                  const(wg.shape), const(wu.shape), const(wd.shape)],
        out_specs=tok(d),
        scratch_shapes=[pltpu.VMEM((2, aw // LANES, tm, LANES), F32), pltpu.VMEM((2, 1, tm, LANES), F32)],
        compiler_params=pltpu.CompilerParams(dimension_semantics=("arbitrary", "arbitrary"),
                                             vmem_limit_bytes=VMEM_LIMIT_BYTES),
        name="outffn",
    )(x, mod, g_ffn, *os_, *lses, hm, expand, wo, wg, wu, wd)


def _block_diag_mean(width, group):
    idx = jnp.arange(width) // group
    return jnp.where(idx[:, None] == idx[None, :], 1.0 / group, 0.0).astype(BF16)


def kernel(x, c, g_mix, w_in, w_conv, b_conv, b_igate, b_fgate, q_norm_g, k_norm_g, mlstm_norm_g, w_out,
           g_ffn, w_gate, w_up, w_down, w_ada, b_ada):
    b, s, d = x.shape
    depth = g_mix.shape[0]
    attn_w = d // 2
    ml_w = d - attn_w
    nh = N_MLSTM_HEADS
    n_attn_heads = attn_w // ATTN_HEAD_DIM
    scale = ATTN_HEAD_DIM ** -0.5 * LOG2E
    dilations = tuple(dl for _, dl in DILATED_PATTERNS)
    assert b <= MOD_ROWS and 2 * nh <= GATE_LANES

    c_t = jnp.zeros((d, MOD_ROWS), F32).at[:, :b].set(c.T)
    assert attn_w % MXU_WIDTH == 0 and MXU_WIDTH % ATTN_HEAD_DIM == 0
    hmean_attn = _block_diag_mean(MXU_WIDTH, ATTN_HEAD_DIM)
    assert ml_w % MXU_WIDTH == 0 and MXU_WIDTH % (ml_w // nh) == 0
    hmean_ml = _block_diag_mean(MXU_WIDTH, ml_w // nh)
    stat_lane = jnp.arange(LANES)
    head_of_stat = jnp.where(stat_lane % ATTN_HEAD_DIM < n_attn_heads // 2,
                             2 * (stat_lane % ATTN_HEAD_DIM) + stat_lane // ATTN_HEAD_DIM, -1)
    head_of_col = jnp.arange(attn_w) // ATTN_HEAD_DIM
    expand = (head_of_stat[:, None] == head_of_col[None, :]).astype(BF16)
    tri = jnp.tril(jnp.ones((MLSTM_CHUNK, MLSTM_CHUNK), F32))

    for l in range(depth):
        mod = _mod_call(c_t, w_ada, b_ada[l][None, :], layer=l, n_rows=b)[:b].reshape(b, N_MOD, d)

        n_main = 3 * attn_w + 4 * ml_w
        assert n_main % MXU_WIDTH == 0
        wg_cols = w_in[l][:, n_main:]
        wgt = jnp.zeros((BF16_SUBLANES, d), F32).at[:2 * nh].set(wg_cols.T).astype(BF16)
        brow = jnp.concatenate([b_igate[l], b_fgate[l]])[:, None]
        gq = jnp.tile(q_norm_g[l], n_attn_heads)[None, :]
        gk = jnp.tile(k_norm_g[l], n_attn_heads)[None, :]

        qs, ks, vs, qkm, vm, og, gcol, grow = _inproj_call(
            x, mod, g_mix[l][None, :], w_in, wgt, gq, gk, hmean_attn, brow,
            layer=l, attn_w=attn_w, ml_w=ml_w, scale=scale, dilations=dilations)

        os_, lses = [], []
        for (window, dilation), q, k, v in zip(DILATED_PATTERNS, qs, ks, vs):
            o, lse = _attn_call(q, k, v, width=attn_w, window=window, dilation=dilation)
            os_.append(o)
            lses.append(lse)

        hm = _mlstm_call(qkm, vm, og, gcol, grow, w_conv[l], b_conv[l][None, :],
                         mlstm_norm_g[l][None, :], tri, hmean_ml)

        x = _outffn_call(x, mod, g_ffn[l][None, :], os_, lses, hm, expand,
                         w_out[l].astype(BF16), w_gate[l].astype(BF16), w_up[l].astype(BF16), w_down[l].astype(BF16),
                         dilations=dilations)
    return x
```

```python
import functools

import jax
import jax.numpy as jnp
from jax import lax
from jax.experimental import pallas as pl
from jax.experimental.pallas import tpu as pltpu

F32 = jnp.float32
BF16 = jnp.bfloat16

ATTN_HEAD_DIM = 64
N_MLSTM_HEADS = 4
CONV_WIDTH = 4
DILATED_PATTERNS = ((128, 1), (512, 4), (2048, 16))
ATTN_BLOCK = 128
N_MOD = 6
RMS_EPS = 1e-6

LANES = 128
MXU_WIDTH = 256
BF16_SUBLANES = 16
VMEM_LIMIT_BYTES = 56 * 1024 * 1024
F32_SUBLANES = 8
MOD_ROWS = F32_SUBLANES
MOD_COL_STEPS = 4
LAYOUT_ROWS = 512
GATE_LANES = 128
INPROJ_ROWS = LAYOUT_ROWS
ATTN_MAX_SUB_BLOCKS = 16
MLSTM_CHUNK = 256
MLSTM_BATCH_ROWS = 1
FFN_ROWS = LAYOUT_ROWS

NEG_INF = float("-inf")
LOG2E = 1.4426950408889634
LN2 = 0.6931471805599453


def _dot(a, b, **kw):
    return jnp.dot(a, b, preferred_element_type=F32, **kw)


def _dot_nt(a, b, **kw):
    return lax.dot_general(a, b, (((1,), (1,)), ((), ())), preferred_element_type=F32, **kw)


def _dot_tn(a, b):
    return lax.dot_general(a, b, (((0,), (0,)), ((), ())), preferred_element_type=F32)


def _sigmoid(z):
    return 1.0 / (1.0 + jnp.exp(-z))


def _log_sigmoid(z):
    return jnp.minimum(z, 0.0) - jnp.log1p(jnp.exp(-jnp.abs(z)))


def _strided_shape(b, s, w, dilation):
    return (b, s // dilation, dilation * w)


def _gather_chain(dilations):
    chain, parent = [], 1
    for d in sorted(set(dilations)):
        if d == 1:
            continue
        assert d % parent == 0 and d // parent < F32_SUBLANES
        chain.append((parent, d, d // parent))
        parent = d
    return chain


def _mod_kernel(ct_ref, w_ref, b_ref, o_ref, *, n_rows):
    ct = ct_ref[...]
    sc = ct * _sigmoid(ct)
    w = w_ref[...]
    rows = [jnp.sum(w * sc[:, r:r + 1], axis=0, keepdims=True) for r in range(n_rows)]
    rows += [jnp.zeros_like(rows[0])] * (MOD_ROWS - n_rows)
    o_ref[...] = jnp.concatenate(rows, axis=0) + b_ref[...]


def _mod_call(c_t, w_ada, b_ada, *, layer, n_rows):
    _, d, n = w_ada.shape
    assert n % (MOD_COL_STEPS * LANES) == 0
    tn = n // MOD_COL_STEPS
    return pl.pallas_call(
        functools.partial(_mod_kernel, n_rows=n_rows),
        out_shape=jax.ShapeDtypeStruct((MOD_ROWS, n), F32),
        grid=(n // tn,),
        in_specs=[pl.BlockSpec((d, MOD_ROWS), lambda j: (0, 0)),
                  pl.BlockSpec((None, d, tn), lambda j: (layer, 0, j)),
                  pl.BlockSpec((1, tn), lambda j: (0, j))],
        out_specs=pl.BlockSpec((MOD_ROWS, tn), lambda j: (0, j)),
        compiler_params=pltpu.CompilerParams(dimension_semantics=("arbitrary",),
                                             vmem_limit_bytes=VMEM_LIMIT_BYTES),
        name="mod",
    )(c_t, w_ada, b_ada)


def _inproj_kernel(x_ref, mod_ref, g_ref, w_ref, wgt_ref, gq_ref, gk_ref, hp_ref,
                   brow_ref, *rest, attn_w, ml_w, dilations):
    n_lay = len(dilations)
    q_refs, k_refs, v_refs = rest[:n_lay], rest[n_lay:2 * n_lay], rest[2 * n_lay:3 * n_lay]
    qkm_ref, vm_ref, og_ref, gcol_ref, grow_ref, perm_s, wb_s = rest[3 * n_lay:]
    tm = x_ref.shape[1]

    @pl.when((pl.program_id(0) == 0) & (pl.program_id(1) == 0))
    def _():
        for c0 in range(0, wb_s.shape[1], MXU_WIDTH):
            wb_s[:, c0:c0 + MXU_WIDTH] = w_ref[:, c0:c0 + MXU_WIDTH].astype(BF16)

    def emit(val, refs):
        w = val.shape[-1]
        n_cg = w // LANES
        ref_of = dict(zip(dilations, refs))
        if 1 in ref_of:
            ref_of[1][0] = val.astype(BF16)
        chain = _gather_chain(dilations)
        for cg in range(n_cg):
            perm_s[0, cg] = val[:, cg * LANES:(cg + 1) * LANES]
        for level, (parent, d, f) in enumerate(chain):
            src, dst = perm_s.at[level % 2], perm_s.at[(level + 1) % 2]
            n = tm // d
            for c in range(parent):
                for a in range(f):
                    r = a * parent + c
                    for cg in range(n_cg):
                        rows = src[cg, pl.ds(c * (tm // parent) + a, n, stride=f), :]
                        c0 = r * w + cg * LANES
                        ref_of[d][0, :, c0:c0 + LANES] = rows.astype(BF16)
                        if level + 1 < len(chain):
                            dst[cg, r * n:(r + 1) * n, :] = rows

    x = x_ref[0]
    ms = jnp.mean(x * x, axis=-1, keepdims=True)
    y = x * lax.rsqrt(ms + RMS_EPS) * g_ref[...]
    h = (y * (1.0 + mod_ref[0, 1:2, :]) + mod_ref[0, 0:1, :]).astype(BF16)

    n_attn = 3 * attn_w
    xa = _dot(h, wb_s[:, :n_attn])
    hp = hp_ref[...]

    def head_norm(t, g):
        t2 = (t * t).astype(BF16)
        gw = hp.shape[0]
        msq = jnp.concatenate([_dot(t2[:, c0:c0 + gw], hp) for c0 in range(0, t.shape[-1], gw)], axis=-1)
        return t * lax.rsqrt(msq + RMS_EPS) * g

    emit(head_norm(xa[:, :attn_w], gq_ref[...]), q_refs)
    emit(head_norm(xa[:, attn_w:2 * attn_w], gk_ref[...]), k_refs)
    emit(xa[:, 2 * attn_w:], v_refs)

    xm = _dot(h, wb_s[:, n_attn:n_attn + 4 * ml_w])
    qkm_ref[0] = xm[:, :2 * ml_w].astype(BF16)
    vm_ref[0] = xm[:, 2 * ml_w:3 * ml_w].astype(BF16)
    og_ref[0] = _sigmoid(xm[:, 3 * ml_w:]).astype(BF16)

    nh = N_MLSTM_HEADS
    zr = _dot_nt(wgt_ref[...], h)[:2 * nh] + brow_ref[...]
    row = lax.broadcasted_iota(jnp.int32, zr.shape, 0)
    gates = jnp.where(row < nh, zr, _log_sigmoid(zr)) * LOG2E
    grow_ref[0] = gates
    padded = jnp.concatenate([gates, jnp.zeros((GATE_LANES - 2 * nh, tm), F32)], axis=0)
    gcol_ref[0] = padded.T


def _inproj_call(x, mod, g_mix, w, wgt, gq, gk, hp, brow, *, layer, attn_w, ml_w, dilations):
    b, s, d = x.shape
    tm = INPROJ_ROWS
    nh2 = 2 * N_MLSTM_HEADS
    assert all(tm % (dl * BF16_SUBLANES) == 0 for dl in dilations)
    const = lambda shape: pl.BlockSpec(shape, lambda bi, i: (0,) * len(shape))
    tok = lambda w: pl.BlockSpec((1, tm, w), lambda bi, i: (bi, i, 0))
    strided_shapes = tuple(jax.ShapeDtypeStruct(_strided_shape(b, s, attn_w, dl), BF16) for dl in dilations)
    strided_specs = tuple(pl.BlockSpec((1, tm // dl, dl * attn_w), lambda bi, i: (bi, i, 0)) for dl in dilations)
    out_shape = strided_shapes * 3 + (
        jax.ShapeDtypeStruct((b, s, 2 * ml_w), BF16),
        jax.ShapeDtypeStruct((b, s, ml_w), BF16),
        jax.ShapeDtypeStruct((b, s, ml_w), BF16),
        jax.ShapeDtypeStruct((b, s, GATE_LANES), F32),
        jax.ShapeDtypeStruct((b, nh2, s), F32))
    outs = pl.pallas_call(
        functools.partial(_inproj_kernel, attn_w=attn_w, ml_w=ml_w, dilations=dilations),
        out_shape=out_shape,
        grid=(b, s // tm),
        in_specs=[tok(d),
                  pl.BlockSpec((1, N_MOD, d), lambda bi, i: (bi, 0, 0)),
                  const((1, d)),
                  pl.BlockSpec((None,) + w.shape[1:], lambda bi, i: (layer, 0, 0),
                               pipeline_mode=pl.Buffered(1)),
                  const(wgt.shape),
                  const((1, attn_w)), const((1, attn_w)), const(hp.shape),
                  const((nh2, 1))],
        out_specs=strided_specs * 3 + (tok(2 * ml_w), tok(ml_w), tok(ml_w), tok(GATE_LANES),
                                       pl.BlockSpec((1, nh2, tm), lambda bi, i: (bi, 0, i))),
        scratch_shapes=[pltpu.VMEM((2, attn_w // LANES, tm, LANES), F32),
                        pltpu.VMEM((d, 3 * attn_w + 4 * ml_w), BF16)],
        compiler_params=pltpu.CompilerParams(dimension_semantics=("arbitrary", "arbitrary"),
                                             vmem_limit_bytes=VMEM_LIMIT_BYTES),
        name="inproj",
    )(x, mod, g_mix, w, wgt, gq, gk, hp, brow)
    n_lay = len(dilations)
    return (outs[:n_lay], outs[n_lay:2 * n_lay], outs[2 * n_lay:3 * n_lay]) + tuple(outs[3 * n_lay:])


def _attn_kernel(q_ref, kp_ref, kc_ref, vp_ref, vc_ref, bias_ref, ones_ref, o_ref, lse_ref, *, n_heads):
    blk = ATTN_BLOCK
    sub_blocks = q_ref.shape[1] // blk
    hd = ATTN_HEAD_DIM
    w = n_heads * hd
    n_res = q_ref.shape[2] // w
    n_pairs = n_heads // 2
    n = pl.program_id(2)
    lane = lax.broadcasted_iota(jnp.int32, (blk, LANES), 1)
    first_head = lane < hd

    def key_rows(prev_ref, cur_ref, i, cols):
        if i == 0:
            return jnp.concatenate([prev_ref[0, :, cols], cur_ref[0, 0:blk, cols]], axis=0)
        return cur_ref[0, (i - 1) * blk:(i + 1) * blk, cols]

    units = [(res, i, j) for res in range(n_res) for i in range(sub_blocks) for j in range(n_pairs)]
    scores, maxes, probs = [], [], []
    for res, i, j in units:
        cols = slice(res * w + j * LANES, res * w + (j + 1) * LANES)
        q = q_ref[0, i * blk:(i + 1) * blk, cols]
        ks = key_rows(kp_ref, kc_ref, i, cols)
        bias = bias_ref[jnp.where(n == 0, 1, 0)] if i == 0 else bias_ref[0]
        q2 = jnp.concatenate([jnp.where(first_head, q, 0), jnp.where(first_head, 0, q)], axis=0)
        s = _dot_nt(q2, ks) + bias
        scores.append(s)
        maxes.append(jnp.max(s, axis=-1, keepdims=True))
    for s, m in zip(scores, maxes):
        probs.append(jnp.exp2(s - m).astype(BF16))
    for res in range(n_res):
        for i in range(sub_blocks):
            outs = []
            stats = jnp.zeros((blk, LANES), F32)
            for j in range(n_pairs):
                u = (res * sub_blocks + i) * n_pairs + j
                cols = slice(res * w + j * LANES, res * w + (j + 1) * LANES)
                vw = jnp.concatenate([key_rows(vp_ref, vc_ref, i, cols), ones_ref[...]], axis=1)
                ov = _dot(probs[u], vw)
                num = jnp.where(first_head, ov[:blk, :LANES], ov[blk:, :LANES])
                den = jnp.where(first_head, ov[:blk, LANES:], ov[blk:, LANES:])
                outs.append(num / den)
                m = jnp.broadcast_to(maxes[u], (2 * blk, LANES))
                lse = jnp.where(first_head, m[:blk], m[blk:]) * LN2 + jnp.log(den)
                stats = jnp.where((lane & (hd - 1)) == j, lse, stats)
            rows = slice(i * blk, (i + 1) * blk)
            o_ref[0, rows, res * w:(res + 1) * w] = jnp.concatenate(outs, axis=-1).astype(o_ref.dtype)
            lse_ref[0, rows, res * LANES:(res + 1) * LANES] = stats


def _attn_band_masks(n_back):
    blk = ATTN_BLOCK
    row = jnp.arange(2 * blk)[:, None] % blk
    col = jnp.arange(2 * blk)[None, :]
    band = (col >= row + (blk - n_back)) & (col <= row + blk)
    return jnp.stack([jnp.where(band, 0.0, NEG_INF), jnp.where(band & (col >= blk), 0.0, NEG_INF)]).astype(F32)


def _attn_call(q, k, v, *, width, window, dilation):
    b, ls, _ = q.shape
    w = width
    n_back = window // dilation
    blk = ATTN_BLOCK
    assert n_back <= blk
    sub_blocks = min(ATTN_MAX_SUB_BLOCKS, ls // blk)
    step_rows = sub_blocks * blk
    assert ls % step_rows == 0
    n_heads = w // ATTN_HEAD_DIM
    assert n_heads <= LANES and n_heads % 2 == 0 and 2 * ATTN_HEAD_DIM == LANES
    n_res = max(1, min(dilation, ATTN_MAX_SUB_BLOCKS // sub_blocks))
    assert dilation % n_res == 0
    cur = pl.BlockSpec((1, step_rows, n_res * w), lambda bi, r, n: (bi, n, r))
    prev = pl.BlockSpec((1, blk, n_res * w),
                        lambda bi, r, n: (bi, jnp.maximum(n * sub_blocks - 1, 0), r))
    masks = _attn_band_masks(n_back)
    ones = jnp.ones((2 * blk, LANES), BF16)
    const = lambda shape: pl.BlockSpec(shape, lambda bi, r, n: (0,) * len(shape))
    return pl.pallas_call(
        functools.partial(_attn_kernel, n_heads=n_heads),
        out_shape=(jax.ShapeDtypeStruct((b, ls, dilation * w), BF16),
                   jax.ShapeDtypeStruct((b, ls, dilation * LANES), F32)),
        grid=(b, dilation // n_res, ls // step_rows),
        in_specs=[cur, prev, cur, prev, cur, const(masks.shape), const(ones.shape)],
        out_specs=(cur, pl.BlockSpec((1, step_rows, n_res * LANES), lambda bi, r, n: (bi, n, r))),
        compiler_params=pltpu.CompilerParams(
            dimension_semantics=("arbitrary", "arbitrary", "arbitrary"),
            vmem_limit_bytes=VMEM_LIMIT_BYTES),
        name=f"attn_d{dilation}",
    )(q, k, k, v, v, masks, ones)


def _mlstm_kernel(qk_ref, v_ref, og_ref, gcol_ref, grow_ref, wc_ref, bc_ref, gn_ref, tri_ref, hm_ref,
                  o_ref, tail_s, c_s, m_s, *, ml_w):
    L = MLSTM_CHUNK
    nh = N_MLSTM_HEADS
    dh = ml_w // nh

    @pl.when(pl.program_id(1) == 0)
    def _():
        tail_s[...] = jnp.zeros_like(tail_s)
        c_s[...] = jnp.zeros_like(c_s)
        m_s[...] = jnp.zeros_like(m_s)

    tri = tri_ref[...]
    ti = lax.broadcasted_iota(jnp.int32, (L, L), 0)
    si = lax.broadcasted_iota(jnp.int32, (L, L), 1)
    causal = si <= ti
    ones = jnp.ones((L, dh), BF16)
    hp = hm_ref[...]
    for bi in range(qk_ref.shape[0]):
        _mlstm_chunk(bi, qk_ref, v_ref, og_ref, gcol_ref, grow_ref, wc_ref, bc_ref, gn_ref, o_ref,
                     tail_s, c_s, m_s, tri, causal, ones, hp, ml_w=ml_w)


def _mlstm_chunk(bi, qk_ref, v_ref, og_ref, gcol_ref, grow_ref, wc_ref, bc_ref, gn_ref, o_ref,
                 tail_s, c_s, m_s, tri, causal, ones, hp, *, ml_w):
    L = MLSTM_CHUNK
    nh = N_MLSTM_HEADS
    dh = ml_w // nh
    x = qk_ref[bi].astype(F32)
    tail = tail_s[bi]
    row8 = lax.broadcasted_iota(jnp.int32, tail.shape, 0)
    acc = bc_ref[...] + x * wc_ref[CONV_WIDTH - 1:CONV_WIDTH, :]
    for back in range(1, CONV_WIDTH):
        rolled = pltpu.roll(x, back, 0)
        top = jnp.where(row8 >= back, rolled[:MOD_ROWS], pltpu.roll(tail, back, 0))
        shifted = jnp.concatenate([top, rolled[MOD_ROWS:]], axis=0)
        acc = acc + shifted * wc_ref[CONV_WIDTH - 1 - back:CONV_WIDTH - back, :]
    tail_s[bi] = x[L - MOD_ROWS:]
    qk = acc * _sigmoid(acc)

    gcol = gcol_ref[bi]
    grow = grow_ref[bi]
    bcol_all = _dot(tri, gcol, precision=lax.Precision.HIGHEST)
    brow_all = _dot_nt(grow, tri, precision=lax.Precision.HIGHEST)

    ks = [qk[:, ml_w + h * dh:ml_w + (h + 1) * dh] * (dh ** -0.5) for h in range(nh)]

    heads = []
    for h in range(nh):
        st = bi * nh + h
        b_c = jnp.broadcast_to(bcol_all[:, nh + h:nh + h + 1], (L, dh))
        i_c = jnp.broadcast_to(gcol[:, h:h + 1], (L, dh))
        b_r = brow_all[nh + h:nh + h + 1, :]
        i_r = grow[h:h + 1, :]
        m_prev = m_s[st, 0:1, :]
        log_d = jnp.where(causal, b_c[:, 0:1] + (i_r - b_r), NEG_INF)
        m_inter = b_c + m_prev
        m_t = jnp.maximum(m_inter, jnp.max(log_d, axis=-1, keepdims=True))
        d_mat = jnp.exp2(log_d - m_t[:, 0:1])
        inter = jnp.exp2(m_inter - m_t)
        b_last = b_c[L - 1:L, :]
        w_log = b_last - b_c + i_c
        m_new = jnp.maximum(b_last + m_prev, jnp.max(w_log, axis=0, keepdims=True))
        wgt = jnp.exp2(w_log - m_new)
        decay = jnp.exp2(b_last + m_prev - m_new)
        heads.append((m_t, d_mat, inter, m_new, wgt, decay))

    outs = []
    for h, (m_t, d_mat, inter, _, _, _) in enumerate(heads):
        hcols = slice(h * dh, (h + 1) * dh)
        q = qk[:, hcols].astype(BF16)
        v_ext = jnp.concatenate([v_ref[bi, :, hcols], ones], axis=-1)
        c_prev = c_s[bi * nh + h]
        s_qk = _dot_nt(q, ks[h].astype(BF16)) * d_mat
        ext = jnp.concatenate([inter, inter], axis=-1) * _dot(q, c_prev.astype(BF16)) \
            + _dot(s_qk.astype(BF16), v_ext)
        outs.append(ext[:, :dh] / jnp.maximum(jnp.abs(ext[:, dh:]), jnp.exp2(-m_t)))
    hh = jnp.concatenate(outs, axis=-1)
    h2 = (hh * hh).astype(BF16)
    gw = hp.shape[0]
    msq = jnp.concatenate([_dot(h2[:, c0:c0 + gw], hp) for c0 in range(0, ml_w, gw)], axis=-1)
    hn = hh * lax.rsqrt(msq + RMS_EPS) * gn_ref[...]
    o_ref[bi] = (og_ref[bi].astype(F32) * hn).astype(o_ref.dtype)

    for h, (_, _, _, m_new, wgt, decay) in enumerate(heads):
        st = bi * nh + h
        hcols = slice(h * dh, (h + 1) * dh)
        v_ext = jnp.concatenate([v_ref[bi, :, hcols], ones], axis=-1)
        kw = (ks[h] * wgt).astype(BF16)
        c_s[st] = jnp.concatenate([decay, decay], axis=-1) * c_s[st] + _dot_tn(kw, v_ext)
        m_s[st] = jnp.broadcast_to(m_new, m_s.shape[1:])


def _mlstm_call(qkm, vm, og, gcol, grow, w_conv, b_conv, g_norm, tri, hmean):
    b, s, ml_w = vm.shape
    L = MLSTM_CHUNK
    nh = N_MLSTM_HEADS
    dh = ml_w // nh
    bb = MLSTM_BATCH_ROWS
    assert b % bb == 0
    const = lambda shape: pl.BlockSpec(shape, lambda g, i: (0,) * len(shape))
    tok = lambda w: pl.BlockSpec((bb, L, w), lambda g, i: (g, i, 0))
    return pl.pallas_call(
        functools.partial(_mlstm_kernel, ml_w=ml_w),
        out_shape=jax.ShapeDtypeStruct((b, s, ml_w), BF16),
        grid=(b // bb, s // L),
        in_specs=[tok(2 * ml_w), tok(ml_w), tok(ml_w), tok(GATE_LANES),
                  pl.BlockSpec((bb, 2 * nh, L), lambda g, i: (g, 0, i)),
                  const(w_conv.shape), const((1, 2 * ml_w)), const((1, ml_w)),
                  const((L, L)), const(hmean.shape)],
        out_specs=tok(ml_w),
        scratch_shapes=[pltpu.VMEM((bb, MOD_ROWS, 2 * ml_w), F32),
                        pltpu.VMEM((bb * nh, dh, 2 * dh), F32),
                        pltpu.VMEM((bb * nh, MOD_ROWS, dh), F32)],
        compiler_params=pltpu.CompilerParams(dimension_semantics=("arbitrary", "arbitrary"),
                                             vmem_limit_bytes=VMEM_LIMIT_BYTES),
        name="mlstm",
    )(qkm, vm, og, gcol, grow, w_conv, b_conv, g_norm, tri, hmean)


def _outffn_kernel(x_ref, mod_ref, g_ref, *rest, dilations):
    n_lay = len(dilations)
    o_refs, l_refs = rest[:n_lay], rest[n_lay:2 * n_lay]
    (hm_ref, ex_ref, wo_ref, wg_ref, wu_ref, wd_ref, out_ref,
     operm_s, lperm_s) = rest[2 * n_lay:]
    tm = x_ref.shape[1]

    def natural(ref, d, scratch):
        if d == 1:
            return ref[0].astype(F32)
        n_cg = scratch.shape[1]
        chain = [step for step in _gather_chain(dilations) if step[1] <= d]
        for level, (parent, dd, f) in enumerate(reversed(chain)):
            dst = scratch.at[level % 2]
            src = scratch.at[(level + 1) % 2]
            n = tm // dd
            for c in range(parent):
                for a in range(f):
                    r = a * parent + c
                    for cg in range(n_cg):
                        if level == 0:
                            c0 = (r * n_cg + cg) * LANES
                            rows = ref[0, :, c0:c0 + LANES].astype(F32)
                        else:
                            rows = src[cg, r * n:(r + 1) * n, :]
                        dst[cg, pl.ds(c * (tm // parent) + a, n, stride=f), :] = rows
        final = scratch.at[(len(chain) - 1) % 2]
        return jnp.concatenate([final[cg] for cg in range(n_cg)], axis=-1)

    lses = [natural(ref, d, lperm_s) for ref, d in zip(l_refs, dilations)]
    mx = functools.reduce(jnp.maximum, lses)
    es = [jnp.exp(l - mx) for l in lses]
    inv = 1.0 / functools.reduce(jnp.add, es)
    ex = ex_ref[...]
    attn = None
    for e, ref, d in zip(es, o_refs, dilations):
        term = _dot((e * inv).astype(BF16), ex) * natural(ref, d, operm_s)
        attn = term if attn is None else attn + term
    aw = attn.shape[-1]
    y = _dot(attn.astype(BF16), wo_ref[:aw, :]) + _dot(hm_ref[0], wo_ref[aw:, :])
    x1 = x_ref[0] + mod_ref[0, 2:3, :] * y
    ms = jnp.mean(x1 * x1, axis=-1, keepdims=True)
    hn = x1 * lax.rsqrt(ms + RMS_EPS) * g_ref[...]
    hb = (hn * (1.0 + mod_ref[0, 4:5, :]) + mod_ref[0, 3:4, :]).astype(BF16)
    g = _dot(hb, wg_ref[...])
    u = _dot(hb, wu_ref[...])
    a = (g * _sigmoid(g) * u).astype(BF16)
    out_ref[0] = x1 + mod_ref[0, 5:6, :] * _dot(a, wd_ref[...])


def _outffn_call(x, mod, g_ffn, os_, lses, hm, expand, wo, wg, wu, wd, *, dilations):
    b, s, d = x.shape
    tm = FFN_ROWS
    assert all(tm % (dl * BF16_SUBLANES) == 0 for dl in dilations)
    aw = expand.shape[1]
    const = lambda shape: pl.BlockSpec(shape, lambda bi, i: (0,) * len(shape))
    tok = lambda w: pl.BlockSpec((1, tm, w), lambda bi, i: (bi, i, 0))
    strided = lambda w: tuple(pl.BlockSpec((1, tm // dl, dl * w), lambda bi, i: (bi, i, 0)) for dl in dilations)
    return pl.pallas_call(
        functools.partial(_outffn_kernel, dilations=dilations),
        out_shape=jax.ShapeDtypeStruct((b, s, d), F32),
        grid=(b, s // tm),
        in_specs=[tok(d), pl.BlockSpec((1, N_MOD, d), lambda bi, i: (bi, 0, 0)), const((1, d)),
                  *strided(aw), *strided(LANES), tok(hm.shape[-1]),
                  const(expand.shape), const(wo.shape),
                  const(wg.shape), const(wu.shape), const(wd.shape)],
        out_specs=tok(d),
        scratch_shapes=[pltpu.VMEM((2, aw // LANES, tm, LANES), F32), pltpu.VMEM((2, 1, tm, LANES), F32)],
        compiler_params=pltpu.CompilerParams(dimension_semantics=("arbitrary", "arbitrary"),
                                             vmem_limit_bytes=VMEM_LIMIT_BYTES),
        name="outffn",
    )(x, mod, g_ffn, *os_, *lses, hm, expand, wo, wg, wu, wd)


def _block_diag_mean(width, group):
    idx = jnp.arange(width) // group
    return jnp.where(idx[:, None] == idx[None, :], 1.0 / group, 0.0).astype(BF16)


def kernel(x, c, g_mix, w_in, w_conv, b_conv, b_igate, b_fgate, q_norm_g, k_norm_g, mlstm_norm_g, w_out,
           g_ffn, w_gate, w_up, w_down, w_ada, b_ada):
    b, s, d = x.shape
    depth = g_mix.shape[0]
    attn_w = d // 2
    ml_w = d - attn_w
    nh = N_MLSTM_HEADS
    n_attn_heads = attn_w // ATTN_HEAD_DIM
    scale = ATTN_HEAD_DIM ** -0.5 * LOG2E
    dilations = tuple(dl for _, dl in DILATED_PATTERNS)
    assert b <= MOD_ROWS and 2 * nh <= GATE_LANES

    c_t = jnp.zeros((d, MOD_ROWS), F32).at[:, :b].set(c.T)
    assert attn_w % MXU_WIDTH == 0 and MXU_WIDTH % ATTN_HEAD_DIM == 0
    hmean_attn = _block_diag_mean(MXU_WIDTH, ATTN_HEAD_DIM)
    assert ml_w % MXU_WIDTH == 0 and MXU_WIDTH % (ml_w // nh) == 0
    hmean_ml = _block_diag_mean(MXU_WIDTH, ml_w // nh)
    stat_lane = jnp.arange(LANES)
    head_of_stat = jnp.where(stat_lane % ATTN_HEAD_DIM < n_attn_heads // 2,
                             2 * (stat_lane % ATTN_HEAD_DIM) + stat_lane // ATTN_HEAD_DIM, -1)
    head_of_col = jnp.arange(attn_w) // ATTN_HEAD_DIM
    expand = (head_of_stat[:, None] == head_of_col[None, :]).astype(BF16)
    tri = jnp.tril(jnp.ones((MLSTM_CHUNK, MLSTM_CHUNK), F32))

    for l in range(depth):
        mod = _mod_call(c_t, w_ada, b_ada[l][None, :], layer=l, n_rows=b)[:b].reshape(b, N_MOD, d)

        n_main = 3 * attn_w + 4 * ml_w
        assert n_main % MXU_WIDTH == 0
        wg_cols = w_in[l][:, n_main:]
        wgt = jnp.zeros((BF16_SUBLANES, d), F32).at[:2 * nh].set(wg_cols.T).astype(BF16)
        brow = jnp.concatenate([b_igate[l], b_fgate[l]])[:, None]
        gq = jnp.tile(q_norm_g[l] * scale, n_attn_heads)[None, :]
        gk = jnp.tile(k_norm_g[l], n_attn_heads)[None, :]

        qs, ks, vs, qkm, vm, og, gcol, grow = _inproj_call(
            x, mod, g_mix[l][None, :], w_in, wgt, gq, gk, hmean_attn, brow,
            layer=l, attn_w=attn_w, ml_w=ml_w, dilations=dilations)

        os_, lses = [], []
        for (window, dilation), q, k, v in zip(DILATED_PATTERNS, qs, ks, vs):
            o, lse = _attn_call(q, k, v, width=attn_w, window=window, dilation=dilation)
            os_.append(o)
            lses.append(lse)

        hm = _mlstm_call(qkm, vm, og, gcol, grow, w_conv[l], b_conv[l][None, :],
                         mlstm_norm_g[l][None, :], tri, hmean_ml)

        x = _outffn_call(x, mod, g_ffn[l][None, :], os_, lses, hm, expand,
                         w_out[l].astype(BF16), w_gate[l].astype(BF16), w_up[l].astype(BF16), w_down[l].astype(BF16),
                         dilations=dilations)
    return x
```

```python
import functools

import jax
import jax.numpy as jnp
from jax import lax
from jax.experimental import pallas as pl
from jax.experimental.pallas import tpu as pltpu

F32 = jnp.float32
BF16 = jnp.bfloat16

ATTN_HEAD_DIM = 64
N_MLSTM_HEADS = 4
CONV_WIDTH = 4
DILATED_PATTERNS = ((128, 1), (512, 4), (2048, 16))
ATTN_BLOCK = 128
N_MOD = 6
RMS_EPS = 1e-6

LANES = 128
MXU_WIDTH = 256
BF16_SUBLANES = 16
VMEM_LIMIT_BYTES = 56 * 1024 * 1024
F32_SUBLANES = 8
MOD_ROWS = F32_SUBLANES
MOD_COL_STEPS = 4
LAYOUT_ROWS = 512
GATE_LANES = 128
INPROJ_ROWS = LAYOUT_ROWS
ATTN_MAX_SUB_BLOCKS = 16
MLSTM_CHUNK = 256
MLSTM_BATCH_ROWS = 1
FFN_ROWS = LAYOUT_ROWS

NEG_INF = float("-inf")
LOG2E = 1.4426950408889634
LN2 = 0.6931471805599453


def _dot(a, b, **kw):
    return jnp.dot(a, b, preferred_element_type=F32, **kw)


def _dot_nt(a, b, **kw):
    return lax.dot_general(a, b, (((1,), (1,)), ((), ())), preferred_element_type=F32, **kw)


def _dot_tn(a, b):
    return lax.dot_general(a, b, (((0,), (0,)), ((), ())), preferred_element_type=F32)


def _sigmoid(z):
    return 1.0 / (1.0 + jnp.exp(-z))


def _log_sigmoid(z):
    return jnp.minimum(z, 0.0) - jnp.log1p(jnp.exp(-jnp.abs(z)))


def _strided_shape(b, s, w, dilation):
    return (b, s // dilation, dilation * w)


def _gather_chain(dilations):
    chain, parent = [], 1
    for d in sorted(set(dilations)):
        if d == 1:
            continue
        assert d % parent == 0 and d // parent < F32_SUBLANES
        chain.append((parent, d, d // parent))
        parent = d
    return chain


def _mod_kernel(ct_ref, w_ref, b_ref, o_ref, *, n_rows):
    ct = ct_ref[...]
    sc = ct * _sigmoid(ct)
    w = w_ref[...]
    rows = [jnp.sum(w * sc[:, r:r + 1], axis=0, keepdims=True) for r in range(n_rows)]
    rows += [jnp.zeros_like(rows[0])] * (MOD_ROWS - n_rows)
    o_ref[...] = jnp.concatenate(rows, axis=0) + b_ref[...]


def _mod_call(c_t, w_ada, b_ada, *, layer, n_rows):
    _, d, n = w_ada.shape
    assert n % (MOD_COL_STEPS * LANES) == 0
    tn = n // MOD_COL_STEPS
    return pl.pallas_call(
        functools.partial(_mod_kernel, n_rows=n_rows),
        out_shape=jax.ShapeDtypeStruct((MOD_ROWS, n), F32),
        grid=(n // tn,),
        in_specs=[pl.BlockSpec((d, MOD_ROWS), lambda j: (0, 0)),
                  pl.BlockSpec((None, d, tn), lambda j: (layer, 0, j)),
                  pl.BlockSpec((1, tn), lambda j: (0, j))],
        out_specs=pl.BlockSpec((MOD_ROWS, tn), lambda j: (0, j)),
        compiler_params=pltpu.CompilerParams(dimension_semantics=("arbitrary",),
                                             vmem_limit_bytes=VMEM_LIMIT_BYTES),
        name="mod",
    )(c_t, w_ada, b_ada)


def _inproj_kernel(x_ref, mod_ref, g_ref, w_ref, wgt_ref, gq_ref, gk_ref, hp_ref,
                   brow_ref, *rest, attn_w, ml_w, dilations):
    n_lay = len(dilations)
    q_refs, k_refs, v_refs = rest[:n_lay], rest[n_lay:2 * n_lay], rest[2 * n_lay:3 * n_lay]
    qkm_ref, vm_ref, og_ref, gcol_ref, grow_ref, perm_s, wb_s = rest[3 * n_lay:]
    tm = x_ref.shape[1]

    @pl.when((pl.program_id(0) == 0) & (pl.program_id(1) == 0))
    def _():
        for c0 in range(0, wb_s.shape[1], MXU_WIDTH):
            wb_s[:, c0:c0 + MXU_WIDTH] = w_ref[:, c0:c0 + MXU_WIDTH].astype(BF16)

    def emit(val, refs):
        w = val.shape[-1]
        n_cg = w // LANES
        ref_of = dict(zip(dilations, refs))
        if 1 in ref_of:
            ref_of[1][0] = val.astype(BF16)
        chain = _gather_chain(dilations)
        for cg in range(n_cg):
            perm_s[0, cg] = val[:, cg * LANES:(cg + 1) * LANES]
        for level, (parent, d, f) in enumerate(chain):
            src, dst = perm_s.at[level % 2], perm_s.at[(level + 1) % 2]
            n = tm // d
            for c in range(parent):
                for a in range(f):
                    r = a * parent + c
                    for cg in range(n_cg):
                        rows = src[cg, pl.ds(c * (tm // parent) + a, n, stride=f), :]
                        c0 = r * w + cg * LANES
                        ref_of[d][0, :, c0:c0 + LANES] = rows.astype(BF16)
                        if level + 1 < len(chain):
                            dst[cg, r * n:(r + 1) * n, :] = rows

    x = x_ref[0]
    ms = jnp.mean(x * x, axis=-1, keepdims=True)
    y = x * lax.rsqrt(ms + RMS_EPS) * g_ref[...]
    h = (y * (1.0 + mod_ref[0, 1:2, :]) + mod_ref[0, 0:1, :]).astype(BF16)

    n_attn = 3 * attn_w
    xa = _dot(h, wb_s[:, :n_attn])
    hp = hp_ref[...]

    def head_norm(t, g):
        t2 = (t * t).astype(BF16)
        gw = hp.shape[0]
        msq = jnp.concatenate([_dot(t2[:, c0:c0 + gw], hp) for c0 in range(0, t.shape[-1], gw)], axis=-1)
        return t * lax.rsqrt(msq + RMS_EPS) * g

    emit(head_norm(xa[:, :attn_w], gq_ref[...]), q_refs)
    emit(head_norm(xa[:, attn_w:2 * attn_w], gk_ref[...]), k_refs)
    emit(xa[:, 2 * attn_w:], v_refs)

    xm = _dot(h, wb_s[:, n_attn:n_attn + 4 * ml_w])
    qkm_ref[0] = xm[:, :2 * ml_w].astype(BF16)
    vm_ref[0] = xm[:, 2 * ml_w:3 * ml_w].astype(BF16)
    og_ref[0] = _sigmoid(xm[:, 3 * ml_w:]).astype(BF16)

    nh = N_MLSTM_HEADS
    zr = _dot_nt(wgt_ref[...], h)[:2 * nh] + brow_ref[...]
    row = lax.broadcasted_iota(jnp.int32, zr.shape, 0)
    gates = jnp.where(row < nh, zr, _log_sigmoid(zr)) * LOG2E
    grow_ref[0] = gates
    padded = jnp.concatenate([gates, jnp.zeros((GATE_LANES - 2 * nh, tm), F32)], axis=0)
    gcol_ref[0] = padded.T


def _inproj_call(x, mod, g_mix, w, wgt, gq, gk, hp, brow, *, layer, attn_w, ml_w, dilations):
    b, s, d = x.shape
    tm = INPROJ_ROWS
    nh2 = 2 * N_MLSTM_HEADS
    assert all(tm % (dl * BF16_SUBLANES) == 0 for dl in dilations)
    const = lambda shape: pl.BlockSpec(shape, lambda bi, i: (0,) * len(shape))
    tok = lambda w: pl.BlockSpec((1, tm, w), lambda bi, i: (bi, i, 0))
    strided_shapes = tuple(jax.ShapeDtypeStruct(_strided_shape(b, s, attn_w, dl), BF16) for dl in dilations)
    strided_specs = tuple(pl.BlockSpec((1, tm // dl, dl * attn_w), lambda bi, i: (bi, i, 0)) for dl in dilations)
    out_shape = strided_shapes * 3 + (
        jax.ShapeDtypeStruct((b, s, 2 * ml_w), BF16),
        jax.ShapeDtypeStruct((b, s, ml_w), BF16),
        jax.ShapeDtypeStruct((b, s, ml_w), BF16),
        jax.ShapeDtypeStruct((b, s, GATE_LANES), F32),
        jax.ShapeDtypeStruct((b, nh2, s), F32))
    outs = pl.pallas_call(
        functools.partial(_inproj_kernel, attn_w=attn_w, ml_w=ml_w, dilations=dilations),
        out_shape=out_shape,
        grid=(b, s // tm),
        in_specs=[tok(d),
                  pl.BlockSpec((1, N_MOD, d), lambda bi, i: (bi, 0, 0)),
                  const((1, d)),
                  pl.BlockSpec((d, w.shape[-1]), lambda bi, i: (layer, 0), pipeline_mode=pl.Buffered(1)),
                  const(wgt.shape),
                  const((1, attn_w)), const((1, attn_w)), const(hp.shape),
                  const((nh2, 1))],
        out_specs=strided_specs * 3 + (tok(2 * ml_w), tok(ml_w), tok(ml_w), tok(GATE_LANES),
                                       pl.BlockSpec((1, nh2, tm), lambda bi, i: (bi, 0, i))),
        scratch_shapes=[pltpu.VMEM((2, attn_w // LANES, tm, LANES), F32),
                        pltpu.VMEM((d, 3 * attn_w + 4 * ml_w), BF16)],
        compiler_params=pltpu.CompilerParams(dimension_semantics=("arbitrary", "arbitrary"),
                                             vmem_limit_bytes=VMEM_LIMIT_BYTES),
        name="inproj",
    )(x, mod, g_mix, w, wgt, gq, gk, hp, brow)
    n_lay = len(dilations)
    return (outs[:n_lay], outs[n_lay:2 * n_lay], outs[2 * n_lay:3 * n_lay]) + tuple(outs[3 * n_lay:])


def _attn_kernel(q_ref, kp_ref, kc_ref, vp_ref, vc_ref, bias_ref, ones_ref, o_ref, lse_ref, *, n_heads):
    blk = ATTN_BLOCK
    sub_blocks = q_ref.shape[1] // blk
    hd = ATTN_HEAD_DIM
    w = n_heads * hd
    n_res = q_ref.shape[2] // w
    n_pairs = n_heads // 2
    n = pl.program_id(2)
    lane = lax.broadcasted_iota(jnp.int32, (blk, LANES), 1)
    first_head = lane < hd

    def key_rows(prev_ref, cur_ref, i, cols):
        if i == 0:
            return jnp.concatenate([prev_ref[0, :, cols], cur_ref[0, 0:blk, cols]], axis=0)
        return cur_ref[0, (i - 1) * blk:(i + 1) * blk, cols]

    units = [(res, i, j) for res in range(n_res) for i in range(sub_blocks) for j in range(n_pairs)]
    scores, maxes, probs = [], [], []
    for res, i, j in units:
        cols = slice(res * w + j * LANES, res * w + (j + 1) * LANES)
        q = q_ref[0, i * blk:(i + 1) * blk, cols]
        ks = key_rows(kp_ref, kc_ref, i, cols)
        bias = bias_ref[jnp.where(n == 0, 1, 0)] if i == 0 else bias_ref[0]
        q2 = jnp.concatenate([jnp.where(first_head, q, 0), jnp.where(first_head, 0, q)], axis=0)
        s = _dot_nt(q2, ks) + bias
        scores.append(s)
        maxes.append(jnp.max(s, axis=-1, keepdims=True))
    for s, m in zip(scores, maxes):
        probs.append(jnp.exp2(s - m).astype(BF16))
    for res in range(n_res):
        for i in range(sub_blocks):
            outs = []
            stats = jnp.zeros((blk, LANES), F32)
            for j in range(n_pairs):
                u = (res * sub_blocks + i) * n_pairs + j
                cols = slice(res * w + j * LANES, res * w + (j + 1) * LANES)
                vw = jnp.concatenate([key_rows(vp_ref, vc_ref, i, cols), ones_ref[...]], axis=1)
                ov = _dot(probs[u], vw)
                num = jnp.where(first_head, ov[:blk, :LANES], ov[blk:, :LANES])
                den = jnp.where(first_head, ov[:blk, LANES:], ov[blk:, LANES:])
                outs.append(num / den)
                m = jnp.broadcast_to(maxes[u], (2 * blk, LANES))
                lse = jnp.where(first_head, m[:blk], m[blk:]) * LN2 + jnp.log(den)
                stats = jnp.where((lane & (hd - 1)) == j, lse, stats)
            rows = slice(i * blk, (i + 1) * blk)
            o_ref[0, rows, res * w:(res + 1) * w] = jnp.concatenate(outs, axis=-1).astype(o_ref.dtype)
            lse_ref[0, rows, res * LANES:(res + 1) * LANES] = stats


def _attn_band_masks(n_back):
    blk = ATTN_BLOCK
    row = jnp.arange(2 * blk)[:, None] % blk
    col = jnp.arange(2 * blk)[None, :]
    band = (col >= row + (blk - n_back)) & (col <= row + blk)
    return jnp.stack([jnp.where(band, 0.0, NEG_INF), jnp.where(band & (col >= blk), 0.0, NEG_INF)]).astype(F32)


def _attn_call(q, k, v, *, width, window, dilation):
    b, ls, _ = q.shape
    w = width
    n_back = window // dilation
    blk = ATTN_BLOCK
    assert n_back <= blk
    sub_blocks = min(ATTN_MAX_SUB_BLOCKS, ls // blk)
    step_rows = sub_blocks * blk
    assert ls % step_rows == 0
    n_heads = w // ATTN_HEAD_DIM
    assert n_heads <= LANES and n_heads % 2 == 0 and 2 * ATTN_HEAD_DIM == LANES
    n_res = max(1, min(dilation, ATTN_MAX_SUB_BLOCKS // sub_blocks))
    assert dilation % n_res == 0
    cur = pl.BlockSpec((1, step_rows, n_res * w), lambda bi, r, n: (bi, n, r))
    prev = pl.BlockSpec((1, blk, n_res * w),
                        lambda bi, r, n: (bi, jnp.maximum(n * sub_blocks - 1, 0), r))
    masks = _attn_band_masks(n_back)
    ones = jnp.ones((2 * blk, LANES), BF16)
    const = lambda shape: pl.BlockSpec(shape, lambda bi, r, n: (0,) * len(shape))
    return pl.pallas_call(
        functools.partial(_attn_kernel, n_heads=n_heads),
        out_shape=(jax.ShapeDtypeStruct((b, ls, dilation * w), BF16),
                   jax.ShapeDtypeStruct((b, ls, dilation * LANES), F32)),
        grid=(b, dilation // n_res, ls // step_rows),
        in_specs=[cur, prev, cur, prev, cur, const(masks.shape), const(ones.shape)],
        out_specs=(cur, pl.BlockSpec((1, step_rows, n_res * LANES), lambda bi, r, n: (bi, n, r))),
        compiler_params=pltpu.CompilerParams(
            dimension_semantics=("arbitrary", "arbitrary", "arbitrary"),
            vmem_limit_bytes=VMEM_LIMIT_BYTES),
        name=f"attn_d{dilation}",
    )(q, k, k, v, v, masks, ones)


def _mlstm_kernel(qk_ref, v_ref, og_ref, gcol_ref, grow_ref, wc_ref, bc_ref, gn_ref, tri_ref, hm_ref,
                  o_ref, tail_s, c_s, m_s, *, ml_w):
    L = MLSTM_CHUNK
    nh = N_MLSTM_HEADS
    dh = ml_w // nh

    @pl.when(pl.program_id(1) == 0)
    def _():
        tail_s[...] = jnp.zeros_like(tail_s)
        c_s[...] = jnp.zeros_like(c_s)
        m_s[...] = jnp.zeros_like(m_s)

    tri = tri_ref[...]
    ti = lax.broadcasted_iota(jnp.int32, (L, L), 0)
    si = lax.broadcasted_iota(jnp.int32, (L, L), 1)
    causal = si <= ti
    ones = jnp.ones((L, dh), BF16)
    hp = hm_ref[...]
    for bi in range(qk_ref.shape[0]):
        _mlstm_chunk(bi, qk_ref, v_ref, og_ref, gcol_ref, grow_ref, wc_ref, bc_ref, gn_ref, o_ref,
                     tail_s, c_s, m_s, tri, causal, ones, hp, ml_w=ml_w)


def _mlstm_chunk(bi, qk_ref, v_ref, og_ref, gcol_ref, grow_ref, wc_ref, bc_ref, gn_ref, o_ref,
                 tail_s, c_s, m_s, tri, causal, ones, hp, *, ml_w):
    L = MLSTM_CHUNK
    nh = N_MLSTM_HEADS
    dh = ml_w // nh
    x = qk_ref[bi].astype(F32)
    tail = tail_s[bi]
    row8 = lax.broadcasted_iota(jnp.int32, tail.shape, 0)
    acc = bc_ref[...] + x * wc_ref[CONV_WIDTH - 1:CONV_WIDTH, :]
    for back in range(1, CONV_WIDTH):
        rolled = pltpu.roll(x, back, 0)
        top = jnp.where(row8 >= back, rolled[:MOD_ROWS], pltpu.roll(tail, back, 0))
        shifted = jnp.concatenate([top, rolled[MOD_ROWS:]], axis=0)
        acc = acc + shifted * wc_ref[CONV_WIDTH - 1 - back:CONV_WIDTH - back, :]
    tail_s[bi] = x[L - MOD_ROWS:]
    qk = acc * _sigmoid(acc)

    gcol = gcol_ref[bi]
    grow = grow_ref[bi]
    bcol_all = _dot(tri, gcol, precision=lax.Precision.HIGHEST)
    brow_all = _dot_nt(grow, tri, precision=lax.Precision.HIGHEST)

    ks = [qk[:, ml_w + h * dh:ml_w + (h + 1) * dh] * (dh ** -0.5) for h in range(nh)]

    heads = []
    for h in range(nh):
        st = bi * nh + h
        b_c = jnp.broadcast_to(bcol_all[:, nh + h:nh + h + 1], (L, dh))
        i_c = jnp.broadcast_to(gcol[:, h:h + 1], (L, dh))
        b_r = brow_all[nh + h:nh + h + 1, :]
        i_r = grow[h:h + 1, :]
        m_prev = m_s[st, 0:1, :]
        log_d = jnp.where(causal, b_c[:, 0:1] + (i_r - b_r), NEG_INF)
        m_inter = b_c + m_prev
        m_t = jnp.maximum(m_inter, jnp.max(log_d, axis=-1, keepdims=True))
        d_mat = jnp.exp2(log_d - m_t[:, 0:1])
        inter = jnp.exp2(m_inter - m_t)
        b_last = b_c[L - 1:L, :]
        w_log = b_last - b_c + i_c
        m_new = jnp.maximum(b_last + m_prev, jnp.max(w_log, axis=0, keepdims=True))
        wgt = jnp.exp2(w_log - m_new)
        decay = jnp.exp2(b_last + m_prev - m_new)
        heads.append((m_t, d_mat, inter, m_new, wgt, decay))

    outs = []
    for h, (m_t, d_mat, inter, _, _, _) in enumerate(heads):
        hcols = slice(h * dh, (h + 1) * dh)
        q = qk[:, hcols].astype(BF16)
        v_ext = jnp.concatenate([v_ref[bi, :, hcols], ones], axis=-1)
        c_prev = c_s[bi * nh + h]
        s_qk = _dot_nt(q, ks[h].astype(BF16)) * d_mat
        ext = jnp.concatenate([inter, inter], axis=-1) * _dot(q, c_prev.astype(BF16)) \
            + _dot(s_qk.astype(BF16), v_ext)
        outs.append(ext[:, :dh] / jnp.maximum(jnp.abs(ext[:, dh:]), jnp.exp2(-m_t)))
    hh = jnp.concatenate(outs, axis=-1)
    h2 = (hh * hh).astype(BF16)
    gw = hp.shape[0]
    msq = jnp.concatenate([_dot(h2[:, c0:c0 + gw], hp) for c0 in range(0, ml_w, gw)], axis=-1)
    hn = hh * lax.rsqrt(msq + RMS_EPS) * gn_ref[...]
    o_ref[bi] = (og_ref[bi].astype(F32) * hn).astype(o_ref.dtype)

    for h, (_, _, _, m_new, wgt, decay) in enumerate(heads):
        st = bi * nh + h
        hcols = slice(h * dh, (h + 1) * dh)
        v_ext = jnp.concatenate([v_ref[bi, :, hcols], ones], axis=-1)
        kw = (ks[h] * wgt).astype(BF16)
        c_s[st] = jnp.concatenate([decay, decay], axis=-1) * c_s[st] + _dot_tn(kw, v_ext)
        m_s[st] = jnp.broadcast_to(m_new, m_s.shape[1:])


def _mlstm_call(qkm, vm, og, gcol, grow, w_conv, b_conv, g_norm, tri, hmean):
    b, s, ml_w = vm.shape
    L = MLSTM_CHUNK
    nh = N_MLSTM_HEADS
    dh = ml_w // nh
    bb = MLSTM_BATCH_ROWS
    assert b % bb == 0
    const = lambda shape: pl.BlockSpec(shape, lambda g, i: (0,) * len(shape))
    tok = lambda w: pl.BlockSpec((bb, L, w), lambda g, i: (g, i, 0))
    return pl.pallas_call(
        functools.partial(_mlstm_kernel, ml_w=ml_w),
        out_shape=jax.ShapeDtypeStruct((b, s, ml_w), BF16),
        grid=(b // bb, s // L),
        in_specs=[tok(2 * ml_w), tok(ml_w), tok(ml_w), tok(GATE_LANES),
                  pl.BlockSpec((bb, 2 * nh, L), lambda g, i: (g, 0, i)),
                  const(w_conv.shape), const((1, 2 * ml_w)), const((1, ml_w)),
                  const((L, L)), const(hmean.shape)],
        out_specs=tok(ml_w),
        scratch_shapes=[pltpu.VMEM((bb, MOD_ROWS, 2 * ml_w), F32),
                        pltpu.VMEM((bb * nh, dh, 2 * dh), F32),
                        pltpu.VMEM((bb * nh, MOD_ROWS, dh), F32)],
        compiler_params=pltpu.CompilerParams(dimension_semantics=("arbitrary", "arbitrary"),
                                             vmem_limit_bytes=VMEM_LIMIT_BYTES),
        name="mlstm",
    )(qkm, vm, og, gcol, grow, w_conv, b_conv, g_norm, tri, hmean)


def _outffn_kernel(x_ref, mod_ref, g_ref, *rest, dilations):
    n_lay = len(dilations)
    o_refs, l_refs = rest[:n_lay], rest[n_lay:2 * n_lay]
    (hm_ref, ex_ref, wo_ref, wg_ref, wu_ref, wd_ref, out_ref,
     operm_s, lperm_s) = rest[2 * n_lay:]
    tm = x_ref.shape[1]

    def natural(ref, d, scratch):
        if d == 1:
            return ref[0].astype(F32)
        n_cg = scratch.shape[1]
        chain = [step for step in _gather_chain(dilations) if step[1] <= d]
        for level, (parent, dd, f) in enumerate(reversed(chain)):
            dst = scratch.at[level % 2]
            src = scratch.at[(level + 1) % 2]
            n = tm // dd
            for c in range(parent):
                for a in range(f):
                    r = a * parent + c
                    for cg in range(n_cg):
                        if level == 0:
                            c0 = (r * n_cg + cg) * LANES
                            rows = ref[0, :, c0:c0 + LANES].astype(F32)
                        else:
                            rows = src[cg, r * n:(r + 1) * n, :]
                        dst[cg, pl.ds(c * (tm // parent) + a, n, stride=f), :] = rows
        final = scratch.at[(len(chain) - 1) % 2]
        return jnp.concatenate([final[cg] for cg in range(n_cg)], axis=-1)

    lses = [natural(ref, d, lperm_s) for ref, d in zip(l_refs, dilations)]
    mx = functools.reduce(jnp.maximum, lses)
    es = [jnp.exp(l - mx) for l in lses]
    inv = 1.0 / functools.reduce(jnp.add, es)
    ex = ex_ref[...]
    attn = None
    for e, ref, d in zip(es, o_refs, dilations):
        term = _dot((e * inv).astype(BF16), ex) * natural(ref, d, operm_s)
        attn = term if attn is None else attn + term
    aw = attn.shape[-1]
    y = _dot(attn.astype(BF16), wo_ref[:aw, :]) + _dot(hm_ref[0], wo_ref[aw:, :])
    x1 = x_ref[0] + mod_ref[0, 2:3, :] * y
    ms = jnp.mean(x1 * x1, axis=-1, keepdims=True)
    hn = x1 * lax.rsqrt(ms + RMS_EPS) * g_ref[...]
    hb = (hn * (1.0 + mod_ref[0, 4:5, :]) + mod_ref[0, 3:4, :]).astype(BF16)
    g = _dot(hb, wg_ref[...])
    u = _dot(hb, wu_ref[...])
    a = (g * _sigmoid(g) * u).astype(BF16)
    out_ref[0] = x1 + mod_ref[0, 5:6, :] * _dot(a, wd_ref[...])


def _outffn_call(x, mod, g_ffn, os_, lses, hm, expand, wo, wg, wu, wd, *, dilations):
    b, s, d = x.shape
    tm = FFN_ROWS
    assert all(tm % (dl * BF16_SUBLANES) == 0 for dl in dilations)
    aw = expand.shape[1]
    const = lambda shape: pl.BlockSpec(shape, lambda bi, i: (0,) * len(shape))
    tok = lambda w: pl.BlockSpec((1, tm, w), lambda bi, i: (bi, i, 0))
    strided = lambda w: tuple(pl.BlockSpec((1, tm // dl, dl * w), lambda bi, i: (bi, i, 0)) for dl in dilations)
    return pl.pallas_call(
        functools.partial(_outffn_kernel, dilations=dilations),
        out_shape=jax.ShapeDtypeStruct((b, s, d), F32),
        grid=(b, s // tm),
        in_specs=[tok(d), pl.BlockSpec((1, N_MOD, d), lambda bi, i: (bi, 0, 0)), const((1, d)),
                  *strided(aw), *strided(LANES), tok(hm.shape[-1]),
                  const(expand.shape), const(wo.shape),
                  const(wg.shape), const(wu.shape), const(wd.shape)],
        out_specs=tok(d),
        scratch_shapes=[pltpu.VMEM((2, aw // LANES, tm, LANES), F32), pltpu.VMEM((2, 1, tm, LANES), F32)],
        compiler_params=pltpu.CompilerParams(dimension_semantics=("arbitrary", "arbitrary"),
                                             vmem_limit_bytes=VMEM_LIMIT_BYTES),
        name="outffn",
    )(x, mod, g_ffn, *os_, *lses, hm, expand, wo, wg, wu, wd)


def _block_diag_mean(width, group):
    idx = jnp.arange(width) // group
    return jnp.where(idx[:, None] == idx[None, :], 1.0 / group, 0.0).astype(BF16)


def kernel(x, c, g_mix, w_in, w_conv, b_conv, b_igate, b_fgate, q_norm_g, k_norm_g, mlstm_norm_g, w_out,
           g_ffn, w_gate, w_up, w_down, w_ada, b_ada):
    b, s, d = x.shape
    depth = g_mix.shape[0]
    attn_w = d // 2
    ml_w = d - attn_w
    nh = N_MLSTM_HEADS
    n_attn_heads = attn_w // ATTN_HEAD_DIM
    scale = ATTN_HEAD_DIM ** -0.5 * LOG2E
    dilations = tuple(dl for _, dl in DILATED_PATTERNS)
    assert b <= MOD_ROWS and 2 * nh <= GATE_LANES

    c_t = jnp.zeros((d, MOD_ROWS), F32).at[:, :b].set(c.T)
    assert attn_w % MXU_WIDTH == 0 and MXU_WIDTH % ATTN_HEAD_DIM == 0
    hmean_attn = _block_diag_mean(MXU_WIDTH, ATTN_HEAD_DIM)
    assert ml_w % MXU_WIDTH == 0 and MXU_WIDTH % (ml_w // nh) == 0
    hmean_ml = _block_diag_mean(MXU_WIDTH, ml_w // nh)
    stat_lane = jnp.arange(LANES)
    head_of_stat = jnp.where(stat_lane % ATTN_HEAD_DIM < n_attn_heads // 2,
                             2 * (stat_lane % ATTN_HEAD_DIM) + stat_lane // ATTN_HEAD_DIM, -1)
    head_of_col = jnp.arange(attn_w) // ATTN_HEAD_DIM
    expand = (head_of_stat[:, None] == head_of_col[None, :]).astype(BF16)
    tri = jnp.tril(jnp.ones((MLSTM_CHUNK, MLSTM_CHUNK), F32))

    for l in range(depth):
        mod = _mod_call(c_t, w_ada, b_ada[l][None, :], layer=l, n_rows=b)[:b].reshape(b, N_MOD, d)

        n_main = 3 * attn_w + 4 * ml_w
        assert n_main % MXU_WIDTH == 0
        wg_cols = w_in[l][:, n_main:]
        wgt = jnp.zeros((BF16_SUBLANES, d), F32).at[:2 * nh].set(wg_cols.T).astype(BF16)
        brow = jnp.concatenate([b_igate[l], b_fgate[l]])[:, None]
        gq = jnp.tile(q_norm_g[l] * scale, n_attn_heads)[None, :]
        gk = jnp.tile(k_norm_g[l], n_attn_heads)[None, :]

        qs, ks, vs, qkm, vm, og, gcol, grow = _inproj_call(
            x, mod, g_mix[l][None, :], w_in.reshape(depth * d, -1), wgt, gq, gk, hmean_attn, brow,
            layer=l, attn_w=attn_w, ml_w=ml_w, dilations=dilations)

        os_, lses = [], []
        for (window, dilation), q, k, v in zip(DILATED_PATTERNS, qs, ks, vs):
            o, lse = _attn_call(q, k, v, width=attn_w, window=window, dilation=dilation)
            os_.append(o)
            lses.append(lse)

        hm = _mlstm_call(qkm, vm, og, gcol, grow, w_conv[l], b_conv[l][None, :],
                         mlstm_norm_g[l][None, :], tri, hmean_ml)

        x = _outffn_call(x, mod, g_ffn[l][None, :], os_, lses, hm, expand,
                         w_out[l].astype(BF16), w_gate[l].astype(BF16), w_up[l].astype(BF16), w_down[l].astype(BF16),
                         dilations=dilations)
    return x
```

```python
import functools

import jax
import jax.numpy as jnp
from jax import lax
from jax.experimental import pallas as pl
from jax.experimental.pallas import tpu as pltpu

F32 = jnp.float32
BF16 = jnp.bfloat16

ATTN_HEAD_DIM = 64
N_MLSTM_HEADS = 4
CONV_WIDTH = 4
DILATED_PATTERNS = ((128, 1), (512, 4), (2048, 16))
ATTN_BLOCK = 128
N_MOD = 6
RMS_EPS = 1e-6

LANES = 128
MXU_WIDTH = 256
BF16_SUBLANES = 16
VMEM_LIMIT_BYTES = 56 * 1024 * 1024
F32_SUBLANES = 8
MOD_ROWS = F32_SUBLANES
MOD_COL_STEPS = 4
MOD_DMA_STREAMS = 2
LAYOUT_ROWS = 512
GATE_LANES = 128
INPROJ_ROWS = LAYOUT_ROWS
ATTN_MAX_SUB_BLOCKS = 16
MLSTM_CHUNK = 256
MLSTM_BATCH_ROWS = 1
FFN_ROWS = LAYOUT_ROWS

NEG_INF = float("-inf")
LOG2E = 1.4426950408889634
LN2 = 0.6931471805599453


def _dot(a, b, **kw):
    return jnp.dot(a, b, preferred_element_type=F32, **kw)


def _dot_nt(a, b, **kw):
    return lax.dot_general(a, b, (((1,), (1,)), ((), ())), preferred_element_type=F32, **kw)


def _dot_tn(a, b):
    return lax.dot_general(a, b, (((0,), (0,)), ((), ())), preferred_element_type=F32)


def _sigmoid(z):
    return 1.0 / (1.0 + jnp.exp(-z))


def _log_sigmoid(z):
    return jnp.minimum(z, 0.0) - jnp.log1p(jnp.exp(-jnp.abs(z)))


def _strided_shape(b, s, w, dilation):
    return (b, s // dilation, dilation * w)


def _gather_chain(dilations):
    chain, parent = [], 1
    for d in sorted(set(dilations)):
        if d == 1:
            continue
        assert d % parent == 0 and d // parent < F32_SUBLANES
        chain.append((parent, d, d // parent))
        parent = d
    return chain


def _mod_kernel(ct_ref, *rest, n_rows):
    w_refs, (b_ref, o_ref) = rest[:-2], rest[-2:]
    ct = ct_ref[...]
    sc = ct * _sigmoid(ct)
    for k, w_ref in enumerate(w_refs):
        w = w_ref[...]
        tw = w.shape[1]
        rows = [jnp.sum(w * sc[:, r:r + 1], axis=0, keepdims=True) for r in range(n_rows)]
        rows += [jnp.zeros_like(rows[0])] * (MOD_ROWS - n_rows)
        o_ref[:, k * tw:(k + 1) * tw] = jnp.concatenate(rows, axis=0) + b_ref[:, k * tw:(k + 1) * tw]


def _mod_call(c_t, w_ada, b_ada, *, layer, n_rows):
    _, d, n = w_ada.shape
    ns = MOD_DMA_STREAMS
    assert n % (MOD_COL_STEPS * ns * LANES) == 0
    tn = n // MOD_COL_STEPS
    tw = tn // ns
    w_specs = [pl.BlockSpec((None, d, tw), functools.partial(lambda j, k: (layer, 0, ns * j + k), k=k))
               for k in range(ns)]
    return pl.pallas_call(
        functools.partial(_mod_kernel, n_rows=n_rows),
        out_shape=jax.ShapeDtypeStruct((MOD_ROWS, n), F32),
        grid=(n // tn,),
        in_specs=[pl.BlockSpec((d, MOD_ROWS), lambda j: (0, 0)), *w_specs,
                  pl.BlockSpec((1, tn), lambda j: (0, j))],
        out_specs=pl.BlockSpec((MOD_ROWS, tn), lambda j: (0, j)),
        compiler_params=pltpu.CompilerParams(dimension_semantics=("arbitrary",),
                                             vmem_limit_bytes=VMEM_LIMIT_BYTES),
        name="mod",
    )(c_t, *([w_ada] * ns), b_ada)


def _inproj_kernel(x_ref, mod_ref, g_ref, w_ref, wgt_ref, gq_ref, gk_ref, hp_ref,
                   brow_ref, *rest, attn_w, ml_w, dilations):
    n_lay = len(dilations)
    q_refs, k_refs, v_refs = rest[:n_lay], rest[n_lay:2 * n_lay], rest[2 * n_lay:3 * n_lay]
    qkm_ref, vm_ref, og_ref, gcol_ref, grow_ref, perm_s, wb_s = rest[3 * n_lay:]
    tm = x_ref.shape[1]

    @pl.when((pl.program_id(0) == 0) & (pl.program_id(1) == 0))
    def _():
        for c0 in range(0, wb_s.shape[1], MXU_WIDTH):
            wb_s[:, c0:c0 + MXU_WIDTH] = w_ref[:, c0:c0 + MXU_WIDTH].astype(BF16)

    def emit(val, refs):
        w = val.shape[-1]
        n_cg = w // LANES
        ref_of = dict(zip(dilations, refs))
        if 1 in ref_of:
            ref_of[1][0] = val.astype(BF16)
        chain = _gather_chain(dilations)
        for cg in range(n_cg):
            perm_s[0, cg] = val[:, cg * LANES:(cg + 1) * LANES]
        for level, (parent, d, f) in enumerate(chain):
            src, dst = perm_s.at[level % 2], perm_s.at[(level + 1) % 2]
            n = tm // d
            for c in range(parent):
                for a in range(f):
                    r = a * parent + c
                    for cg in range(n_cg):
                        rows = src[cg, pl.ds(c * (tm // parent) + a, n, stride=f), :]
                        c0 = r * w + cg * LANES
                        ref_of[d][0, :, c0:c0 + LANES] = rows.astype(BF16)
                        if level + 1 < len(chain):
                            dst[cg, r * n:(r + 1) * n, :] = rows

    x = x_ref[0]
    ms = jnp.mean(x * x, axis=-1, keepdims=True)
    y = x * lax.rsqrt(ms + RMS_EPS) * g_ref[...]
    h = (y * (1.0 + mod_ref[0, 1:2, :]) + mod_ref[0, 0:1, :]).astype(BF16)

    n_attn = 3 * attn_w
    xa = _dot(h, wb_s[:, :n_attn])
    hp = hp_ref[...]

    def head_norm(t, g):
        t2 = (t * t).astype(BF16)
        gw = hp.shape[0]
        msq = jnp.concatenate([_dot(t2[:, c0:c0 + gw], hp) for c0 in range(0, t.shape[-1], gw)], axis=-1)
        return t * lax.rsqrt(msq + RMS_EPS) * g

    emit(head_norm(xa[:, :attn_w], gq_ref[...]), q_refs)
    emit(head_norm(xa[:, attn_w:2 * attn_w], gk_ref[...]), k_refs)
    emit(xa[:, 2 * attn_w:], v_refs)

    xm = _dot(h, wb_s[:, n_attn:n_attn + 4 * ml_w])
    qkm_ref[0] = xm[:, :2 * ml_w].astype(BF16)
    vm_ref[0] = xm[:, 2 * ml_w:3 * ml_w].astype(BF16)
    og_ref[0] = _sigmoid(xm[:, 3 * ml_w:]).astype(BF16)

    nh = N_MLSTM_HEADS
    zr = _dot_nt(wgt_ref[...], h)[:2 * nh] + brow_ref[...]
    row = lax.broadcasted_iota(jnp.int32, zr.shape, 0)
    gates = jnp.where(row < nh, zr, _log_sigmoid(zr)) * LOG2E
    grow_ref[0] = gates
    padded = jnp.concatenate([gates, jnp.zeros((GATE_LANES - 2 * nh, tm), F32)], axis=0)
    gcol_ref[0] = padded.T


def _inproj_call(x, mod, g_mix, w, wgt, gq, gk, hp, brow, *, layer, attn_w, ml_w, dilations):
    b, s, d = x.shape
    tm = INPROJ_ROWS
    nh2 = 2 * N_MLSTM_HEADS
    assert all(tm % (dl * BF16_SUBLANES) == 0 for dl in dilations)
    const = lambda shape: pl.BlockSpec(shape, lambda bi, i: (0,) * len(shape))
    tok = lambda w: pl.BlockSpec((1, tm, w), lambda bi, i: (bi, i, 0))
    strided_shapes = tuple(jax.ShapeDtypeStruct(_strided_shape(b, s, attn_w, dl), BF16) for dl in dilations)
    strided_specs = tuple(pl.BlockSpec((1, tm // dl, dl * attn_w), lambda bi, i: (bi, i, 0)) for dl in dilations)
    out_shape = strided_shapes * 3 + (
        jax.ShapeDtypeStruct((b, s, 2 * ml_w), BF16),
        jax.ShapeDtypeStruct((b, s, ml_w), BF16),
        jax.ShapeDtypeStruct((b, s, ml_w), BF16),
        jax.ShapeDtypeStruct((b, s, GATE_LANES), F32),
        jax.ShapeDtypeStruct((b, nh2, s), F32))
    outs = pl.pallas_call(
        functools.partial(_inproj_kernel, attn_w=attn_w, ml_w=ml_w, dilations=dilations),
        out_shape=out_shape,
        grid=(b, s // tm),
        in_specs=[tok(d),
                  pl.BlockSpec((1, N_MOD, d), lambda bi, i: (bi, 0, 0)),
                  const((1, d)),
                  pl.BlockSpec((d, w.shape[-1]), lambda bi, i: (layer, 0), pipeline_mode=pl.Buffered(1)),
                  const(wgt.shape),
                  const((1, attn_w)), const((1, attn_w)), const(hp.shape),
                  const((nh2, 1))],
        out_specs=strided_specs * 3 + (tok(2 * ml_w), tok(ml_w), tok(ml_w), tok(GATE_LANES),
                                       pl.BlockSpec((1, nh2, tm), lambda bi, i: (bi, 0, i))),
        scratch_shapes=[pltpu.VMEM((2, attn_w // LANES, tm, LANES), F32),
                        pltpu.VMEM((d, 3 * attn_w + 4 * ml_w), BF16)],
        compiler_params=pltpu.CompilerParams(dimension_semantics=("arbitrary", "arbitrary"),
                                             vmem_limit_bytes=VMEM_LIMIT_BYTES),
        name="inproj",
    )(x, mod, g_mix, w, wgt, gq, gk, hp, brow)
    n_lay = len(dilations)
    return (outs[:n_lay], outs[n_lay:2 * n_lay], outs[2 * n_lay:3 * n_lay]) + tuple(outs[3 * n_lay:])


def _attn_kernel(q_ref, kp_ref, kc_ref, vp_ref, vc_ref, bias_ref, ones_ref, o_ref, lse_ref, *, n_heads):
    blk = ATTN_BLOCK
    sub_blocks = q_ref.shape[1] // blk
    hd = ATTN_HEAD_DIM
    w = n_heads * hd
    n_res = q_ref.shape[2] // w
    n_pairs = n_heads // 2
    n = pl.program_id(2)
    lane = lax.broadcasted_iota(jnp.int32, (blk, LANES), 1)
    first_head = lane < hd

    def key_rows(prev_ref, cur_ref, i, cols):
        if i == 0:
            return jnp.concatenate([prev_ref[0, :, cols], cur_ref[0, 0:blk, cols]], axis=0)
        return cur_ref[0, (i - 1) * blk:(i + 1) * blk, cols]

    units = [(res, i, j) for res in range(n_res) for i in range(sub_blocks) for j in range(n_pairs)]
    scores, maxes, probs = [], [], []
    for res, i, j in units:
        cols = slice(res * w + j * LANES, res * w + (j + 1) * LANES)
        q = q_ref[0, i * blk:(i + 1) * blk, cols]
        ks = key_rows(kp_ref, kc_ref, i, cols)
        bias = bias_ref[jnp.where(n == 0, 1, 0)] if i == 0 else bias_ref[0]
        q2 = jnp.concatenate([jnp.where(first_head, q, 0), jnp.where(first_head, 0, q)], axis=0)
        s = _dot_nt(q2, ks) + bias
        scores.append(s)
        maxes.append(jnp.max(s, axis=-1, keepdims=True))
    for s, m in zip(scores, maxes):
        probs.append(jnp.exp2(s - m).astype(BF16))
    for res in range(n_res):
        for i in range(sub_blocks):
            outs = []
            stats = jnp.zeros((blk, LANES), F32)
            for j in range(n_pairs):
                u = (res * sub_blocks + i) * n_pairs + j
                cols = slice(res * w + j * LANES, res * w + (j + 1) * LANES)
                vw = jnp.concatenate([key_rows(vp_ref, vc_ref, i, cols), ones_ref[...]], axis=1)
                ov = _dot(probs[u], vw)
                num = jnp.where(first_head, ov[:blk, :LANES], ov[blk:, :LANES])
                den = jnp.where(first_head, ov[:blk, LANES:], ov[blk:, LANES:])
                outs.append(num / den)
                m = jnp.broadcast_to(maxes[u], (2 * blk, LANES))
                lse = jnp.where(first_head, m[:blk], m[blk:]) * LN2 + jnp.log(den)
                stats = jnp.where((lane & (hd - 1)) == j, lse, stats)
            rows = slice(i * blk, (i + 1) * blk)
            o_ref[0, rows, res * w:(res + 1) * w] = jnp.concatenate(outs, axis=-1).astype(o_ref.dtype)
            lse_ref[0, rows, res * LANES:(res + 1) * LANES] = stats


def _attn_band_masks(n_back):
    blk = ATTN_BLOCK
    row = jnp.arange(2 * blk)[:, None] % blk
    col = jnp.arange(2 * blk)[None, :]
    band = (col >= row + (blk - n_back)) & (col <= row + blk)
    return jnp.stack([jnp.where(band, 0.0, NEG_INF), jnp.where(band & (col >= blk), 0.0, NEG_INF)]).astype(F32)


def _attn_call(q, k, v, *, width, window, dilation):
    b, ls, _ = q.shape
    w = width
    n_back = window // dilation
    blk = ATTN_BLOCK
    assert n_back <= blk
    sub_blocks = min(ATTN_MAX_SUB_BLOCKS, ls // blk)
    step_rows = sub_blocks * blk
    assert ls % step_rows == 0
    n_heads = w // ATTN_HEAD_DIM
    assert n_heads <= LANES and n_heads % 2 == 0 and 2 * ATTN_HEAD_DIM == LANES
    n_res = max(1, min(dilation, ATTN_MAX_SUB_BLOCKS // sub_blocks))
    assert dilation % n_res == 0
    cur = pl.BlockSpec((1, step_rows, n_res * w), lambda bi, r, n: (bi, n, r))
    prev = pl.BlockSpec((1, blk, n_res * w),
                        lambda bi, r, n: (bi, jnp.maximum(n * sub_blocks - 1, 0), r))
    masks = _attn_band_masks(n_back)
    ones = jnp.ones((2 * blk, LANES), BF16)
    const = lambda shape: pl.BlockSpec(shape, lambda bi, r, n: (0,) * len(shape))
    return pl.pallas_call(
        functools.partial(_attn_kernel, n_heads=n_heads),
        out_shape=(jax.ShapeDtypeStruct((b, ls, dilation * w), BF16),
                   jax.ShapeDtypeStruct((b, ls, dilation * LANES), F32)),
        grid=(b, dilation // n_res, ls // step_rows),
        in_specs=[cur, prev, cur, prev, cur, const(masks.shape), const(ones.shape)],
        out_specs=(cur, pl.BlockSpec((1, step_rows, n_res * LANES), lambda bi, r, n: (bi, n, r))),
        compiler_params=pltpu.CompilerParams(
            dimension_semantics=("arbitrary", "arbitrary", "arbitrary"),
            vmem_limit_bytes=VMEM_LIMIT_BYTES),
        name=f"attn_d{dilation}",
    )(q, k, k, v, v, masks, ones)


def _mlstm_kernel(qk_ref, v_ref, og_ref, gcol_ref, grow_ref, wc_ref, bc_ref, gn_ref, tri_ref, hm_ref,
                  o_ref, tail_s, c_s, m_s, *, ml_w):
    L = MLSTM_CHUNK
    nh = N_MLSTM_HEADS
    dh = ml_w // nh

    @pl.when(pl.program_id(1) == 0)
    def _():
        tail_s[...] = jnp.zeros_like(tail_s)
        c_s[...] = jnp.zeros_like(c_s)
        m_s[...] = jnp.zeros_like(m_s)

    tri = tri_ref[...]
    ti = lax.broadcasted_iota(jnp.int32, (L, L), 0)
    si = lax.broadcasted_iota(jnp.int32, (L, L), 1)
    causal = si <= ti
    ones = jnp.ones((L, dh), BF16)
    hp = hm_ref[...]
    for bi in range(qk_ref.shape[0]):
        _mlstm_chunk(bi, qk_ref, v_ref, og_ref, gcol_ref, grow_ref, wc_ref, bc_ref, gn_ref, o_ref,
                     tail_s, c_s, m_s, tri, causal, ones, hp, ml_w=ml_w)


def _mlstm_chunk(bi, qk_ref, v_ref, og_ref, gcol_ref, grow_ref, wc_ref, bc_ref, gn_ref, o_ref,
                 tail_s, c_s, m_s, tri, causal, ones, hp, *, ml_w):
    L = MLSTM_CHUNK
    nh = N_MLSTM_HEADS
    dh = ml_w // nh
    x = qk_ref[bi].astype(F32)
    tail = tail_s[bi]
    row8 = lax.broadcasted_iota(jnp.int32, tail.shape, 0)
    acc = bc_ref[...] + x * wc_ref[CONV_WIDTH - 1:CONV_WIDTH, :]
    for back in range(1, CONV_WIDTH):
        rolled = pltpu.roll(x, back, 0)
        top = jnp.where(row8 >= back, rolled[:MOD_ROWS], pltpu.roll(tail, back, 0))
        shifted = jnp.concatenate([top, rolled[MOD_ROWS:]], axis=0)
        acc = acc + shifted * wc_ref[CONV_WIDTH - 1 - back:CONV_WIDTH - back, :]
    tail_s[bi] = x[L - MOD_ROWS:]
    qk = acc * _sigmoid(acc)

    gcol = gcol_ref[bi]
    grow = grow_ref[bi]
    bcol_all = _dot(tri, gcol, precision=lax.Precision.HIGHEST)
    brow_all = _dot_nt(grow, tri, precision=lax.Precision.HIGHEST)

    ks = [qk[:, ml_w + h * dh:ml_w + (h + 1) * dh] * (dh ** -0.5) for h in range(nh)]

    heads = []
    for h in range(nh):
        st = bi * nh + h
        b_c = jnp.broadcast_to(bcol_all[:, nh + h:nh + h + 1], (L, dh))
        i_c = jnp.broadcast_to(gcol[:, h:h + 1], (L, dh))
        b_r = brow_all[nh + h:nh + h + 1, :]
        i_r = grow[h:h + 1, :]
        m_prev = m_s[st, 0:1, :]
        log_d = jnp.where(causal, b_c[:, 0:1] + (i_r - b_r), NEG_INF)
        m_inter = b_c + m_prev
        m_t = jnp.maximum(m_inter, jnp.max(log_d, axis=-1, keepdims=True))
        d_mat = jnp.exp2(log_d - m_t[:, 0:1])
        inter = jnp.exp2(m_inter - m_t)
        b_last = b_c[L - 1:L, :]
        w_log = b_last - b_c + i_c
        m_new = jnp.maximum(b_last + m_prev, jnp.max(w_log, axis=0, keepdims=True))
        wgt = jnp.exp2(w_log - m_new)
        decay = jnp.exp2(b_last + m_prev - m_new)
        heads.append((m_t, d_mat, inter, m_new, wgt, decay))

    outs = []
    for h, (m_t, d_mat, inter, _, _, _) in enumerate(heads):
        hcols = slice(h * dh, (h + 1) * dh)
        q = qk[:, hcols].astype(BF16)
        v_ext = jnp.concatenate([v_ref[bi, :, hcols], ones], axis=-1)
        c_prev = c_s[bi * nh + h]
        s_qk = _dot_nt(q, ks[h].astype(BF16)) * d_mat
        ext = jnp.concatenate([inter, inter], axis=-1) * _dot(q, c_prev.astype(BF16)) \
            + _dot(s_qk.astype(BF16), v_ext)
        outs.append(ext[:, :dh] / jnp.maximum(jnp.abs(ext[:, dh:]), jnp.exp2(-m_t)))
    hh = jnp.concatenate(outs, axis=-1)
    h2 = (hh * hh).astype(BF16)
    gw = hp.shape[0]
    msq = jnp.concatenate([_dot(h2[:, c0:c0 + gw], hp) for c0 in range(0, ml_w, gw)], axis=-1)
    hn = hh * lax.rsqrt(msq + RMS_EPS) * gn_ref[...]
    o_ref[bi] = (og_ref[bi].astype(F32) * hn).astype(o_ref.dtype)

    for h, (_, _, _, m_new, wgt, decay) in enumerate(heads):
        st = bi * nh + h
        hcols = slice(h * dh, (h + 1) * dh)
        v_ext = jnp.concatenate([v_ref[bi, :, hcols], ones], axis=-1)
        kw = (ks[h] * wgt).astype(BF16)
        c_s[st] = jnp.concatenate([decay, decay], axis=-1) * c_s[st] + _dot_tn(kw, v_ext)
        m_s[st] = jnp.broadcast_to(m_new, m_s.shape[1:])


def _mlstm_call(qkm, vm, og, gcol, grow, w_conv, b_conv, g_norm, tri, hmean):
    b, s, ml_w = vm.shape
    L = MLSTM_CHUNK
    nh = N_MLSTM_HEADS
    dh = ml_w // nh
    bb = MLSTM_BATCH_ROWS
    assert b % bb == 0
    const = lambda shape: pl.BlockSpec(shape, lambda g, i: (0,) * len(shape))
    tok = lambda w: pl.BlockSpec((bb, L, w), lambda g, i: (g, i, 0))
    return pl.pallas_call(
        functools.partial(_mlstm_kernel, ml_w=ml_w),
        out_shape=jax.ShapeDtypeStruct((b, s, ml_w), BF16),
        grid=(b // bb, s // L),
        in_specs=[tok(2 * ml_w), tok(ml_w), tok(ml_w), tok(GATE_LANES),
                  pl.BlockSpec((bb, 2 * nh, L), lambda g, i: (g, 0, i)),
                  const(w_conv.shape), const((1, 2 * ml_w)), const((1, ml_w)),
                  const((L, L)), const(hmean.shape)],
        out_specs=tok(ml_w),
        scratch_shapes=[pltpu.VMEM((bb, MOD_ROWS, 2 * ml_w), F32),
                        pltpu.VMEM((bb * nh, dh, 2 * dh), F32),
                        pltpu.VMEM((bb * nh, MOD_ROWS, dh), F32)],
        compiler_params=pltpu.CompilerParams(dimension_semantics=("arbitrary", "arbitrary"),
                                             vmem_limit_bytes=VMEM_LIMIT_BYTES),
        name="mlstm",
    )(qkm, vm, og, gcol, grow, w_conv, b_conv, g_norm, tri, hmean)


def _outffn_kernel(x_ref, mod_ref, g_ref, *rest, dilations):
    n_lay = len(dilations)
    o_refs, l_refs = rest[:n_lay], rest[n_lay:2 * n_lay]
    (hm_ref, ex_ref, wo_ref, wg_ref, wu_ref, wd_ref, out_ref,
     operm_s, lperm_s) = rest[2 * n_lay:]
    tm = x_ref.shape[1]

    def natural(ref, d, scratch):
        if d == 1:
            return ref[0].astype(F32)
        n_cg = scratch.shape[1]
        chain = [step for step in _gather_chain(dilations) if step[1] <= d]
        for level, (parent, dd, f) in enumerate(reversed(chain)):
            dst = scratch.at[level % 2]
            src = scratch.at[(level + 1) % 2]
            n = tm // dd
            for c in range(parent):
                for a in range(f):
                    r = a * parent + c
                    for cg in range(n_cg):
                        if level == 0:
                            c0 = (r * n_cg + cg) * LANES
                            rows = ref[0, :, c0:c0 + LANES].astype(F32)
                        else:
                            rows = src[cg, r * n:(r + 1) * n, :]
                        dst[cg, pl.ds(c * (tm // parent) + a, n, stride=f), :] = rows
        final = scratch.at[(len(chain) - 1) % 2]
        return jnp.concatenate([final[cg] for cg in range(n_cg)], axis=-1)

    lses = [natural(ref, d, lperm_s) for ref, d in zip(l_refs, dilations)]
    mx = functools.reduce(jnp.maximum, lses)
    es = [jnp.exp(l - mx) for l in lses]
    inv = 1.0 / functools.reduce(jnp.add, es)
    ex = ex_ref[...]
    attn = None
    for e, ref, d in zip(es, o_refs, dilations):
        term = _dot((e * inv).astype(BF16), ex) * natural(ref, d, operm_s)
        attn = term if attn is None else attn + term
    aw = attn.shape[-1]
    y = _dot(attn.astype(BF16), wo_ref[:aw, :]) + _dot(hm_ref[0], wo_ref[aw:, :])
    x1 = x_ref[0] + mod_ref[0, 2:3, :] * y
    ms = jnp.mean(x1 * x1, axis=-1, keepdims=True)
    hn = x1 * lax.rsqrt(ms + RMS_EPS) * g_ref[...]
    hb = (hn * (1.0 + mod_ref[0, 4:5, :]) + mod_ref[0, 3:4, :]).astype(BF16)
    g = _dot(hb, wg_ref[...])
    u = _dot(hb, wu_ref[...])
    a = (g * _sigmoid(g) * u).astype(BF16)
    out_ref[0] = x1 + mod_ref[0, 5:6, :] * _dot(a, wd_ref[...])


def _outffn_call(x, mod, g_ffn, os_, lses, hm, expand, wo, wg, wu, wd, *, dilations):
    b, s, d = x.shape
    tm = FFN_ROWS
    assert all(tm % (dl * BF16_SUBLANES) == 0 for dl in dilations)
    aw = expand.shape[1]
    const = lambda shape: pl.BlockSpec(shape, lambda bi, i: (0,) * len(shape))
    tok = lambda w: pl.BlockSpec((1, tm, w), lambda bi, i: (bi, i, 0))
    strided = lambda w: tuple(pl.BlockSpec((1, tm // dl, dl * w), lambda bi, i: (bi, i, 0)) for dl in dilations)
    return pl.pallas_call(
        functools.partial(_outffn_kernel, dilations=dilations),
        out_shape=jax.ShapeDtypeStruct((b, s, d), F32),
        grid=(b, s // tm),
        in_specs=[tok(d), pl.BlockSpec((1, N_MOD, d), lambda bi, i: (bi, 0, 0)), const((1, d)),
                  *strided(aw), *strided(LANES), tok(hm.shape[-1]),
                  const(expand.shape), const(wo.shape),
                  const(wg.shape), const(wu.shape), const(wd.shape)],
        out_specs=tok(d),
        scratch_shapes=[pltpu.VMEM((2, aw // LANES, tm, LANES), F32), pltpu.VMEM((2, 1, tm, LANES), F32)],
        compiler_params=pltpu.CompilerParams(dimension_semantics=("arbitrary", "arbitrary"),
                                             vmem_limit_bytes=VMEM_LIMIT_BYTES),
        name="outffn",
    )(x, mod, g_ffn, *os_, *lses, hm, expand, wo, wg, wu, wd)


def _block_diag_mean(width, group):
    idx = jnp.arange(width) // group
    return jnp.where(idx[:, None] == idx[None, :], 1.0 / group, 0.0).astype(BF16)


def kernel(x, c, g_mix, w_in, w_conv, b_conv, b_igate, b_fgate, q_norm_g, k_norm_g, mlstm_norm_g, w_out,
           g_ffn, w_gate, w_up, w_down, w_ada, b_ada):
    b, s, d = x.shape
    depth = g_mix.shape[0]
    attn_w = d // 2
    ml_w = d - attn_w
    nh = N_MLSTM_HEADS
    n_attn_heads = attn_w // ATTN_HEAD_DIM
    scale = ATTN_HEAD_DIM ** -0.5 * LOG2E
    dilations = tuple(dl for _, dl in DILATED_PATTERNS)
    assert b <= MOD_ROWS and 2 * nh <= GATE_LANES

    c_t = jnp.zeros((d, MOD_ROWS), F32).at[:, :b].set(c.T)
    assert attn_w % MXU_WIDTH == 0 and MXU_WIDTH % ATTN_HEAD_DIM == 0
    hmean_attn = _block_diag_mean(MXU_WIDTH, ATTN_HEAD_DIM)
    assert ml_w % MXU_WIDTH == 0 and MXU_WIDTH % (ml_w // nh) == 0
    hmean_ml = _block_diag_mean(MXU_WIDTH, ml_w // nh)
    stat_lane = jnp.arange(LANES)
    head_of_stat = jnp.where(stat_lane % ATTN_HEAD_DIM < n_attn_heads // 2,
                             2 * (stat_lane % ATTN_HEAD_DIM) + stat_lane // ATTN_HEAD_DIM, -1)
    head_of_col = jnp.arange(attn_w) // ATTN_HEAD_DIM
    expand = (head_of_stat[:, None] == head_of_col[None, :]).astype(BF16)
    tri = jnp.tril(jnp.ones((MLSTM_CHUNK, MLSTM_CHUNK), F32))

    for l in range(depth):
        mod = _mod_call(c_t, w_ada, b_ada[l][None, :], layer=l, n_rows=b)[:b].reshape(b, N_MOD, d)

        n_main = 3 * attn_w + 4 * ml_w
        assert n_main % MXU_WIDTH == 0
        wg_cols = w_in[l][:, n_main:]
        wgt = jnp.zeros((BF16_SUBLANES, d), F32).at[:2 * nh].set(wg_cols.T).astype(BF16)
        brow = jnp.concatenate([b_igate[l], b_fgate[l]])[:, None]
        gq = jnp.tile(q_norm_g[l] * scale, n_attn_heads)[None, :]
        gk = jnp.tile(k_norm_g[l], n_attn_heads)[None, :]

        qs, ks, vs, qkm, vm, og, gcol, grow = _inproj_call(
            x, mod, g_mix[l][None, :], w_in.reshape(depth * d, -1), wgt, gq, gk, hmean_attn, brow,
            layer=l, attn_w=attn_w, ml_w=ml_w, dilations=dilations)

        os_, lses = [], []
        for (window, dilation), q, k, v in zip(DILATED_PATTERNS, qs, ks, vs):
            o, lse = _attn_call(q, k, v, width=attn_w, window=window, dilation=dilation)
            os_.append(o)
            lses.append(lse)

        hm = _mlstm_call(qkm, vm, og, gcol, grow, w_conv[l], b_conv[l][None, :],
                         mlstm_norm_g[l][None, :], tri, hmean_ml)

        x = _outffn_call(x, mod, g_ffn[l][None, :], os_, lses, hm, expand,
                         w_out[l].astype(BF16), w_gate[l].astype(BF16), w_up[l].astype(BF16), w_down[l].astype(BF16),
                         dilations=dilations)
    return x
```

```python
import functools

import jax
import jax.numpy as jnp
from jax import lax
from jax.experimental import pallas as pl
from jax.experimental.pallas import tpu as pltpu

F32 = jnp.float32
BF16 = jnp.bfloat16

ATTN_HEAD_DIM = 64
N_MLSTM_HEADS = 4
CONV_WIDTH = 4
DILATED_PATTERNS = ((128, 1), (512, 4), (2048, 16))
ATTN_BLOCK = 128
N_MOD = 6
RMS_EPS = 1e-6

LANES = 128
MXU_WIDTH = 256
BF16_SUBLANES = 16
VMEM_LIMIT_BYTES = 56 * 1024 * 1024
F32_SUBLANES = 8
MOD_ROWS = F32_SUBLANES
MOD_COL_STEPS = 4
LAYOUT_ROWS = 512
GATE_LANES = 128
INPROJ_ROWS = LAYOUT_ROWS
ATTN_MAX_SUB_BLOCKS = 16
MLSTM_CHUNK = 256
MLSTM_BATCH_ROWS = 1
FFN_ROWS = LAYOUT_ROWS

NEG_INF = float("-inf")
LOG2E = 1.4426950408889634
LN2 = 0.6931471805599453


def _dot(a, b, **kw):
    return jnp.dot(a, b, preferred_element_type=F32, **kw)


def _dot_nt(a, b, **kw):
    return lax.dot_general(a, b, (((1,), (1,)), ((), ())), preferred_element_type=F32, **kw)


def _dot_tn(a, b):
    return lax.dot_general(a, b, (((0,), (0,)), ((), ())), preferred_element_type=F32)


def _sigmoid(z):
    return 1.0 / (1.0 + jnp.exp(-z))


def _log_sigmoid(z):
    return jnp.minimum(z, 0.0) - jnp.log1p(jnp.exp(-jnp.abs(z)))


def _strided_shape(b, s, w, dilation):
    return (b, s // dilation, dilation * w)


def _gather_chain(dilations):
    chain, parent = [], 1
    for d in sorted(set(dilations)):
        if d == 1:
            continue
        assert d % parent == 0 and d // parent < F32_SUBLANES
        chain.append((parent, d, d // parent))
        parent = d
    return chain


def _mod_kernel(ct_ref, w_ref, b_ref, o_ref, *, n_rows):
    ct = ct_ref[...]
    sc = ct * _sigmoid(ct)
    w = w_ref[...]
    rows = [jnp.sum(w * sc[:, r:r + 1], axis=0, keepdims=True) for r in range(n_rows)]
    rows += [jnp.zeros_like(rows[0])] * (MOD_ROWS - n_rows)
    o_ref[...] = jnp.concatenate(rows, axis=0) + b_ref[...]


def _mod_call(c_t, w_ada, b_ada, *, layer, n_rows):
    _, d, n = w_ada.shape
    assert n % (MOD_COL_STEPS * LANES) == 0
    tn = n // MOD_COL_STEPS
    return pl.pallas_call(
        functools.partial(_mod_kernel, n_rows=n_rows),
        out_shape=jax.ShapeDtypeStruct((MOD_ROWS, n), F32),
        grid=(n // tn,),
        in_specs=[pl.BlockSpec((d, MOD_ROWS), lambda j: (0, 0)),
                  pl.BlockSpec((None, d, tn), lambda j: (layer, 0, j)),
                  pl.BlockSpec((1, tn), lambda j: (0, j))],
        out_specs=pl.BlockSpec((MOD_ROWS, tn), lambda j: (0, j)),
        compiler_params=pltpu.CompilerParams(dimension_semantics=("arbitrary",),
                                             vmem_limit_bytes=VMEM_LIMIT_BYTES),
        name="mod",
    )(c_t, w_ada, b_ada)


def _inproj_kernel(x_ref, mod_ref, g_ref, wt_ref, gq_ref, gk_ref, hp_ref,
                   brow_ref, *rest, attn_w, ml_w, dilations):
    n_lay = len(dilations)
    q_refs, k_refs, v_refs = rest[:n_lay], rest[n_lay:2 * n_lay], rest[2 * n_lay:3 * n_lay]
    qkm_ref, vm_ref, og_ref, gcol_ref, grow_ref, perm_s, wb_s = rest[3 * n_lay:]
    tm = x_ref.shape[1]
    nh = N_MLSTM_HEADS
    n_attn = 3 * attn_w
    n_main = n_attn + 4 * ml_w

    @pl.when((pl.program_id(0) == 0) & (pl.program_id(1) == 0))
    def _():
        for r0 in range(0, n_main, MXU_WIDTH):
            wb_s[r0:r0 + MXU_WIDTH, :] = wt_ref[r0:r0 + MXU_WIDTH, :].astype(BF16)
        gate_rows = jnp.concatenate([wt_ref[n_main:n_main + 2 * nh, :],
                                     jnp.zeros((BF16_SUBLANES - 2 * nh, wt_ref.shape[1]), F32)], axis=0)
        wb_s[n_main:, :] = gate_rows.astype(BF16)

    def emit(val, refs):
        w = val.shape[-1]
        n_cg = w // LANES
        ref_of = dict(zip(dilations, refs))
        if 1 in ref_of:
            ref_of[1][0] = val.astype(BF16)
        chain = _gather_chain(dilations)
        for cg in range(n_cg):
            perm_s[0, cg] = val[:, cg * LANES:(cg + 1) * LANES]
        for level, (parent, d, f) in enumerate(chain):
            src, dst = perm_s.at[level % 2], perm_s.at[(level + 1) % 2]
            n = tm // d
            for c in range(parent):
                for a in range(f):
                    r = a * parent + c
                    for cg in range(n_cg):
                        rows = src[cg, pl.ds(c * (tm // parent) + a, n, stride=f), :]
                        c0 = r * w + cg * LANES
                        ref_of[d][0, :, c0:c0 + LANES] = rows.astype(BF16)
                        if level + 1 < len(chain):
                            dst[cg, r * n:(r + 1) * n, :] = rows

    x = x_ref[0]
    ms = jnp.mean(x * x, axis=-1, keepdims=True)
    y = x * lax.rsqrt(ms + RMS_EPS) * g_ref[...]
    h = (y * (1.0 + mod_ref[0, 1:2, :]) + mod_ref[0, 0:1, :]).astype(BF16)

    xa = _dot_nt(h, wb_s[:n_attn, :])
    hp = hp_ref[...]

    def head_norm(t, g):
        t2 = (t * t).astype(BF16)
        gw = hp.shape[0]
        msq = jnp.concatenate([_dot(t2[:, c0:c0 + gw], hp) for c0 in range(0, t.shape[-1], gw)], axis=-1)
        return t * lax.rsqrt(msq + RMS_EPS) * g

    emit(head_norm(xa[:, :attn_w], gq_ref[...]), q_refs)
    emit(head_norm(xa[:, attn_w:2 * attn_w], gk_ref[...]), k_refs)
    emit(xa[:, 2 * attn_w:], v_refs)

    xm = _dot_nt(h, wb_s[n_attn:n_main, :])
    qkm_ref[0] = xm[:, :2 * ml_w].astype(BF16)
    vm_ref[0] = xm[:, 2 * ml_w:3 * ml_w].astype(BF16)
    og_ref[0] = _sigmoid(xm[:, 3 * ml_w:]).astype(BF16)

    zr = _dot_nt(wb_s[n_main:, :], h)[:2 * nh] + brow_ref[...]
    row = lax.broadcasted_iota(jnp.int32, zr.shape, 0)
    gates = jnp.where(row < nh, zr, _log_sigmoid(zr)) * LOG2E
    grow_ref[0] = gates
    padded = jnp.concatenate([gates, jnp.zeros((GATE_LANES - 2 * nh, tm), F32)], axis=0)
    gcol_ref[0] = padded.T


def _inproj_call(x, mod, g_mix, w_t, gq, gk, hp, brow, *, layer, attn_w, ml_w, dilations):
    b, s, d = x.shape
    tm = INPROJ_ROWS
    nh2 = 2 * N_MLSTM_HEADS
    n_main = 3 * attn_w + 4 * ml_w
    cols = n_main + nh2
    assert w_t.shape[0] % cols == 0 and cols % F32_SUBLANES == 0 and n_main % MXU_WIDTH == 0
    assert all(tm % (dl * BF16_SUBLANES) == 0 for dl in dilations)
    const = lambda shape: pl.BlockSpec(shape, lambda bi, i: (0,) * len(shape))
    tok = lambda w: pl.BlockSpec((1, tm, w), lambda bi, i: (bi, i, 0))
    strided_shapes = tuple(jax.ShapeDtypeStruct(_strided_shape(b, s, attn_w, dl), BF16) for dl in dilations)
    strided_specs = tuple(pl.BlockSpec((1, tm // dl, dl * attn_w), lambda bi, i: (bi, i, 0)) for dl in dilations)
    out_shape = strided_shapes * 3 + (
        jax.ShapeDtypeStruct((b, s, 2 * ml_w), BF16),
        jax.ShapeDtypeStruct((b, s, ml_w), BF16),
        jax.ShapeDtypeStruct((b, s, ml_w), BF16),
        jax.ShapeDtypeStruct((b, s, GATE_LANES), F32),
        jax.ShapeDtypeStruct((b, nh2, s), F32))
    outs = pl.pallas_call(
        functools.partial(_inproj_kernel, attn_w=attn_w, ml_w=ml_w, dilations=dilations),
        out_shape=out_shape,
        grid=(b, s // tm),
        in_specs=[tok(d),
                  pl.BlockSpec((1, N_MOD, d), lambda bi, i: (bi, 0, 0)),
                  const((1, d)),
                  pl.BlockSpec((cols, d), lambda bi, i: (layer, 0), pipeline_mode=pl.Buffered(1)),
                  const((1, attn_w)), const((1, attn_w)), const(hp.shape),
                  const((nh2, 1))],
        out_specs=strided_specs * 3 + (tok(2 * ml_w), tok(ml_w), tok(ml_w), tok(GATE_LANES),
                                       pl.BlockSpec((1, nh2, tm), lambda bi, i: (bi, 0, i))),
        scratch_shapes=[pltpu.VMEM((2, attn_w // LANES, tm, LANES), F32),
                        pltpu.VMEM((n_main + BF16_SUBLANES, d), BF16)],
        compiler_params=pltpu.CompilerParams(dimension_semantics=("arbitrary", "arbitrary"),
                                             vmem_limit_bytes=VMEM_LIMIT_BYTES),
        name="inproj",
    )(x, mod, g_mix, w_t, gq, gk, hp, brow)
    n_lay = len(dilations)
    return (outs[:n_lay], outs[n_lay:2 * n_lay], outs[2 * n_lay:3 * n_lay]) + tuple(outs[3 * n_lay:])


def _attn_kernel(q_ref, kp_ref, kc_ref, vp_ref, vc_ref, bias_ref, ones_ref, o_ref, lse_ref, *, n_heads):
    blk = ATTN_BLOCK
    sub_blocks = q_ref.shape[1] // blk
    hd = ATTN_HEAD_DIM
    w = n_heads * hd
    n_res = q_ref.shape[2] // w
    n_pairs = n_heads // 2
    n = pl.program_id(2)
    lane = lax.broadcasted_iota(jnp.int32, (blk, LANES), 1)
    first_head = lane < hd

    def key_rows(prev_ref, cur_ref, i, cols):
        if i == 0:
            return jnp.concatenate([prev_ref[0, :, cols], cur_ref[0, 0:blk, cols]], axis=0)
        return cur_ref[0, (i - 1) * blk:(i + 1) * blk, cols]

    units = [(res, i, j) for res in range(n_res) for i in range(sub_blocks) for j in range(n_pairs)]
    scores, maxes, probs = [], [], []
    for res, i, j in units:
        cols = slice(res * w + j * LANES, res * w + (j + 1) * LANES)
        q = q_ref[0, i * blk:(i + 1) * blk, cols]
        ks = key_rows(kp_ref, kc_ref, i, cols)
        bias = bias_ref[jnp.where(n == 0, 1, 0)] if i == 0 else bias_ref[0]
        q2 = jnp.concatenate([jnp.where(first_head, q, 0), jnp.where(first_head, 0, q)], axis=0)
        s = _dot_nt(q2, ks) + bias
        scores.append(s)
        maxes.append(jnp.max(s, axis=-1, keepdims=True))
    for s, m in zip(scores, maxes):
        probs.append(jnp.exp2(s - m).astype(BF16))
    for res in range(n_res):
        for i in range(sub_blocks):
            outs = []
            stats = jnp.zeros((blk, LANES), F32)
            for j in range(n_pairs):
                u = (res * sub_blocks + i) * n_pairs + j
                cols = slice(res * w + j * LANES, res * w + (j + 1) * LANES)
                vw = jnp.concatenate([key_rows(vp_ref, vc_ref, i, cols), ones_ref[...]], axis=1)
                ov = _dot(probs[u], vw)
                num = jnp.where(first_head, ov[:blk, :LANES], ov[blk:, :LANES])
                den = jnp.where(first_head, ov[:blk, LANES:], ov[blk:, LANES:])
                outs.append(num / den)
                m = jnp.broadcast_to(maxes[u], (2 * blk, LANES))
                lse = jnp.where(first_head, m[:blk], m[blk:]) * LN2 + jnp.log(den)
                stats = jnp.where((lane & (hd - 1)) == j, lse, stats)
            rows = slice(i * blk, (i + 1) * blk)
            o_ref[0, rows, res * w:(res + 1) * w] = jnp.concatenate(outs, axis=-1).astype(o_ref.dtype)
            lse_ref[0, rows, res * LANES:(res + 1) * LANES] = stats


def _attn_band_masks(n_back):
    blk = ATTN_BLOCK
    row = jnp.arange(2 * blk)[:, None] % blk
    col = jnp.arange(2 * blk)[None, :]
    band = (col >= row + (blk - n_back)) & (col <= row + blk)
    return jnp.stack([jnp.where(band, 0.0, NEG_INF), jnp.where(band & (col >= blk), 0.0, NEG_INF)]).astype(F32)


def _attn_call(q, k, v, *, width, window, dilation):
    b, ls, _ = q.shape
    w = width
    n_back = window // dilation
    blk = ATTN_BLOCK
    assert n_back <= blk
    sub_blocks = min(ATTN_MAX_SUB_BLOCKS, ls // blk)
    step_rows = sub_blocks * blk
    assert ls % step_rows == 0
    n_heads = w // ATTN_HEAD_DIM
    assert n_heads <= LANES and n_heads % 2 == 0 and 2 * ATTN_HEAD_DIM == LANES
    n_res = max(1, min(dilation, ATTN_MAX_SUB_BLOCKS // sub_blocks))
    assert dilation % n_res == 0
    cur = pl.BlockSpec((1, step_rows, n_res * w), lambda bi, r, n: (bi, n, r))
    prev = pl.BlockSpec((1, blk, n_res * w),
                        lambda bi, r, n: (bi, jnp.maximum(n * sub_blocks - 1, 0), r))
    masks = _attn_band_masks(n_back)
    ones = jnp.ones((2 * blk, LANES), BF16)
    const = lambda shape: pl.BlockSpec(shape, lambda bi, r, n: (0,) * len(shape))
    return pl.pallas_call(
        functools.partial(_attn_kernel, n_heads=n_heads),
        out_shape=(jax.ShapeDtypeStruct((b, ls, dilation * w), BF16),
                   jax.ShapeDtypeStruct((b, ls, dilation * LANES), F32)),
        grid=(b, dilation // n_res, ls // step_rows),
        in_specs=[cur, prev, cur, prev, cur, const(masks.shape), const(ones.shape)],
        out_specs=(cur, pl.BlockSpec((1, step_rows, n_res * LANES), lambda bi, r, n: (bi, n, r))),
        compiler_params=pltpu.CompilerParams(
            dimension_semantics=("arbitrary", "arbitrary", "arbitrary"),
            vmem_limit_bytes=VMEM_LIMIT_BYTES),
        name=f"attn_d{dilation}",
    )(q, k, k, v, v, masks, ones)


def _mlstm_kernel(qk_ref, v_ref, og_ref, gcol_ref, grow_ref, wc_ref, bc_ref, gn_ref, tri_ref, hm_ref,
                  o_ref, tail_s, c_s, m_s, *, ml_w):
    L = MLSTM_CHUNK
    nh = N_MLSTM_HEADS
    dh = ml_w // nh

    @pl.when(pl.program_id(1) == 0)
    def _():
        tail_s[...] = jnp.zeros_like(tail_s)
        c_s[...] = jnp.zeros_like(c_s)
        m_s[...] = jnp.zeros_like(m_s)

    tri = tri_ref[...]
    ti = lax.broadcasted_iota(jnp.int32, (L, L), 0)
    si = lax.broadcasted_iota(jnp.int32, (L, L), 1)
    causal = si <= ti
    ones = jnp.ones((L, dh), BF16)
    hp = hm_ref[...]
    for bi in range(qk_ref.shape[0]):
        _mlstm_chunk(bi, qk_ref, v_ref, og_ref, gcol_ref, grow_ref, wc_ref, bc_ref, gn_ref, o_ref,
                     tail_s, c_s, m_s, tri, causal, ones, hp, ml_w=ml_w)


def _mlstm_chunk(bi, qk_ref, v_ref, og_ref, gcol_ref, grow_ref, wc_ref, bc_ref, gn_ref, o_ref,
                 tail_s, c_s, m_s, tri, causal, ones, hp, *, ml_w):
    L = MLSTM_CHUNK
    nh = N_MLSTM_HEADS
    dh = ml_w // nh
    x = qk_ref[bi].astype(F32)
    tail = tail_s[bi]
    row8 = lax.broadcasted_iota(jnp.int32, tail.shape, 0)
    acc = bc_ref[...] + x * wc_ref[CONV_WIDTH - 1:CONV_WIDTH, :]
    for back in range(1, CONV_WIDTH):
        rolled = pltpu.roll(x, back, 0)
        top = jnp.where(row8 >= back, rolled[:MOD_ROWS], pltpu.roll(tail, back, 0))
        shifted = jnp.concatenate([top, rolled[MOD_ROWS:]], axis=0)
        acc = acc + shifted * wc_ref[CONV_WIDTH - 1 - back:CONV_WIDTH - back, :]
    tail_s[bi] = x[L - MOD_ROWS:]
    qk = acc * _sigmoid(acc)

    gcol = gcol_ref[bi]
    grow = grow_ref[bi]
    bcol_all = _dot(tri, gcol, precision=lax.Precision.HIGHEST)
    brow_all = _dot_nt(grow, tri, precision=lax.Precision.HIGHEST)

    ks = [qk[:, ml_w + h * dh:ml_w + (h + 1) * dh] * (dh ** -0.5) for h in range(nh)]

    heads = []
    for h in range(nh):
        st = bi * nh + h
        b_c = jnp.broadcast_to(bcol_all[:, nh + h:nh + h + 1], (L, dh))
        i_c = jnp.broadcast_to(gcol[:, h:h + 1], (L, dh))
        b_r = brow_all[nh + h:nh + h + 1, :]
        i_r = grow[h:h + 1, :]
        m_prev = m_s[st, 0:1, :]
        log_d = jnp.where(causal, b_c[:, 0:1] + (i_r - b_r), NEG_INF)
        m_inter = b_c + m_prev
        m_t = jnp.maximum(m_inter, jnp.max(log_d, axis=-1, keepdims=True))
        d_mat = jnp.exp2(log_d - m_t[:, 0:1])
        inter = jnp.exp2(m_inter - m_t)
        b_last = b_c[L - 1:L, :]
        w_log = b_last - b_c + i_c
        m_new = jnp.maximum(b_last + m_prev, jnp.max(w_log, axis=0, keepdims=True))
        wgt = jnp.exp2(w_log - m_new)
        decay = jnp.exp2(b_last + m_prev - m_new)
        heads.append((m_t, d_mat, inter, m_new, wgt, decay))

    outs = []
    for h, (m_t, d_mat, inter, _, _, _) in enumerate(heads):
        hcols = slice(h * dh, (h + 1) * dh)
        q = qk[:, hcols].astype(BF16)
        v_ext = jnp.concatenate([v_ref[bi, :, hcols], ones], axis=-1)
        c_prev = c_s[bi * nh + h]
        s_qk = _dot_nt(q, ks[h].astype(BF16)) * d_mat
        ext = jnp.concatenate([inter, inter], axis=-1) * _dot(q, c_prev.astype(BF16)) \
            + _dot(s_qk.astype(BF16), v_ext)
        outs.append(ext[:, :dh] / jnp.maximum(jnp.abs(ext[:, dh:]), jnp.exp2(-m_t)))
    hh = jnp.concatenate(outs, axis=-1)
    h2 = (hh * hh).astype(BF16)
    gw = hp.shape[0]
    msq = jnp.concatenate([_dot(h2[:, c0:c0 + gw], hp) for c0 in range(0, ml_w, gw)], axis=-1)
    hn = hh * lax.rsqrt(msq + RMS_EPS) * gn_ref[...]
    o_ref[bi] = (og_ref[bi].astype(F32) * hn).astype(o_ref.dtype)

    for h, (_, _, _, m_new, wgt, decay) in enumerate(heads):
        st = bi * nh + h
        hcols = slice(h * dh, (h + 1) * dh)
        v_ext = jnp.concatenate([v_ref[bi, :, hcols], ones], axis=-1)
        kw = (ks[h] * wgt).astype(BF16)
        c_s[st] = jnp.concatenate([decay, decay], axis=-1) * c_s[st] + _dot_tn(kw, v_ext)
        m_s[st] = jnp.broadcast_to(m_new, m_s.shape[1:])


def _mlstm_call(qkm, vm, og, gcol, grow, w_conv, b_conv, g_norm, tri, hmean):
    b, s, ml_w = vm.shape
    L = MLSTM_CHUNK
    nh = N_MLSTM_HEADS
    dh = ml_w // nh
    bb = MLSTM_BATCH_ROWS
    assert b % bb == 0
    const = lambda shape: pl.BlockSpec(shape, lambda g, i: (0,) * len(shape))
    tok = lambda w: pl.BlockSpec((bb, L, w), lambda g, i: (g, i, 0))
    return pl.pallas_call(
        functools.partial(_mlstm_kernel, ml_w=ml_w),
        out_shape=jax.ShapeDtypeStruct((b, s, ml_w), BF16),
        grid=(b // bb, s // L),
        in_specs=[tok(2 * ml_w), tok(ml_w), tok(ml_w), tok(GATE_LANES),
                  pl.BlockSpec((bb, 2 * nh, L), lambda g, i: (g, 0, i)),
                  const(w_conv.shape), const((1, 2 * ml_w)), const((1, ml_w)),
                  const((L, L)), const(hmean.shape)],
        out_specs=tok(ml_w),
        scratch_shapes=[pltpu.VMEM((bb, MOD_ROWS, 2 * ml_w), F32),
                        pltpu.VMEM((bb * nh, dh, 2 * dh), F32),
                        pltpu.VMEM((bb * nh, MOD_ROWS, dh), F32)],
        compiler_params=pltpu.CompilerParams(dimension_semantics=("arbitrary", "arbitrary"),
                                             vmem_limit_bytes=VMEM_LIMIT_BYTES),
        name="mlstm",
    )(qkm, vm, og, gcol, grow, w_conv, b_conv, g_norm, tri, hmean)


def _outffn_kernel(x_ref, mod_ref, g_ref, *rest, dilations):
    n_lay = len(dilations)
    o_refs, l_refs = rest[:n_lay], rest[n_lay:2 * n_lay]
    (hm_ref, ex_ref, wo_ref, wg_ref, wu_ref, wd_ref, out_ref,
     operm_s, lperm_s) = rest[2 * n_lay:]
    tm = x_ref.shape[1]

    def natural(ref, d, scratch):
        if d == 1:
            return ref[0].astype(F32)
        n_cg = scratch.shape[1]
        chain = [step for step in _gather_chain(dilations) if step[1] <= d]
        for level, (parent, dd, f) in enumerate(reversed(chain)):
            dst = scratch.at[level % 2]
            src = scratch.at[(level + 1) % 2]
            n = tm // dd
            for c in range(parent):
                for a in range(f):
                    r = a * parent + c
                    for cg in range(n_cg):
                        if level == 0:
                            c0 = (r * n_cg + cg) * LANES
                            rows = ref[0, :, c0:c0 + LANES].astype(F32)
                        else:
                            rows = src[cg, r * n:(r + 1) * n, :]
                        dst[cg, pl.ds(c * (tm // parent) + a, n, stride=f), :] = rows
        final = scratch.at[(len(chain) - 1) % 2]
        return jnp.concatenate([final[cg] for cg in range(n_cg)], axis=-1)

    lses = [natural(ref, d, lperm_s) for ref, d in zip(l_refs, dilations)]
    mx = functools.reduce(jnp.maximum, lses)
    es = [jnp.exp(l - mx) for l in lses]
    inv = 1.0 / functools.reduce(jnp.add, es)
    ex = ex_ref[...]
    attn = None
    for e, ref, d in zip(es, o_refs, dilations):
        term = _dot((e * inv).astype(BF16), ex) * natural(ref, d, operm_s)
        attn = term if attn is None else attn + term
    aw = attn.shape[-1]
    y = _dot(attn.astype(BF16), wo_ref[:aw, :]) + _dot(hm_ref[0], wo_ref[aw:, :])
    x1 = x_ref[0] + mod_ref[0, 2:3, :] * y
    ms = jnp.mean(x1 * x1, axis=-1, keepdims=True)
    hn = x1 * lax.rsqrt(ms + RMS_EPS) * g_ref[...]
    hb = (hn * (1.0 + mod_ref[0, 4:5, :]) + mod_ref[0, 3:4, :]).astype(BF16)
    g = _dot(hb, wg_ref[...])
    u = _dot(hb, wu_ref[...])
    a = (g * _sigmoid(g) * u).astype(BF16)
    out_ref[0] = x1 + mod_ref[0, 5:6, :] * _dot(a, wd_ref[...])


def _outffn_call(x, mod, g_ffn, os_, lses, hm, expand, wo, wg, wu, wd, *, dilations):
    b, s, d = x.shape
    tm = FFN_ROWS
    assert all(tm % (dl * BF16_SUBLANES) == 0 for dl in dilations)
    aw = expand.shape[1]
    const = lambda shape: pl.BlockSpec(shape, lambda bi, i: (0,) * len(shape))
    tok = lambda w: pl.BlockSpec((1, tm, w), lambda bi, i: (bi, i, 0))
    strided = lambda w: tuple(pl.BlockSpec((1, tm // dl, dl * w), lambda bi, i: (bi, i, 0)) for dl in dilations)
    return pl.pallas_call(
        functools.partial(_outffn_kernel, dilations=dilations),
        out_shape=jax.ShapeDtypeStruct((b, s, d), F32),
        grid=(b, s // tm),
        in_specs=[tok(d), pl.BlockSpec((1, N_MOD, d), lambda bi, i: (bi, 0, 0)), const((1, d)),
                  *strided(aw), *strided(LANES), tok(hm.shape[-1]),
                  const(expand.shape), const(wo.shape),
                  const(wg.shape), const(wu.shape), const(wd.shape)],
        out_specs=tok(d),
        scratch_shapes=[pltpu.VMEM((2, aw // LANES, tm, LANES), F32), pltpu.VMEM((2, 1, tm, LANES), F32)],
        compiler_params=pltpu.CompilerParams(dimension_semantics=("arbitrary", "arbitrary"),
                                             vmem_limit_bytes=VMEM_LIMIT_BYTES),
        name="outffn",
    )(x, mod, g_ffn, *os_, *lses, hm, expand, wo, wg, wu, wd)


def _block_diag_mean(width, group):
    idx = jnp.arange(width) // group
    return jnp.where(idx[:, None] == idx[None, :], 1.0 / group, 0.0).astype(BF16)


def kernel(x, c, g_mix, w_in, w_conv, b_conv, b_igate, b_fgate, q_norm_g, k_norm_g, mlstm_norm_g, w_out,
           g_ffn, w_gate, w_up, w_down, w_ada, b_ada):
    b, s, d = x.shape
    depth = g_mix.shape[0]
    attn_w = d // 2
    ml_w = d - attn_w
    nh = N_MLSTM_HEADS
    n_attn_heads = attn_w // ATTN_HEAD_DIM
    scale = ATTN_HEAD_DIM ** -0.5 * LOG2E
    dilations = tuple(dl for _, dl in DILATED_PATTERNS)
    assert b <= MOD_ROWS and 2 * nh <= GATE_LANES

    c_t = jnp.zeros((d, MOD_ROWS), F32).at[:, :b].set(c.T)
    assert attn_w % MXU_WIDTH == 0 and MXU_WIDTH % ATTN_HEAD_DIM == 0
    hmean_attn = _block_diag_mean(MXU_WIDTH, ATTN_HEAD_DIM)
    assert ml_w % MXU_WIDTH == 0 and MXU_WIDTH % (ml_w // nh) == 0
    hmean_ml = _block_diag_mean(MXU_WIDTH, ml_w // nh)
    stat_lane = jnp.arange(LANES)
    head_of_stat = jnp.where(stat_lane % ATTN_HEAD_DIM < n_attn_heads // 2,
                             2 * (stat_lane % ATTN_HEAD_DIM) + stat_lane // ATTN_HEAD_DIM, -1)
    head_of_col = jnp.arange(attn_w) // ATTN_HEAD_DIM
    expand = (head_of_stat[:, None] == head_of_col[None, :]).astype(BF16)
    tri = jnp.tril(jnp.ones((MLSTM_CHUNK, MLSTM_CHUNK), F32))
    w_in_t = jnp.swapaxes(w_in, 1, 2).reshape(-1, d)

    for l in range(depth):
        mod = _mod_call(c_t, w_ada, b_ada[l][None, :], layer=l, n_rows=b)[:b].reshape(b, N_MOD, d)

        brow = jnp.concatenate([b_igate[l], b_fgate[l]])[:, None]
        gq = jnp.tile(q_norm_g[l] * scale, n_attn_heads)[None, :]
        gk = jnp.tile(k_norm_g[l], n_attn_heads)[None, :]

        qs, ks, vs, qkm, vm, og, gcol, grow = _inproj_call(
            x, mod, g_mix[l][None, :], w_in_t, gq, gk, hmean_attn, brow,
            layer=l, attn_w=attn_w, ml_w=ml_w, dilations=dilations)

        os_, lses = [], []
        for (window, dilation), q, k, v in zip(DILATED_PATTERNS, qs, ks, vs):
            o, lse = _attn_call(q, k, v, width=attn_w, window=window, dilation=dilation)
            os_.append(o)
            lses.append(lse)

        hm = _mlstm_call(qkm, vm, og, gcol, grow, w_conv[l], b_conv[l][None, :],
                         mlstm_norm_g[l][None, :], tri, hmean_ml)

        x = _outffn_call(x, mod, g_ffn[l][None, :], os_, lses, hm, expand,
                         w_out[l].astype(BF16), w_gate[l].astype(BF16), w_up[l].astype(BF16), w_down[l].astype(BF16),
                         dilations=dilations)
    return x
```

```python
import functools

import jax
import jax.numpy as jnp
from jax import lax
from jax.experimental import pallas as pl
from jax.experimental.pallas import tpu as pltpu

F32 = jnp.float32
BF16 = jnp.bfloat16

ATTN_HEAD_DIM = 64
N_MLSTM_HEADS = 4
CONV_WIDTH = 4
DILATED_PATTERNS = ((128, 1), (512, 4), (2048, 16))
ATTN_BLOCK = 128
N_MOD = 6
RMS_EPS = 1e-6

LANES = 128
MXU_WIDTH = 256
BF16_SUBLANES = 16
VMEM_LIMIT_BYTES = 56 * 1024 * 1024
F32_SUBLANES = 8
MOD_ROWS = F32_SUBLANES
MOD_COL_STEPS = 4
LAYOUT_ROWS = 512
GATE_LANES = 128
INPROJ_ROWS = LAYOUT_ROWS
ATTN_MAX_SUB_BLOCKS = 16
MLSTM_CHUNK = 256
MLSTM_BATCH_ROWS = 1
FFN_ROWS = LAYOUT_ROWS

NEG_INF = float("-inf")
LOG2E = 1.4426950408889634
LN2 = 0.6931471805599453


def _dot(a, b, **kw):
    return jnp.dot(a, b, preferred_element_type=F32, **kw)


def _dot_nt(a, b, **kw):
    return lax.dot_general(a, b, (((1,), (1,)), ((), ())), preferred_element_type=F32, **kw)


def _dot_tn(a, b):
    return lax.dot_general(a, b, (((0,), (0,)), ((), ())), preferred_element_type=F32)


def _sigmoid(z):
    return 1.0 / (1.0 + jnp.exp(-z))


def _log_sigmoid(z):
    return jnp.minimum(z, 0.0) - jnp.log1p(jnp.exp(-jnp.abs(z)))


def _bf16_terms(x):
    hi = x.astype(BF16)
    r1 = x - hi.astype(F32)
    mid = r1.astype(BF16)
    lo = (r1 - mid.astype(F32)).astype(BF16)
    return [hi, mid, lo]


def _strided_shape(b, s, w, dilation):
    return (b, s // dilation, dilation * w)


def _gather_chain(dilations):
    chain, parent = [], 1
    for d in sorted(set(dilations)):
        if d == 1:
            continue
        assert d % parent == 0 and d // parent < F32_SUBLANES
        chain.append((parent, d, d // parent))
        parent = d
    return chain


def _mod_kernel(ct_ref, w_ref, b_ref, o_ref, *, n_rows):
    ct = ct_ref[...]
    sc = ct * _sigmoid(ct)
    w = w_ref[...]
    rows = [jnp.sum(w * sc[:, r:r + 1], axis=0, keepdims=True) for r in range(n_rows)]
    rows += [jnp.zeros_like(rows[0])] * (MOD_ROWS - n_rows)
    o_ref[...] = jnp.concatenate(rows, axis=0) + b_ref[...]


def _mod_call(c_t, w_ada, b_ada, *, layer, n_rows):
    _, d, n = w_ada.shape
    assert n % (MOD_COL_STEPS * LANES) == 0
    tn = n // MOD_COL_STEPS
    return pl.pallas_call(
        functools.partial(_mod_kernel, n_rows=n_rows),
        out_shape=jax.ShapeDtypeStruct((MOD_ROWS, n), F32),
        grid=(n // tn,),
        in_specs=[pl.BlockSpec((d, MOD_ROWS), lambda j: (0, 0)),
                  pl.BlockSpec((None, d, tn), lambda j: (layer, 0, j)),
                  pl.BlockSpec((1, tn), lambda j: (0, j))],
        out_specs=pl.BlockSpec((MOD_ROWS, tn), lambda j: (0, j)),
        compiler_params=pltpu.CompilerParams(dimension_semantics=("arbitrary",),
                                             vmem_limit_bytes=VMEM_LIMIT_BYTES),
        name="mod",
    )(c_t, w_ada, b_ada)


def _inproj_kernel(x_ref, mod_ref, g_ref, wt_ref, gq_ref, gk_ref, hp_ref,
                   brow_ref, *rest, attn_w, ml_w, dilations):
    n_lay = len(dilations)
    q_refs, k_refs, v_refs = rest[:n_lay], rest[n_lay:2 * n_lay], rest[2 * n_lay:3 * n_lay]
    qkm_ref, vm_ref, og_ref, gcol_ref, grow_ref, perm_s, wb_s = rest[3 * n_lay:]
    tm = x_ref.shape[1]
    nh = N_MLSTM_HEADS
    n_attn = 3 * attn_w
    n_main = n_attn + 4 * ml_w

    @pl.when((pl.program_id(0) == 0) & (pl.program_id(1) == 0))
    def _():
        for r0 in range(0, n_main, MXU_WIDTH):
            wb_s[r0:r0 + MXU_WIDTH, :] = wt_ref[r0:r0 + MXU_WIDTH, :].astype(BF16)
        gate_rows = jnp.concatenate([wt_ref[n_main:n_main + 2 * nh, :],
                                     jnp.zeros((BF16_SUBLANES - 2 * nh, wt_ref.shape[1]), F32)], axis=0)
        wb_s[n_main:, :] = gate_rows.astype(BF16)

    def emit(val, refs):
        w = val.shape[-1]
        n_cg = w // LANES
        ref_of = dict(zip(dilations, refs))
        if 1 in ref_of:
            ref_of[1][0] = val.astype(BF16)
        chain = _gather_chain(dilations)
        for cg in range(n_cg):
            perm_s[0, cg] = val[:, cg * LANES:(cg + 1) * LANES]
        for level, (parent, d, f) in enumerate(chain):
            src, dst = perm_s.at[level % 2], perm_s.at[(level + 1) % 2]
            n = tm // d
            for c in range(parent):
                for a in range(f):
                    r = a * parent + c
                    for cg in range(n_cg):
                        rows = src[cg, pl.ds(c * (tm // parent) + a, n, stride=f), :]
                        c0 = r * w + cg * LANES
                        ref_of[d][0, :, c0:c0 + LANES] = rows.astype(BF16)
                        if level + 1 < len(chain):
                            dst[cg, r * n:(r + 1) * n, :] = rows

    x = x_ref[0]
    ms = jnp.mean(x * x, axis=-1, keepdims=True)
    y = x * lax.rsqrt(ms + RMS_EPS) * g_ref[...]
    h = (y * (1.0 + mod_ref[0, 1:2, :]) + mod_ref[0, 0:1, :]).astype(BF16)

    xa = _dot_nt(h, wb_s[:n_attn, :])
    hp = hp_ref[...]

    def head_norm(t, g):
        t2 = (t * t).astype(BF16)
        gw = hp.shape[0]
        msq = jnp.concatenate([_dot(t2[:, c0:c0 + gw], hp) for c0 in range(0, t.shape[-1], gw)], axis=-1)
        return t * lax.rsqrt(msq + RMS_EPS) * g

    emit(head_norm(xa[:, :attn_w], gq_ref[...]), q_refs)
    emit(head_norm(xa[:, attn_w:2 * attn_w], gk_ref[...]), k_refs)
    emit(xa[:, 2 * attn_w:], v_refs)

    xm = _dot_nt(h, wb_s[n_attn:n_main, :])
    qkm_ref[0] = xm[:, :2 * ml_w].astype(BF16)
    vm_ref[0] = xm[:, 2 * ml_w:3 * ml_w].astype(BF16)
    og_ref[0] = _sigmoid(xm[:, 3 * ml_w:]).astype(BF16)

    zr = _dot_nt(wb_s[n_main:, :], h)[:2 * nh] + brow_ref[...]
    row = lax.broadcasted_iota(jnp.int32, zr.shape, 0)
    gates = jnp.where(row < nh, zr, _log_sigmoid(zr)) * LOG2E
    grow_ref[0] = gates
    padded = jnp.concatenate([gates, jnp.zeros((GATE_LANES - 2 * nh, tm), F32)], axis=0)
    gcol_ref[0] = padded.T


def _inproj_call(x, mod, g_mix, w_t, gq, gk, hp, brow, *, layer, attn_w, ml_w, dilations):
    b, s, d = x.shape
    tm = INPROJ_ROWS
    nh2 = 2 * N_MLSTM_HEADS
    n_main = 3 * attn_w + 4 * ml_w
    cols = n_main + nh2
    assert w_t.shape[0] % cols == 0 and cols % F32_SUBLANES == 0 and n_main % MXU_WIDTH == 0
    assert all(tm % (dl * BF16_SUBLANES) == 0 for dl in dilations)
    const = lambda shape: pl.BlockSpec(shape, lambda bi, i: (0,) * len(shape))
    tok = lambda w: pl.BlockSpec((1, tm, w), lambda bi, i: (bi, i, 0))
    strided_shapes = tuple(jax.ShapeDtypeStruct(_strided_shape(b, s, attn_w, dl), BF16) for dl in dilations)
    strided_specs = tuple(pl.BlockSpec((1, tm // dl, dl * attn_w), lambda bi, i: (bi, i, 0)) for dl in dilations)
    out_shape = strided_shapes * 3 + (
        jax.ShapeDtypeStruct((b, s, 2 * ml_w), BF16),
        jax.ShapeDtypeStruct((b, s, ml_w), BF16),
        jax.ShapeDtypeStruct((b, s, ml_w), BF16),
        jax.ShapeDtypeStruct((b, s, GATE_LANES), F32),
        jax.ShapeDtypeStruct((b, nh2, s), F32))
    outs = pl.pallas_call(
        functools.partial(_inproj_kernel, attn_w=attn_w, ml_w=ml_w, dilations=dilations),
        out_shape=out_shape,
        grid=(b, s // tm),
        in_specs=[tok(d),
                  pl.BlockSpec((1, N_MOD, d), lambda bi, i: (bi, 0, 0)),
                  const((1, d)),
                  pl.BlockSpec((cols, d), lambda bi, i: (layer, 0), pipeline_mode=pl.Buffered(1)),
                  const((1, attn_w)), const((1, attn_w)), const(hp.shape),
                  const((nh2, 1))],
        out_specs=strided_specs * 3 + (tok(2 * ml_w), tok(ml_w), tok(ml_w), tok(GATE_LANES),
                                       pl.BlockSpec((1, nh2, tm), lambda bi, i: (bi, 0, i))),
        scratch_shapes=[pltpu.VMEM((2, attn_w // LANES, tm, LANES), F32),
                        pltpu.VMEM((n_main + BF16_SUBLANES, d), BF16)],
        compiler_params=pltpu.CompilerParams(dimension_semantics=("arbitrary", "arbitrary"),
                                             vmem_limit_bytes=VMEM_LIMIT_BYTES),
        name="inproj",
    )(x, mod, g_mix, w_t, gq, gk, hp, brow)
    n_lay = len(dilations)
    return (outs[:n_lay], outs[n_lay:2 * n_lay], outs[2 * n_lay:3 * n_lay]) + tuple(outs[3 * n_lay:])


def _attn_kernel(q_ref, kp_ref, kc_ref, vp_ref, vc_ref, bias_ref, ones_ref, o_ref, lse_ref, *, n_heads):
    blk = ATTN_BLOCK
    sub_blocks = q_ref.shape[1] // blk
    hd = ATTN_HEAD_DIM
    w = n_heads * hd
    n_res = q_ref.shape[2] // w
    n_pairs = n_heads // 2
    n = pl.program_id(2)
    lane = lax.broadcasted_iota(jnp.int32, (blk, LANES), 1)
    first_head = lane < hd

    def key_rows(prev_ref, cur_ref, i, cols):
        if i == 0:
            return jnp.concatenate([prev_ref[0, :, cols], cur_ref[0, 0:blk, cols]], axis=0)
        return cur_ref[0, (i - 1) * blk:(i + 1) * blk, cols]

    units = [(res, i, j) for res in range(n_res) for i in range(sub_blocks) for j in range(n_pairs)]
    scores, maxes, probs = [], [], []
    for res, i, j in units:
        cols = slice(res * w + j * LANES, res * w + (j + 1) * LANES)
        q = q_ref[0, i * blk:(i + 1) * blk, cols]
        ks = key_rows(kp_ref, kc_ref, i, cols)
        bias = bias_ref[jnp.where(n == 0, 1, 0)] if i == 0 else bias_ref[0]
        q2 = jnp.concatenate([jnp.where(first_head, q, 0), jnp.where(first_head, 0, q)], axis=0)
        s = _dot_nt(q2, ks) + bias
        scores.append(s)
        maxes.append(jnp.max(s, axis=-1, keepdims=True))
    for s, m in zip(scores, maxes):
        probs.append(jnp.exp2(s - m).astype(BF16))
    for res in range(n_res):
        for i in range(sub_blocks):
            outs = []
            stats = jnp.zeros((blk, LANES), F32)
            for j in range(n_pairs):
                u = (res * sub_blocks + i) * n_pairs + j
                cols = slice(res * w + j * LANES, res * w + (j + 1) * LANES)
                vw = jnp.concatenate([key_rows(vp_ref, vc_ref, i, cols), ones_ref[...]], axis=1)
                ov = _dot(probs[u], vw)
                num = jnp.where(first_head, ov[:blk, :LANES], ov[blk:, :LANES])
                den = jnp.where(first_head, ov[:blk, LANES:], ov[blk:, LANES:])
                outs.append(num / den)
                m = jnp.broadcast_to(maxes[u], (2 * blk, LANES))
                lse = jnp.where(first_head, m[:blk], m[blk:]) * LN2 + jnp.log(den)
                stats = jnp.where((lane & (hd - 1)) == j, lse, stats)
            rows = slice(i * blk, (i + 1) * blk)
            o_ref[0, rows, res * w:(res + 1) * w] = jnp.concatenate(outs, axis=-1).astype(o_ref.dtype)
            lse_ref[0, rows, res * LANES:(res + 1) * LANES] = stats


def _attn_band_masks(n_back):
    blk = ATTN_BLOCK
    row = jnp.arange(2 * blk)[:, None] % blk
    col = jnp.arange(2 * blk)[None, :]
    band = (col >= row + (blk - n_back)) & (col <= row + blk)
    return jnp.stack([jnp.where(band, 0.0, NEG_INF), jnp.where(band & (col >= blk), 0.0, NEG_INF)]).astype(F32)


def _attn_call(q, k, v, *, width, window, dilation):
    b, ls, _ = q.shape
    w = width
    n_back = window // dilation
    blk = ATTN_BLOCK
    assert n_back <= blk
    sub_blocks = min(ATTN_MAX_SUB_BLOCKS, ls // blk)
    step_rows = sub_blocks * blk
    assert ls % step_rows == 0
    n_heads = w // ATTN_HEAD_DIM
    assert n_heads <= LANES and n_heads % 2 == 0 and 2 * ATTN_HEAD_DIM == LANES
    n_res = max(1, min(dilation, ATTN_MAX_SUB_BLOCKS // sub_blocks))
    assert dilation % n_res == 0
    cur = pl.BlockSpec((1, step_rows, n_res * w), lambda bi, r, n: (bi, n, r))
    prev = pl.BlockSpec((1, blk, n_res * w),
                        lambda bi, r, n: (bi, jnp.maximum(n * sub_blocks - 1, 0), r))
    masks = _attn_band_masks(n_back)
    ones = jnp.ones((2 * blk, LANES), BF16)
    const = lambda shape: pl.BlockSpec(shape, lambda bi, r, n: (0,) * len(shape))
    return pl.pallas_call(
        functools.partial(_attn_kernel, n_heads=n_heads),
        out_shape=(jax.ShapeDtypeStruct((b, ls, dilation * w), BF16),
                   jax.ShapeDtypeStruct((b, ls, dilation * LANES), F32)),
        grid=(b, dilation // n_res, ls // step_rows),
        in_specs=[cur, prev, cur, prev, cur, const(masks.shape), const(ones.shape)],
        out_specs=(cur, pl.BlockSpec((1, step_rows, n_res * LANES), lambda bi, r, n: (bi, n, r))),
        compiler_params=pltpu.CompilerParams(
            dimension_semantics=("arbitrary", "arbitrary", "arbitrary"),
            vmem_limit_bytes=VMEM_LIMIT_BYTES),
        name=f"attn_d{dilation}",
    )(q, k, k, v, v, masks, ones)


def _mlstm_kernel(qk_ref, v_ref, og_ref, gcol_ref, grow_ref, wc_ref, bc_ref, gn_ref, tri_ref, hm_ref,
                  o_ref, tail_s, c_s, m_s, *, ml_w):
    L = MLSTM_CHUNK
    nh = N_MLSTM_HEADS
    dh = ml_w // nh

    @pl.when(pl.program_id(1) == 0)
    def _():
        tail_s[...] = jnp.zeros_like(tail_s)
        c_s[...] = jnp.zeros_like(c_s)
        m_s[...] = jnp.zeros_like(m_s)

    tri = tri_ref[...]
    ti = lax.broadcasted_iota(jnp.int32, (L, L), 0)
    si = lax.broadcasted_iota(jnp.int32, (L, L), 1)
    causal = si <= ti
    ones = jnp.ones((L, dh), BF16)
    hp = hm_ref[...]
    for bi in range(qk_ref.shape[0]):
        _mlstm_chunk(bi, qk_ref, v_ref, og_ref, gcol_ref, grow_ref, wc_ref, bc_ref, gn_ref, o_ref,
                     tail_s, c_s, m_s, tri, causal, ones, hp, ml_w=ml_w)


def _mlstm_chunk(bi, qk_ref, v_ref, og_ref, gcol_ref, grow_ref, wc_ref, bc_ref, gn_ref, o_ref,
                 tail_s, c_s, m_s, tri, causal, ones, hp, *, ml_w):
    L = MLSTM_CHUNK
    nh = N_MLSTM_HEADS
    dh = ml_w // nh
    x = qk_ref[bi].astype(F32)
    tail = tail_s[bi]
    row8 = lax.broadcasted_iota(jnp.int32, tail.shape, 0)
    acc = bc_ref[...] + x * wc_ref[CONV_WIDTH - 1:CONV_WIDTH, :]
    for back in range(1, CONV_WIDTH):
        rolled = pltpu.roll(x, back, 0)
        top = jnp.where(row8 >= back, rolled[:MOD_ROWS], pltpu.roll(tail, back, 0))
        shifted = jnp.concatenate([top, rolled[MOD_ROWS:]], axis=0)
        acc = acc + shifted * wc_ref[CONV_WIDTH - 1 - back:CONV_WIDTH - back, :]
    tail_s[bi] = x[L - MOD_ROWS:]
    qk = acc * _sigmoid(acc)

    gcol = gcol_ref[bi]
    grow = grow_ref[bi]
    gc_parts, gr_parts = _bf16_terms(gcol), _bf16_terms(grow)
    bcol_3 = _dot(tri, jnp.concatenate(gc_parts, axis=1))
    bcol_all = bcol_3[:, :GATE_LANES] + bcol_3[:, GATE_LANES:2 * GATE_LANES] + bcol_3[:, 2 * GATE_LANES:]
    pad = jnp.zeros((BF16_SUBLANES - grow.shape[0], L), BF16)
    brow_3 = [_dot_nt(jnp.concatenate([part, pad], axis=0), tri)[:grow.shape[0]] for part in gr_parts]
    brow_all = brow_3[0] + brow_3[1] + brow_3[2]

    ks = [qk[:, ml_w + h * dh:ml_w + (h + 1) * dh] * (dh ** -0.5) for h in range(nh)]

    heads = []
    for h in range(nh):
        st = bi * nh + h
        b_c = jnp.broadcast_to(bcol_all[:, nh + h:nh + h + 1], (L, dh))
        i_c = jnp.broadcast_to(gcol[:, h:h + 1], (L, dh))
        b_r = brow_all[nh + h:nh + h + 1, :]
        i_r = grow[h:h + 1, :]
        m_prev = m_s[st, 0:1, :]
        log_d = jnp.where(causal, b_c[:, 0:1] + (i_r - b_r), NEG_INF)
        m_inter = b_c + m_prev
        m_t = jnp.maximum(m_inter, jnp.max(log_d, axis=-1, keepdims=True))
        d_mat = jnp.exp2(log_d - m_t[:, 0:1])
        inter = jnp.exp2(m_inter - m_t)
        b_last = b_c[L - 1:L, :]
        w_log = b_last - b_c + i_c
        m_new = jnp.maximum(b_last + m_prev, jnp.max(w_log, axis=0, keepdims=True))
        wgt = jnp.exp2(w_log - m_new)
        decay = jnp.exp2(b_last + m_prev - m_new)
        heads.append((m_t, d_mat, inter, m_new, wgt, decay))

    outs = []
    for h, (m_t, d_mat, inter, _, _, _) in enumerate(heads):
        hcols = slice(h * dh, (h + 1) * dh)
        q = qk[:, hcols].astype(BF16)
        v_ext = jnp.concatenate([v_ref[bi, :, hcols], ones], axis=-1)
        c_prev = c_s[bi * nh + h]
        s_qk = _dot_nt(q, ks[h].astype(BF16)) * d_mat
        ext = jnp.concatenate([inter, inter], axis=-1) * _dot(q, c_prev.astype(BF16)) \
            + _dot(s_qk.astype(BF16), v_ext)
        outs.append(ext[:, :dh] / jnp.maximum(jnp.abs(ext[:, dh:]), jnp.exp2(-m_t)))
    hh = jnp.concatenate(outs, axis=-1)
    h2 = (hh * hh).astype(BF16)
    gw = hp.shape[0]
    msq = jnp.concatenate([_dot(h2[:, c0:c0 + gw], hp) for c0 in range(0, ml_w, gw)], axis=-1)
    hn = hh * lax.rsqrt(msq + RMS_EPS) * gn_ref[...]
    o_ref[bi] = (og_ref[bi].astype(F32) * hn).astype(o_ref.dtype)

    for h, (_, _, _, m_new, wgt, decay) in enumerate(heads):
        st = bi * nh + h
        hcols = slice(h * dh, (h + 1) * dh)
        v_ext = jnp.concatenate([v_ref[bi, :, hcols], ones], axis=-1)
        kw = (ks[h] * wgt).astype(BF16)
        c_s[st] = jnp.concatenate([decay, decay], axis=-1) * c_s[st] + _dot_tn(kw, v_ext)
        m_s[st] = jnp.broadcast_to(m_new, m_s.shape[1:])


def _mlstm_call(qkm, vm, og, gcol, grow, w_conv, b_conv, g_norm, tri, hmean):
    b, s, ml_w = vm.shape
    L = MLSTM_CHUNK
    nh = N_MLSTM_HEADS
    dh = ml_w // nh
    bb = MLSTM_BATCH_ROWS
    assert b % bb == 0
    const = lambda shape: pl.BlockSpec(shape, lambda g, i: (0,) * len(shape))
    tok = lambda w: pl.BlockSpec((bb, L, w), lambda g, i: (g, i, 0))
    return pl.pallas_call(
        functools.partial(_mlstm_kernel, ml_w=ml_w),
        out_shape=jax.ShapeDtypeStruct((b, s, ml_w), BF16),
        grid=(b // bb, s // L),
        in_specs=[tok(2 * ml_w), tok(ml_w), tok(ml_w), tok(GATE_LANES),
                  pl.BlockSpec((bb, 2 * nh, L), lambda g, i: (g, 0, i)),
                  const(w_conv.shape), const((1, 2 * ml_w)), const((1, ml_w)),
                  const((L, L)), const(hmean.shape)],
        out_specs=tok(ml_w),
        scratch_shapes=[pltpu.VMEM((bb, MOD_ROWS, 2 * ml_w), F32),
                        pltpu.VMEM((bb * nh, dh, 2 * dh), F32),
                        pltpu.VMEM((bb * nh, MOD_ROWS, dh), F32)],
        compiler_params=pltpu.CompilerParams(dimension_semantics=("arbitrary", "arbitrary"),
                                             vmem_limit_bytes=VMEM_LIMIT_BYTES),
        name="mlstm",
    )(qkm, vm, og, gcol, grow, w_conv, b_conv, g_norm, tri, hmean)


def _outffn_kernel(x_ref, mod_ref, g_ref, *rest, dilations):
    n_lay = len(dilations)
    o_refs, l_refs = rest[:n_lay], rest[n_lay:2 * n_lay]
    (hm_ref, ex_ref, wo_ref, wg_ref, wu_ref, wd_ref, out_ref,
     operm_s, lperm_s) = rest[2 * n_lay:]
    tm = x_ref.shape[1]

    def natural(ref, d, scratch):
        if d == 1:
            return ref[0].astype(F32)
        n_cg = scratch.shape[1]
        chain = [step for step in _gather_chain(dilations) if step[1] <= d]
        for level, (parent, dd, f) in enumerate(reversed(chain)):
            dst = scratch.at[level % 2]
            src = scratch.at[(level + 1) % 2]
            n = tm // dd
            for c in range(parent):
                for a in range(f):
                    r = a * parent + c
                    for cg in range(n_cg):
                        if level == 0:
                            c0 = (r * n_cg + cg) * LANES
                            rows = ref[0, :, c0:c0 + LANES].astype(F32)
                        else:
                            rows = src[cg, r * n:(r + 1) * n, :]
                        dst[cg, pl.ds(c * (tm // parent) + a, n, stride=f), :] = rows
        final = scratch.at[(len(chain) - 1) % 2]
        return jnp.concatenate([final[cg] for cg in range(n_cg)], axis=-1)

    lses = [natural(ref, d, lperm_s) for ref, d in zip(l_refs, dilations)]
    mx = functools.reduce(jnp.maximum, lses)
    es = [jnp.exp(l - mx) for l in lses]
    inv = 1.0 / functools.reduce(jnp.add, es)
    ex = ex_ref[...]
    attn = None
    for e, ref, d in zip(es, o_refs, dilations):
        term = _dot((e * inv).astype(BF16), ex) * natural(ref, d, operm_s)
        attn = term if attn is None else attn + term
    aw = attn.shape[-1]
    y = _dot(attn.astype(BF16), wo_ref[:aw, :]) + _dot(hm_ref[0], wo_ref[aw:, :])
    x1 = x_ref[0] + mod_ref[0, 2:3, :] * y
    ms = jnp.mean(x1 * x1, axis=-1, keepdims=True)
    hn = x1 * lax.rsqrt(ms + RMS_EPS) * g_ref[...]
    hb = (hn * (1.0 + mod_ref[0, 4:5, :]) + mod_ref[0, 3:4, :]).astype(BF16)
    g = _dot(hb, wg_ref[...])
    u = _dot(hb, wu_ref[...])
    a = (g * _sigmoid(g) * u).astype(BF16)
    out_ref[0] = x1 + mod_ref[0, 5:6, :] * _dot(a, wd_ref[...])


def _outffn_call(x, mod, g_ffn, os_, lses, hm, expand, wo, wg, wu, wd, *, dilations):
    b, s, d = x.shape
    tm = FFN_ROWS
    assert all(tm % (dl * BF16_SUBLANES) == 0 for dl in dilations)
    aw = expand.shape[1]
    const = lambda shape: pl.BlockSpec(shape, lambda bi, i: (0,) * len(shape))
    tok = lambda w: pl.BlockSpec((1, tm, w), lambda bi, i: (bi, i, 0))
    strided = lambda w: tuple(pl.BlockSpec((1, tm // dl, dl * w), lambda bi, i: (bi, i, 0)) for dl in dilations)
    return pl.pallas_call(
        functools.partial(_outffn_kernel, dilations=dilations),
        out_shape=jax.ShapeDtypeStruct((b, s, d), F32),
        grid=(b, s // tm),
        in_specs=[tok(d), pl.BlockSpec((1, N_MOD, d), lambda bi, i: (bi, 0, 0)), const((1, d)),
                  *strided(aw), *strided(LANES), tok(hm.shape[-1]),
                  const(expand.shape), const(wo.shape),
                  const(wg.shape), const(wu.shape), const(wd.shape)],
        out_specs=tok(d),
        scratch_shapes=[pltpu.VMEM((2, aw // LANES, tm, LANES), F32), pltpu.VMEM((2, 1, tm, LANES), F32)],
        compiler_params=pltpu.CompilerParams(dimension_semantics=("arbitrary", "arbitrary"),
                                             vmem_limit_bytes=VMEM_LIMIT_BYTES),
        name="outffn",
    )(x, mod, g_ffn, *os_, *lses, hm, expand, wo, wg, wu, wd)


def _block_diag_mean(width, group):
    idx = jnp.arange(width) // group
    return jnp.where(idx[:, None] == idx[None, :], 1.0 / group, 0.0).astype(BF16)


def kernel(x, c, g_mix, w_in, w_conv, b_conv, b_igate, b_fgate, q_norm_g, k_norm_g, mlstm_norm_g, w_out,
           g_ffn, w_gate, w_up, w_down, w_ada, b_ada):
    b, s, d = x.shape
    depth = g_mix.shape[0]
    attn_w = d // 2
    ml_w = d - attn_w
    nh = N_MLSTM_HEADS
    n_attn_heads = attn_w // ATTN_HEAD_DIM
    scale = ATTN_HEAD_DIM ** -0.5 * LOG2E
    dilations = tuple(dl for _, dl in DILATED_PATTERNS)
    assert b <= MOD_ROWS and 2 * nh <= GATE_LANES

    c_t = jnp.zeros((d, MOD_ROWS), F32).at[:, :b].set(c.T)
    assert attn_w % MXU_WIDTH == 0 and MXU_WIDTH % ATTN_HEAD_DIM == 0
    hmean_attn = _block_diag_mean(MXU_WIDTH, ATTN_HEAD_DIM)
    assert ml_w % MXU_WIDTH == 0 and MXU_WIDTH % (ml_w // nh) == 0
    hmean_ml = _block_diag_mean(MXU_WIDTH, ml_w // nh)
    stat_lane = jnp.arange(LANES)
    head_of_stat = jnp.where(stat_lane % ATTN_HEAD_DIM < n_attn_heads // 2,
                             2 * (stat_lane % ATTN_HEAD_DIM) + stat_lane // ATTN_HEAD_DIM, -1)
    head_of_col = jnp.arange(attn_w) // ATTN_HEAD_DIM
    expand = (head_of_stat[:, None] == head_of_col[None, :]).astype(BF16)
    tri = jnp.tril(jnp.ones((MLSTM_CHUNK, MLSTM_CHUNK), BF16))
    w_in_t = jnp.swapaxes(w_in, 1, 2).reshape(-1, d)

    for l in range(depth):
        mod = _mod_call(c_t, w_ada, b_ada[l][None, :], layer=l, n_rows=b)[:b].reshape(b, N_MOD, d)

        brow = jnp.concatenate([b_igate[l], b_fgate[l]])[:, None]
        gq = jnp.tile(q_norm_g[l] * scale, n_attn_heads)[None, :]
        gk = jnp.tile(k_norm_g[l], n_attn_heads)[None, :]

        qs, ks, vs, qkm, vm, og, gcol, grow = _inproj_call(
            x, mod, g_mix[l][None, :], w_in_t, gq, gk, hmean_attn, brow,
            layer=l, attn_w=attn_w, ml_w=ml_w, dilations=dilations)

        os_, lses = [], []
        for (window, dilation), q, k, v in zip(DILATED_PATTERNS, qs, ks, vs):
            o, lse = _attn_call(q, k, v, width=attn_w, window=window, dilation=dilation)
            os_.append(o)
            lses.append(lse)

        hm = _mlstm_call(qkm, vm, og, gcol, grow, w_conv[l], b_conv[l][None, :],
                         mlstm_norm_g[l][None, :], tri, hmean_ml)

        x = _outffn_call(x, mod, g_ffn[l][None, :], os_, lses, hm, expand,
                         w_out[l].astype(BF16), w_gate[l].astype(BF16), w_up[l].astype(BF16), w_down[l].astype(BF16),
                         dilations=dilations)
    return x
```

```python
import functools

import jax
import jax.numpy as jnp
from jax import lax
from jax.experimental import pallas as pl
from jax.experimental.pallas import tpu as pltpu

F32 = jnp.float32
BF16 = jnp.bfloat16

ATTN_HEAD_DIM = 64
N_MLSTM_HEADS = 4
CONV_WIDTH = 4
DILATED_PATTERNS = ((128, 1), (512, 4), (2048, 16))
ATTN_BLOCK = 128
N_MOD = 6
RMS_EPS = 1e-6

LANES = 128
MXU_WIDTH = 256
BF16_SUBLANES = 16
VMEM_LIMIT_BYTES = 56 * 1024 * 1024
F32_SUBLANES = 8
MOD_ROWS = F32_SUBLANES
MOD_COL_STEPS = 4
LAYOUT_ROWS = 512
GATE_LANES = 128
INPROJ_ROWS = LAYOUT_ROWS
ATTN_MAX_SUB_BLOCKS = 16
MLSTM_CHUNK = 256
WEIGHT_CHUNK_ROWS = MXU_WIDTH
MLSTM_BATCH_ROWS = 1
FFN_ROWS = LAYOUT_ROWS

NEG_INF = float("-inf")
LOG2E = 1.4426950408889634
LN2 = 0.6931471805599453


def _dot(a, b, **kw):
    return jnp.dot(a, b, preferred_element_type=F32, **kw)


def _dot_nt(a, b, **kw):
    return lax.dot_general(a, b, (((1,), (1,)), ((), ())), preferred_element_type=F32, **kw)


def _dot_tn(a, b):
    return lax.dot_general(a, b, (((0,), (0,)), ((), ())), preferred_element_type=F32)


def _sigmoid(z):
    return 1.0 / (1.0 + jnp.exp(-z))


def _log_sigmoid(z):
    return jnp.minimum(z, 0.0) - jnp.log1p(jnp.exp(-jnp.abs(z)))


def _bf16_terms(x):
    hi = x.astype(BF16)
    r1 = x - hi.astype(F32)
    mid = r1.astype(BF16)
    lo = (r1 - mid.astype(F32)).astype(BF16)
    return [hi, mid, lo]


def _strided_shape(b, s, w, dilation):
    return (b, s // dilation, dilation * w)


def _gather_chain(dilations):
    chain, parent = [], 1
    for d in sorted(set(dilations)):
        if d == 1:
            continue
        assert d % parent == 0 and d // parent < F32_SUBLANES
        chain.append((parent, d, d // parent))
        parent = d
    return chain


def _mod_kernel(ct_ref, w_ref, b_ref, o_ref, *, n_rows):
    ct = ct_ref[...]
    sc = ct * _sigmoid(ct)
    w = w_ref[...]
    rows = [jnp.sum(w * sc[:, r:r + 1], axis=0, keepdims=True) for r in range(n_rows)]
    rows += [jnp.zeros_like(rows[0])] * (MOD_ROWS - n_rows)
    o_ref[...] = jnp.concatenate(rows, axis=0) + b_ref[...]


def _mod_call(c_t, w_ada, b_ada, *, layer, n_rows):
    _, d, n = w_ada.shape
    assert n % (MOD_COL_STEPS * LANES) == 0
    tn = n // MOD_COL_STEPS
    return pl.pallas_call(
        functools.partial(_mod_kernel, n_rows=n_rows),
        out_shape=jax.ShapeDtypeStruct((MOD_ROWS, n), F32),
        grid=(n // tn,),
        in_specs=[pl.BlockSpec((d, MOD_ROWS), lambda j: (0, 0)),
                  pl.BlockSpec((None, d, tn), lambda j: (layer, 0, j)),
                  pl.BlockSpec((1, tn), lambda j: (0, j))],
        out_specs=pl.BlockSpec((MOD_ROWS, tn), lambda j: (0, j)),
        compiler_params=pltpu.CompilerParams(dimension_semantics=("arbitrary",),
                                             vmem_limit_bytes=VMEM_LIMIT_BYTES),
        name="mod",
    )(c_t, w_ada, b_ada)


def _inproj_kernel(x_ref, mod_ref, g_ref, wt_ref, gq_ref, gk_ref, hp_ref,
                   brow_ref, *rest, attn_w, ml_w, dilations):
    n_lay = len(dilations)
    q_refs, k_refs, v_refs = rest[:n_lay], rest[n_lay:2 * n_lay], rest[2 * n_lay:3 * n_lay]
    qkm_ref, vm_ref, og_ref, gcol_ref, grow_ref, perm_s, wb_s = rest[3 * n_lay:]
    tm = x_ref.shape[1]
    nh = N_MLSTM_HEADS
    n_attn = 3 * attn_w
    n_main = n_attn + 4 * ml_w

    @pl.when((pl.program_id(0) == 0) & (pl.program_id(1) == 0))
    def _():
        for r0 in range(0, n_main, MXU_WIDTH):
            wb_s[r0:r0 + MXU_WIDTH, :] = wt_ref[r0:r0 + MXU_WIDTH, :].astype(BF16)
        gate_rows = jnp.concatenate([wt_ref[n_main:n_main + 2 * nh, :],
                                     jnp.zeros((BF16_SUBLANES - 2 * nh, wt_ref.shape[1]), F32)], axis=0)
        wb_s[n_main:, :] = gate_rows.astype(BF16)

    def emit(val, refs):
        w = val.shape[-1]
        n_cg = w // LANES
        ref_of = dict(zip(dilations, refs))
        if 1 in ref_of:
            ref_of[1][0] = val.astype(BF16)
        chain = _gather_chain(dilations)
        for cg in range(n_cg):
            perm_s[0, cg] = val[:, cg * LANES:(cg + 1) * LANES]
        for level, (parent, d, f) in enumerate(chain):
            src, dst = perm_s.at[level % 2], perm_s.at[(level + 1) % 2]
            n = tm // d
            for c in range(parent):
                for a in range(f):
                    r = a * parent + c
                    for cg in range(n_cg):
                        rows = src[cg, pl.ds(c * (tm // parent) + a, n, stride=f), :]
                        c0 = r * w + cg * LANES
                        ref_of[d][0, :, c0:c0 + LANES] = rows.astype(BF16)
                        if level + 1 < len(chain):
                            dst[cg, r * n:(r + 1) * n, :] = rows

    x = x_ref[0]
    ms = jnp.mean(x * x, axis=-1, keepdims=True)
    y = x * lax.rsqrt(ms + RMS_EPS) * g_ref[...]
    h = (y * (1.0 + mod_ref[0, 1:2, :]) + mod_ref[0, 0:1, :]).astype(BF16)

    xa = _dot_nt(h, wb_s[:n_attn, :])
    hp = hp_ref[...]

    def head_norm(t, g):
        t2 = (t * t).astype(BF16)
        gw = hp.shape[0]
        msq = jnp.concatenate([_dot(t2[:, c0:c0 + gw], hp) for c0 in range(0, t.shape[-1], gw)], axis=-1)
        return t * lax.rsqrt(msq + RMS_EPS) * g

    emit(head_norm(xa[:, :attn_w], gq_ref[...]), q_refs)
    emit(head_norm(xa[:, attn_w:2 * attn_w], gk_ref[...]), k_refs)
    emit(xa[:, 2 * attn_w:], v_refs)

    xm = _dot_nt(h, wb_s[n_attn:n_main, :])
    qkm_ref[0] = xm[:, :2 * ml_w].astype(BF16)
    vm_ref[0] = xm[:, 2 * ml_w:3 * ml_w].astype(BF16)
    og_ref[0] = _sigmoid(xm[:, 3 * ml_w:]).astype(BF16)

    zr = _dot_nt(wb_s[n_main:, :], h)[:2 * nh] + brow_ref[...]
    row = lax.broadcasted_iota(jnp.int32, zr.shape, 0)
    gates = jnp.where(row < nh, zr, _log_sigmoid(zr)) * LOG2E
    grow_ref[0] = gates
    padded = jnp.concatenate([gates, jnp.zeros((GATE_LANES - 2 * nh, tm), F32)], axis=0)
    gcol_ref[0] = padded.T


def _inproj_call(x, mod, g_mix, w_t, gq, gk, hp, brow, *, layer, attn_w, ml_w, dilations):
    b, s, d = x.shape
    tm = INPROJ_ROWS
    nh2 = 2 * N_MLSTM_HEADS
    n_main = 3 * attn_w + 4 * ml_w
    cols = n_main + nh2
    assert w_t.shape[0] % cols == 0 and cols % F32_SUBLANES == 0 and n_main % MXU_WIDTH == 0
    assert all(tm % (dl * BF16_SUBLANES) == 0 for dl in dilations)
    const = lambda shape: pl.BlockSpec(shape, lambda bi, i: (0,) * len(shape))
    tok = lambda w: pl.BlockSpec((1, tm, w), lambda bi, i: (bi, i, 0))
    strided_shapes = tuple(jax.ShapeDtypeStruct(_strided_shape(b, s, attn_w, dl), BF16) for dl in dilations)
    strided_specs = tuple(pl.BlockSpec((1, tm // dl, dl * attn_w), lambda bi, i: (bi, i, 0)) for dl in dilations)
    out_shape = strided_shapes * 3 + (
        jax.ShapeDtypeStruct((b, s, 2 * ml_w), BF16),
        jax.ShapeDtypeStruct((b, s, ml_w), BF16),
        jax.ShapeDtypeStruct((b, s, ml_w), BF16),
        jax.ShapeDtypeStruct((b, s, GATE_LANES), F32),
        jax.ShapeDtypeStruct((b, nh2, s), F32))
    outs = pl.pallas_call(
        functools.partial(_inproj_kernel, attn_w=attn_w, ml_w=ml_w, dilations=dilations),
        out_shape=out_shape,
        grid=(b, s // tm),
        in_specs=[tok(d),
                  pl.BlockSpec((1, N_MOD, d), lambda bi, i: (bi, 0, 0)),
                  const((1, d)),
                  pl.BlockSpec((cols, d), lambda bi, i: (layer, 0), pipeline_mode=pl.Buffered(1)),
                  const((1, attn_w)), const((1, attn_w)), const(hp.shape),
                  const((nh2, 1))],
        out_specs=strided_specs * 3 + (tok(2 * ml_w), tok(ml_w), tok(ml_w), tok(GATE_LANES),
                                       pl.BlockSpec((1, nh2, tm), lambda bi, i: (bi, 0, i))),
        scratch_shapes=[pltpu.VMEM((2, attn_w // LANES, tm, LANES), F32),
                        pltpu.VMEM((n_main + BF16_SUBLANES, d), BF16)],
        compiler_params=pltpu.CompilerParams(dimension_semantics=("arbitrary", "arbitrary"),
                                             vmem_limit_bytes=VMEM_LIMIT_BYTES),
        name="inproj",
    )(x, mod, g_mix, w_t, gq, gk, hp, brow)
    n_lay = len(dilations)
    return (outs[:n_lay], outs[n_lay:2 * n_lay], outs[2 * n_lay:3 * n_lay]) + tuple(outs[3 * n_lay:])


def _attn_kernel(q_ref, kp_ref, kc_ref, vp_ref, vc_ref, bias_ref, ones_ref, o_ref, lse_ref, *, n_heads):
    blk = ATTN_BLOCK
    sub_blocks = q_ref.shape[1] // blk
    hd = ATTN_HEAD_DIM
    w = n_heads * hd
    n_res = q_ref.shape[2] // w
    n_pairs = n_heads // 2
    n = pl.program_id(2)
    lane = lax.broadcasted_iota(jnp.int32, (blk, LANES), 1)
    first_head = lane < hd

    def key_rows(prev_ref, cur_ref, i, cols):
        if i == 0:
            return jnp.concatenate([prev_ref[0, :, cols], cur_ref[0, 0:blk, cols]], axis=0)
        return cur_ref[0, (i - 1) * blk:(i + 1) * blk, cols]

    units = [(res, i, j) for res in range(n_res) for i in range(sub_blocks) for j in range(n_pairs)]
    scores, maxes, probs = [], [], []
    for res, i, j in units:
        cols = slice(res * w + j * LANES, res * w + (j + 1) * LANES)
        q = q_ref[0, i * blk:(i + 1) * blk, cols]
        ks = key_rows(kp_ref, kc_ref, i, cols)
        bias = bias_ref[jnp.where(n == 0, 1, 0)] if i == 0 else bias_ref[0]
        q2 = jnp.concatenate([jnp.where(first_head, q, 0), jnp.where(first_head, 0, q)], axis=0)
        s = _dot_nt(q2, ks) + bias
        scores.append(s)
        maxes.append(jnp.max(s, axis=-1, keepdims=True))
    for s, m in zip(scores, maxes):
        probs.append(jnp.exp2(s - m).astype(BF16))
    for res in range(n_res):
        for i in range(sub_blocks):
            outs = []
            stats = jnp.zeros((blk, LANES), F32)
            for j in range(n_pairs):
                u = (res * sub_blocks + i) * n_pairs + j
                cols = slice(res * w + j * LANES, res * w + (j + 1) * LANES)
                vw = jnp.concatenate([key_rows(vp_ref, vc_ref, i, cols), ones_ref[...]], axis=1)
                ov = _dot(probs[u], vw)
                num = jnp.where(first_head, ov[:blk, :LANES], ov[blk:, :LANES])
                den = jnp.where(first_head, ov[:blk, LANES:], ov[blk:, LANES:])
                outs.append(num / den)
                m = jnp.broadcast_to(maxes[u], (2 * blk, LANES))
                lse = jnp.where(first_head, m[:blk], m[blk:]) * LN2 + jnp.log(den)
                stats = jnp.where((lane & (hd - 1)) == j, lse, stats)
            rows = slice(i * blk, (i + 1) * blk)
            o_ref[0, rows, res * w:(res + 1) * w] = jnp.concatenate(outs, axis=-1).astype(o_ref.dtype)
            lse_ref[0, rows, res * LANES:(res + 1) * LANES] = stats


def _attn_band_masks(n_back):
    blk = ATTN_BLOCK
    row = jnp.arange(2 * blk)[:, None] % blk
    col = jnp.arange(2 * blk)[None, :]
    band = (col >= row + (blk - n_back)) & (col <= row + blk)
    return jnp.stack([jnp.where(band, 0.0, NEG_INF), jnp.where(band & (col >= blk), 0.0, NEG_INF)]).astype(F32)


def _attn_call(q, k, v, *, width, window, dilation):
    b, ls, _ = q.shape
    w = width
    n_back = window // dilation
    blk = ATTN_BLOCK
    assert n_back <= blk
    sub_blocks = min(ATTN_MAX_SUB_BLOCKS, ls // blk)
    step_rows = sub_blocks * blk
    assert ls % step_rows == 0
    n_heads = w // ATTN_HEAD_DIM
    assert n_heads <= LANES and n_heads % 2 == 0 and 2 * ATTN_HEAD_DIM == LANES
    n_res = max(1, min(dilation, ATTN_MAX_SUB_BLOCKS // sub_blocks))
    assert dilation % n_res == 0
    cur = pl.BlockSpec((1, step_rows, n_res * w), lambda bi, r, n: (bi, n, r))
    prev = pl.BlockSpec((1, blk, n_res * w),
                        lambda bi, r, n: (bi, jnp.maximum(n * sub_blocks - 1, 0), r))
    masks = _attn_band_masks(n_back)
    ones = jnp.ones((2 * blk, LANES), BF16)
    const = lambda shape: pl.BlockSpec(shape, lambda bi, r, n: (0,) * len(shape))
    return pl.pallas_call(
        functools.partial(_attn_kernel, n_heads=n_heads),
        out_shape=(jax.ShapeDtypeStruct((b, ls, dilation * w), BF16),
                   jax.ShapeDtypeStruct((b, ls, dilation * LANES), F32)),
        grid=(b, dilation // n_res, ls // step_rows),
        in_specs=[cur, prev, cur, prev, cur, const(masks.shape), const(ones.shape)],
        out_specs=(cur, pl.BlockSpec((1, step_rows, n_res * LANES), lambda bi, r, n: (bi, n, r))),
        compiler_params=pltpu.CompilerParams(
            dimension_semantics=("arbitrary", "arbitrary", "arbitrary"),
            vmem_limit_bytes=VMEM_LIMIT_BYTES),
        name=f"attn_d{dilation}",
    )(q, k, k, v, v, masks, ones)


def _mlstm_kernel(qk_ref, v_ref, og_ref, gcol_ref, grow_ref, wc_ref, bc_ref, gn_ref, tri_ref, hm_ref,
                  o_ref, tail_s, c_s, m_s, *, ml_w):
    L = MLSTM_CHUNK
    nh = N_MLSTM_HEADS
    dh = ml_w // nh

    @pl.when(pl.program_id(1) == 0)
    def _():
        tail_s[...] = jnp.zeros_like(tail_s)
        c_s[...] = jnp.zeros_like(c_s)
        m_s[...] = jnp.zeros_like(m_s)

    tri = tri_ref[...]
    ti = lax.broadcasted_iota(jnp.int32, (L, L), 0)
    si = lax.broadcasted_iota(jnp.int32, (L, L), 1)
    causal = si <= ti
    ones = jnp.ones((L, dh), BF16)
    hp = hm_ref[...]
    for bi in range(qk_ref.shape[0]):
        _mlstm_chunk(bi, qk_ref, v_ref, og_ref, gcol_ref, grow_ref, wc_ref, bc_ref, gn_ref, o_ref,
                     tail_s, c_s, m_s, tri, causal, ones, hp, ml_w=ml_w)


def _mlstm_chunk(bi, qk_ref, v_ref, og_ref, gcol_ref, grow_ref, wc_ref, bc_ref, gn_ref, o_ref,
                 tail_s, c_s, m_s, tri, causal, ones, hp, *, ml_w):
    L = MLSTM_CHUNK
    nh = N_MLSTM_HEADS
    dh = ml_w // nh
    x = qk_ref[bi].astype(F32)
    tail = tail_s[bi]
    row8 = lax.broadcasted_iota(jnp.int32, tail.shape, 0)
    acc = bc_ref[...] + x * wc_ref[CONV_WIDTH - 1:CONV_WIDTH, :]
    for back in range(1, CONV_WIDTH):
        rolled = pltpu.roll(x, back, 0)
        top = jnp.where(row8 >= back, rolled[:MOD_ROWS], pltpu.roll(tail, back, 0))
        shifted = jnp.concatenate([top, rolled[MOD_ROWS:]], axis=0)
        acc = acc + shifted * wc_ref[CONV_WIDTH - 1 - back:CONV_WIDTH - back, :]
    tail_s[bi] = x[L - MOD_ROWS:]
    qk = acc * _sigmoid(acc)

    gcol = gcol_ref[bi]
    grow = grow_ref[bi]
    gc_parts, gr_parts = _bf16_terms(gcol), _bf16_terms(grow)
    bcol_3 = _dot(tri, jnp.concatenate(gc_parts, axis=1))
    bcol_all = bcol_3[:, :GATE_LANES] + bcol_3[:, GATE_LANES:2 * GATE_LANES] + bcol_3[:, 2 * GATE_LANES:]
    pad = jnp.zeros((BF16_SUBLANES - grow.shape[0], L), BF16)
    brow_3 = [_dot_nt(jnp.concatenate([part, pad], axis=0), tri)[:grow.shape[0]] for part in gr_parts]
    brow_all = brow_3[0] + brow_3[1] + brow_3[2]

    ks = [qk[:, ml_w + h * dh:ml_w + (h + 1) * dh] * (dh ** -0.5) for h in range(nh)]

    heads = []
    for h in range(nh):
        st = bi * nh + h
        b_c = jnp.broadcast_to(bcol_all[:, nh + h:nh + h + 1], (L, dh))
        i_c = jnp.broadcast_to(gcol[:, h:h + 1], (L, dh))
        b_r = brow_all[nh + h:nh + h + 1, :]
        i_r = grow[h:h + 1, :]
        m_prev = m_s[st, 0:1, :]
        log_d = jnp.where(causal, b_c[:, 0:1] + (i_r - b_r), NEG_INF)
        m_inter = b_c + m_prev
        m_t = jnp.maximum(m_inter, jnp.max(log_d, axis=-1, keepdims=True))
        d_mat = jnp.exp2(log_d - m_t[:, 0:1])
        inter = jnp.exp2(m_inter - m_t)
        b_last = b_c[L - 1:L, :]
        w_log = b_last - b_c + i_c
        m_new = jnp.maximum(b_last + m_prev, jnp.max(w_log, axis=0, keepdims=True))
        wgt = jnp.exp2(w_log - m_new)
        decay = jnp.exp2(b_last + m_prev - m_new)
        heads.append((m_t, d_mat, inter, m_new, wgt, decay))

    outs = []
    for h, (m_t, d_mat, inter, _, _, _) in enumerate(heads):
        hcols = slice(h * dh, (h + 1) * dh)
        q = qk[:, hcols].astype(BF16)
        v_ext = jnp.concatenate([v_ref[bi, :, hcols], ones], axis=-1)
        c_prev = c_s[bi * nh + h]
        s_qk = _dot_nt(q, ks[h].astype(BF16)) * d_mat
        ext = jnp.concatenate([inter, inter], axis=-1) * _dot(q, c_prev.astype(BF16)) \
            + _dot(s_qk.astype(BF16), v_ext)
        outs.append(ext[:, :dh] / jnp.maximum(jnp.abs(ext[:, dh:]), jnp.exp2(-m_t)))
    hh = jnp.concatenate(outs, axis=-1)
    h2 = (hh * hh).astype(BF16)
    gw = hp.shape[0]
    msq = jnp.concatenate([_dot(h2[:, c0:c0 + gw], hp) for c0 in range(0, ml_w, gw)], axis=-1)
    hn = hh * lax.rsqrt(msq + RMS_EPS) * gn_ref[...]
    o_ref[bi] = (og_ref[bi].astype(F32) * hn).astype(o_ref.dtype)

    for h, (_, _, _, m_new, wgt, decay) in enumerate(heads):
        st = bi * nh + h
        hcols = slice(h * dh, (h + 1) * dh)
        v_ext = jnp.concatenate([v_ref[bi, :, hcols], ones], axis=-1)
        kw = (ks[h] * wgt).astype(BF16)
        c_s[st] = jnp.concatenate([decay, decay], axis=-1) * c_s[st] + _dot_tn(kw, v_ext)
        m_s[st] = jnp.broadcast_to(m_new, m_s.shape[1:])


def _mlstm_call(qkm, vm, og, gcol, grow, w_conv, b_conv, g_norm, tri, hmean):
    b, s, ml_w = vm.shape
    L = MLSTM_CHUNK
    nh = N_MLSTM_HEADS
    dh = ml_w // nh
    bb = MLSTM_BATCH_ROWS
    assert b % bb == 0
    const = lambda shape: pl.BlockSpec(shape, lambda g, i: (0,) * len(shape))
    tok = lambda w: pl.BlockSpec((bb, L, w), lambda g, i: (g, i, 0))
    return pl.pallas_call(
        functools.partial(_mlstm_kernel, ml_w=ml_w),
        out_shape=jax.ShapeDtypeStruct((b, s, ml_w), BF16),
        grid=(b // bb, s // L),
        in_specs=[tok(2 * ml_w), tok(ml_w), tok(ml_w), tok(GATE_LANES),
                  pl.BlockSpec((bb, 2 * nh, L), lambda g, i: (g, 0, i)),
                  const(w_conv.shape), const((1, 2 * ml_w)), const((1, ml_w)),
                  const((L, L)), const(hmean.shape)],
        out_specs=tok(ml_w),
        scratch_shapes=[pltpu.VMEM((bb, MOD_ROWS, 2 * ml_w), F32),
                        pltpu.VMEM((bb * nh, dh, 2 * dh), F32),
                        pltpu.VMEM((bb * nh, MOD_ROWS, dh), F32)],
        compiler_params=pltpu.CompilerParams(dimension_semantics=("arbitrary", "arbitrary"),
                                             vmem_limit_bytes=VMEM_LIMIT_BYTES),
        name="mlstm",
    )(qkm, vm, og, gcol, grow, w_conv, b_conv, g_norm, tri, hmean)


def _outffn_kernel(x_ref, mod_ref, g_ref, *rest, dilations):
    n_lay = len(dilations)
    o_refs, l_refs = rest[:n_lay], rest[n_lay:2 * n_lay]
    (hm_ref, ex_ref, wo_hbm, wg_hbm, wu_hbm, wd_hbm, out_ref,
     operm_s, lperm_s, wo_ref, wg_ref, wu_ref, wd_ref, wide_s, narrow_s, sems) = rest[2 * n_lay:]
    tm = x_ref.shape[1]

    @pl.when((pl.program_id(0) == 0) & (pl.program_id(1) == 0))
    def _():
        jobs = []
        for src, dst, stage, kind in ((wg_hbm, wg_ref, wide_s, 0), (wu_hbm, wu_ref, wide_s, 0),
                                      (wd_hbm, wd_ref, narrow_s, 1), (wo_hbm, wo_ref, narrow_s, 1)):
            rc = stage.shape[1]
            for c in range(src.shape[0] // rc):
                slot = sum(1 for j in jobs if j[0] == kind) % 2
                copy = pltpu.make_async_copy(src.at[pl.ds(c * rc, rc), :], stage.at[slot], sems.at[kind, slot])
                jobs.append((kind, copy, stage, slot, dst, c * rc, rc))
        jobs[0][1].start()
        for n, (_, copy, stage, slot, dst, r0, rc) in enumerate(jobs):
            if n + 1 < len(jobs):
                jobs[n + 1][1].start()
            copy.wait()
            dst[r0:r0 + rc, :] = stage[slot].astype(BF16)

    def natural(ref, d, scratch):
        if d == 1:
            return ref[0].astype(F32)
        n_cg = scratch.shape[1]
        chain = [step for step in _gather_chain(dilations) if step[1] <= d]
        for level, (parent, dd, f) in enumerate(reversed(chain)):
            dst = scratch.at[level % 2]
            src = scratch.at[(level + 1) % 2]
            n = tm // dd
            for c in range(parent):
                for a in range(f):
                    r = a * parent + c
                    for cg in range(n_cg):
                        if level == 0:
                            c0 = (r * n_cg + cg) * LANES
                            rows = ref[0, :, c0:c0 + LANES].astype(F32)
                        else:
                            rows = src[cg, r * n:(r + 1) * n, :]
                        dst[cg, pl.ds(c * (tm // parent) + a, n, stride=f), :] = rows
        final = scratch.at[(len(chain) - 1) % 2]
        return jnp.concatenate([final[cg] for cg in range(n_cg)], axis=-1)

    lses = [natural(ref, d, lperm_s) for ref, d in zip(l_refs, dilations)]
    mx = functools.reduce(jnp.maximum, lses)
    es = [jnp.exp(l - mx) for l in lses]
    inv = 1.0 / functools.reduce(jnp.add, es)
    ex = ex_ref[...]
    attn = None
    for e, ref, d in zip(es, o_refs, dilations):
        term = _dot((e * inv).astype(BF16), ex) * natural(ref, d, operm_s)
        attn = term if attn is None else attn + term
    aw = attn.shape[-1]
    y = _dot(attn.astype(BF16), wo_ref[:aw, :]) + _dot(hm_ref[0], wo_ref[aw:, :])
    x1 = x_ref[0] + mod_ref[0, 2:3, :] * y
    ms = jnp.mean(x1 * x1, axis=-1, keepdims=True)
    hn = x1 * lax.rsqrt(ms + RMS_EPS) * g_ref[...]
    hb = (hn * (1.0 + mod_ref[0, 4:5, :]) + mod_ref[0, 3:4, :]).astype(BF16)
    g = _dot(hb, wg_ref[...])
    u = _dot(hb, wu_ref[...])
    a = (g * _sigmoid(g) * u).astype(BF16)
    out_ref[0] = x1 + mod_ref[0, 5:6, :] * _dot(a, wd_ref[...])


def _outffn_call(x, mod, g_ffn, os_, lses, hm, expand, wo, wg, wu, wd, *, dilations):
    b, s, d = x.shape
    tm = FFN_ROWS
    assert all(tm % (dl * BF16_SUBLANES) == 0 for dl in dilations)
    aw = expand.shape[1]
    const = lambda shape: pl.BlockSpec(shape, lambda bi, i: (0,) * len(shape))
    tok = lambda w: pl.BlockSpec((1, tm, w), lambda bi, i: (bi, i, 0))
    strided = lambda w: tuple(pl.BlockSpec((1, tm // dl, dl * w), lambda bi, i: (bi, i, 0)) for dl in dilations)
    return pl.pallas_call(
        functools.partial(_outffn_kernel, dilations=dilations),
        out_shape=jax.ShapeDtypeStruct((b, s, d), F32),
        grid=(b, s // tm),
        in_specs=[tok(d), pl.BlockSpec((1, N_MOD, d), lambda bi, i: (bi, 0, 0)), const((1, d)),
                  *strided(aw), *strided(LANES), tok(hm.shape[-1]),
                  const(expand.shape)] + [pl.BlockSpec(memory_space=pl.ANY)] * 4,
        out_specs=tok(d),
        scratch_shapes=[pltpu.VMEM((2, aw // LANES, tm, LANES), F32), pltpu.VMEM((2, 1, tm, LANES), F32),
                        pltpu.VMEM(wo.shape, BF16), pltpu.VMEM(wg.shape, BF16), pltpu.VMEM(wu.shape, BF16),
                        pltpu.VMEM(wd.shape, BF16),
                        pltpu.VMEM((2, WEIGHT_CHUNK_ROWS // 2, wg.shape[1]), F32),
                        pltpu.VMEM((2, WEIGHT_CHUNK_ROWS, wd.shape[1]), F32),
                        pltpu.SemaphoreType.DMA((2, 2))],
        compiler_params=pltpu.CompilerParams(dimension_semantics=("arbitrary", "arbitrary"),
                                             vmem_limit_bytes=VMEM_LIMIT_BYTES),
        name="outffn",
    )(x, mod, g_ffn, *os_, *lses, hm, expand, wo, wg, wu, wd)


def _block_diag_mean(width, group):
    idx = jnp.arange(width) // group
    return jnp.where(idx[:, None] == idx[None, :], 1.0 / group, 0.0).astype(BF16)


def kernel(x, c, g_mix, w_in, w_conv, b_conv, b_igate, b_fgate, q_norm_g, k_norm_g, mlstm_norm_g, w_out,
           g_ffn, w_gate, w_up, w_down, w_ada, b_ada):
    b, s, d = x.shape
    depth = g_mix.shape[0]
    attn_w = d // 2
    ml_w = d - attn_w
    nh = N_MLSTM_HEADS
    n_attn_heads = attn_w // ATTN_HEAD_DIM
    scale = ATTN_HEAD_DIM ** -0.5 * LOG2E
    dilations = tuple(dl for _, dl in DILATED_PATTERNS)
    assert b <= MOD_ROWS and 2 * nh <= GATE_LANES

    c_t = jnp.zeros((d, MOD_ROWS), F32).at[:, :b].set(c.T)
    assert attn_w % MXU_WIDTH == 0 and MXU_WIDTH % ATTN_HEAD_DIM == 0
    hmean_attn = _block_diag_mean(MXU_WIDTH, ATTN_HEAD_DIM)
    assert ml_w % MXU_WIDTH == 0 and MXU_WIDTH % (ml_w // nh) == 0
    hmean_ml = _block_diag_mean(MXU_WIDTH, ml_w // nh)
    stat_lane = jnp.arange(LANES)
    head_of_stat = jnp.where(stat_lane % ATTN_HEAD_DIM < n_attn_heads // 2,
                             2 * (stat_lane % ATTN_HEAD_DIM) + stat_lane // ATTN_HEAD_DIM, -1)
    head_of_col = jnp.arange(attn_w) // ATTN_HEAD_DIM
    expand = (head_of_stat[:, None] == head_of_col[None, :]).astype(BF16)
    tri = jnp.tril(jnp.ones((MLSTM_CHUNK, MLSTM_CHUNK), BF16))
    w_in_t = jnp.swapaxes(w_in, 1, 2).reshape(-1, d)

    for l in range(depth):
        mod = _mod_call(c_t, w_ada, b_ada[l][None, :], layer=l, n_rows=b)[:b].reshape(b, N_MOD, d)

        brow = jnp.concatenate([b_igate[l], b_fgate[l]])[:, None]
        gq = jnp.tile(q_norm_g[l] * scale, n_attn_heads)[None, :]
        gk = jnp.tile(k_norm_g[l], n_attn_heads)[None, :]

        qs, ks, vs, qkm, vm, og, gcol, grow = _inproj_call(
            x, mod, g_mix[l][None, :], w_in_t, gq, gk, hmean_attn, brow,
            layer=l, attn_w=attn_w, ml_w=ml_w, dilations=dilations)

        os_, lses = [], []
        for (window, dilation), q, k, v in zip(DILATED_PATTERNS, qs, ks, vs):
            o, lse = _attn_call(q, k, v, width=attn_w, window=window, dilation=dilation)
            os_.append(o)
            lses.append(lse)

        hm = _mlstm_call(qkm, vm, og, gcol, grow, w_conv[l], b_conv[l][None, :],
                         mlstm_norm_g[l][None, :], tri, hmean_ml)

        x = _outffn_call(x, mod, g_ffn[l][None, :], os_, lses, hm, expand,
                         w_out[l], w_gate[l], w_up[l], w_down[l], dilations=dilations)
    return x
```

```python
import functools

import jax
import jax.numpy as jnp
from jax import lax
from jax.experimental import pallas as pl
from jax.experimental.pallas import tpu as pltpu

F32 = jnp.float32
BF16 = jnp.bfloat16

ATTN_HEAD_DIM = 64
N_MLSTM_HEADS = 4
CONV_WIDTH = 4
DILATED_PATTERNS = ((128, 1), (512, 4), (2048, 16))
ATTN_BLOCK = 128
N_MOD = 6
RMS_EPS = 1e-6

LANES = 128
MXU_WIDTH = 256
BF16_SUBLANES = 16
VMEM_LIMIT_BYTES = 56 * 1024 * 1024
F32_SUBLANES = 8
MOD_ROWS = F32_SUBLANES
MOD_COL_STEPS = 4
LAYOUT_ROWS = 512
GATE_LANES = 128
INPROJ_ROWS = LAYOUT_ROWS
ATTN_MAX_SUB_BLOCKS = 16
MLSTM_CHUNK = 256
WEIGHT_CHUNK_ROWS = 128
WEIGHT_STAGE_SLOTS = 4
MLSTM_BATCH_ROWS = 1
FFN_ROWS = LAYOUT_ROWS

NEG_INF = float("-inf")
LOG2E = 1.4426950408889634
LN2 = 0.6931471805599453


def _dot(a, b, **kw):
    return jnp.dot(a, b, preferred_element_type=F32, **kw)


def _dot_nt(a, b, **kw):
    return lax.dot_general(a, b, (((1,), (1,)), ((), ())), preferred_element_type=F32, **kw)


def _dot_tn(a, b):
    return lax.dot_general(a, b, (((0,), (0,)), ((), ())), preferred_element_type=F32)


def _sigmoid(z):
    return 1.0 / (1.0 + jnp.exp(-z))


def _log_sigmoid(z):
    return jnp.minimum(z, 0.0) - jnp.log1p(jnp.exp(-jnp.abs(z)))


def _bf16_terms(x):
    hi = x.astype(BF16)
    r1 = x - hi.astype(F32)
    mid = r1.astype(BF16)
    lo = (r1 - mid.astype(F32)).astype(BF16)
    return [hi, mid, lo]


def _strided_shape(b, s, w, dilation):
    return (b, s // dilation, dilation * w)


def _gather_chain(dilations):
    chain, parent = [], 1
    for d in sorted(set(dilations)):
        if d == 1:
            continue
        assert d % parent == 0 and d // parent < F32_SUBLANES
        chain.append((parent, d, d // parent))
        parent = d
    return chain


def _mod_kernel(ct_ref, w_ref, b_ref, o_ref, *, n_rows):
    ct = ct_ref[...]
    sc = ct * _sigmoid(ct)
    w = w_ref[...]
    rows = [jnp.sum(w * sc[:, r:r + 1], axis=0, keepdims=True) for r in range(n_rows)]
    rows += [jnp.zeros_like(rows[0])] * (MOD_ROWS - n_rows)
    o_ref[...] = jnp.concatenate(rows, axis=0) + b_ref[...]


def _mod_call(c_t, w_ada, b_ada, *, layer, n_rows):
    _, d, n = w_ada.shape
    assert n % (MOD_COL_STEPS * LANES) == 0
    tn = n // MOD_COL_STEPS
    return pl.pallas_call(
        functools.partial(_mod_kernel, n_rows=n_rows),
        out_shape=jax.ShapeDtypeStruct((MOD_ROWS, n), F32),
        grid=(n // tn,),
        in_specs=[pl.BlockSpec((d, MOD_ROWS), lambda j: (0, 0)),
                  pl.BlockSpec((None, d, tn), lambda j: (layer, 0, j)),
                  pl.BlockSpec((1, tn), lambda j: (0, j))],
        out_specs=pl.BlockSpec((MOD_ROWS, tn), lambda j: (0, j)),
        compiler_params=pltpu.CompilerParams(dimension_semantics=("arbitrary",),
                                             vmem_limit_bytes=VMEM_LIMIT_BYTES),
        name="mod",
    )(c_t, w_ada, b_ada)


def _inproj_kernel(x_ref, mod_ref, g_ref, wt_ref, gq_ref, gk_ref, hp_ref,
                   brow_ref, *rest, attn_w, ml_w, dilations):
    n_lay = len(dilations)
    q_refs, k_refs, v_refs = rest[:n_lay], rest[n_lay:2 * n_lay], rest[2 * n_lay:3 * n_lay]
    qkm_ref, vm_ref, og_ref, gcol_ref, grow_ref, perm_s, wb_s = rest[3 * n_lay:]
    tm = x_ref.shape[1]
    nh = N_MLSTM_HEADS
    n_attn = 3 * attn_w
    n_main = n_attn + 4 * ml_w

    @pl.when((pl.program_id(0) == 0) & (pl.program_id(1) == 0))
    def _():
        for r0 in range(0, n_main, MXU_WIDTH):
            wb_s[r0:r0 + MXU_WIDTH, :] = wt_ref[r0:r0 + MXU_WIDTH, :].astype(BF16)
        gate_rows = jnp.concatenate([wt_ref[n_main:n_main + 2 * nh, :],
                                     jnp.zeros((BF16_SUBLANES - 2 * nh, wt_ref.shape[1]), F32)], axis=0)
        wb_s[n_main:, :] = gate_rows.astype(BF16)

    def emit(val, refs):
        w = val.shape[-1]
        n_cg = w // LANES
        ref_of = dict(zip(dilations, refs))
        if 1 in ref_of:
            ref_of[1][0] = val.astype(BF16)
        chain = _gather_chain(dilations)
        for cg in range(n_cg):
            perm_s[0, cg] = val[:, cg * LANES:(cg + 1) * LANES]
        for level, (parent, d, f) in enumerate(chain):
            src, dst = perm_s.at[level % 2], perm_s.at[(level + 1) % 2]
            n = tm // d
            for c in range(parent):
                for a in range(f):
                    r = a * parent + c
                    for cg in range(n_cg):
                        rows = src[cg, pl.ds(c * (tm // parent) + a, n, stride=f), :]
                        c0 = r * w + cg * LANES
                        ref_of[d][0, :, c0:c0 + LANES] = rows.astype(BF16)
                        if level + 1 < len(chain):
                            dst[cg, r * n:(r + 1) * n, :] = rows

    x = x_ref[0]
    ms = jnp.mean(x * x, axis=-1, keepdims=True)
    y = x * lax.rsqrt(ms + RMS_EPS) * g_ref[...]
    h = (y * (1.0 + mod_ref[0, 1:2, :]) + mod_ref[0, 0:1, :]).astype(BF16)

    xa = _dot_nt(h, wb_s[:n_attn, :])
    hp = hp_ref[...]

    def head_norm(t, g):
        t2 = (t * t).astype(BF16)
        gw = hp.shape[0]
        msq = jnp.concatenate([_dot(t2[:, c0:c0 + gw], hp) for c0 in range(0, t.shape[-1], gw)], axis=-1)
        return t * lax.rsqrt(msq + RMS_EPS) * g

    emit(head_norm(xa[:, :attn_w], gq_ref[...]), q_refs)
    emit(head_norm(xa[:, attn_w:2 * attn_w], gk_ref[...]), k_refs)
    emit(xa[:, 2 * attn_w:], v_refs)

    xm = _dot_nt(h, wb_s[n_attn:n_main, :])
    qkm_ref[0] = xm[:, :2 * ml_w].astype(BF16)
    vm_ref[0] = xm[:, 2 * ml_w:3 * ml_w].astype(BF16)
    og_ref[0] = _sigmoid(xm[:, 3 * ml_w:]).astype(BF16)

    zr = _dot_nt(wb_s[n_main:, :], h)[:2 * nh] + brow_ref[...]
    row = lax.broadcasted_iota(jnp.int32, zr.shape, 0)
    gates = jnp.where(row < nh, zr, _log_sigmoid(zr)) * LOG2E
    grow_ref[0] = gates
    padded = jnp.concatenate([gates, jnp.zeros((GATE_LANES - 2 * nh, tm), F32)], axis=0)
    gcol_ref[0] = padded.T


def _inproj_call(x, mod, g_mix, w_t, gq, gk, hp, brow, *, layer, attn_w, ml_w, dilations):
    b, s, d = x.shape
    tm = INPROJ_ROWS
    nh2 = 2 * N_MLSTM_HEADS
    n_main = 3 * attn_w + 4 * ml_w
    cols = n_main + nh2
    assert w_t.shape[0] % cols == 0 and cols % F32_SUBLANES == 0 and n_main % MXU_WIDTH == 0
    assert all(tm % (dl * BF16_SUBLANES) == 0 for dl in dilations)
    const = lambda shape: pl.BlockSpec(shape, lambda bi, i: (0,) * len(shape))
    tok = lambda w: pl.BlockSpec((1, tm, w), lambda bi, i: (bi, i, 0))
    strided_shapes = tuple(jax.ShapeDtypeStruct(_strided_shape(b, s, attn_w, dl), BF16) for dl in dilations)
    strided_specs = tuple(pl.BlockSpec((1, tm // dl, dl * attn_w), lambda bi, i: (bi, i, 0)) for dl in dilations)
    out_shape = strided_shapes * 3 + (
        jax.ShapeDtypeStruct((b, s, 2 * ml_w), BF16),
        jax.ShapeDtypeStruct((b, s, ml_w), BF16),
        jax.ShapeDtypeStruct((b, s, ml_w), BF16),
        jax.ShapeDtypeStruct((b, s, GATE_LANES), F32),
        jax.ShapeDtypeStruct((b, nh2, s), F32))
    outs = pl.pallas_call(
        functools.partial(_inproj_kernel, attn_w=attn_w, ml_w=ml_w, dilations=dilations),
        out_shape=out_shape,
        grid=(b, s // tm),
        in_specs=[tok(d),
                  pl.BlockSpec((1, N_MOD, d), lambda bi, i: (bi, 0, 0)),
                  const((1, d)),
                  pl.BlockSpec((cols, d), lambda bi, i: (layer, 0), pipeline_mode=pl.Buffered(1)),
                  const((1, attn_w)), const((1, attn_w)), const(hp.shape),
                  const((nh2, 1))],
        out_specs=strided_specs * 3 + (tok(2 * ml_w), tok(ml_w), tok(ml_w), tok(GATE_LANES),
                                       pl.BlockSpec((1, nh2, tm), lambda bi, i: (bi, 0, i))),
        scratch_shapes=[pltpu.VMEM((2, attn_w // LANES, tm, LANES), F32),
                        pltpu.VMEM((n_main + BF16_SUBLANES, d), BF16)],
        compiler_params=pltpu.CompilerParams(dimension_semantics=("arbitrary", "arbitrary"),
                                             vmem_limit_bytes=VMEM_LIMIT_BYTES),
        name="inproj",
    )(x, mod, g_mix, w_t, gq, gk, hp, brow)
    n_lay = len(dilations)
    return (outs[:n_lay], outs[n_lay:2 * n_lay], outs[2 * n_lay:3 * n_lay]) + tuple(outs[3 * n_lay:])


def _attn_kernel(q_ref, kp_ref, kc_ref, vp_ref, vc_ref, bias_ref, ones_ref, o_ref, lse_ref, *, n_heads):
    blk = ATTN_BLOCK
    sub_blocks = q_ref.shape[1] // blk
    hd = ATTN_HEAD_DIM
    w = n_heads * hd
    n_res = q_ref.shape[2] // w
    n_pairs = n_heads // 2
    n = pl.program_id(2)
    lane = lax.broadcasted_iota(jnp.int32, (blk, LANES), 1)
    first_head = lane < hd

    def key_rows(prev_ref, cur_ref, i, cols):
        if i == 0:
            return jnp.concatenate([prev_ref[0, :, cols], cur_ref[0, 0:blk, cols]], axis=0)
        return cur_ref[0, (i - 1) * blk:(i + 1) * blk, cols]

    units = [(res, i, j) for res in range(n_res) for i in range(sub_blocks) for j in range(n_pairs)]
    scores, maxes, probs = [], [], []
    for res, i, j in units:
        cols = slice(res * w + j * LANES, res * w + (j + 1) * LANES)
        q = q_ref[0, i * blk:(i + 1) * blk, cols]
        ks = key_rows(kp_ref, kc_ref, i, cols)
        bias = bias_ref[jnp.where(n == 0, 1, 0)] if i == 0 else bias_ref[0]
        q2 = jnp.concatenate([jnp.where(first_head, q, 0), jnp.where(first_head, 0, q)], axis=0)
        s = _dot_nt(q2, ks) + bias
        scores.append(s)
        maxes.append(jnp.max(s, axis=-1, keepdims=True))
    for s, m in zip(scores, maxes):
        probs.append(jnp.exp2(s - m).astype(BF16))
    for res in range(n_res):
        for i in range(sub_blocks):
            outs = []
            stats = jnp.zeros((blk, LANES), F32)
            for j in range(n_pairs):
                u = (res * sub_blocks + i) * n_pairs + j
                cols = slice(res * w + j * LANES, res * w + (j + 1) * LANES)
                vw = jnp.concatenate([key_rows(vp_ref, vc_ref, i, cols), ones_ref[...]], axis=1)
                ov = _dot(probs[u], vw)
                num = jnp.where(first_head, ov[:blk, :LANES], ov[blk:, :LANES])
                den = jnp.where(first_head, ov[:blk, LANES:], ov[blk:, LANES:])
                outs.append(num / den)
                m = jnp.broadcast_to(maxes[u], (2 * blk, LANES))
                lse = jnp.where(first_head, m[:blk], m[blk:]) * LN2 + jnp.log(den)
                stats = jnp.where((lane & (hd - 1)) == j, lse, stats)
            rows = slice(i * blk, (i + 1) * blk)
            o_ref[0, rows, res * w:(res + 1) * w] = jnp.concatenate(outs, axis=-1).astype(o_ref.dtype)
            lse_ref[0, rows, res * LANES:(res + 1) * LANES] = stats


def _attn_band_masks(n_back):
    blk = ATTN_BLOCK
    row = jnp.arange(2 * blk)[:, None] % blk
    col = jnp.arange(2 * blk)[None, :]
    band = (col >= row + (blk - n_back)) & (col <= row + blk)
    return jnp.stack([jnp.where(band, 0.0, NEG_INF), jnp.where(band & (col >= blk), 0.0, NEG_INF)]).astype(F32)


def _attn_call(q, k, v, *, width, window, dilation):
    b, ls, _ = q.shape
    w = width
    n_back = window // dilation
    blk = ATTN_BLOCK
    assert n_back <= blk
    sub_blocks = min(ATTN_MAX_SUB_BLOCKS, ls // blk)
    step_rows = sub_blocks * blk
    assert ls % step_rows == 0
    n_heads = w // ATTN_HEAD_DIM
    assert n_heads <= LANES and n_heads % 2 == 0 and 2 * ATTN_HEAD_DIM == LANES
    n_res = max(1, min(dilation, ATTN_MAX_SUB_BLOCKS // sub_blocks))
    assert dilation % n_res == 0
    cur = pl.BlockSpec((1, step_rows, n_res * w), lambda bi, r, n: (bi, n, r))
    prev = pl.BlockSpec((1, blk, n_res * w),
                        lambda bi, r, n: (bi, jnp.maximum(n * sub_blocks - 1, 0), r))
    masks = _attn_band_masks(n_back)
    ones = jnp.ones((2 * blk, LANES), BF16)
    const = lambda shape: pl.BlockSpec(shape, lambda bi, r, n: (0,) * len(shape))
    return pl.pallas_call(
        functools.partial(_attn_kernel, n_heads=n_heads),
        out_shape=(jax.ShapeDtypeStruct((b, ls, dilation * w), BF16),
                   jax.ShapeDtypeStruct((b, ls, dilation * LANES), F32)),
        grid=(b, dilation // n_res, ls // step_rows),
        in_specs=[cur, prev, cur, prev, cur, const(masks.shape), const(ones.shape)],
        out_specs=(cur, pl.BlockSpec((1, step_rows, n_res * LANES), lambda bi, r, n: (bi, n, r))),
        compiler_params=pltpu.CompilerParams(
            dimension_semantics=("arbitrary", "arbitrary", "arbitrary"),
            vmem_limit_bytes=VMEM_LIMIT_BYTES),
        name=f"attn_d{dilation}",
    )(q, k, k, v, v, masks, ones)


def _mlstm_kernel(qk_ref, v_ref, og_ref, gcol_ref, grow_ref, wc_ref, bc_ref, gn_ref, tri_ref, hm_ref,
                  o_ref, tail_s, c_s, m_s, *, ml_w):
    L = MLSTM_CHUNK
    nh = N_MLSTM_HEADS
    dh = ml_w // nh

    @pl.when(pl.program_id(1) == 0)
    def _():
        tail_s[...] = jnp.zeros_like(tail_s)
        c_s[...] = jnp.zeros_like(c_s)
        m_s[...] = jnp.zeros_like(m_s)

    tri = tri_ref[...]
    ti = lax.broadcasted_iota(jnp.int32, (L, L), 0)
    si = lax.broadcasted_iota(jnp.int32, (L, L), 1)
    causal = si <= ti
    ones = jnp.ones((L, dh), BF16)
    hp = hm_ref[...]
    for bi in range(qk_ref.shape[0]):
        _mlstm_chunk(bi, qk_ref, v_ref, og_ref, gcol_ref, grow_ref, wc_ref, bc_ref, gn_ref, o_ref,
                     tail_s, c_s, m_s, tri, causal, ones, hp, ml_w=ml_w)


def _mlstm_chunk(bi, qk_ref, v_ref, og_ref, gcol_ref, grow_ref, wc_ref, bc_ref, gn_ref, o_ref,
                 tail_s, c_s, m_s, tri, causal, ones, hp, *, ml_w):
    L = MLSTM_CHUNK
    nh = N_MLSTM_HEADS
    dh = ml_w // nh
    x = qk_ref[bi].astype(F32)
    tail = tail_s[bi]
    row8 = lax.broadcasted_iota(jnp.int32, tail.shape, 0)
    acc = bc_ref[...] + x * wc_ref[CONV_WIDTH - 1:CONV_WIDTH, :]
    for back in range(1, CONV_WIDTH):
        rolled = pltpu.roll(x, back, 0)
        top = jnp.where(row8 >= back, rolled[:MOD_ROWS], pltpu.roll(tail, back, 0))
        shifted = jnp.concatenate([top, rolled[MOD_ROWS:]], axis=0)
        acc = acc + shifted * wc_ref[CONV_WIDTH - 1 - back:CONV_WIDTH - back, :]
    tail_s[bi] = x[L - MOD_ROWS:]
    qk = acc * _sigmoid(acc)

    gcol = gcol_ref[bi]
    grow = grow_ref[bi]
    gc_parts, gr_parts = _bf16_terms(gcol), _bf16_terms(grow)
    bcol_3 = _dot(tri, jnp.concatenate(gc_parts, axis=1))
    bcol_all = bcol_3[:, :GATE_LANES] + bcol_3[:, GATE_LANES:2 * GATE_LANES] + bcol_3[:, 2 * GATE_LANES:]
    pad = jnp.zeros((BF16_SUBLANES - grow.shape[0], L), BF16)
    brow_3 = [_dot_nt(jnp.concatenate([part, pad], axis=0), tri)[:grow.shape[0]] for part in gr_parts]
    brow_all = brow_3[0] + brow_3[1] + brow_3[2]

    ks = [qk[:, ml_w + h * dh:ml_w + (h + 1) * dh] * (dh ** -0.5) for h in range(nh)]

    heads = []
    for h in range(nh):
        st = bi * nh + h
        b_c = jnp.broadcast_to(bcol_all[:, nh + h:nh + h + 1], (L, dh))
        i_c = jnp.broadcast_to(gcol[:, h:h + 1], (L, dh))
        b_r = brow_all[nh + h:nh + h + 1, :]
        i_r = grow[h:h + 1, :]
        m_prev = m_s[st, 0:1, :]
        log_d = jnp.where(causal, b_c[:, 0:1] + (i_r - b_r), NEG_INF)
        m_inter = b_c + m_prev
        m_t = jnp.maximum(m_inter, jnp.max(log_d, axis=-1, keepdims=True))
        d_mat = jnp.exp2(log_d - m_t[:, 0:1])
        inter = jnp.exp2(m_inter - m_t)
        b_last = b_c[L - 1:L, :]
        w_log = b_last - b_c + i_c
        m_new = jnp.maximum(b_last + m_prev, jnp.max(w_log, axis=0, keepdims=True))
        wgt = jnp.exp2(w_log - m_new)
        decay = jnp.exp2(b_last + m_prev - m_new)
        heads.append((m_t, d_mat, inter, m_new, wgt, decay))

    outs = []
    for h, (m_t, d_mat, inter, _, _, _) in enumerate(heads):
        hcols = slice(h * dh, (h + 1) * dh)
        q = qk[:, hcols].astype(BF16)
        v_ext = jnp.concatenate([v_ref[bi, :, hcols], ones], axis=-1)
        c_prev = c_s[bi * nh + h]
        s_qk = _dot_nt(q, ks[h].astype(BF16)) * d_mat
        ext = jnp.concatenate([inter, inter], axis=-1) * _dot(q, c_prev.astype(BF16)) \
            + _dot(s_qk.astype(BF16), v_ext)
        outs.append(ext[:, :dh] / jnp.maximum(jnp.abs(ext[:, dh:]), jnp.exp2(-m_t)))
    hh = jnp.concatenate(outs, axis=-1)
    h2 = (hh * hh).astype(BF16)
    gw = hp.shape[0]
    msq = jnp.concatenate([_dot(h2[:, c0:c0 + gw], hp) for c0 in range(0, ml_w, gw)], axis=-1)
    hn = hh * lax.rsqrt(msq + RMS_EPS) * gn_ref[...]
    o_ref[bi] = (og_ref[bi].astype(F32) * hn).astype(o_ref.dtype)

    for h, (_, _, _, m_new, wgt, decay) in enumerate(heads):
        st = bi * nh + h
        hcols = slice(h * dh, (h + 1) * dh)
        v_ext = jnp.concatenate([v_ref[bi, :, hcols], ones], axis=-1)
        kw = (ks[h] * wgt).astype(BF16)
        c_s[st] = jnp.concatenate([decay, decay], axis=-1) * c_s[st] + _dot_tn(kw, v_ext)
        m_s[st] = jnp.broadcast_to(m_new, m_s.shape[1:])


def _mlstm_call(qkm, vm, og, gcol, grow, w_conv, b_conv, g_norm, tri, hmean):
    b, s, ml_w = vm.shape
    L = MLSTM_CHUNK
    nh = N_MLSTM_HEADS
    dh = ml_w // nh
    bb = MLSTM_BATCH_ROWS
    assert b % bb == 0
    const = lambda shape: pl.BlockSpec(shape, lambda g, i: (0,) * len(shape))
    tok = lambda w: pl.BlockSpec((bb, L, w), lambda g, i: (g, i, 0))
    return pl.pallas_call(
        functools.partial(_mlstm_kernel, ml_w=ml_w),
        out_shape=jax.ShapeDtypeStruct((b, s, ml_w), BF16),
        grid=(b // bb, s // L),
        in_specs=[tok(2 * ml_w), tok(ml_w), tok(ml_w), tok(GATE_LANES),
                  pl.BlockSpec((bb, 2 * nh, L), lambda g, i: (g, 0, i)),
                  const(w_conv.shape), const((1, 2 * ml_w)), const((1, ml_w)),
                  const((L, L)), const(hmean.shape)],
        out_specs=tok(ml_w),
        scratch_shapes=[pltpu.VMEM((bb, MOD_ROWS, 2 * ml_w), F32),
                        pltpu.VMEM((bb * nh, dh, 2 * dh), F32),
                        pltpu.VMEM((bb * nh, MOD_ROWS, dh), F32)],
        compiler_params=pltpu.CompilerParams(dimension_semantics=("arbitrary", "arbitrary"),
                                             vmem_limit_bytes=VMEM_LIMIT_BYTES),
        name="mlstm",
    )(qkm, vm, og, gcol, grow, w_conv, b_conv, g_norm, tri, hmean)


def _outffn_kernel(x_ref, mod_ref, g_ref, *rest, dilations):
    n_lay = len(dilations)
    o_refs, l_refs = rest[:n_lay], rest[n_lay:2 * n_lay]
    (hm_ref, ex_ref, wo_hbm, wg_hbm, wu_hbm, wd_hbm, out_ref,
     operm_s, lperm_s, wo_ref, wg_ref, wu_ref, wd_ref, wide_s, narrow_s, sems) = rest[2 * n_lay:]
    tm = x_ref.shape[1]

    @pl.when((pl.program_id(0) == 0) & (pl.program_id(1) == 0))
    def _():
        jobs = []
        for src, dst, stage, kind in ((wg_hbm, wg_ref, wide_s, 0), (wu_hbm, wu_ref, wide_s, 0),
                                      (wd_hbm, wd_ref, narrow_s, 1), (wo_hbm, wo_ref, narrow_s, 1)):
            rc = stage.shape[1]
            for c in range(src.shape[0] // rc):
                slot = sum(1 for j in jobs if j[0] == kind) % WEIGHT_STAGE_SLOTS
                copy = pltpu.make_async_copy(src.at[pl.ds(c * rc, rc), :], stage.at[slot], sems.at[kind, slot])
                jobs.append((kind, copy, stage, slot, dst, c * rc, rc))
        ahead = WEIGHT_STAGE_SLOTS - 1
        for job in jobs[:ahead]:
            job[1].start()
        for n, (_, copy, stage, slot, dst, r0, rc) in enumerate(jobs):
            if n + ahead < len(jobs):
                jobs[n + ahead][1].start()
            copy.wait()
            dst[r0:r0 + rc, :] = stage[slot].astype(BF16)

    def natural(ref, d, scratch):
        if d == 1:
            return ref[0].astype(F32)
        n_cg = scratch.shape[1]
        chain = [step for step in _gather_chain(dilations) if step[1] <= d]
        for level, (parent, dd, f) in enumerate(reversed(chain)):
            dst = scratch.at[level % 2]
            src = scratch.at[(level + 1) % 2]
            n = tm // dd
            for c in range(parent):
                for a in range(f):
                    r = a * parent + c
                    for cg in range(n_cg):
                        if level == 0:
                            c0 = (r * n_cg + cg) * LANES
                            rows = ref[0, :, c0:c0 + LANES].astype(F32)
                        else:
                            rows = src[cg, r * n:(r + 1) * n, :]
                        dst[cg, pl.ds(c * (tm // parent) + a, n, stride=f), :] = rows
        final = scratch.at[(len(chain) - 1) % 2]
        return jnp.concatenate([final[cg] for cg in range(n_cg)], axis=-1)

    lses = [natural(ref, d, lperm_s) for ref, d in zip(l_refs, dilations)]
    mx = functools.reduce(jnp.maximum, lses)
    es = [jnp.exp(l - mx) for l in lses]
    inv = 1.0 / functools.reduce(jnp.add, es)
    ex = ex_ref[...]
    attn = None
    for e, ref, d in zip(es, o_refs, dilations):
        term = _dot((e * inv).astype(BF16), ex) * natural(ref, d, operm_s)
        attn = term if attn is None else attn + term
    aw = attn.shape[-1]
    y = _dot(attn.astype(BF16), wo_ref[:aw, :]) + _dot(hm_ref[0], wo_ref[aw:, :])
    x1 = x_ref[0] + mod_ref[0, 2:3, :] * y
    ms = jnp.mean(x1 * x1, axis=-1, keepdims=True)
    hn = x1 * lax.rsqrt(ms + RMS_EPS) * g_ref[...]
    hb = (hn * (1.0 + mod_ref[0, 4:5, :]) + mod_ref[0, 3:4, :]).astype(BF16)
    g = _dot(hb, wg_ref[...])
    u = _dot(hb, wu_ref[...])
    a = (g * _sigmoid(g) * u).astype(BF16)
    out_ref[0] = x1 + mod_ref[0, 5:6, :] * _dot(a, wd_ref[...])


def _outffn_call(x, mod, g_ffn, os_, lses, hm, expand, wo, wg, wu, wd, *, dilations):
    b, s, d = x.shape
    tm = FFN_ROWS
    assert all(tm % (dl * BF16_SUBLANES) == 0 for dl in dilations)
    aw = expand.shape[1]
    const = lambda shape: pl.BlockSpec(shape, lambda bi, i: (0,) * len(shape))
    tok = lambda w: pl.BlockSpec((1, tm, w), lambda bi, i: (bi, i, 0))
    strided = lambda w: tuple(pl.BlockSpec((1, tm // dl, dl * w), lambda bi, i: (bi, i, 0)) for dl in dilations)
    return pl.pallas_call(
        functools.partial(_outffn_kernel, dilations=dilations),
        out_shape=jax.ShapeDtypeStruct((b, s, d), F32),
        grid=(b, s // tm),
        in_specs=[tok(d), pl.BlockSpec((1, N_MOD, d), lambda bi, i: (bi, 0, 0)), const((1, d)),
                  *strided(aw), *strided(LANES), tok(hm.shape[-1]),
                  const(expand.shape)] + [pl.BlockSpec(memory_space=pl.ANY)] * 4,
        out_specs=tok(d),
        scratch_shapes=[pltpu.VMEM((2, aw // LANES, tm, LANES), F32), pltpu.VMEM((2, 1, tm, LANES), F32),
                        pltpu.VMEM(wo.shape, BF16), pltpu.VMEM(wg.shape, BF16), pltpu.VMEM(wu.shape, BF16),
                        pltpu.VMEM(wd.shape, BF16),
                        pltpu.VMEM((WEIGHT_STAGE_SLOTS, WEIGHT_CHUNK_ROWS // 2, wg.shape[1]), F32),
                        pltpu.VMEM((WEIGHT_STAGE_SLOTS, WEIGHT_CHUNK_ROWS, wd.shape[1]), F32),
                        pltpu.SemaphoreType.DMA((2, WEIGHT_STAGE_SLOTS))],
        compiler_params=pltpu.CompilerParams(dimension_semantics=("arbitrary", "arbitrary"),
                                             vmem_limit_bytes=VMEM_LIMIT_BYTES),
        name="outffn",
    )(x, mod, g_ffn, *os_, *lses, hm, expand, wo, wg, wu, wd)


def _block_diag_mean(width, group):
    idx = jnp.arange(width) // group
    return jnp.where(idx[:, None] == idx[None, :], 1.0 / group, 0.0).astype(BF16)


def kernel(x, c, g_mix, w_in, w_conv, b_conv, b_igate, b_fgate, q_norm_g, k_norm_g, mlstm_norm_g, w_out,
           g_ffn, w_gate, w_up, w_down, w_ada, b_ada):
    b, s, d = x.shape
    depth = g_mix.shape[0]
    attn_w = d // 2
    ml_w = d - attn_w
    nh = N_MLSTM_HEADS
    n_attn_heads = attn_w // ATTN_HEAD_DIM
    scale = ATTN_HEAD_DIM ** -0.5 * LOG2E
    dilations = tuple(dl for _, dl in DILATED_PATTERNS)
    assert b <= MOD_ROWS and 2 * nh <= GATE_LANES

    c_t = jnp.zeros((d, MOD_ROWS), F32).at[:, :b].set(c.T)
    assert attn_w % MXU_WIDTH == 0 and MXU_WIDTH % ATTN_HEAD_DIM == 0
    hmean_attn = _block_diag_mean(MXU_WIDTH, ATTN_HEAD_DIM)
    assert ml_w % MXU_WIDTH == 0 and MXU_WIDTH % (ml_w // nh) == 0
    hmean_ml = _block_diag_mean(MXU_WIDTH, ml_w // nh)
    stat_lane = jnp.arange(LANES)
    head_of_stat = jnp.where(stat_lane % ATTN_HEAD_DIM < n_attn_heads // 2,
                             2 * (stat_lane % ATTN_HEAD_DIM) + stat_lane // ATTN_HEAD_DIM, -1)
    head_of_col = jnp.arange(attn_w) // ATTN_HEAD_DIM
    expand = (head_of_stat[:, None] == head_of_col[None, :]).astype(BF16)
    tri = jnp.tril(jnp.ones((MLSTM_CHUNK, MLSTM_CHUNK), BF16))
    w_in_t = jnp.swapaxes(w_in, 1, 2).reshape(-1, d)

    for l in range(depth):
        mod = _mod_call(c_t, w_ada, b_ada[l][None, :], layer=l, n_rows=b)[:b].reshape(b, N_MOD, d)

        brow = jnp.concatenate([b_igate[l], b_fgate[l]])[:, None]
        gq = jnp.tile(q_norm_g[l] * scale, n_attn_heads)[None, :]
        gk = jnp.tile(k_norm_g[l], n_attn_heads)[None, :]

        qs, ks, vs, qkm, vm, og, gcol, grow = _inproj_call(
            x, mod, g_mix[l][None, :], w_in_t, gq, gk, hmean_attn, brow,
            layer=l, attn_w=attn_w, ml_w=ml_w, dilations=dilations)

        os_, lses = [], []
        for (window, dilation), q, k, v in zip(DILATED_PATTERNS, qs, ks, vs):
            o, lse = _attn_call(q, k, v, width=attn_w, window=window, dilation=dilation)
            os_.append(o)
            lses.append(lse)

        hm = _mlstm_call(qkm, vm, og, gcol, grow, w_conv[l], b_conv[l][None, :],
                         mlstm_norm_g[l][None, :], tri, hmean_ml)

        x = _outffn_call(x, mod, g_ffn[l][None, :], os_, lses, hm, expand,
                         w_out[l], w_gate[l], w_up[l], w_down[l], dilations=dilations)
    return x
```

```python
import functools

import jax
import jax.numpy as jnp
from jax import lax
from jax.experimental import pallas as pl
from jax.experimental.pallas import tpu as pltpu

F32 = jnp.float32
BF16 = jnp.bfloat16

ATTN_HEAD_DIM = 64
N_MLSTM_HEADS = 4
CONV_WIDTH = 4
DILATED_PATTERNS = ((128, 1), (512, 4), (2048, 16))
ATTN_BLOCK = 128
N_MOD = 6
RMS_EPS = 1e-6

LANES = 128
MXU_WIDTH = 256
BF16_SUBLANES = 16
VMEM_LIMIT_BYTES = 56 * 1024 * 1024
F32_SUBLANES = 8
MOD_ROWS = F32_SUBLANES
MOD_COL_STEPS = 4
LAYOUT_ROWS = 512
GATE_LANES = 128
INPROJ_ROWS = LAYOUT_ROWS
ATTN_MAX_SUB_BLOCKS = 16
MLSTM_CHUNK = 256
WEIGHT_CHUNK_ROWS = 128
WEIGHT_STAGE_SLOTS = 4
MLSTM_BATCH_ROWS = 1
FFN_ROWS = LAYOUT_ROWS

NEG_INF = float("-inf")
LOG2E = 1.4426950408889634
LN2 = 0.6931471805599453


def _dot(a, b, **kw):
    return jnp.dot(a, b, preferred_element_type=F32, **kw)


def _dot_nt(a, b, **kw):
    return lax.dot_general(a, b, (((1,), (1,)), ((), ())), preferred_element_type=F32, **kw)


def _dot_tn(a, b):
    return lax.dot_general(a, b, (((0,), (0,)), ((), ())), preferred_element_type=F32)


def _sigmoid(z):
    return 1.0 / (1.0 + jnp.exp(-z))


def _log_sigmoid(z):
    return jnp.minimum(z, 0.0) - jnp.log1p(jnp.exp(-jnp.abs(z)))


def _bf16_terms(x):
    hi = x.astype(BF16)
    r1 = x - hi.astype(F32)
    mid = r1.astype(BF16)
    lo = (r1 - mid.astype(F32)).astype(BF16)
    return [hi, mid, lo]


def _strided_shape(b, s, w, dilation):
    return (b, s // dilation, dilation * w)


def _gather_chain(dilations):
    chain, parent = [], 1
    for d in sorted(set(dilations)):
        if d == 1:
            continue
        assert d % parent == 0 and d // parent < F32_SUBLANES
        chain.append((parent, d, d // parent))
        parent = d
    return chain


def _mod_kernel(ct_ref, w_ref, b_ref, o_ref, *, n_rows):
    ct = ct_ref[...]
    sc = ct * _sigmoid(ct)
    w = w_ref[...]
    rows = [jnp.sum(w * sc[:, r:r + 1], axis=0, keepdims=True) for r in range(n_rows)]
    rows += [jnp.zeros_like(rows[0])] * (MOD_ROWS - n_rows)
    o_ref[...] = jnp.concatenate(rows, axis=0) + b_ref[...]


def _mod_call(c_t, w_ada, b_ada, *, layer, n_rows):
    _, d, n = w_ada.shape
    assert n % (MOD_COL_STEPS * LANES) == 0
    tn = n // MOD_COL_STEPS
    return pl.pallas_call(
        functools.partial(_mod_kernel, n_rows=n_rows),
        out_shape=jax.ShapeDtypeStruct((MOD_ROWS, n), F32),
        grid=(n // tn,),
        in_specs=[pl.BlockSpec((d, MOD_ROWS), lambda j: (0, 0)),
                  pl.BlockSpec((None, d, tn), lambda j: (layer, 0, j)),
                  pl.BlockSpec((1, tn), lambda j: (0, j))],
        out_specs=pl.BlockSpec((MOD_ROWS, tn), lambda j: (0, j)),
        compiler_params=pltpu.CompilerParams(dimension_semantics=("arbitrary",),
                                             vmem_limit_bytes=VMEM_LIMIT_BYTES),
        name="mod",
    )(c_t, w_ada, b_ada)


def _inproj_kernel(x_ref, mod_ref, g_ref, wt_ref, gq_ref, gk_ref, hp_ref,
                   brow_ref, *rest, attn_w, ml_w, dilations, w_row0):
    n_lay = len(dilations)
    q_refs, k_refs, v_refs = rest[:n_lay], rest[n_lay:2 * n_lay], rest[2 * n_lay:3 * n_lay]
    qkm_ref, vm_ref, og_ref, gcol_ref, grow_ref, perm_s, wb_s, stage_s, gate_s, sems = rest[3 * n_lay:]
    tm = x_ref.shape[1]
    nh = N_MLSTM_HEADS
    n_attn = 3 * attn_w
    n_main = n_attn + 4 * ml_w

    @pl.when((pl.program_id(0) == 0) & (pl.program_id(1) == 0))
    def _():
        n_slots, rc = stage_s.shape[0], stage_s.shape[1]
        gate_copy = pltpu.make_async_copy(wt_ref.at[pl.ds(w_row0 + n_main, 2 * nh), :], gate_s, sems.at[n_slots])
        gate_copy.start()
        copies = [pltpu.make_async_copy(wt_ref.at[pl.ds(w_row0 + c * rc, rc), :], stage_s.at[c % n_slots],
                                        sems.at[c % n_slots]) for c in range(n_main // rc)]
        ahead = n_slots - 1
        for copy in copies[:ahead]:
            copy.start()
        for c, copy in enumerate(copies):
            if c + ahead < len(copies):
                copies[c + ahead].start()
            copy.wait()
            wb_s[c * rc:(c + 1) * rc, :] = stage_s[c % n_slots].astype(BF16)
        gate_copy.wait()
        gate_rows = jnp.concatenate([gate_s[...],
                                     jnp.zeros((BF16_SUBLANES - 2 * nh, gate_s.shape[1]), F32)], axis=0)
        wb_s[n_main:, :] = gate_rows.astype(BF16)

    def emit(val, refs):
        w = val.shape[-1]
        n_cg = w // LANES
        ref_of = dict(zip(dilations, refs))
        if 1 in ref_of:
            ref_of[1][0] = val.astype(BF16)
        chain = _gather_chain(dilations)
        for cg in range(n_cg):
            perm_s[0, cg] = val[:, cg * LANES:(cg + 1) * LANES]
        for level, (parent, d, f) in enumerate(chain):
            src, dst = perm_s.at[level % 2], perm_s.at[(level + 1) % 2]
            n = tm // d
            for c in range(parent):
                for a in range(f):
                    r = a * parent + c
                    for cg in range(n_cg):
                        rows = src[cg, pl.ds(c * (tm // parent) + a, n, stride=f), :]
                        c0 = r * w + cg * LANES
                        ref_of[d][0, :, c0:c0 + LANES] = rows.astype(BF16)
                        if level + 1 < len(chain):
                            dst[cg, r * n:(r + 1) * n, :] = rows

    x = x_ref[0]
    ms = jnp.mean(x * x, axis=-1, keepdims=True)
    y = x * lax.rsqrt(ms + RMS_EPS) * g_ref[...]
    h = (y * (1.0 + mod_ref[0, 1:2, :]) + mod_ref[0, 0:1, :]).astype(BF16)

    xa = _dot_nt(h, wb_s[:n_attn, :])
    hp = hp_ref[...]

    def head_norm(t, g):
        t2 = (t * t).astype(BF16)
        gw = hp.shape[0]
        msq = jnp.concatenate([_dot(t2[:, c0:c0 + gw], hp) for c0 in range(0, t.shape[-1], gw)], axis=-1)
        return t * lax.rsqrt(msq + RMS_EPS) * g

    emit(head_norm(xa[:, :attn_w], gq_ref[...]), q_refs)
    emit(head_norm(xa[:, attn_w:2 * attn_w], gk_ref[...]), k_refs)
    emit(xa[:, 2 * attn_w:], v_refs)

    xm = _dot_nt(h, wb_s[n_attn:n_main, :])
    qkm_ref[0] = xm[:, :2 * ml_w].astype(BF16)
    vm_ref[0] = xm[:, 2 * ml_w:3 * ml_w].astype(BF16)
    og_ref[0] = _sigmoid(xm[:, 3 * ml_w:]).astype(BF16)

    zr = _dot_nt(wb_s[n_main:, :], h)[:2 * nh] + brow_ref[...]
    row = lax.broadcasted_iota(jnp.int32, zr.shape, 0)
    gates = jnp.where(row < nh, zr, _log_sigmoid(zr)) * LOG2E
    grow_ref[0] = gates
    padded = jnp.concatenate([gates, jnp.zeros((GATE_LANES - 2 * nh, tm), F32)], axis=0)
    gcol_ref[0] = padded.T


def _inproj_call(x, mod, g_mix, w_t, gq, gk, hp, brow, *, layer, attn_w, ml_w, dilations):
    b, s, d = x.shape
    tm = INPROJ_ROWS
    nh2 = 2 * N_MLSTM_HEADS
    n_main = 3 * attn_w + 4 * ml_w
    cols = n_main + nh2
    assert w_t.shape[0] % cols == 0 and cols % F32_SUBLANES == 0 and n_main % MXU_WIDTH == 0
    assert all(tm % (dl * BF16_SUBLANES) == 0 for dl in dilations)
    const = lambda shape: pl.BlockSpec(shape, lambda bi, i: (0,) * len(shape))
    tok = lambda w: pl.BlockSpec((1, tm, w), lambda bi, i: (bi, i, 0))
    strided_shapes = tuple(jax.ShapeDtypeStruct(_strided_shape(b, s, attn_w, dl), BF16) for dl in dilations)
    strided_specs = tuple(pl.BlockSpec((1, tm // dl, dl * attn_w), lambda bi, i: (bi, i, 0)) for dl in dilations)
    out_shape = strided_shapes * 3 + (
        jax.ShapeDtypeStruct((b, s, 2 * ml_w), BF16),
        jax.ShapeDtypeStruct((b, s, ml_w), BF16),
        jax.ShapeDtypeStruct((b, s, ml_w), BF16),
        jax.ShapeDtypeStruct((b, s, GATE_LANES), F32),
        jax.ShapeDtypeStruct((b, nh2, s), F32))
    outs = pl.pallas_call(
        functools.partial(_inproj_kernel, attn_w=attn_w, ml_w=ml_w, dilations=dilations, w_row0=layer * cols),
        out_shape=out_shape,
        grid=(b, s // tm),
        in_specs=[tok(d),
                  pl.BlockSpec((1, N_MOD, d), lambda bi, i: (bi, 0, 0)),
                  const((1, d)),
                  pl.BlockSpec(memory_space=pl.ANY),
                  const((1, attn_w)), const((1, attn_w)), const(hp.shape),
                  const((nh2, 1))],
        out_specs=strided_specs * 3 + (tok(2 * ml_w), tok(ml_w), tok(ml_w), tok(GATE_LANES),
                                       pl.BlockSpec((1, nh2, tm), lambda bi, i: (bi, 0, i))),
        scratch_shapes=[pltpu.VMEM((2, attn_w // LANES, tm, LANES), F32),
                        pltpu.VMEM((n_main + BF16_SUBLANES, d), BF16),
                        pltpu.VMEM((WEIGHT_STAGE_SLOTS, MXU_WIDTH, d), F32), pltpu.VMEM((nh2, d), F32),
                        pltpu.SemaphoreType.DMA((WEIGHT_STAGE_SLOTS + 1,))],
        compiler_params=pltpu.CompilerParams(dimension_semantics=("arbitrary", "arbitrary"),
                                             vmem_limit_bytes=VMEM_LIMIT_BYTES),
        name="inproj",
    )(x, mod, g_mix, w_t, gq, gk, hp, brow)
    n_lay = len(dilations)
    return (outs[:n_lay], outs[n_lay:2 * n_lay], outs[2 * n_lay:3 * n_lay]) + tuple(outs[3 * n_lay:])


def _attn_kernel(q_ref, kp_ref, kc_ref, vp_ref, vc_ref, bias_ref, ones_ref, o_ref, lse_ref, *, n_heads):
    blk = ATTN_BLOCK
    sub_blocks = q_ref.shape[1] // blk
    hd = ATTN_HEAD_DIM
    w = n_heads * hd
    n_res = q_ref.shape[2] // w
    n_pairs = n_heads // 2
    n = pl.program_id(2)
    lane = lax.broadcasted_iota(jnp.int32, (blk, LANES), 1)
    first_head = lane < hd

    def key_rows(prev_ref, cur_ref, i, cols):
        if i == 0:
            return jnp.concatenate([prev_ref[0, :, cols], cur_ref[0, 0:blk, cols]], axis=0)
        return cur_ref[0, (i - 1) * blk:(i + 1) * blk, cols]

    units = [(res, i, j) for res in range(n_res) for i in range(sub_blocks) for j in range(n_pairs)]
    scores, maxes, probs = [], [], []
    for res, i, j in units:
        cols = slice(res * w + j * LANES, res * w + (j + 1) * LANES)
        q = q_ref[0, i * blk:(i + 1) * blk, cols]
        ks = key_rows(kp_ref, kc_ref, i, cols)
        bias = bias_ref[jnp.where(n == 0, 1, 0)] if i == 0 else bias_ref[0]
        q2 = jnp.concatenate([jnp.where(first_head, q, 0), jnp.where(first_head, 0, q)], axis=0)
        s = _dot_nt(q2, ks) + bias
        scores.append(s)
        maxes.append(jnp.max(s, axis=-1, keepdims=True))
    for s, m in zip(scores, maxes):
        probs.append(jnp.exp2(s - m).astype(BF16))
    for res in range(n_res):
        for i in range(sub_blocks):
            outs = []
            stats = jnp.zeros((blk, LANES), F32)
            for j in range(n_pairs):
                u = (res * sub_blocks + i) * n_pairs + j
                cols = slice(res * w + j * LANES, res * w + (j + 1) * LANES)
                vw = jnp.concatenate([key_rows(vp_ref, vc_ref, i, cols), ones_ref[...]], axis=1)
                ov = _dot(probs[u], vw)
                num = jnp.where(first_head, ov[:blk, :LANES], ov[blk:, :LANES])
                den = jnp.where(first_head, ov[:blk, LANES:], ov[blk:, LANES:])
                outs.append(num / den)
                m = jnp.broadcast_to(maxes[u], (2 * blk, LANES))
                lse = jnp.where(first_head, m[:blk], m[blk:]) * LN2 + jnp.log(den)
                stats = jnp.where((lane & (hd - 1)) == j, lse, stats)
            rows = slice(i * blk, (i + 1) * blk)
            o_ref[0, rows, res * w:(res + 1) * w] = jnp.concatenate(outs, axis=-1).astype(o_ref.dtype)
            lse_ref[0, rows, res * LANES:(res + 1) * LANES] = stats


def _attn_band_masks(n_back):
    blk = ATTN_BLOCK
    row = jnp.arange(2 * blk)[:, None] % blk
    col = jnp.arange(2 * blk)[None, :]
    band = (col >= row + (blk - n_back)) & (col <= row + blk)
    return jnp.stack([jnp.where(band, 0.0, NEG_INF), jnp.where(band & (col >= blk), 0.0, NEG_INF)]).astype(F32)


def _attn_call(q, k, v, *, width, window, dilation):
    b, ls, _ = q.shape
    w = width
    n_back = window // dilation
    blk = ATTN_BLOCK
    assert n_back <= blk
    sub_blocks = min(ATTN_MAX_SUB_BLOCKS, ls // blk)
    step_rows = sub_blocks * blk
    assert ls % step_rows == 0
    n_heads = w // ATTN_HEAD_DIM
    assert n_heads <= LANES and n_heads % 2 == 0 and 2 * ATTN_HEAD_DIM == LANES
    n_res = max(1, min(dilation, ATTN_MAX_SUB_BLOCKS // sub_blocks))
    assert dilation % n_res == 0
    cur = pl.BlockSpec((1, step_rows, n_res * w), lambda bi, r, n: (bi, n, r))
    prev = pl.BlockSpec((1, blk, n_res * w),
                        lambda bi, r, n: (bi, jnp.maximum(n * sub_blocks - 1, 0), r))
    masks = _attn_band_masks(n_back)
    ones = jnp.ones((2 * blk, LANES), BF16)
    const = lambda shape: pl.BlockSpec(shape, lambda bi, r, n: (0,) * len(shape))
    return pl.pallas_call(
        functools.partial(_attn_kernel, n_heads=n_heads),
        out_shape=(jax.ShapeDtypeStruct((b, ls, dilation * w), BF16),
                   jax.ShapeDtypeStruct((b, ls, dilation * LANES), F32)),
        grid=(b, dilation // n_res, ls // step_rows),
        in_specs=[cur, prev, cur, prev, cur, const(masks.shape), const(ones.shape)],
        out_specs=(cur, pl.BlockSpec((1, step_rows, n_res * LANES), lambda bi, r, n: (bi, n, r))),
        compiler_params=pltpu.CompilerParams(
            dimension_semantics=("arbitrary", "arbitrary", "arbitrary"),
            vmem_limit_bytes=VMEM_LIMIT_BYTES),
        name=f"attn_d{dilation}",
    )(q, k, k, v, v, masks, ones)


def _mlstm_kernel(qk_ref, v_ref, og_ref, gcol_ref, grow_ref, wc_ref, bc_ref, gn_ref, tri_ref, hm_ref,
                  o_ref, tail_s, c_s, m_s, *, ml_w):
    L = MLSTM_CHUNK
    nh = N_MLSTM_HEADS
    dh = ml_w // nh

    @pl.when(pl.program_id(1) == 0)
    def _():
        tail_s[...] = jnp.zeros_like(tail_s)
        c_s[...] = jnp.zeros_like(c_s)
        m_s[...] = jnp.zeros_like(m_s)

    tri = tri_ref[...]
    ti = lax.broadcasted_iota(jnp.int32, (L, L), 0)
    si = lax.broadcasted_iota(jnp.int32, (L, L), 1)
    causal = si <= ti
    ones = jnp.ones((L, dh), BF16)
    hp = hm_ref[...]
    for bi in range(qk_ref.shape[0]):
        _mlstm_chunk(bi, qk_ref, v_ref, og_ref, gcol_ref, grow_ref, wc_ref, bc_ref, gn_ref, o_ref,
                     tail_s, c_s, m_s, tri, causal, ones, hp, ml_w=ml_w)


def _mlstm_chunk(bi, qk_ref, v_ref, og_ref, gcol_ref, grow_ref, wc_ref, bc_ref, gn_ref, o_ref,
                 tail_s, c_s, m_s, tri, causal, ones, hp, *, ml_w):
    L = MLSTM_CHUNK
    nh = N_MLSTM_HEADS
    dh = ml_w // nh
    x = qk_ref[bi].astype(F32)
    tail = tail_s[bi]
    row8 = lax.broadcasted_iota(jnp.int32, tail.shape, 0)
    acc = bc_ref[...] + x * wc_ref[CONV_WIDTH - 1:CONV_WIDTH, :]
    for back in range(1, CONV_WIDTH):
        rolled = pltpu.roll(x, back, 0)
        top = jnp.where(row8 >= back, rolled[:MOD_ROWS], pltpu.roll(tail, back, 0))
        shifted = jnp.concatenate([top, rolled[MOD_ROWS:]], axis=0)
        acc = acc + shifted * wc_ref[CONV_WIDTH - 1 - back:CONV_WIDTH - back, :]
    tail_s[bi] = x[L - MOD_ROWS:]
    qk = acc * _sigmoid(acc)

    gcol = gcol_ref[bi]
    grow = grow_ref[bi]
    gc_parts, gr_parts = _bf16_terms(gcol), _bf16_terms(grow)
    bcol_3 = _dot(tri, jnp.concatenate(gc_parts, axis=1))
    bcol_all = bcol_3[:, :GATE_LANES] + bcol_3[:, GATE_LANES:2 * GATE_LANES] + bcol_3[:, 2 * GATE_LANES:]
    pad = jnp.zeros((BF16_SUBLANES - grow.shape[0], L), BF16)
    brow_3 = [_dot_nt(jnp.concatenate([part, pad], axis=0), tri)[:grow.shape[0]] for part in gr_parts]
    brow_all = brow_3[0] + brow_3[1] + brow_3[2]

    ks = [qk[:, ml_w + h * dh:ml_w + (h + 1) * dh] * (dh ** -0.5) for h in range(nh)]

    heads = []
    for h in range(nh):
        st = bi * nh + h
        b_c = jnp.broadcast_to(bcol_all[:, nh + h:nh + h + 1], (L, dh))
        i_c = jnp.broadcast_to(gcol[:, h:h + 1], (L, dh))
        b_r = brow_all[nh + h:nh + h + 1, :]
        i_r = grow[h:h + 1, :]
        m_prev = m_s[st, 0:1, :]
        log_d = jnp.where(causal, b_c[:, 0:1] + (i_r - b_r), NEG_INF)
        m_inter = b_c + m_prev
        m_t = jnp.maximum(m_inter, jnp.max(log_d, axis=-1, keepdims=True))
        d_mat = jnp.exp2(log_d - m_t[:, 0:1])
        inter = jnp.exp2(m_inter - m_t)
        b_last = b_c[L - 1:L, :]
        w_log = b_last - b_c + i_c
        m_new = jnp.maximum(b_last + m_prev, jnp.max(w_log, axis=0, keepdims=True))
        wgt = jnp.exp2(w_log - m_new)
        decay = jnp.exp2(b_last + m_prev - m_new)
        heads.append((m_t, d_mat, inter, m_new, wgt, decay))

    outs = []
    for h, (m_t, d_mat, inter, _, _, _) in enumerate(heads):
        hcols = slice(h * dh, (h + 1) * dh)
        q = qk[:, hcols].astype(BF16)
        v_ext = jnp.concatenate([v_ref[bi, :, hcols], ones], axis=-1)
        c_prev = c_s[bi * nh + h]
        s_qk = _dot_nt(q, ks[h].astype(BF16)) * d_mat
        ext = jnp.concatenate([inter, inter], axis=-1) * _dot(q, c_prev.astype(BF16)) \
            + _dot(s_qk.astype(BF16), v_ext)
        outs.append(ext[:, :dh] / jnp.maximum(jnp.abs(ext[:, dh:]), jnp.exp2(-m_t)))
    hh = jnp.concatenate(outs, axis=-1)
    h2 = (hh * hh).astype(BF16)
    gw = hp.shape[0]
    msq = jnp.concatenate([_dot(h2[:, c0:c0 + gw], hp) for c0 in range(0, ml_w, gw)], axis=-1)
    hn = hh * lax.rsqrt(msq + RMS_EPS) * gn_ref[...]
    o_ref[bi] = (og_ref[bi].astype(F32) * hn).astype(o_ref.dtype)

    for h, (_, _, _, m_new, wgt, decay) in enumerate(heads):
        st = bi * nh + h
        hcols = slice(h * dh, (h + 1) * dh)
        v_ext = jnp.concatenate([v_ref[bi, :, hcols], ones], axis=-1)
        kw = (ks[h] * wgt).astype(BF16)
        c_s[st] = jnp.concatenate([decay, decay], axis=-1) * c_s[st] + _dot_tn(kw, v_ext)
        m_s[st] = jnp.broadcast_to(m_new, m_s.shape[1:])


def _mlstm_call(qkm, vm, og, gcol, grow, w_conv, b_conv, g_norm, tri, hmean):
    b, s, ml_w = vm.shape
    L = MLSTM_CHUNK
    nh = N_MLSTM_HEADS
    dh = ml_w // nh
    bb = MLSTM_BATCH_ROWS
    assert b % bb == 0
    const = lambda shape: pl.BlockSpec(shape, lambda g, i: (0,) * len(shape))
    tok = lambda w: pl.BlockSpec((bb, L, w), lambda g, i: (g, i, 0))
    return pl.pallas_call(
        functools.partial(_mlstm_kernel, ml_w=ml_w),
        out_shape=jax.ShapeDtypeStruct((b, s, ml_w), BF16),
        grid=(b // bb, s // L),
        in_specs=[tok(2 * ml_w), tok(ml_w), tok(ml_w), tok(GATE_LANES),
                  pl.BlockSpec((bb, 2 * nh, L), lambda g, i: (g, 0, i)),
                  const(w_conv.shape), const((1, 2 * ml_w)), const((1, ml_w)),
                  const((L, L)), const(hmean.shape)],
        out_specs=tok(ml_w),
        scratch_shapes=[pltpu.VMEM((bb, MOD_ROWS, 2 * ml_w), F32),
                        pltpu.VMEM((bb * nh, dh, 2 * dh), F32),
                        pltpu.VMEM((bb * nh, MOD_ROWS, dh), F32)],
        compiler_params=pltpu.CompilerParams(dimension_semantics=("arbitrary", "arbitrary"),
                                             vmem_limit_bytes=VMEM_LIMIT_BYTES),
        name="mlstm",
    )(qkm, vm, og, gcol, grow, w_conv, b_conv, g_norm, tri, hmean)


def _outffn_kernel(x_ref, mod_ref, g_ref, *rest, dilations):
    n_lay = len(dilations)
    o_refs, l_refs = rest[:n_lay], rest[n_lay:2 * n_lay]
    (hm_ref, ex_ref, wo_hbm, wg_hbm, wu_hbm, wd_hbm, out_ref,
     operm_s, lperm_s, wo_ref, wg_ref, wu_ref, wd_ref, wide_s, narrow_s, sems) = rest[2 * n_lay:]
    tm = x_ref.shape[1]

    @pl.when((pl.program_id(0) == 0) & (pl.program_id(1) == 0))
    def _():
        jobs = []
        for src, dst, stage, kind in ((wg_hbm, wg_ref, wide_s, 0), (wu_hbm, wu_ref, wide_s, 0),
                                      (wd_hbm, wd_ref, narrow_s, 1), (wo_hbm, wo_ref, narrow_s, 1)):
            rc = stage.shape[1]
            for c in range(src.shape[0] // rc):
                slot = sum(1 for j in jobs if j[0] == kind) % WEIGHT_STAGE_SLOTS
                copy = pltpu.make_async_copy(src.at[pl.ds(c * rc, rc), :], stage.at[slot], sems.at[kind, slot])
                jobs.append((kind, copy, stage, slot, dst, c * rc, rc))
        ahead = WEIGHT_STAGE_SLOTS - 1
        for job in jobs[:ahead]:
            job[1].start()
        for n, (_, copy, stage, slot, dst, r0, rc) in enumerate(jobs):
            if n + ahead < len(jobs):
                jobs[n + ahead][1].start()
            copy.wait()
            dst[r0:r0 + rc, :] = stage[slot].astype(BF16)

    def natural(ref, d, scratch):
        if d == 1:
            return ref[0].astype(F32)
        n_cg = scratch.shape[1]
        chain = [step for step in _gather_chain(dilations) if step[1] <= d]
        for level, (parent, dd, f) in enumerate(reversed(chain)):
            dst = scratch.at[level % 2]
            src = scratch.at[(level + 1) % 2]
            n = tm // dd
            for c in range(parent):
                for a in range(f):
                    r = a * parent + c
                    for cg in range(n_cg):
                        if level == 0:
                            c0 = (r * n_cg + cg) * LANES
                            rows = ref[0, :, c0:c0 + LANES].astype(F32)
                        else:
                            rows = src[cg, r * n:(r + 1) * n, :]
                        dst[cg, pl.ds(c * (tm // parent) + a, n, stride=f), :] = rows
        final = scratch.at[(len(chain) - 1) % 2]
        return jnp.concatenate([final[cg] for cg in range(n_cg)], axis=-1)

    lses = [natural(ref, d, lperm_s) for ref, d in zip(l_refs, dilations)]
    mx = functools.reduce(jnp.maximum, lses)
    es = [jnp.exp(l - mx) for l in lses]
    inv = 1.0 / functools.reduce(jnp.add, es)
    ex = ex_ref[...]
    attn = None
    for e, ref, d in zip(es, o_refs, dilations):
        term = _dot((e * inv).astype(BF16), ex) * natural(ref, d, operm_s)
        attn = term if attn is None else attn + term
    aw = attn.shape[-1]
    y = _dot(attn.astype(BF16), wo_ref[:aw, :]) + _dot(hm_ref[0], wo_ref[aw:, :])
    x1 = x_ref[0] + mod_ref[0, 2:3, :] * y
    ms = jnp.mean(x1 * x1, axis=-1, keepdims=True)
    hn = x1 * lax.rsqrt(ms + RMS_EPS) * g_ref[...]
    hb = (hn * (1.0 + mod_ref[0, 4:5, :]) + mod_ref[0, 3:4, :]).astype(BF16)
    g = _dot(hb, wg_ref[...])
    u = _dot(hb, wu_ref[...])
    a = (g * _sigmoid(g) * u).astype(BF16)
    out_ref[0] = x1 + mod_ref[0, 5:6, :] * _dot(a, wd_ref[...])


def _outffn_call(x, mod, g_ffn, os_, lses, hm, expand, wo, wg, wu, wd, *, dilations):
    b, s, d = x.shape
    tm = FFN_ROWS
    assert all(tm % (dl * BF16_SUBLANES) == 0 for dl in dilations)
    aw = expand.shape[1]
    const = lambda shape: pl.BlockSpec(shape, lambda bi, i: (0,) * len(shape))
    tok = lambda w: pl.BlockSpec((1, tm, w), lambda bi, i: (bi, i, 0))
    strided = lambda w: tuple(pl.BlockSpec((1, tm // dl, dl * w), lambda bi, i: (bi, i, 0)) for dl in dilations)
    return pl.pallas_call(
        functools.partial(_outffn_kernel, dilations=dilations),
        out_shape=jax.ShapeDtypeStruct((b, s, d), F32),
        grid=(b, s // tm),
        in_specs=[tok(d), pl.BlockSpec((1, N_MOD, d), lambda bi, i: (bi, 0, 0)), const((1, d)),
                  *strided(aw), *strided(LANES), tok(hm.shape[-1]),
                  const(expand.shape)] + [pl.BlockSpec(memory_space=pl.ANY)] * 4,
        out_specs=tok(d),
        scratch_shapes=[pltpu.VMEM((2, aw // LANES, tm, LANES), F32), pltpu.VMEM((2, 1, tm, LANES), F32),
                        pltpu.VMEM(wo.shape, BF16), pltpu.VMEM(wg.shape, BF16), pltpu.VMEM(wu.shape, BF16),
                        pltpu.VMEM(wd.shape, BF16),
                        pltpu.VMEM((WEIGHT_STAGE_SLOTS, WEIGHT_CHUNK_ROWS // 2, wg.shape[1]), F32),
                        pltpu.VMEM((WEIGHT_STAGE_SLOTS, WEIGHT_CHUNK_ROWS, wd.shape[1]), F32),
                        pltpu.SemaphoreType.DMA((2, WEIGHT_STAGE_SLOTS))],
        compiler_params=pltpu.CompilerParams(dimension_semantics=("arbitrary", "arbitrary"),
                                             vmem_limit_bytes=VMEM_LIMIT_BYTES),
        name="outffn",
    )(x, mod, g_ffn, *os_, *lses, hm, expand, wo, wg, wu, wd)


def _block_diag_mean(width, group):
    idx = jnp.arange(width) // group
    return jnp.where(idx[:, None] == idx[None, :], 1.0 / group, 0.0).astype(BF16)


def kernel(x, c, g_mix, w_in, w_conv, b_conv, b_igate, b_fgate, q_norm_g, k_norm_g, mlstm_norm_g, w_out,
           g_ffn, w_gate, w_up, w_down, w_ada, b_ada):
    b, s, d = x.shape
    depth = g_mix.shape[0]
    attn_w = d // 2
    ml_w = d - attn_w
    nh = N_MLSTM_HEADS
    n_attn_heads = attn_w // ATTN_HEAD_DIM
    scale = ATTN_HEAD_DIM ** -0.5 * LOG2E
    dilations = tuple(dl for _, dl in DILATED_PATTERNS)
    assert b <= MOD_ROWS and 2 * nh <= GATE_LANES

    c_t = jnp.zeros((d, MOD_ROWS), F32).at[:, :b].set(c.T)
    assert attn_w % MXU_WIDTH == 0 and MXU_WIDTH % ATTN_HEAD_DIM == 0
    hmean_attn = _block_diag_mean(MXU_WIDTH, ATTN_HEAD_DIM)
    assert ml_w % MXU_WIDTH == 0 and MXU_WIDTH % (ml_w // nh) == 0
    hmean_ml = _block_diag_mean(MXU_WIDTH, ml_w // nh)
    stat_lane = jnp.arange(LANES)
    head_of_stat = jnp.where(stat_lane % ATTN_HEAD_DIM < n_attn_heads // 2,
                             2 * (stat_lane % ATTN_HEAD_DIM) + stat_lane // ATTN_HEAD_DIM, -1)
    head_of_col = jnp.arange(attn_w) // ATTN_HEAD_DIM
    expand = (head_of_stat[:, None] == head_of_col[None, :]).astype(BF16)
    tri = jnp.tril(jnp.ones((MLSTM_CHUNK, MLSTM_CHUNK), BF16))
    w_in_t = jnp.swapaxes(w_in, 1, 2).reshape(-1, d)

    for l in range(depth):
        mod = _mod_call(c_t, w_ada, b_ada[l][None, :], layer=l, n_rows=b)[:b].reshape(b, N_MOD, d)

        brow = jnp.concatenate([b_igate[l], b_fgate[l]])[:, None]
        gq = jnp.tile(q_norm_g[l] * scale, n_attn_heads)[None, :]
        gk = jnp.tile(k_norm_g[l], n_attn_heads)[None, :]

        qs, ks, vs, qkm, vm, og, gcol, grow = _inproj_call(
            x, mod, g_mix[l][None, :], w_in_t, gq, gk, hmean_attn, brow,
            layer=l, attn_w=attn_w, ml_w=ml_w, dilations=dilations)

        os_, lses = [], []
        for (window, dilation), q, k, v in zip(DILATED_PATTERNS, qs, ks, vs):
            o, lse = _attn_call(q, k, v, width=attn_w, window=window, dilation=dilation)
            os_.append(o)
            lses.append(lse)

        hm = _mlstm_call(qkm, vm, og, gcol, grow, w_conv[l], b_conv[l][None, :],
                         mlstm_norm_g[l][None, :], tri, hmean_ml)

        x = _outffn_call(x, mod, g_ffn[l][None, :], os_, lses, hm, expand,
                         w_out[l], w_gate[l], w_up[l], w_down[l], dilations=dilations)
    return x
```
